```python
import jax, jax.numpy as jnp
from jax import lax
import numpy as np

D_MODEL = 1024
BATCH = 16
SEQ = 2048
DEPTH = 2
DEC_BATCH = 32
DEC_SEQ = 4
PAST_LEN = 16384
PAGE_SIZE = 128

N_A_LAYERS = DEPTH // 2
N_B_LAYERS = DEPTH - N_A_LAYERS
N_DENSE = (DEPTH + 1) // 2
N_MOE = DEPTH // 2
RW_HEAD_DIM = 64
RW_HEADS = D_MODEL // RW_HEAD_DIM
RW_LORA_DECAY = 64
RW_LORA_ICLR = 64
RW_LORA_GATE = 160
RW_GN_EPS = 64e-5
N_HEADS = 16
HEAD_DIM = D_MODEL // N_HEADS
N_KV_HEADS = 4
HEADS_PER_KV = N_HEADS // N_KV_HEADS
N_BRANCH = 3
CMP_BLOCK = 32
CMP_STRIDE = 16
CMP_RATIO = CMP_BLOCK // CMP_STRIDE
CMP_HIDDEN = HEAD_DIM
SLC_BLOCK = 64
SLC_TOP = 16
WINDOW = 512
SLC_QBLOCK = 64
WIN_QBLOCK = 128
ATTN_SCALE = HEAD_DIM ** -0.5
D_FF = 2816
N_EXPERTS = 8
TOP_K = 2
MOE_BLOCK = 128
RMS_EPS = 1e-6

kernel_name = 'rwkv7_yoco_nsa_moe_decode_step'


def rmsnorm(x, g):
    xf = x.astype(jnp.float32)
    y = xf * lax.rsqrt(jnp.mean(xf * xf, -1, keepdims=True) + RMS_EPS)
    return (y * g.astype(jnp.float32)).astype(x.dtype)


def masked_softmax(s, mask):
    s = jnp.where(mask, s.astype(jnp.float32), -jnp.inf)
    m = jnp.max(s, -1, keepdims=True)
    m = jnp.where(jnp.isfinite(m), m, 0.0)
    e = jnp.exp(s - m)
    return e / jnp.maximum(jnp.sum(e, -1, keepdims=True), 1e-30)


def swiglu(x, wg, wu, wd):
    return (jax.nn.silu(x @ wg) * (x @ wu)) @ wd


def moe_swiglu(x, w_router, wg, wu, wd):
    shp = x.shape
    xt = x.reshape(-1, D_MODEL)
    n = xt.shape[0]
    logits = (xt @ w_router).astype(jnp.float32)
    top_v, top_e = lax.top_k(logits, TOP_K)
    gates = jax.nn.softmax(top_v, -1)
    p = n * TOP_K
    flat_e = top_e.reshape(p)
    flat_tok = jnp.repeat(jnp.arange(n, dtype=jnp.int32), TOP_K)
    flat_g = gates.reshape(p)
    order = jnp.argsort(flat_e)
    se = flat_e[order]
    counts = jnp.zeros((N_EXPERTS,), jnp.int32).at[flat_e].add(1)
    padded = (counts + MOE_BLOCK - 1) // MOE_BLOCK * MOE_BLOCK
    pad_end = jnp.cumsum(padded)
    pad_start = pad_end - padded
    start = jnp.cumsum(counts) - counts
    dest = pad_start[se] + jnp.arange(p, dtype=jnp.int32) - start[se]
    n_blocks = -(-p // MOE_BLOCK) + N_EXPERTS
    cap = n_blocks * MOE_BLOCK
    buf_tok = jnp.full((cap,), n, jnp.int32).at[dest].set(flat_tok[order])
    buf_gate = jnp.zeros((cap,), jnp.float32).at[dest].set(flat_g[order])
    block_e = jnp.minimum(jnp.searchsorted(pad_end, jnp.arange(n_blocks) * MOE_BLOCK, side='right'), N_EXPERTS - 1)
    x_pad = jnp.concatenate([xt, jnp.zeros((1, D_MODEL), xt.dtype)], 0)

    def expert_block(args):
        tok, e = args
        xb = x_pad[tok]
        return (jax.nn.silu(xb @ wg[e]) * (xb @ wu[e])) @ wd[e]

    yb = lax.map(expert_block, (buf_tok.reshape(n_blocks, MOE_BLOCK), block_e))
    y = jnp.zeros((n + 1, D_MODEL), jnp.float32).at[buf_tok].add(yb.reshape(cap, D_MODEL).astype(jnp.float32) * buf_gate[:, None])
    return y[:n].astype(x.dtype).reshape(shp)


def rwkv7_time_mix(x, shift0, wkv0, mix, vec, w_rkv, w_o, dw1, dw2, aw1, aw2, gw1, gw2):
    B, T, D = x.shape
    f32 = jnp.float32
    xx = jnp.concatenate([shift0[:, None, :].astype(x.dtype), x[:, :-1]], axis=1) - x
    xr, xw, xk, xv, xa, xg = (x + xx * mix[i] for i in range(6))
    w0, a0, k_k, k_a, r_k, ln_w, ln_b = (vec[i] for i in range(7))
    heads = lambda t: t.reshape(B, T, RW_HEADS, RW_HEAD_DIM).astype(f32)
    r = heads(xr @ w_rkv[0])
    k = xk @ w_rkv[1]
    v = heads(xv @ w_rkv[2])
    log_w = -jax.nn.softplus(-(w0 + jnp.tanh(xw @ dw1) @ dw2)) - 0.5
    decay = jnp.exp(-jnp.exp(heads(log_w)))
    a = jax.nn.sigmoid(a0 + (xa @ aw1) @ aw2)
    g = jax.nn.sigmoid(xg @ gw1) @ gw2
    kk = heads(k * k_k)
    kk = kk * lax.rsqrt(jnp.maximum(jnp.sum(kk * kk, -1, keepdims=True), 1e-24))
    k = heads(k * (1 + (a - 1) * k_a))
    a = heads(a)

    def step(S, inp):
        r_t, w_t, k_t, v_t, kk_t, a_t = inp
        sa = jnp.einsum('bhvk,bhk->bhv', S, -kk_t)
        S = S * w_t[:, :, None, :] + sa[..., None] * (kk_t * a_t)[:, :, None, :] + v_t[..., None] * k_t[:, :, None, :]
        return S, jnp.einsum('bhvk,bhk->bhv', S, r_t)

    seq = tuple(jnp.swapaxes(t, 0, 1) for t in (r, decay, k, v, kk, a))
    S_T, y = lax.scan(step, wkv0.astype(f32), seq)
    y = jnp.swapaxes(y, 0, 1)
    mu = jnp.mean(y, -1, keepdims=True)
    var = jnp.mean(jnp.square(y - mu), -1, keepdims=True)
    y = ((y - mu) * lax.rsqrt(var + RW_GN_EPS)).reshape(B, T, D) * ln_w.astype(f32) + ln_b.astype(f32)
    bonus = jnp.sum(r * k * r_k.reshape(RW_HEADS, RW_HEAD_DIM).astype(f32), -1, keepdims=True) * v
    y = y + bonus.reshape(B, T, D)
    out = (y * g.astype(f32)).astype(x.dtype) @ w_o
    return out, x[:, -1], S_T.astype(wkv0.dtype)


def shared_kv_rows(h, kv_norm_g, w_kv, k_norm_g):
    B, T, _ = h.shape
    kv = (rmsnorm(h, kv_norm_g) @ w_kv).reshape(B, T, N_BRANCH, 2, N_KV_HEADS, HEAD_DIM)
    normed = lambda c: jnp.stack([rmsnorm(kv[:, :, c, 0], k_norm_g[c]), kv[:, :, c, 1]], axis=2)
    return kv[:, :, 0], normed(1), normed(2)


def compress(rows, pe, w1, w2):
    lead = rows.shape[:-3]
    n_sub = rows.shape[-3] // CMP_STRIDE
    n_cmp = n_sub - CMP_RATIO + 1
    sub = rows[..., :n_sub * CMP_STRIDE, :, :].reshape(lead + (n_sub, CMP_STRIDE, N_KV_HEADS, HEAD_DIM))
    pe_r = pe.reshape(CMP_RATIO, CMP_STRIDE, HEAD_DIM)
    w1_r = w1.reshape(CMP_RATIO, CMP_STRIDE, HEAD_DIM, CMP_HIDDEN)
    pre = sum(jnp.einsum('...nsgd,sdh->...ngh', sub[..., j:j + n_cmp, :, :, :] + pe_r[j][:, None, :], w1_r[j])
              for j in range(CMP_RATIO))
    return jax.nn.silu(pre) @ w2


def compressed_kv(rows, k_norm_g0, cmp_pe, cmp_w1, cmp_w2):
    kc = rmsnorm(compress(rows[..., 0, :, :], cmp_pe[0], cmp_w1[0], cmp_w2[0]), k_norm_g0)
    vc = compress(rows[..., 1, :, :], cmp_pe[1], cmp_w1[1], cmp_w2[1])
    return kc, vc


def cmp_to_slc(n_cmp, n_slc):
    c0 = jnp.arange(n_cmp)[:, None] * CMP_STRIDE
    s0 = jnp.arange(n_slc)[None, :] * SLC_BLOCK
    return ((c0 < s0 + SLC_BLOCK) & (c0 + CMP_BLOCK > s0)).astype(jnp.float32)


def nsa_cmp_slc(q, q_pos, kc, vc, gather_slc, n_slc):
    Q = q.shape[0]
    n_cmp = kc.shape[0]
    qg = q.reshape(Q, N_KV_HEADS, HEADS_PER_KV, HEAD_DIM)
    s = jnp.einsum('qgid,ngd->qgin', qg, kc) * ATTN_SCALE
    c_end = jnp.arange(n_cmp) * CMP_STRIDE + CMP_BLOCK - 1
    p = masked_softmax(s, (c_end[None, :] <= q_pos[:, None])[:, None, None, :])
    o_cmp = jnp.einsum('qgin,ngd->qgid', p.astype(vc.dtype), vc)
    imp = jnp.einsum('qgin,nj->qgj', p, cmp_to_slc(n_cmp, n_slc))
    blk = jnp.arange(n_slc)[None, None, :]
    cur = (q_pos // SLC_BLOCK)[:, None, None]
    forced = (blk == 0) | (blk == cur) | (blk == cur - 1)
    score = jnp.where(blk <= cur, jnp.where(forced, jnp.inf, imp), -jnp.inf)
    _, idx = lax.top_k(score, min(SLC_TOP, n_slc))
    n_sel = idx.shape[-1]
    kpos = idx[..., None] * SLC_BLOCK + jnp.arange(SLC_BLOCK)
    kv = gather_slc(kpos)
    mask = (kpos <= q_pos[:, None, None, None]).reshape(Q, N_KV_HEADS, 1, n_sel * SLC_BLOCK)
    s2 = jnp.einsum('qgid,qgnkd->qgink', qg, kv[..., 0, :]) * ATTN_SCALE
    p2 = masked_softmax(s2.reshape(Q, N_KV_HEADS, HEADS_PER_KV, n_sel * SLC_BLOCK), mask)
    v_sel = kv[..., 1, :].reshape(Q, N_KV_HEADS, n_sel * SLC_BLOCK, HEAD_DIM)
    o_slc = jnp.einsum('qgim,qgmd->qgid', p2.astype(v_sel.dtype), v_sel)
    return o_cmp.reshape(Q, N_HEADS, HEAD_DIM), o_slc.reshape(Q, N_HEADS, HEAD_DIM)


def band_mask(q_pos, k_pos):
    d = q_pos[:, None] - k_pos[None, :]
    return (d >= 0) & (d < WINDOW) & (k_pos[None, :] >= 0)


def gqa_attend(q, k, v, mask):
    B, Q = q.shape[:2]
    qg = q.reshape(B, Q, N_KV_HEADS, HEADS_PER_KV, HEAD_DIM)
    s = jnp.einsum('bqgid,bkgd->bqgik', qg, k) * ATTN_SCALE
    p = masked_softmax(s, mask[None, :, None, None, :])
    o = jnp.einsum('bqgik,bkgd->bqgid', p.astype(v.dtype), v)
    return o.reshape(B, Q, N_HEADS, HEAD_DIM)


def window_prompt(q, kv):
    B, T = q.shape[:2]
    kvp = jnp.concatenate([jnp.zeros((B, WINDOW) + kv.shape[2:], kv.dtype), kv], axis=1)

    def blk(i):
        q0 = i * WIN_QBLOCK
        qi = lax.dynamic_slice_in_dim(q, q0, WIN_QBLOCK, axis=1)
        kvi = lax.dynamic_slice_in_dim(kvp, q0, WIN_QBLOCK + WINDOW, axis=1)
        q_pos = q0 + jnp.arange(WIN_QBLOCK)
        k_pos = q0 - WINDOW + jnp.arange(WIN_QBLOCK + WINDOW)
        return gqa_attend(qi, kvi[:, :, 0], kvi[:, :, 1], band_mask(q_pos, k_pos))

    o = lax.map(blk, jnp.arange(T // WIN_QBLOCK))
    return jnp.moveaxis(o, 0, 1).reshape(B, T, N_HEADS, HEAD_DIM)


def query_side(hn, w_qg, q_norm_g):
    B, T, _ = hn.shape
    proj = hn @ w_qg
    q = rmsnorm(proj[..., :N_HEADS * HEAD_DIM].reshape(B, T, N_HEADS, HEAD_DIM), q_norm_g)
    gates = jax.nn.sigmoid(proj[..., N_HEADS * HEAD_DIM:].astype(jnp.float32)).reshape(B, T, N_BRANCH, N_HEADS)
    return q, gates


def merge_branches(gates, o_cmp, o_slc, o_win, w_o):
    f32 = jnp.float32
    o = (gates[..., 0, :, None] * o_cmp.astype(f32) + gates[..., 1, :, None] * o_slc.astype(f32)
         + gates[..., 2, :, None] * o_win.astype(f32)).astype(w_o.dtype)
    B, T = o.shape[:2]
    return o.reshape(B, T, N_HEADS * HEAD_DIM) @ w_o


def nsa_prompt(hn, kc, vc, slc_kv, win_kv, w_qg, q_norm_g, w_o):
    B, T, _ = hn.shape
    q, gates = query_side(hn, w_qg, q_norm_g)
    n_slc = T // SLC_BLOCK
    n_qb = T // SLC_QBLOCK
    g_ar = jnp.arange(N_KV_HEADS)[None, :, None, None]

    def item(args):
        b, qb = args
        q0 = qb * SLC_QBLOCK
        q_blk = lax.dynamic_slice_in_dim(q[b], q0, SLC_QBLOCK, axis=0)
        q_pos = q0 + jnp.arange(SLC_QBLOCK)
        gather = lambda kpos: slc_kv[b, kpos, :, g_ar, :]
        return nsa_cmp_slc(q_blk, q_pos, kc[b], vc[b], gather, n_slc)

    b_ids = jnp.repeat(jnp.arange(B), n_qb)
    qb_ids = jnp.tile(jnp.arange(n_qb), B)
    o_cmp, o_slc = lax.map(item, (b_ids, qb_ids))
    o_cmp = o_cmp.reshape(B, T, N_HEADS, HEAD_DIM)
    o_slc = o_slc.reshape(B, T, N_HEADS, HEAD_DIM)
    o_win = window_prompt(q, win_kv)
    return merge_branches(gates, o_cmp, o_slc, o_win, w_o)


def sample_compressed_kv(cache_cmp_kv, page_table, cmp_new, k_norm_g0, cmp_pe, cmp_w1, cmp_w2):
    def per_seq(args):
        pt_row, new_rows = args
        past = cache_cmp_kv[pt_row]
        past = past.reshape((-1,) + past.shape[2:])
        return compressed_kv(jnp.concatenate([past, new_rows.astype(past.dtype)], 0), k_norm_g0, cmp_pe, cmp_w1, cmp_w2)
    return lax.map(per_seq, (page_table, cmp_new))


def nsa_sample(hn, kc, vc, slc_new, win_all, wb, cache_slc_kv, page_table, w_qg, q_norm_g, w_o):
    B, S, _ = hn.shape
    q, gates = query_side(hn, w_qg, q_norm_g)
    q_pos = PAST_LEN + jnp.arange(S)
    n_slc = (PAST_LEN + S - 1) // SLC_BLOCK + 1
    g_ar = jnp.arange(N_KV_HEADS)[None, :, None, None]

    def per_seq(args):
        q_b, kc_b, vc_b, pt_b, new_b = args

        def gather(kpos):
            kp = jnp.minimum(kpos, PAST_LEN - 1)
            from_pool = cache_slc_kv[pt_b[kp // PAGE_SIZE], kp % PAGE_SIZE, :, g_ar, :]
            from_new = new_b[jnp.clip(kpos - PAST_LEN, 0, S - 1), :, g_ar, :].astype(from_pool.dtype)
            return jnp.where((kpos < PAST_LEN)[..., None, None], from_pool, from_new)

        return nsa_cmp_slc(q_b, q_pos, kc_b, vc_b, gather, n_slc)

    o_cmp, o_slc = lax.map(per_seq, (q, kc, vc, page_table, slc_new))
    k_pos = PAST_LEN - wb + jnp.arange(wb + S)
    o_win = gqa_attend(q, win_all[:, :, 0], win_all[:, :, 1], band_mask(q_pos, k_pos))
    return merge_branches(gates, o_cmp, o_slc, o_win, w_o)


def setup_inputs(seed: int = 0) -> dict:
    key = jax.random.key(seed)
    ks = iter(jax.random.split(key, 64))
    f32 = jnp.float32
    nrm = lambda shape, scale: jax.random.normal(next(ks), shape, f32) * scale
    n_pages = PAST_LEN // PAGE_SIZE
    n_used = DEC_BATCH * n_pages
    n_pool = n_used + max(1, n_used // 4)
    page_table = jax.random.permutation(next(ks), n_pool)[:n_used].reshape(DEC_BATCH, n_pages).astype(jnp.int32)
    inv = D_MODEL ** -0.5
    kv_width = N_BRANCH * 2 * N_KV_HEADS * HEAD_DIM
    base = jnp.array([0.0, 0.0, 0.85, 1.0, 0.0, 1.0, 0.0], f32)[None, :, None]
    spread = jnp.array([0.0, 0.3, 0.05, 0.05, 0.1, 0.02, 0.02], f32)[None, :, None]
    rw_vec = base + spread * nrm((N_A_LAYERS, 7, D_MODEL), 1.0)
    rw_vec = rw_vec.at[:, 0].set(jax.random.uniform(next(ks), (N_A_LAYERS, D_MODEL), f32, -3.0, 1.0))
    return {
        'x_prompt': nrm((BATCH, SEQ, D_MODEL), 1.0),
        'x_sample': nrm((DEC_BATCH, DEC_SEQ, D_MODEL), 1.0),
        'cache_cmp_kv': nrm((n_pool, PAGE_SIZE, 2, N_KV_HEADS, HEAD_DIM), 1.0),
        'cache_slc_kv': nrm((n_pool, PAGE_SIZE, 2, N_KV_HEADS, HEAD_DIM), 1.0),
        'state_win_kv': nrm((DEC_BATCH, min(WINDOW, PAST_LEN), 2, N_KV_HEADS, HEAD_DIM), 1.0),
        'state_wkv': nrm((N_A_LAYERS, DEC_BATCH, RW_HEADS, RW_HEAD_DIM, RW_HEAD_DIM), 0.5),
        'state_shift': nrm((N_A_LAYERS, DEC_BATCH, D_MODEL), 1.0),
        'page_table': page_table,
        'norm_g': 1.0 + nrm((DEPTH, 2, D_MODEL), 0.02),
        'rw_mix': jax.random.uniform(next(ks), (N_A_LAYERS, 6, D_MODEL), f32),
        'rw_vec': rw_vec,
        'rw_w_rkv': nrm((N_A_LAYERS, 3, D_MODEL, D_MODEL), inv),
        'rw_w_o': nrm((N_A_LAYERS, D_MODEL, D_MODEL), inv),
        'rw_decay_w1': nrm((N_A_LAYERS, D_MODEL, RW_LORA_DECAY), inv),
        'rw_decay_w2': nrm((N_A_LAYERS, RW_LORA_DECAY, D_MODEL), 0.1 * RW_LORA_DECAY ** -0.5),
        'rw_iclr_w1': nrm((N_A_LAYERS, D_MODEL, RW_LORA_ICLR), inv),
        'rw_iclr_w2': nrm((N_A_LAYERS, RW_LORA_ICLR, D_MODEL), 0.1 * RW_LORA_ICLR ** -0.5),
        'rw_gate_w1': nrm((N_A_LAYERS, D_MODEL, RW_LORA_GATE), inv),
        'rw_gate_w2': nrm((N_A_LAYERS, RW_LORA_GATE, D_MODEL), RW_LORA_GATE ** -0.5),
        'ffn_w_gate': nrm((N_DENSE, D_MODEL, D_FF), inv),
        'ffn_w_up': nrm((N_DENSE, D_MODEL, D_FF), inv),
        'ffn_w_down': nrm((N_DENSE, D_FF, D_MODEL), D_FF ** -0.5),
        'moe_router': nrm((N_MOE, D_MODEL, N_EXPERTS), inv),
        'moe_w_gate': nrm((N_MOE, N_EXPERTS, D_MODEL, D_FF), inv),
        'moe_w_up': nrm((N_MOE, N_EXPERTS, D_MODEL, D_FF), inv),
        'moe_w_down': nrm((N_MOE, N_EXPERTS, D_FF, D_MODEL), D_FF ** -0.5),
        'kv_norm_g': 1.0 + nrm((D_MODEL,), 0.02),
        'w_kv': nrm((D_MODEL, kv_width), inv),
        'k_norm_g': 1.0 + nrm((N_BRANCH, HEAD_DIM), 0.02),
        'cmp_pe': nrm((2, CMP_BLOCK, HEAD_DIM), 0.1),
        'cmp_w1': nrm((2, CMP_BLOCK, HEAD_DIM, CMP_HIDDEN), (CMP_BLOCK * HEAD_DIM) ** -0.5),
        'cmp_w2': nrm((2, CMP_HIDDEN, HEAD_DIM), CMP_HIDDEN ** -0.5),
        'w_qg': nrm((N_B_LAYERS, D_MODEL, N_HEADS * HEAD_DIM + N_BRANCH * N_HEADS), inv),
        'q_norm_g': 1.0 + nrm((N_B_LAYERS, HEAD_DIM), 0.02),
        'w_o': nrm((N_B_LAYERS, N_HEADS * HEAD_DIM, D_MODEL), (N_HEADS * HEAD_DIM) ** -0.5),
    }


def reference(x_prompt, x_sample, cache_cmp_kv, cache_slc_kv, state_win_kv, state_wkv, state_shift, page_table,
              norm_g, rw_mix, rw_vec, rw_w_rkv, rw_w_o, rw_decay_w1, rw_decay_w2, rw_iclr_w1, rw_iclr_w2,
              rw_gate_w1, rw_gate_w2, ffn_w_gate, ffn_w_up, ffn_w_down, moe_router, moe_w_gate, moe_w_up,
              moe_w_down, kv_norm_g, w_kv, k_norm_g, cmp_pe, cmp_w1, cmp_w2, w_qg, q_norm_g, w_o):
    hp, hs = x_prompt, x_sample
    B, T, _ = hp.shape
    wb = state_win_kv.shape[1]
    wkv_p, wkv_s, shift_p, shift_s = [], [], [], []
    for layer in range(DEPTH):
        if layer < N_A_LAYERS:
            rw = (rw_mix[layer], rw_vec[layer], rw_w_rkv[layer], rw_w_o[layer], rw_decay_w1[layer],
                  rw_decay_w2[layer], rw_iclr_w1[layer], rw_iclr_w2[layer], rw_gate_w1[layer], rw_gate_w2[layer])
            o, sh, st = rwkv7_time_mix(rmsnorm(hp, norm_g[layer, 0]), jnp.zeros((B, D_MODEL), hp.dtype),
                                       jnp.zeros((B, RW_HEADS, RW_HEAD_DIM, RW_HEAD_DIM), hp.dtype), *rw)
            hp = hp + o
            shift_p.append(sh)
            wkv_p.append(st)
            o, sh, st = rwkv7_time_mix(rmsnorm(hs, norm_g[layer, 0]), state_shift[layer], state_wkv[layer], *rw)
            hs = hs + o
            shift_s.append(sh)
            wkv_s.append(st)
        else:
            if layer == N_A_LAYERS:
                cmp_p, slc_p, win_p = shared_kv_rows(hp, kv_norm_g, w_kv, k_norm_g)
                kc_p, vc_p = compressed_kv(cmp_p, k_norm_g[0], cmp_pe, cmp_w1, cmp_w2)
                cmp_s, slc_s, win_s = shared_kv_rows(hs, kv_norm_g, w_kv, k_norm_g)
                kc_s, vc_s = sample_compressed_kv(cache_cmp_kv, page_table, cmp_s, k_norm_g[0], cmp_pe, cmp_w1, cmp_w2)
                win_all_s = jnp.concatenate([state_win_kv, win_s.astype(state_win_kv.dtype)], axis=1)
            j = layer - N_A_LAYERS
            hp = hp + nsa_prompt(rmsnorm(hp, norm_g[layer, 0]), kc_p, vc_p, slc_p, win_p, w_qg[j], q_norm_g[j], w_o[j])
            hs = hs + nsa_sample(rmsnorm(hs, norm_g[layer, 0]), kc_s, vc_s, slc_s, win_all_s, wb, cache_slc_kv,
                                 page_table, w_qg[j], q_norm_g[j], w_o[j])
        f = layer // 2
        if layer % 2 == 0:
            hp = hp + swiglu(rmsnorm(hp, norm_g[layer, 1]), ffn_w_gate[f], ffn_w_up[f], ffn_w_down[f])
            hs = hs + swiglu(rmsnorm(hs, norm_g[layer, 1]), ffn_w_gate[f], ffn_w_up[f], ffn_w_down[f])
        else:
            hp = hp + moe_swiglu(rmsnorm(hp, norm_g[layer, 1]), moe_router[f], moe_w_gate[f], moe_w_up[f], moe_w_down[f])
            hs = hs + moe_swiglu(rmsnorm(hs, norm_g[layer, 1]), moe_router[f], moe_w_gate[f], moe_w_up[f], moe_w_down[f])
    keep_p = min(WINDOW, T)
    return (hp, hs, cmp_p, cmp_s, slc_p, slc_s, win_p[:, T - keep_p:], win_all_s[:, win_all_s.shape[1] - wb:],
            jnp.stack(wkv_p), jnp.stack(wkv_s), jnp.stack(shift_p), jnp.stack(shift_s))
```

```python
import functools

import numpy as np
import jax
import jax.numpy as jnp
from jax import lax
from jax.experimental import pallas as pl
from jax.experimental.pallas import tpu as pltpu

F32 = jnp.float32
BF16 = jnp.bfloat16

RW_HEAD_DIM = 64
RW_GN_EPS = 64e-5
N_HEADS = 16
HEAD_DIM = 64
N_KV_HEADS = 4
HEADS_PER_KV = N_HEADS // N_KV_HEADS
KV_W = N_KV_HEADS * HEAD_DIM
N_BRANCH = 3
CMP_BLOCK = 32
CMP_STRIDE = 16
SLC_BLOCK = 64
SLC_TOP = 16
WINDOW = 512
ATTN_SCALE = HEAD_DIM ** -0.5
N_EXPERTS = 8
TOP_K = 2
RMS_EPS = 1e-6
NEG = -1e30

SUBLANES = 8
LANES = 128
VMEM_LIMIT = 56 * 1024 * 1024


def _cparams(sem):
    return pltpu.CompilerParams(dimension_semantics=sem, vmem_limit_bytes=VMEM_LIMIT)


def _dot(a, b):
    return jnp.dot(a.astype(BF16), b.astype(BF16), preferred_element_type=F32)


def _dot_nt(a, b):
    return lax.dot_general(a.astype(BF16), b.astype(BF16), (((1,), (1,)), ((), ())), preferred_element_type=F32)


def _dot_tn(a, b):
    return lax.dot_general(a.astype(BF16), b.astype(BF16), (((0,), (0,)), ((), ())), preferred_element_type=F32)


def _split2(x):
    hi = x.astype(BF16)
    lo = (x - hi.astype(F32)).astype(BF16)
    return hi, lo


def _dot_hilo(x, m):
    hi, lo = _split2(x)
    return jnp.dot(hi, m, preferred_element_type=F32) + jnp.dot(lo, m, preferred_element_type=F32)


def _dot_exact_rhs(m, x):
    hi = x.astype(BF16)
    r1 = x - hi.astype(F32)
    mid = r1.astype(BF16)
    lo = (r1 - mid.astype(F32)).astype(BF16)
    d = lambda t: jnp.dot(m, t, preferred_element_type=F32)
    return d(hi) + d(mid) + d(lo)


def _dot3(a, b):
    ah, al = _split2(a)
    bh, bl = _split2(b)
    d = lambda s, t: jnp.dot(s, t, preferred_element_type=F32)
    return d(ah, bh) + d(ah, bl) + d(al, bh)


def _rms(x, g):
    return x * lax.rsqrt(jnp.mean(x * x, -1, keepdims=True) + RMS_EPS) * g


def _silu(x):
    return x * jax.nn.sigmoid(x)


def _softplus(z):
    return jnp.maximum(z, 0.0) + jnp.log(1.0 + jnp.exp(-jnp.abs(z)))


def _block_diag_mean(n, seg):
    i = np.arange(n)
    return jnp.asarray((i[:, None] // seg == i[None, :] // seg).astype(np.float32) / seg, dtype=BF16)


def _rwkv_proj_kernel(x_ref, halo_ref, sh_ref, g_ref, mix_ref, vec_ref, wrkv_ref, dw1_ref, dw2_ref, aw1_ref, aw2_ref,
                      gw1_ref, gw2_ref, r_o, lw_o, k_o, v_o, a_o, g_o, shift_o, *, last_tile, last_row):
    i = pl.program_id(1)
    g = g_ref[...]
    xn = _rms(x_ref[0], g)
    hn = _rms(halo_ref[0, SUBLANES - 1:SUBLANES, :], g)
    prev_last = jnp.where(i == 0, sh_ref[0], hn)
    row = lax.broadcasted_iota(jnp.int32, xn.shape, 0)
    prev = jnp.where(row == 0, prev_last, pltpu.roll(xn, 1, 0))
    xx = prev - xn
    mixed = lambda j: (xn + xx * mix_ref[j:j + 1, :]).astype(BF16)
    r_o[0] = jnp.dot(mixed(0), wrkv_ref[0], preferred_element_type=F32)
    k_o[0] = jnp.dot(mixed(2), wrkv_ref[1], preferred_element_type=F32)
    v_o[0] = jnp.dot(mixed(3), wrkv_ref[2], preferred_element_type=F32)
    w0 = vec_ref[0:1, :]
    a0 = vec_ref[1:2, :]
    dl = _dot(jnp.tanh(jnp.dot(mixed(1), dw1_ref[...], preferred_element_type=F32)), dw2_ref[...])
    log_w = -_softplus(-(w0 + dl)) - 0.5
    lw_o[0] = -jnp.exp(log_w)
    al = _dot(jnp.dot(mixed(4), aw1_ref[...], preferred_element_type=F32), aw2_ref[...])
    a_o[0] = jax.nn.sigmoid(a0 + al)
    gl = jax.nn.sigmoid(jnp.dot(mixed(5), gw1_ref[...], preferred_element_type=F32))
    g_o[0] = _dot(gl, gw2_ref[...])

    @pl.when(i == last_tile)
    def _():
        shift_o[0] = xn[last_row:last_row + 1, :]


def _rwkv_proj(x, shift0, g, mix, vec, wrkv, dw1, dw2, aw1, aw2, gw1, gw2, *, t_real, tm):
    B, T, D = x.shape
    nt = T // tm
    row_spec = pl.BlockSpec((1, tm, D), lambda b, i: (b, i, 0))
    halo_spec = pl.BlockSpec((1, SUBLANES, D), lambda b, i: (b, jnp.maximum(i * (tm // SUBLANES) - 1, 0), 0))
    vec_spec = pl.BlockSpec((1, 1, D), lambda b, i: (b, 0, 0))
    full = lambda a: pl.BlockSpec(a.shape, lambda b, i: (0,) * a.ndim)
    consts = (g, mix, vec, wrkv, dw1, dw2, aw1, aw2, gw1, gw2)
    out_sd = jax.ShapeDtypeStruct((B, T, D), F32)
    kern = functools.partial(_rwkv_proj_kernel, last_tile=(t_real - 1) // tm, last_row=(t_real - 1) % tm)
    return pl.pallas_call(
        kern, grid=(B, nt),
        in_specs=[row_spec, halo_spec, vec_spec] + [full(c) for c in consts],
        out_specs=[row_spec] * 6 + [vec_spec],
        out_shape=[out_sd] * 6 + [jax.ShapeDtypeStruct((B, 1, D), F32)],
        compiler_params=_cparams(("parallel", "arbitrary")), name="rwkv_proj",
    )(x, x, shift0, *consts)


def _wkv_kernel(r_ref, lw_ref, k_ref, v_ref, a_ref, vec_ref, h0_ref, y_o, hT_o, h_sc, *, n_heads, n_sq):
    c = pl.program_id(1)
    C = r_ref.shape[1]
    dh = RW_HEAD_DIM

    @pl.when(c == 0)
    def _():
        h_sc[...] = h0_ref[0]

    ti = lax.broadcasted_iota(jnp.int32, (C, C), 0)
    si = lax.broadcasted_iota(jnp.int32, (C, C), 1)
    tri_incl = (si <= ti)
    tri_strict = (si < ti)
    cum_all = _dot_exact_rhs(jnp.where(tri_incl, 1.0, 0.0).astype(BF16), lw_ref[0])
    for h in range(n_heads):
        sl = slice(h * dh, (h + 1) * dh)
        r = r_ref[0, :, sl]
        lw = lw_ref[0, :, sl]
        k = k_ref[0, :, sl]
        v = v_ref[0, :, sl]
        a = a_ref[0, :, sl]
        k_k = vec_ref[2:3, sl]
        k_a = vec_ref[3:4, sl]
        r_k = vec_ref[4:5, sl]
        ln_w = vec_ref[5:6, sl]
        ln_b = vec_ref[6:7, sl]
        cum = cum_all[:, sl]
        kk = k * k_k
        kk = kk * lax.rsqrt(jnp.maximum(jnp.sum(kk * kk, -1, keepdims=True), 1e-24))
        k2 = k * (1.0 + (a - 1.0) * k_a)
        e_pos = jnp.exp(cum)
        e_neg = jnp.exp(-cum)
        rt = r * e_pos
        at = -kk * jnp.exp(cum - lw)
        bt = kk * a * e_neg
        kt = k2 * e_neg
        s0 = h_sc[h]
        ar = jnp.concatenate([at, rt], axis=0)
        bk = jnp.concatenate([bt, kt], axis=0)
        big = _dot_nt(ar, bk)
        l_ab = jnp.where(tri_strict, big[:C, :C], 0.0)
        l_ak = jnp.where(tri_strict, big[:C, C:], 0.0)
        m_rb = jnp.where(tri_incl, big[C:, :C], 0.0)
        m_rk = jnp.where(tri_incl, big[C:, C:], 0.0)
        tinv = jnp.where(si == ti, 1.0, l_ab)
        p = l_ab
        for _ in range(n_sq):
            p = _dot3(p, p)
            tinv = tinv + _dot3(tinv, p)
        arh = _dot_nt(ar, s0)
        u = _dot3(tinv, arh[:C] + _dot(l_ak, v))
        y = arh[C:] + _dot(m_rb, u) + _dot(m_rk, v)
        w_end = e_pos[C - 1:C, :]
        h_sc[h] = s0 * w_end + _dot_tn(u, bt * w_end) + _dot_tn(v, kt * w_end)
        mu = jnp.mean(y, -1, keepdims=True)
        var = jnp.mean(jnp.square(y - mu), -1, keepdims=True)
        yn = (y - mu) * lax.rsqrt(var + RW_GN_EPS) * ln_w + ln_b
        bonus = jnp.sum(r * k2 * r_k, -1, keepdims=True) * v
        y_o[0, :, sl] = yn + bonus

    @pl.when(c == pl.num_programs(1) - 1)
    def _():
        hT_o[0] = h_sc[...]


def _wkv(r, lw, k, v, a, vec, h0, *, chunk):
    B, T, D = r.shape
    H = D // RW_HEAD_DIM
    n_sq = max(int(np.ceil(np.log2(chunk))) - 1, 0)
    row_spec = pl.BlockSpec((1, chunk, D), lambda b, c: (b, c, 0))
    st_spec = pl.BlockSpec((1, H, RW_HEAD_DIM, RW_HEAD_DIM), lambda b, c: (b, 0, 0, 0))
    kern = functools.partial(_wkv_kernel, n_heads=H, n_sq=n_sq)
    return pl.pallas_call(
        kern, grid=(B, T // chunk),
        in_specs=[row_spec] * 5 + [pl.BlockSpec(vec.shape, lambda b, c: (0, 0)), st_spec],
        out_specs=[row_spec, st_spec],
        out_shape=[jax.ShapeDtypeStruct((B, T, D), F32), jax.ShapeDtypeStruct(h0.shape, F32)],
        scratch_shapes=[pltpu.VMEM((H, RW_HEAD_DIM, RW_HEAD_DIM), F32)],
        compiler_params=_cparams(("parallel", "arbitrary")), name="wkv_chunk",
    )(r, lw, k, v, a, vec, h0)


def _row_call(kernel, rows, consts, out_widths, out_dtypes, tm, name):
    n = rows[0].shape[0]
    row_spec = lambda w: pl.BlockSpec((tm, w), lambda i: (i, 0))
    full = lambda a: pl.BlockSpec(a.shape, lambda i: (0,) * a.ndim)
    return pl.pallas_call(
        kernel, grid=(n // tm,),
        in_specs=[row_spec(a.shape[1]) for a in rows] + [full(c) for c in consts],
        out_specs=[row_spec(w) for w in out_widths],
        out_shape=[jax.ShapeDtypeStruct((n, w), dt) for w, dt in zip(out_widths, out_dtypes)],
        compiler_params=_cparams(("parallel",)), name=name,
    )(*rows, *consts)


def _rwkv_out_kernel(x_ref, y_ref, g_ref, wo_ref, o_ref):
    o_ref[...] = x_ref[...] + _dot(y_ref[...] * g_ref[...], wo_ref[...])


def _kv_proj_kernel(h_ref, g_ref, wkv_ref, kg_ref, bd_ref, cmp_o, slc_o, win_o):
    kv = _dot(_rms(h_ref[...], g_ref[...]), wkv_ref[...])
    cmp_o[...] = kv[:, :2 * KV_W]
    for br, out in ((1, slc_o), (2, win_o)):
        kraw = kv[:, br * 2 * KV_W: br * 2 * KV_W + KV_W]
        ms = _dot_hilo(kraw * kraw, bd_ref[...])
        out[:, :KV_W] = kraw * lax.rsqrt(ms + RMS_EPS) * kg_ref[br:br + 1, :]
        out[:, KV_W:] = kv[:, br * 2 * KV_W + KV_W: (br + 1) * 2 * KV_W]


def _q_proj_kernel(h_ref, g_ref, wq_ref, wg_ref, qg_ref, bd_ref, q_o, gate_o):
    hn = _rms(h_ref[...], g_ref[...]).astype(BF16)
    q = jnp.dot(hn, wq_ref[...], preferred_element_type=F32)
    ms = _dot_hilo(q * q, bd_ref[...])
    q_o[...] = (q * lax.rsqrt(ms + RMS_EPS) * qg_ref[...] * ATTN_SCALE).astype(BF16)
    gate_o[...] = jax.nn.sigmoid(jnp.dot(hn, wg_ref[...], preferred_element_type=F32))


def _merge_kernel(h_ref, gate_ref, oc_ref, os_ref, ow_ref, eg_ref, wo_ref, o_ref):
    D = h_ref.shape[1]
    ge = _dot_hilo(gate_ref[...], eg_ref[...])
    o = ge[:, :D] * oc_ref[...] + ge[:, D:2 * D] * os_ref[...] + ge[:, 2 * D:] * ow_ref[...]
    o_ref[...] = h_ref[...] + _dot(o, wo_ref[...])


def _router_kernel(h_ref, g_ref, wr_ref, xn_o, logit_o):
    xn = _rms(h_ref[...], g_ref[...])
    xn_o[...] = xn.astype(BF16)
    logit_o[...] = _dot3(xn, wr_ref[...])


def _ffn_kernel(x_ref, g_ref, wg_ref, wu_ref, wd_ref, o_ref, xn_sc, acc_sc):
    f = pl.program_id(1)

    @pl.when(f == 0)
    def _():
        xn_sc[...] = _rms(x_ref[...], g_ref[...]).astype(BF16)
        acc_sc[...] = jnp.zeros_like(acc_sc)

    xn = xn_sc[...]
    hid = _silu(jnp.dot(xn, wg_ref[...], preferred_element_type=F32)) * jnp.dot(xn, wu_ref[...], preferred_element_type=F32)
    acc_sc[...] += _dot(hid, wd_ref[...])

    @pl.when(f == pl.num_programs(1) - 1)
    def _():
        o_ref[...] = x_ref[...] + acc_sc[...]


def _ffn(x, g, wg, wu, wd, *, tm, tf):
    n, D = x.shape
    dff = wg.shape[1]
    return pl.pallas_call(
        _ffn_kernel, grid=(n // tm, dff // tf),
        in_specs=[pl.BlockSpec((tm, D), lambda i, f: (i, 0)), pl.BlockSpec((1, D), lambda i, f: (0, 0)),
                  pl.BlockSpec((D, tf), lambda i, f: (0, f)), pl.BlockSpec((D, tf), lambda i, f: (0, f)),
                  pl.BlockSpec((tf, D), lambda i, f: (f, 0))],
        out_specs=pl.BlockSpec((tm, D), lambda i, f: (i, 0)),
        out_shape=jax.ShapeDtypeStruct((n, D), F32),
        scratch_shapes=[pltpu.VMEM((tm, D), BF16), pltpu.VMEM((tm, D), F32)],
        compiler_params=_cparams(("parallel", "arbitrary")), name="ffn_dense",
    )(x, g, wg, wu, wd)


def _moe_ffn_kernel(be_ref, nb_ref, x_ref, wg_ref, wu_ref, wd_ref, o_ref, acc_sc):
    i = pl.program_id(0)
    f = pl.program_id(1)

    @pl.when(i < nb_ref[0])
    def _():
        @pl.when(f == 0)
        def _():
            acc_sc[...] = jnp.zeros_like(acc_sc)

        x = x_ref[...]
        hid = _silu(jnp.dot(x, wg_ref[0], preferred_element_type=F32)) * jnp.dot(x, wu_ref[0], preferred_element_type=F32)
        acc_sc[...] += _dot(hid, wd_ref[0])

        @pl.when(f == pl.num_programs(1) - 1)
        def _():
            o_ref[...] = acc_sc[...]

    @pl.when((i >= nb_ref[0]) & (f == pl.num_programs(1) - 1))
    def _():
        o_ref[...] = jnp.zeros_like(o_ref)


def _moe_ffn(block_e, n_used, xs, wg, wu, wd, *, tm, tf):
    cap, D = xs.shape
    dff = wg.shape[2]
    w_idx = lambda i, f, be, nb: jnp.where(i < nb[0], f, dff // tf - 1)
    grid_spec = pltpu.PrefetchScalarGridSpec(
        num_scalar_prefetch=2, grid=(cap // tm, dff // tf),
        in_specs=[pl.BlockSpec((tm, D), lambda i, f, be, nb: (i, 0)),
                  pl.BlockSpec((1, D, tf), lambda i, f, be, nb: (be[i], 0, w_idx(i, f, be, nb))),
                  pl.BlockSpec((1, D, tf), lambda i, f, be, nb: (be[i], 0, w_idx(i, f, be, nb))),
                  pl.BlockSpec((1, tf, D), lambda i, f, be, nb: (be[i], w_idx(i, f, be, nb), 0))],
        out_specs=pl.BlockSpec((tm, D), lambda i, f, be, nb: (i, 0)),
        scratch_shapes=[pltpu.VMEM((tm, D), F32)])
    return pl.pallas_call(
        _moe_ffn_kernel, grid_spec=grid_spec, out_shape=jax.ShapeDtypeStruct((cap, D), F32),
        compiler_params=_cparams(("arbitrary", "arbitrary")), name="moe_ffn",
    )(block_e, n_used, xs, wg, wu, wd)


def _compress_kernel(x_ref, pe_ref, w1_ref, w2_ref, kg_ref, o_ref):
    c = pl.program_id(1)
    x = x_ref[0, 0, 0]
    R = x.shape[0]
    p0 = _dot(x + pe_ref[0, 0], w1_ref[0, 0])
    p1 = _dot(x + pe_ref[0, 1], w1_ref[0, 1])
    pre = p0 + pltpu.roll(p1, R - 1, 0)
    out = _dot(_silu(pre), w2_ref[0])
    o_ref[0, 0, 0] = jnp.where(c == 0, _rms(out, kg_ref[...]), out)


def _compress(xsub, pe, w1, w2, kg):
    S, _, G, R, W = xsub.shape
    return pl.pallas_call(
        _compress_kernel, grid=(S, 2, G),
        in_specs=[pl.BlockSpec((1, 1, 1, R, W), lambda s, c, g: (s, c, g, 0, 0)),
                  pl.BlockSpec((1, 2, 1, W), lambda s, c, g: (c, 0, 0, 0)),
                  pl.BlockSpec((1, 2, W, HEAD_DIM), lambda s, c, g: (c, 0, 0, 0)),
                  pl.BlockSpec((1, HEAD_DIM, HEAD_DIM), lambda s, c, g: (c, 0, 0)),
                  pl.BlockSpec((1, HEAD_DIM), lambda s, c, g: (0, 0))],
        out_specs=pl.BlockSpec((1, 1, 1, R, HEAD_DIM), lambda s, c, g: (s, c, g, 0, 0)),
        out_shape=jax.ShapeDtypeStruct((S, 2, G, R, HEAD_DIM), F32),
        compiler_params=_cparams(("parallel", "arbitrary", "arbitrary")), name="kv_compress",
    )(xsub, pe, w1, w2, kg)


def _stack_heads(q, g, tq):
    return jnp.concatenate([q[:, (HEADS_PER_KV * g + i) * HEAD_DIM:(HEADS_PER_KV * g + i + 1) * HEAD_DIM]
                            for i in range(HEADS_PER_KV)], axis=0)


def _cmp_attn_kernel(q_ref, kc_ref, vc_ref, m_ref, o_o, imp_o, *, n_cmp, pos_base, per_tile):
    tq = q_ref.shape[1]
    ncp = kc_ref.shape[1]
    nsp = m_ref.shape[1]
    q0 = pos_base + (pl.program_id(1) * tq if per_tile else 0)
    rows = HEADS_PER_KV * tq
    q_pos = q0 + lax.broadcasted_iota(jnp.int32, (rows, ncp), 0) % tq
    n_id = lax.broadcasted_iota(jnp.int32, (rows, ncp), 1)
    valid = (n_id * CMP_STRIDE + CMP_BLOCK - 1 <= q_pos) & (n_id < n_cmp)
    q = q_ref[0]
    for g in range(N_KV_HEADS):
        qs = _stack_heads(q, g, tq)
        s = _dot_nt(qs, kc_ref[0, :, g * HEAD_DIM:(g + 1) * HEAD_DIM])
        m = jnp.max(jnp.where(valid, s, NEG), -1, keepdims=True)
        e = jnp.where(valid, jnp.exp(s - m), 0.0)
        p = e / jnp.maximum(jnp.sum(e, -1, keepdims=True), 1e-30)
        o = _dot(p, vc_ref[0, :, g * HEAD_DIM:(g + 1) * HEAD_DIM])
        psum = p[0:tq]
        for i in range(HEADS_PER_KV):
            col = (HEADS_PER_KV * g + i) * HEAD_DIM
            o_o[0, :, col:col + HEAD_DIM] = o[i * tq:(i + 1) * tq]
            if i:
                psum = psum + p[i * tq:(i + 1) * tq]
        imp_o[0, :, g * nsp:(g + 1) * nsp] = _dot_hilo(psum, m_ref[...])


def _cmp_attn(q, kc, vc, m, *, n_cmp, pos_base, per_tile, tq):
    B, T, D = q.shape
    ncp, nsp = m.shape
    kv_spec = pl.BlockSpec((1, ncp, KV_W), lambda b, i: (b, 0, 0))
    kern = functools.partial(_cmp_attn_kernel, n_cmp=n_cmp, pos_base=pos_base, per_tile=per_tile)
    return pl.pallas_call(
        kern, grid=(B, T // tq),
        in_specs=[pl.BlockSpec((1, tq, D), lambda b, i: (b, i, 0)), kv_spec, kv_spec,
                  pl.BlockSpec((ncp, nsp), lambda b, i: (0, 0))],
        out_specs=[pl.BlockSpec((1, tq, D), lambda b, i: (b, i, 0)),
                   pl.BlockSpec((1, tq, N_KV_HEADS * nsp), lambda b, i: (b, i, 0))],
        out_shape=[jax.ShapeDtypeStruct((B, T, D), F32), jax.ShapeDtypeStruct((B, T, N_KV_HEADS * nsp), F32)],
        compiler_params=_cparams(("parallel", "arbitrary")), name="cmp_attn",
    )(q, kc, vc, m)


def _flash(qs, kv_ref, g, kt_lo, kt_hi, tk, mask_fn):
    rows = qs.shape[0]

    def body(kt, carry):
        m, l, acc = carry
        k0 = pl.multiple_of(kt * tk, tk)
        kblk = kv_ref[0, pl.ds(k0, tk), g * HEAD_DIM:(g + 1) * HEAD_DIM]
        vblk = kv_ref[0, pl.ds(k0, tk), KV_W + g * HEAD_DIM:KV_W + (g + 1) * HEAD_DIM]
        s = _dot_nt(qs, kblk)
        mask = mask_fn(k0)
        m_new = jnp.maximum(m, jnp.max(jnp.where(mask, s, NEG), -1, keepdims=True))
        e = jnp.where(mask, jnp.exp(s - m_new), 0.0)
        alpha = jnp.exp(m - m_new)
        return m_new, alpha * l + jnp.sum(e, -1, keepdims=True), alpha * acc + _dot(e, vblk)

    init = (jnp.full((rows, 1), NEG, F32), jnp.zeros((rows, 1), F32), jnp.zeros((rows, HEAD_DIM), F32))
    _, l, acc = lax.fori_loop(kt_lo, kt_hi, body, init)
    return acc / jnp.maximum(l, 1e-30)


def _slc_win_attn_kernel(q_ref, sel_ref, slc_ref, win_ref, os_o, ow_o, *, tk):
    tq = q_ref.shape[1]
    nsp = sel_ref.shape[2] // N_KV_HEADS
    q0 = pl.program_id(1) * tq
    rows = HEADS_PER_KV * tq
    q_pos = q0 + lax.broadcasted_iota(jnp.int32, (rows, tk), 0) % tq
    k_off = lax.broadcasted_iota(jnp.int32, (rows, tk), 1)
    blk_j = lax.broadcasted_iota(jnp.int32, (nsp, tk), 0)
    blk_k = lax.broadcasted_iota(jnp.int32, (nsp, tk), 1)
    q = q_ref[0]
    kt_hi = (q0 + tq + tk - 1) // tk
    kt_lo_win = jnp.maximum(q0 - WINDOW + 1, 0) // tk
    for g in range(N_KV_HEADS):
        qs = _stack_heads(q, g, tq)
        sel_g = sel_ref[0, :, g * nsp:(g + 1) * nsp]

        def slc_mask(k0, sel_g=sel_g):
            expand = ((blk_k + k0) // SLC_BLOCK == blk_j).astype(BF16)
            chosen = jnp.dot(sel_g, expand, preferred_element_type=F32)
            chosen = jnp.concatenate([chosen] * HEADS_PER_KV, axis=0)
            return (chosen > 0.5) & (k_off + k0 <= q_pos)

        def win_mask(k0):
            d = q_pos - (k_off + k0)
            return (d >= 0) & (d < WINDOW)

        o_s = _flash(qs, slc_ref, g, 0, kt_hi, tk, slc_mask)
        o_w = _flash(qs, win_ref, g, kt_lo_win, kt_hi, tk, win_mask)
        for i in range(HEADS_PER_KV):
            col = (HEADS_PER_KV * g + i) * HEAD_DIM
            os_o[0, :, col:col + HEAD_DIM] = o_s[i * tq:(i + 1) * tq]
            ow_o[0, :, col:col + HEAD_DIM] = o_w[i * tq:(i + 1) * tq]


def _slc_win_attn(q, sel, slc_kv, win_kv, *, tq, tk):
    B, T, D = q.shape
    kv_spec = pl.BlockSpec((1, T, 2 * KV_W), lambda b, i: (b, 0, 0))
    row = lambda w: pl.BlockSpec((1, tq, w), lambda b, i: (b, i, 0))
    return pl.pallas_call(
        functools.partial(_slc_win_attn_kernel, tk=tk), grid=(B, T // tq),
        in_specs=[row(D), row(sel.shape[2]), kv_spec, kv_spec],
        out_specs=[row(D), row(D)],
        out_shape=[jax.ShapeDtypeStruct((B, T, D), F32)] * 2,
        compiler_params=_cparams(("parallel", "arbitrary")), name="slc_win_attn",
    )(q, sel, slc_kv, win_kv)


def _sample_attn_kernel(q_ref, kvs_ref, kpos_ref, win_ref, os_o, ow_o, *, n_q, pos_base, win_base, n_win):
    tq = q_ref.shape[1]
    rows = HEADS_PER_KV * tq
    nk = kvs_ref.shape[3]
    nwp = win_ref.shape[1]
    q = q_ref[0]
    row_q = lax.broadcasted_iota(jnp.int32, (rows, 1), 0) % tq
    w_id = lax.broadcasted_iota(jnp.int32, (rows, nwp), 1)
    d_win = (pos_base + row_q) - (win_base + w_id)
    win_mask = (d_win >= 0) & (d_win < WINDOW) & (win_base + w_id >= 0) & (w_id < n_win)

    def attend(s, mask, v):
        m = jnp.max(jnp.where(mask, s, NEG), -1, keepdims=True)
        e = jnp.where(mask, jnp.exp(s - m), 0.0)
        p = e / jnp.maximum(jnp.sum(e, -1, keepdims=True), 1e-30)
        return _dot(p, v)

    for g in range(N_KV_HEADS):
        qs = _stack_heads(q, g, tq)
        o_s = jnp.zeros((rows, HEAD_DIM), F32)
        for qi in range(n_q):
            kv = kvs_ref[0, qi, g]
            kpos = kpos_ref[0, qi * N_KV_HEADS + g:qi * N_KV_HEADS + g + 1, :]
            o_qi = attend(_dot_nt(qs, kv[:, :HEAD_DIM]), jnp.broadcast_to(kpos <= pos_base + qi, (rows, nk)),
                          kv[:, HEAD_DIM:])
            o_s = jnp.where(row_q == qi, o_qi, o_s)
        o_w = attend(_dot_nt(qs, win_ref[0, :, g * HEAD_DIM:(g + 1) * HEAD_DIM]), win_mask,
                     win_ref[0, :, KV_W + g * HEAD_DIM:KV_W + (g + 1) * HEAD_DIM])
        for i in range(HEADS_PER_KV):
            col = (HEADS_PER_KV * g + i) * HEAD_DIM
            os_o[0, :, col:col + HEAD_DIM] = o_s[i * tq:(i + 1) * tq]
            ow_o[0, :, col:col + HEAD_DIM] = o_w[i * tq:(i + 1) * tq]


def _sample_attn(q, kv_sel, kpos, win_all, *, n_q, pos_base, win_base, n_win):
    B, tq, D = q.shape
    kern = functools.partial(_sample_attn_kernel, n_q=n_q, pos_base=pos_base, win_base=win_base, n_win=n_win)
    blk = lambda a: pl.BlockSpec((1,) + a.shape[1:], lambda b: (b,) + (0,) * (a.ndim - 1))
    return pl.pallas_call(
        kern, grid=(B,),
        in_specs=[blk(q), blk(kv_sel), blk(kpos), blk(win_all)],
        out_specs=[blk(q), blk(q)],
        out_shape=[jax.ShapeDtypeStruct((B, tq, D), F32)] * 2,
        compiler_params=_cparams(("parallel",)), name="sample_slc_win_attn",
    )(q, kv_sel, kpos, win_all)


def _pick_tile(n, cap=512):
    return next(t for t in (512, 256, 128, 64, 32, 16, 8) if t <= cap and n % t == 0)


def _pad_rows(x, t_pad):
    return jnp.pad(x, ((0, 0), (0, t_pad - x.shape[1]), (0, 0)))


def _rwkv_layer(x, shift0, wkv0, p, *, tm, chunk):
    B, T, D = x.shape
    t_pad = -(-T // tm) * tm
    xp = _pad_rows(x, t_pad)
    r, lw, k, v, a, g, shift = _rwkv_proj(xp, shift0[:, None, :], p["g"], p["mix"], p["vec"], p["wrkv"], p["dw1"],
                                          p["dw2"], p["aw1"], p["aw2"], p["gw1"], p["gw2"], t_real=T, tm=tm)
    if t_pad != T:
        live = (jnp.arange(t_pad) < T)[None, :, None]
        r, lw, k, v, a = (jnp.where(live, t, 0.0) for t in (r, lw, k, v, a))
    y, hT = _wkv(r, lw, k, v, a, p["vec"], wkv0, chunk=chunk)
    n = B * t_pad
    out = _row_call(_rwkv_out_kernel, [xp.reshape(n, D), y.reshape(n, D), g.reshape(n, D)], [p["wo"]], [D], [F32],
                    _pick_tile(n), "rwkv_out")[0]
    return out.reshape(B, t_pad, D)[:, :T], shift[:, 0], hT


def _sub_blocks(rows):
    S, L = rows.shape[:2]
    n_sub = L // CMP_STRIDE
    r = rows[:, :n_sub * CMP_STRIDE].reshape(S, n_sub, CMP_STRIDE, 2, N_KV_HEADS, HEAD_DIM)
    r = jnp.transpose(r, (0, 3, 4, 1, 2, 5)).reshape(S, 2, N_KV_HEADS, n_sub, CMP_STRIDE * HEAD_DIM)
    r_pad = -(-n_sub // SUBLANES) * SUBLANES
    return jnp.pad(r, ((0, 0),) * 3 + ((0, r_pad - n_sub), (0, 0))), n_sub - CMP_BLOCK // CMP_STRIDE + 1


def _compressed_kv(rows, p):
    xsub, n_cmp = _sub_blocks(rows)
    out = _compress(xsub, p["cmp_pe"], p["cmp_w1"], p["cmp_w2"], p["k_norm_g"][0:1])
    S, _, G, R, dh = out.shape
    out = jnp.transpose(out, (1, 0, 3, 2, 4)).reshape(2, S, R, G * dh).astype(BF16)
    return out[0], out[1], n_cmp


def _overlap_matrix(ncp, n_cmp, n_slc, nsp):
    c0 = np.arange(ncp)[:, None] * CMP_STRIDE
    s0 = np.arange(nsp)[None, :] * SLC_BLOCK
    m = (c0 < s0 + SLC_BLOCK) & (c0 + CMP_BLOCK > s0) & (np.arange(ncp)[:, None] < n_cmp) & (np.arange(nsp)[None, :] < n_slc)
    return jnp.asarray(m.astype(np.float32), dtype=BF16)


def _select_blocks(imp, q_pos, n_slc):
    blk = jnp.arange(n_slc)[None, None, None, :]
    cur = (q_pos // SLC_BLOCK)[None, :, None, None]
    forced = (blk == 0) | (blk == cur) | (blk == cur - 1)
    score = jnp.where(blk <= cur, jnp.where(forced, jnp.inf, imp), -jnp.inf)
    return lax.top_k(score, min(SLC_TOP, n_slc))[1]


def _query_side(h, p, tm):
    n, D = h.shape
    return _row_call(_q_proj_kernel, [h], [p["g_attn"], p["wq"], p["wgate"], p["q_norm_g"], p["bd_q"]],
                     [D, LANES], [BF16, F32], tm, "q_proj")


def _merge(h, gates, o_cmp, o_slc, o_win, p, tm):
    D = h.shape[1]
    return _row_call(_merge_kernel, [h, gates, o_cmp, o_slc, o_win], [p["gate_expand"], p["w_o"]], [D], [F32], tm,
                     "nsa_merge")[0]


def _moe(h, p, *, tm_route, tm, tf):
    n, D = h.shape
    xn, logits = _row_call(_router_kernel, [h], [p["g_moe"], p["w_router"]], [D, LANES], [BF16, F32], tm_route, "moe_router")
    top_v, top_e = lax.top_k(logits[:, :N_EXPERTS], TOP_K)
    gates = jax.nn.softmax(top_v, -1)
    n_pairs = n * TOP_K
    flat_e = top_e.reshape(n_pairs)
    order = jnp.argsort(flat_e)
    se = flat_e[order]
    counts = jnp.zeros((N_EXPERTS,), jnp.int32).at[flat_e].add(1)
    padded = (counts + tm - 1) // tm * tm
    pad_end = jnp.cumsum(padded)
    start = jnp.cumsum(counts) - counts
    dest_sorted = (pad_end - padded)[se] + jnp.arange(n_pairs, dtype=jnp.int32) - start[se]
    n_blocks = -(-n_pairs // tm) + N_EXPERTS
    cap = n_blocks * tm
    buf_tok = jnp.full((cap,), n, jnp.int32).at[dest_sorted].set((order // TOP_K).astype(jnp.int32))
    dest = jnp.zeros((n_pairs,), jnp.int32).at[order].set(dest_sorted).reshape(n, TOP_K)
    block_e = jnp.minimum(jnp.searchsorted(pad_end, jnp.arange(n_blocks) * tm, side="right"), N_EXPERTS - 1).astype(jnp.int32)
    n_used = (pad_end[-1] // tm).astype(jnp.int32).reshape(1)
    xs = jnp.concatenate([xn, jnp.zeros((1, D), BF16)], 0)[buf_tok]
    ys = _moe_ffn(block_e, n_used, xs, p["moe_wg"], p["moe_wu"], p["moe_wd"], tm=tm, tf=tf)
    y = ys[dest[:, 0]] * gates[:, 0:1] + ys[dest[:, 1]] * gates[:, 1:2]
    return h + y


def kernel(x_prompt, x_sample, cache_cmp_kv, cache_slc_kv, state_win_kv, state_wkv, state_shift, page_table, norm_g, rw_mix, rw_vec, rw_w_rkv, rw_w_o, rw_decay_w1, rw_decay_w2, rw_iclr_w1, rw_iclr_w2, rw_gate_w1, rw_gate_w2, ffn_w_gate, ffn_w_up, ffn_w_down, moe_router, moe_w_gate, moe_w_up, moe_w_down, kv_norm_g, w_kv, k_norm_g, cmp_pe, cmp_w1, cmp_w2, w_qg, q_norm_g, w_o):
    B, T, D = x_prompt.shape
    SB, S, _ = x_sample.shape
    assert norm_g.shape[0] == 2 and rw_mix.shape[0] == 1 and w_qg.shape[0] == 1, "one RWKV-7 layer + one NSA layer"
    assert D == N_HEADS * HEAD_DIM and T % 256 == 0 and S <= SUBLANES
    page = cache_cmp_kv.shape[1]
    past = page_table.shape[1] * page
    wb = state_win_kv.shape[1]
    assert past % SLC_BLOCK == 0 and page % SLC_BLOCK == 0 and S <= SLC_BLOCK
    bf = lambda w: w.astype(BF16)
    pad8 = lambda m: jnp.pad(m, ((0, SUBLANES - m.shape[0]), (0, 0)))
    dff = ffn_w_gate.shape[2]
    tf = dff // 2 if (dff // 2) % LANES == 0 else dff

    rw = dict(g=norm_g[0, 0][None], mix=pad8(rw_mix[0]), vec=pad8(rw_vec[0]), wrkv=bf(rw_w_rkv[0]), wo=bf(rw_w_o[0]),
              dw1=bf(rw_decay_w1[0]), dw2=bf(rw_decay_w2[0]), aw1=bf(rw_iclr_w1[0]), aw2=bf(rw_iclr_w2[0]),
              gw1=bf(rw_gate_w1[0]), gw2=bf(rw_gate_w2[0]))
    H = D // RW_HEAD_DIM
    hp, shift_p, wkv_p = _rwkv_layer(x_prompt, jnp.zeros((B, D), F32), jnp.zeros((B, H, RW_HEAD_DIM, RW_HEAD_DIM), F32),
                                     rw, tm=256, chunk=64)
    hs, shift_s, wkv_s = _rwkv_layer(x_sample, state_shift[0], state_wkv[0], rw, tm=SUBLANES, chunk=SUBLANES)
    ffn = (norm_g[0, 1][None], bf(ffn_w_gate[0]), bf(ffn_w_up[0]), bf(ffn_w_down[0]))
    n_p, n_s = B * T, SB * S
    hp = _ffn(hp.reshape(n_p, D), *ffn, tm=512, tf=tf)
    hs = _ffn(hs.reshape(n_s, D), *ffn, tm=n_s, tf=tf)

    kvp = [kv_norm_g[None], bf(w_kv), jnp.tile(k_norm_g, (1, N_KV_HEADS)), _block_diag_mean(KV_W, HEAD_DIM)]
    kv_rows = lambda h, tm: _row_call(_kv_proj_kernel, [h], kvp, [2 * KV_W] * 3, [F32] * 3, tm, "kv_proj")
    as_rows = lambda t, b: t.reshape(b, -1, 2, N_KV_HEADS, HEAD_DIM)
    cmp_p, slc_p, win_p = (as_rows(t, B) for t in kv_rows(hp, 512))
    cmp_s, slc_s, win_s = (as_rows(t, SB) for t in kv_rows(hs, n_s))
    win_all_s = jnp.concatenate([state_win_kv, win_s], axis=1)

    cp = dict(cmp_pe=cmp_pe.reshape(2, CMP_BLOCK // CMP_STRIDE, 1, CMP_STRIDE * HEAD_DIM),
              cmp_w1=bf(cmp_w1.reshape(2, CMP_BLOCK // CMP_STRIDE, CMP_STRIDE * HEAD_DIM, HEAD_DIM)), cmp_w2=bf(cmp_w2),
              k_norm_g=k_norm_g)
    kc_p, vc_p, ncmp_p = _compressed_kv(cmp_p, cp)
    past_rows = cache_cmp_kv[page_table].reshape(SB, past, 2, N_KV_HEADS, HEAD_DIM)
    kc_s, vc_s, ncmp_s = _compressed_kv(jnp.concatenate([past_rows, cmp_s], axis=1), cp)

    nsa = dict(g_attn=norm_g[1, 0][None], wq=bf(w_qg[0, :, :D]),
               wgate=bf(jnp.pad(w_qg[0, :, D:], ((0, 0), (0, LANES - N_BRANCH * N_HEADS)))),
               q_norm_g=jnp.tile(q_norm_g[0], N_HEADS)[None], bd_q=_block_diag_mean(D, HEAD_DIM), w_o=bf(w_o[0]))
    ge = np.zeros((LANES, N_BRANCH * D), np.float32)
    for br in range(N_BRANCH):
        for hd in range(N_HEADS):
            ge[br * N_HEADS + hd, br * D + hd * HEAD_DIM: br * D + (hd + 1) * HEAD_DIM] = 1.0
    nsa["gate_expand"] = jnp.asarray(ge, dtype=BF16)

    q_p, gates_p = _query_side(hp, nsa, 512)
    q_p = q_p.reshape(B, T, D)
    n_slc_p = T // SLC_BLOCK
    nsp_p = -(-n_slc_p // 32) * 32
    m_p = _overlap_matrix(kc_p.shape[1], ncmp_p, n_slc_p, nsp_p)
    o_cmp_p, imp_p = _cmp_attn(q_p, kc_p, vc_p, m_p, n_cmp=ncmp_p, pos_base=0, per_tile=True, tq=256)
    imp_p = imp_p.reshape(B, T, N_KV_HEADS, nsp_p)[..., :n_slc_p]
    idx_p = _select_blocks(imp_p, jnp.arange(T), n_slc_p)
    sel_p = (idx_p[..., None] == jnp.arange(nsp_p)).any(-2).astype(BF16).reshape(B, T, N_KV_HEADS * nsp_p)
    kv_lanes = lambda t: t.reshape(t.shape[0], t.shape[1], 2 * KV_W).astype(BF16)
    o_slc_p, o_win_p = _slc_win_attn(q_p, sel_p, kv_lanes(slc_p), kv_lanes(win_p), tq=128, tk=256)
    hp = _merge(hp, gates_p, o_cmp_p.reshape(n_p, D), o_slc_p.reshape(n_p, D), o_win_p.reshape(n_p, D), nsa, 512)

    q_s, gates_s = _query_side(hs, nsa, n_s)
    q_s = _pad_rows(q_s.reshape(SB, S, D), SUBLANES)
    n_slc_s = (past + S - 1) // SLC_BLOCK + 1
    nsp_s = -(-n_slc_s // LANES) * LANES
    m_s = _overlap_matrix(kc_s.shape[1], ncmp_s, n_slc_s, nsp_s)
    o_cmp_s, imp_s = _cmp_attn(q_s, kc_s, vc_s, m_s, n_cmp=ncmp_s, pos_base=past, per_tile=False, tq=SUBLANES)
    imp_s = imp_s.reshape(SB, SUBLANES, N_KV_HEADS, nsp_s)[:, :S, :, :n_slc_s]
    idx_s = _select_blocks(imp_s, past + jnp.arange(S), n_slc_s)
    n_sel = idx_s.shape[-1]
    in_pool = idx_s < past // SLC_BLOCK
    per_page = page // SLC_BLOCK
    blk_c = jnp.minimum(idx_s, past // SLC_BLOCK - 1)
    pages = jnp.take_along_axis(page_table, (blk_c // per_page).reshape(SB, -1), axis=1).reshape(idx_s.shape)
    pool_blk = pages * per_page + blk_c % per_page
    g_ix = jnp.arange(N_KV_HEADS)[None, None, :, None]
    cache_blocks = cache_slc_kv.reshape(-1, SLC_BLOCK, 2, N_KV_HEADS, HEAD_DIM)
    from_pool = cache_blocks[pool_blk, :, :, g_ix, :]
    new_rows = slc_s[:, jnp.clip(jnp.arange(SLC_BLOCK), 0, S - 1)]
    from_new = jnp.transpose(new_rows, (0, 3, 1, 2, 4))[:, None, :, None]
    kv_sel = jnp.where(in_pool[..., None, None, None], from_pool, from_new)
    kv_sel = jnp.swapaxes(kv_sel, -2, -3).reshape(SB, S, N_KV_HEADS, n_sel * SLC_BLOCK, 2 * HEAD_DIM).astype(BF16)
    kpos = (idx_s[..., None] * SLC_BLOCK + jnp.arange(SLC_BLOCK)).reshape(SB, S * N_KV_HEADS, n_sel * SLC_BLOCK).astype(jnp.int32)
    kpos = jnp.pad(kpos, ((0, 0), (0, -(-S * N_KV_HEADS // SUBLANES) * SUBLANES - S * N_KV_HEADS), (0, 0)))
    n_win = wb + S
    nwp = -(-n_win // SUBLANES) * SUBLANES
    win_lanes = _pad_rows(kv_lanes(win_all_s), nwp)
    o_slc_s, o_win_s = _sample_attn(q_s, kv_sel, kpos, win_lanes, n_q=S, pos_base=past, win_base=past - wb, n_win=n_win)
    unpad = lambda t: t[:, :S].reshape(n_s, D)
    hs = _merge(hs, gates_s, unpad(o_cmp_s), unpad(o_slc_s), unpad(o_win_s), nsa, n_s)

    moe = dict(g_moe=norm_g[1, 1][None], w_router=jnp.pad(moe_router[0], ((0, 0), (0, LANES - N_EXPERTS))),
               moe_wg=bf(moe_w_gate[0]), moe_wu=bf(moe_w_up[0]), moe_wd=bf(moe_w_down[0]))
    hp = _moe(hp, moe, tm_route=512, tm=512, tf=tf).reshape(B, T, D)
    hs = _moe(hs, moe, tm_route=n_s, tm=128, tf=tf).reshape(SB, S, D)

    keep_p = min(WINDOW, T)
    return (hp, hs, cmp_p, cmp_s, slc_p, slc_s, win_p[:, T - keep_p:], win_all_s[:, win_all_s.shape[1] - wb:],
            wkv_p[None], wkv_s[None], shift_p[None], shift_s[None])
```

```python
import functools

import numpy as np
import jax
import jax.numpy as jnp
from jax import lax
from jax.experimental import pallas as pl
from jax.experimental.pallas import tpu as pltpu

F32 = jnp.float32
BF16 = jnp.bfloat16

RW_HEAD_DIM = 64
RW_GN_EPS = 64e-5
N_HEADS = 16
HEAD_DIM = 64
N_KV_HEADS = 4
HEADS_PER_KV = N_HEADS // N_KV_HEADS
KV_W = N_KV_HEADS * HEAD_DIM
N_BRANCH = 3
CMP_BLOCK = 32
CMP_STRIDE = 16
SLC_BLOCK = 64
SLC_TOP = 16
WINDOW = 512
ATTN_SCALE = HEAD_DIM ** -0.5
N_EXPERTS = 8
TOP_K = 2
RMS_EPS = 1e-6
MASKED = -1e30
M_INIT = -1e20
BIG = 1e30
LOG2E = 1.4426950408889634
SCORE_BOUND_MAX = 40.0

SUBLANES = 8
LANES = 128
VMEM_LIMIT = 56 * 1024 * 1024
WKV_CHUNK = RW_HEAD_DIM


def _cparams(sem):
    return pltpu.CompilerParams(dimension_semantics=sem, vmem_limit_bytes=VMEM_LIMIT)


_DIMS = {"nn": (((1,), (0,)), ((), ())), "nt": (((1,), (1,)), ((), ())), "tn": (((0,), (0,)), ((), ()))}


def _split2(x):
    hi = x.astype(BF16)
    lo = (x - hi.astype(F32)).astype(BF16)
    return hi, lo


def _mm(a, b, form="nn", hp=False):
    d = lambda s, t: lax.dot_general(s, t, _DIMS[form], preferred_element_type=F32)
    if not hp:
        return d(a.astype(BF16), b.astype(BF16))
    ah, al = _split2(a.astype(F32))
    bh, bl = _split2(b.astype(F32))
    return d(ah, bh) + d(ah, bl) + d(al, bh)


def _dot_hilo(x, m):
    hi, lo = _split2(x)
    return jnp.dot(hi, m, preferred_element_type=F32) + jnp.dot(lo, m, preferred_element_type=F32)


def _dot_exact_rhs(m, x):
    hi = x.astype(BF16)
    r1 = x - hi.astype(F32)
    mid = r1.astype(BF16)
    lo = (r1 - mid.astype(F32)).astype(BF16)
    d = lambda t: jnp.dot(m, t, preferred_element_type=F32)
    return d(hi) + d(mid) + d(lo)


def _rms(x, g):
    return x * lax.rsqrt(jnp.mean(x * x, -1, keepdims=True) + RMS_EPS) * g


def _silu(x):
    return x * jax.nn.sigmoid(x)


def _softplus(z):
    return jnp.maximum(z, 0.0) + jnp.log(1.0 + jnp.exp(-jnp.abs(z)))


def _block_diag(n, seg, value):
    i = np.arange(n)
    return jnp.asarray((i[:, None] // seg == i[None, :] // seg).astype(np.float32) * value, dtype=BF16)


def _rwkv_proj_kernel(x_ref, halo_ref, sh_ref, g_ref, mix_ref, vec_ref, wrkv_ref, dw1_ref, dw2_ref, aw1_ref, aw2_ref,
                      gw1_ref, gw2_ref, r_o, lw_o, k_o, v_o, a_o, g_o, shift_o, *, last_tile, last_row, hp):
    i = pl.program_id(1)
    mm = functools.partial(_mm, hp=hp)
    g = g_ref[...]
    xn = _rms(x_ref[0], g)
    hn = _rms(halo_ref[0, SUBLANES - 1:SUBLANES, :], g)
    prev_last = jnp.where(i == 0, sh_ref[0], hn)
    row = lax.broadcasted_iota(jnp.int32, xn.shape, 0)
    prev = jnp.where(row == 0, prev_last, pltpu.roll(xn, 1, 0))
    xx = prev - xn
    mixed = lambda j: xn + xx * mix_ref[j:j + 1, :]
    r_o[0] = mm(mixed(0), wrkv_ref[0])
    k_o[0] = mm(mixed(2), wrkv_ref[1])
    v_o[0] = mm(mixed(3), wrkv_ref[2])
    w0 = vec_ref[0:1, :]
    a0 = vec_ref[1:2, :]
    log_w = -_softplus(-(w0 + mm(jnp.tanh(mm(mixed(1), dw1_ref[...])), dw2_ref[...]))) - 0.5
    lw_o[0] = -jnp.exp(log_w)
    a_o[0] = jax.nn.sigmoid(a0 + mm(mm(mixed(4), aw1_ref[...]), aw2_ref[...]))
    g_o[0] = mm(jax.nn.sigmoid(mm(mixed(5), gw1_ref[...])), gw2_ref[...])

    @pl.when(i == last_tile)
    def _():
        shift_o[0] = xn[last_row:last_row + 1, :]


def _rwkv_proj(x, shift0, g, mix, vec, wrkv, dw1, dw2, aw1, aw2, gw1, gw2, *, t_real, tm, hp):
    B, T, D = x.shape
    nt = T // tm
    row_spec = pl.BlockSpec((1, tm, D), lambda b, i: (b, i, 0))
    halo_spec = pl.BlockSpec((1, SUBLANES, D), lambda b, i: (b, jnp.maximum(i * (tm // SUBLANES) - 1, 0), 0))
    vec_spec = pl.BlockSpec((1, 1, D), lambda b, i: (b, 0, 0))
    full = lambda a: pl.BlockSpec(a.shape, lambda b, i: (0,) * a.ndim)
    consts = (g, mix, vec, wrkv, dw1, dw2, aw1, aw2, gw1, gw2)
    out_sd = jax.ShapeDtypeStruct((B, T, D), F32)
    kern = functools.partial(_rwkv_proj_kernel, last_tile=(t_real - 1) // tm, last_row=(t_real - 1) % tm, hp=hp)
    return pl.pallas_call(
        kern, grid=(B, nt),
        in_specs=[row_spec, halo_spec, vec_spec] + [full(c) for c in consts],
        out_specs=[row_spec] * 6 + [vec_spec],
        out_shape=[out_sd] * 6 + [jax.ShapeDtypeStruct((B, 1, D), F32)],
        compiler_params=_cparams(("parallel", "arbitrary")), name="rwkv_proj",
    )(x, x, shift0, *consts)


def _wkv_kernel(r_ref, lw_ref, k_ref, v_ref, a_ref, vec_ref, ones_ref, s0_ref, y_o, sT_o, s_sc, *, n_pairs, hp, inv_hp):
    c = pl.program_id(1)
    C = r_ref.shape[1]
    W = 2 * C
    mm = functools.partial(_mm, hp=hp)
    mm_inv = functools.partial(_mm, hp=inv_hp)

    @pl.when(c == 0)
    def _():
        s_sc[...] = s0_ref[0]

    lane = lax.broadcasted_iota(jnp.int32, (C, W), 1)
    row = lax.broadcasted_iota(jnp.int32, (C, W), 0)
    left = lane < C
    col = jnp.where(left, lane, lane - C)
    strict = col < row
    incl = col <= row
    diag_blocks = (lax.broadcasted_iota(jnp.int32, (W, W), 0) < C) == (lax.broadcasted_iota(jnp.int32, (W, W), 1) < C)
    bd = lambda x: jnp.concatenate([jnp.where(left, x, 0.0), jnp.where(left, 0.0, x)], axis=0)
    seg_sum = lambda x: _dot_hilo(x, ones_ref[...])
    rows = lambda x, p: x[p * C:(p + 1) * C]
    tri = jnp.where(lax.broadcasted_iota(jnp.int32, (C, C), 1) <= lax.broadcasted_iota(jnp.int32, (C, C), 0), 1.0, 0.0)
    cum_all = _dot_exact_rhs(tri.astype(BF16), lw_ref[0])
    pairs = range(n_pairs)
    ps = lambda p: slice(p * W, (p + 1) * W)

    r = [r_ref[0, :, ps(p)] for p in pairs]
    v = [v_ref[0, :, ps(p)] for p in pairs]
    kkr = [k_ref[0, :, ps(p)] * vec_ref[2:3, ps(p)] for p in pairs]
    ss = seg_sum(jnp.concatenate([x * x for x in kkr], axis=0))
    k2, at, rt, bt, kt, w_end = [], [], [], [], [], []
    for p in pairs:
        a = a_ref[0, :, ps(p)]
        cum = cum_all[:, ps(p)]
        kk = kkr[p] * lax.rsqrt(jnp.maximum(rows(ss, p), 1e-24))
        k2.append(k_ref[0, :, ps(p)] * (1.0 + (a - 1.0) * vec_ref[3:4, ps(p)]))
        e_pos = jnp.exp(cum)
        e_neg = jnp.exp(-cum)
        rt.append(r[p] * e_pos)
        at.append(-kk * jnp.exp(cum - lw_ref[0, :, ps(p)]))
        bt.append(kk * a * e_neg)
        kt.append(k2[p] * e_neg)
        w_end.append(e_pos[C - 1:C, :])

    P, U, m_rbk, x_r = [], [], [], []
    for p in pairs:
        ar = jnp.concatenate([at[p], rt[p]], axis=0)
        big = mm(ar, jnp.concatenate([bd(bt[p]), bd(kt[p])], axis=0), "nt")
        x = mm(ar, s_sc[p], "nt")
        P.append(jnp.where(strict, big[:C, :W], 0.0))
        l_ak = jnp.where(strict, big[:C, W:], 0.0)
        m_rbk.append(jnp.concatenate([jnp.where(incl, big[C:, :W], 0.0), jnp.where(incl, big[C:, W:], 0.0)], axis=1))
        U.append(x[:C] + mm(l_ak, bd(v[p])))
        x_r.append(x[C:])

    n_it = int(np.log2(C))
    for it in range(n_it):
        for p in pairs:
            if it == n_it - 1:
                U[p] = U[p] + mm_inv(P[p], bd(U[p]))
            else:
                res = mm_inv(P[p], jnp.concatenate([bd(P[p]), bd(U[p])], axis=1))
                U[p] = U[p] + res[:, W:]
                P[p] = res[:, :W]

    y = []
    for p in pairs:
        y.append(x_r[p] + mm(m_rbk[p], jnp.concatenate([bd(U[p]), bd(v[p])], axis=0)))
        upd = mm(jnp.concatenate([U[p], v[p]], axis=0),
                 jnp.concatenate([bt[p] * w_end[p], kt[p] * w_end[p]], axis=0), "tn")
        s_sc[p] = s_sc[p] * w_end[p] + jnp.where(diag_blocks, upd, 0.0)

    inv_n = 1.0 / RW_HEAD_DIM
    y_all = jnp.concatenate(y, axis=0)
    mu = seg_sum(y_all) * inv_n
    var = seg_sum(jnp.square(y_all - mu)) * inv_n
    rk = seg_sum(jnp.concatenate([r[p] * k2[p] * vec_ref[4:5, ps(p)] for p in pairs], axis=0))
    yn = (y_all - mu) * lax.rsqrt(var + RW_GN_EPS)
    for p in pairs:
        y_o[0, :, ps(p)] = rows(yn, p) * vec_ref[5:6, ps(p)] + vec_ref[6:7, ps(p)] + rows(rk, p) * v[p]

    @pl.when(c == pl.num_programs(1) - 1)
    def _():
        sT_o[0] = s_sc[...]


def _wkv(r, lw, k, v, a, vec, s0, *, hp, inv_hp):
    B, T, D = r.shape
    H = D // RW_HEAD_DIM
    C, W, n_pairs = WKV_CHUNK, 2 * RW_HEAD_DIM, H // 2
    s_bd = jnp.zeros((B, n_pairs, W, W), F32)
    s_bd = s_bd.at[:, :, :C, :C].set(s0[:, 0::2]).at[:, :, C:, C:].set(s0[:, 1::2])
    row_spec = pl.BlockSpec((1, C, D), lambda b, c: (b, c, 0))
    st_spec = pl.BlockSpec((1, n_pairs, W, W), lambda b, c: (b, 0, 0, 0))
    ones = _block_diag(W, RW_HEAD_DIM, 1.0)
    kern = functools.partial(_wkv_kernel, n_pairs=n_pairs, hp=hp, inv_hp=inv_hp)
    y, s_out = pl.pallas_call(
        kern, grid=(B, T // C),
        in_specs=[row_spec] * 5 + [pl.BlockSpec(vec.shape, lambda b, c: (0, 0)), pl.BlockSpec((W, W), lambda b, c: (0, 0)),
                                   st_spec],
        out_specs=[row_spec, st_spec],
        out_shape=[jax.ShapeDtypeStruct((B, T, D), F32), jax.ShapeDtypeStruct(s_bd.shape, F32)],
        scratch_shapes=[pltpu.VMEM((n_pairs, W, W), F32)],
        compiler_params=_cparams(("parallel", "arbitrary")), name="wkv_chunk",
    )(r, lw, k, v, a, vec, ones, s_bd)
    s_fin = jnp.stack([s_out[:, :, :C, :C], s_out[:, :, C:, C:]], axis=2).reshape(B, H, RW_HEAD_DIM, RW_HEAD_DIM)
    return y, s_fin


def _row_call(kernel, rows, consts, out_widths, out_dtypes, tm, name):
    n = rows[0].shape[0]
    row_spec = lambda w: pl.BlockSpec((tm, w), lambda i: (i, 0))
    full = lambda a: pl.BlockSpec(a.shape, lambda i: (0,) * a.ndim)
    return pl.pallas_call(
        kernel, grid=(n // tm,),
        in_specs=[row_spec(a.shape[1]) for a in rows] + [full(c) for c in consts],
        out_specs=[row_spec(w) for w in out_widths],
        out_shape=[jax.ShapeDtypeStruct((n, w), dt) for w, dt in zip(out_widths, out_dtypes)],
        compiler_params=_cparams(("parallel",)), name=name,
    )(*rows, *consts)


def _rwkv_out_kernel(x_ref, y_ref, g_ref, wo_ref, o_ref, *, hp):
    o_ref[...] = x_ref[...] + _mm(y_ref[...] * g_ref[...], wo_ref[...], hp=hp)


def _kv_proj_kernel(h_ref, g_ref, wkv_ref, kg_ref, bd_ref, cmp_o, slc_o, win_o, *, hp):
    kv = _mm(_rms(h_ref[...], g_ref[...]), wkv_ref[...], hp=hp)
    cmp_o[...] = kv[:, :2 * KV_W]
    for br, out in ((1, slc_o), (2, win_o)):
        kraw = kv[:, br * 2 * KV_W: br * 2 * KV_W + KV_W]
        ms = _dot_hilo(kraw * kraw, bd_ref[...])
        out[:, :KV_W] = kraw * lax.rsqrt(ms + RMS_EPS) * kg_ref[br:br + 1, :]
        out[:, KV_W:] = kv[:, br * 2 * KV_W + KV_W: (br + 1) * 2 * KV_W]


def _q_proj_kernel(h_ref, g_ref, wq_ref, wg_ref, qg_ref, bd_ref, q_o, gate_o, *, hp, q_scale):
    hn = _rms(h_ref[...], g_ref[...])
    q = _mm(hn, wq_ref[...], hp=hp)
    ms = _dot_hilo(q * q, bd_ref[...])
    q_o[...] = (q * lax.rsqrt(ms + RMS_EPS) * qg_ref[...] * q_scale).astype(q_o.dtype)
    gate_o[...] = jax.nn.sigmoid(_mm(hn, wg_ref[...], hp=hp))


def _merge_kernel(h_ref, gate_ref, oc_ref, os_ref, ow_ref, eg_ref, wo_ref, o_ref, *, hp):
    D = h_ref.shape[1]
    ge = _dot_hilo(gate_ref[...], eg_ref[...])
    o = ge[:, :D] * oc_ref[...] + ge[:, D:2 * D] * os_ref[...] + ge[:, 2 * D:] * ow_ref[...]
    o_ref[...] = h_ref[...] + _mm(o, wo_ref[...], hp=hp)


def _router_kernel(h_ref, g_ref, wr_ref, xn_o, logit_o):
    xn = _rms(h_ref[...], g_ref[...])
    xn_o[...] = xn.astype(xn_o.dtype)
    logit_o[...] = _mm(xn, wr_ref[...], hp=True)


def _ffn_kernel(x_ref, g_ref, wg_ref, wu_ref, wd_ref, o_ref, xn_sc, acc_sc, *, hp):
    f = pl.program_id(1)

    @pl.when(f == 0)
    def _():
        xn_sc[...] = _rms(x_ref[...], g_ref[...]).astype(xn_sc.dtype)
        acc_sc[...] = jnp.zeros_like(acc_sc)

    xn = xn_sc[...]
    hid = _silu(_mm(xn, wg_ref[...], hp=hp)) * _mm(xn, wu_ref[...], hp=hp)
    acc_sc[...] += _mm(hid, wd_ref[...], hp=hp)

    @pl.when(f == pl.num_programs(1) - 1)
    def _():
        o_ref[...] = x_ref[...] + acc_sc[...]


def _ffn(x, g, wg, wu, wd, *, tm, tf, hp):
    n, D = x.shape
    dff = wg.shape[1]
    return pl.pallas_call(
        functools.partial(_ffn_kernel, hp=hp), grid=(n // tm, dff // tf),
        in_specs=[pl.BlockSpec((tm, D), lambda i, f: (i, 0)), pl.BlockSpec((1, D), lambda i, f: (0, 0)),
                  pl.BlockSpec((D, tf), lambda i, f: (0, f)), pl.BlockSpec((D, tf), lambda i, f: (0, f)),
                  pl.BlockSpec((tf, D), lambda i, f: (f, 0))],
        out_specs=pl.BlockSpec((tm, D), lambda i, f: (i, 0)),
        out_shape=jax.ShapeDtypeStruct((n, D), F32),
        scratch_shapes=[pltpu.VMEM((tm, D), F32 if hp else BF16), pltpu.VMEM((tm, D), F32)],
        compiler_params=_cparams(("parallel", "arbitrary")), name="ffn_dense",
    )(x, g, wg, wu, wd)


def _moe_ffn_kernel(be_ref, nb_ref, x_ref, wg_ref, wu_ref, wd_ref, o_ref, acc_sc, *, hp):
    i = pl.program_id(0)
    f = pl.program_id(1)

    @pl.when(i < nb_ref[0])
    def _():
        @pl.when(f == 0)
        def _():
            acc_sc[...] = jnp.zeros_like(acc_sc)

        x = x_ref[...]
        hid = _silu(_mm(x, wg_ref[0], hp=hp)) * _mm(x, wu_ref[0], hp=hp)
        acc_sc[...] += _mm(hid, wd_ref[0], hp=hp)

        @pl.when(f == pl.num_programs(1) - 1)
        def _():
            o_ref[...] = acc_sc[...]

    @pl.when((i >= nb_ref[0]) & (f == pl.num_programs(1) - 1))
    def _():
        o_ref[...] = jnp.zeros_like(o_ref)


def _moe_ffn(block_e, n_used, xs, wg, wu, wd, *, tm, tf, hp):
    cap, D = xs.shape
    dff = wg.shape[2]
    w_idx = lambda i, f, be, nb: jnp.where(i < nb[0], f, dff // tf - 1)
    grid_spec = pltpu.PrefetchScalarGridSpec(
        num_scalar_prefetch=2, grid=(cap // tm, dff // tf),
        in_specs=[pl.BlockSpec((tm, D), lambda i, f, be, nb: (i, 0)),
                  pl.BlockSpec((1, D, tf), lambda i, f, be, nb: (be[i], 0, w_idx(i, f, be, nb))),
                  pl.BlockSpec((1, D, tf), lambda i, f, be, nb: (be[i], 0, w_idx(i, f, be, nb))),
                  pl.BlockSpec((1, tf, D), lambda i, f, be, nb: (be[i], w_idx(i, f, be, nb), 0))],
        out_specs=pl.BlockSpec((tm, D), lambda i, f, be, nb: (i, 0)),
        scratch_shapes=[pltpu.VMEM((tm, D), F32)])
    return pl.pallas_call(
        functools.partial(_moe_ffn_kernel, hp=hp), grid_spec=grid_spec, out_shape=jax.ShapeDtypeStruct((cap, D), F32),
        compiler_params=_cparams(("arbitrary", "arbitrary")), name="moe_ffn",
    )(block_e, n_used, xs, wg, wu, wd)


def _compress_kernel(x_ref, pe_ref, w1_ref, w2_ref, kg_ref, o_ref, *, hp):
    c = pl.program_id(1)
    x = x_ref[0, 0, 0]
    R = x.shape[0]
    p0 = _mm(x + pe_ref[0, 0], w1_ref[0, 0], hp=hp)
    p1 = _mm(x + pe_ref[0, 1], w1_ref[0, 1], hp=hp)
    pre = p0 + pltpu.roll(p1, R - 1, 0)
    out = _mm(_silu(pre), w2_ref[0], hp=hp)
    o_ref[0, 0, 0] = jnp.where(c == 0, _rms(out, kg_ref[...]), out)


def _compress(xsub, pe, w1, w2, kg, *, hp):
    S, _, G, R, W = xsub.shape
    return pl.pallas_call(
        functools.partial(_compress_kernel, hp=hp), grid=(S, 2, G),
        in_specs=[pl.BlockSpec((1, 1, 1, R, W), lambda s, c, g: (s, c, g, 0, 0)),
                  pl.BlockSpec((1, 2, 1, W), lambda s, c, g: (c, 0, 0, 0)),
                  pl.BlockSpec((1, 2, W, HEAD_DIM), lambda s, c, g: (c, 0, 0, 0)),
                  pl.BlockSpec((1, HEAD_DIM, HEAD_DIM), lambda s, c, g: (c, 0, 0)),
                  pl.BlockSpec((1, HEAD_DIM), lambda s, c, g: (0, 0))],
        out_specs=pl.BlockSpec((1, 1, 1, R, HEAD_DIM), lambda s, c, g: (s, c, g, 0, 0)),
        out_shape=jax.ShapeDtypeStruct((S, 2, G, R, HEAD_DIM), F32),
        compiler_params=_cparams(("parallel", "arbitrary", "arbitrary")), name="kv_compress",
    )(xsub, pe, w1, w2, kg)


def _stack_heads(q, g, tq):
    return jnp.concatenate([q[:, (HEADS_PER_KV * g + i) * HEAD_DIM:(HEADS_PER_KV * g + i + 1) * HEAD_DIM]
                            for i in range(HEADS_PER_KV)], axis=0)


def _cmp_attn_kernel(q_ref, kc_ref, vc_ref, m_ref, o_o, sel_o, *, n_cmp, n_slc, pos_base, per_tile, hp):
    tq = q_ref.shape[1]
    ncp = kc_ref.shape[1]
    nsp = m_ref.shape[1]
    q0 = pos_base + (pl.program_id(1) * tq if per_tile else 0)
    rows = HEADS_PER_KV * tq
    q_pos = q0 + lax.broadcasted_iota(jnp.int32, (rows, ncp), 0) % tq
    n_id = lax.broadcasted_iota(jnp.int32, (rows, ncp), 1)
    bias = jnp.where((n_id * CMP_STRIDE + CMP_BLOCK - 1 <= q_pos) & (n_id < n_cmp), 0.0, MASKED)
    blk = lax.broadcasted_iota(jnp.int32, (tq, nsp), 1)
    cur = (q0 + lax.broadcasted_iota(jnp.int32, (tq, nsp), 0)) // SLC_BLOCK
    forced = (blk == 0) | (blk == cur) | (blk == cur - 1)
    q = q_ref[0]
    scores = []
    for g in range(N_KV_HEADS):
        qs = _stack_heads(q, g, tq)
        s = _mm(qs, kc_ref[0, :, g * HEAD_DIM:(g + 1) * HEAD_DIM], "nt", hp) + bias
        m = jnp.maximum(jnp.max(s, -1, keepdims=True), M_INIT)
        e = jnp.exp(s - m)
        p = e / jnp.maximum(jnp.sum(e, -1, keepdims=True), 1e-30)
        o = _mm(p, vc_ref[0, :, g * HEAD_DIM:(g + 1) * HEAD_DIM], hp=hp)
        psum = p[0:tq]
        for i in range(HEADS_PER_KV):
            col = (HEADS_PER_KV * g + i) * HEAD_DIM
            o_o[0, :, col:col + HEAD_DIM] = o[i * tq:(i + 1) * tq]
            if i:
                psum = psum + p[i * tq:(i + 1) * tq]
        imp = _dot_hilo(psum, m_ref[...])
        scores.append(jnp.where(blk <= cur, jnp.where(forced, BIG, imp), -BIG))
    score = jnp.concatenate(scores, axis=0)
    blk_r = lax.broadcasted_iota(jnp.int32, score.shape, 1)

    def count_ahead(i, rank):
        s_i = jnp.sum(jnp.where(blk_r == i, score, 0.0), axis=-1, keepdims=True)
        return rank + jnp.where(s_i > score, 1.0, jnp.where((s_i == score) & (i < blk_r), 1.0, 0.0))

    rank = lax.fori_loop(0, n_slc, count_ahead, jnp.zeros(score.shape, F32), unroll=8)
    for g in range(N_KV_HEADS):
        sel_o[0, :, g * nsp:(g + 1) * nsp] = jnp.where(rank[g * tq:(g + 1) * tq] < min(SLC_TOP, n_slc), 1.0, 0.0).astype(sel_o.dtype)


def _cmp_attn(q, kc, vc, m, *, n_cmp, n_slc, pos_base, per_tile, tq, hp):
    B, T, D = q.shape
    ncp, nsp = m.shape
    kv_spec = pl.BlockSpec((1, ncp, KV_W), lambda b, i: (b, 0, 0))
    kern = functools.partial(_cmp_attn_kernel, n_cmp=n_cmp, n_slc=n_slc, pos_base=pos_base, per_tile=per_tile, hp=hp)
    return pl.pallas_call(
        kern, grid=(B, T // tq),
        in_specs=[pl.BlockSpec((1, tq, D), lambda b, i: (b, i, 0)), kv_spec, kv_spec,
                  pl.BlockSpec((ncp, nsp), lambda b, i: (0, 0))],
        out_specs=[pl.BlockSpec((1, tq, D), lambda b, i: (b, i, 0)),
                   pl.BlockSpec((1, tq, N_KV_HEADS * nsp), lambda b, i: (b, i, 0))],
        out_shape=[jax.ShapeDtypeStruct((B, T, D), F32), jax.ShapeDtypeStruct((B, T, N_KV_HEADS * nsp), BF16)],
        compiler_params=_cparams(("parallel", "arbitrary")), name="cmp_attn",
    )(q, kc, vc, m)


def _pair_rows(q, g):
    base = g * HEADS_PER_KV * HEAD_DIM
    return jnp.concatenate([q[:, base:base + LANES], q[:, base + LANES:base + 2 * LANES]], axis=0)


def _finish_pair(acc_even, acc_odd, p, tq):
    a0 = acc_even[p * tq:(p + 1) * tq]
    a1 = acc_odd[p * tq:(p + 1) * tq]
    o0 = a0 / jnp.maximum(a0[:, HEAD_DIM:HEAD_DIM + 1], 1e-30)
    o1 = a1 / jnp.maximum(a1[:, HEAD_DIM:HEAD_DIM + 1], 1e-30)
    lane = lax.broadcasted_iota(jnp.int32, (tq, LANES), 1)
    return jnp.where(lane < HEAD_DIM, o0, pltpu.roll(o1, HEAD_DIM, 1))


def _cmp_sel_kernel(q_ref, kab_ref, v1_ref, m_ref, o_o, sel_o, *, n_cmp, n_slc, bounded):
    tq = q_ref.shape[1]
    ncp = kab_ref.shape[1]
    nsp = LANES // N_KV_HEADS
    q0 = pl.program_id(1) * tq
    q_pos = q0 + lax.broadcasted_iota(jnp.int32, (tq, ncp), 0)
    n_id = lax.broadcasted_iota(jnp.int32, (tq, ncp), 1)
    b = jnp.where((n_id * CMP_STRIDE + CMP_BLOCK - 1 <= q_pos) & (n_id < n_cmp), 0.0, MASKED)
    b2 = jnp.concatenate([b, b], axis=0)
    q = q_ref[0]
    imp = jnp.zeros((tq, LANES), F32)
    for g in range(N_KV_HEADS):
        lhs = _pair_rows(q, g)
        vv = v1_ref[0, :, g * LANES:(g + 1) * LANES]
        accs, psum = [], None
        for half in range(2):
            s = _mm(lhs, kab_ref[0, :, (2 * g + half) * LANES:(2 * g + half + 1) * LANES], "nt") + b2
            e = jnp.exp2(s) if bounded else jnp.exp2(s - jnp.maximum(jnp.max(s, -1, keepdims=True), M_INIT))
            acc = _mm(e, vv)
            accs.append(acc)
            p = e / jnp.maximum(acc[:, HEAD_DIM:HEAD_DIM + 1], 1e-30)
            ph = p[:tq] + p[tq:]
            psum = ph if psum is None else psum + ph
        imp = imp + _dot_hilo(psum, m_ref[g])
        for p_ in range(2):
            col = g * 2 * LANES + p_ * LANES
            o_o[0, :, col:col + LANES] = _finish_pair(accs[0], accs[1], p_, tq)
    lane = lax.broadcasted_iota(jnp.int32, (tq, LANES), 1)
    blk = lane % nsp
    cur = (q0 + lax.broadcasted_iota(jnp.int32, (tq, LANES), 0)) // SLC_BLOCK
    forced = (blk == 0) | (blk == cur) | (blk == cur - 1)
    score = jnp.where(blk <= cur, jnp.where(forced, BIG, imp), -BIG)
    rank = jnp.zeros((tq, LANES), F32)
    for d in range(1, nsp):
        wrapped = blk + d >= nsp
        partner = jnp.where(wrapped, pltpu.roll(score, nsp - d, 1), pltpu.roll(score, LANES - d, 1))
        rank = rank + jnp.where(partner > score, 1.0, jnp.where((partner == score) & wrapped, 1.0, 0.0))
    sel_o[0] = jnp.where(rank < min(SLC_TOP, n_slc), 1.0, 0.0).astype(sel_o.dtype)


def _cmp_sel(q, kab, v1, m, *, n_cmp, n_slc, tq, bounded):
    B, T, D = q.shape
    ncp = kab.shape[1]
    assert m.shape == (N_KV_HEADS, ncp, LANES)
    whole = lambda a: pl.BlockSpec((1,) + a.shape[1:], lambda b, i: (b,) + (0,) * (a.ndim - 1))
    kern = functools.partial(_cmp_sel_kernel, n_cmp=n_cmp, n_slc=n_slc, bounded=bounded)
    return pl.pallas_call(
        kern, grid=(B, T // tq),
        in_specs=[pl.BlockSpec((1, tq, D), lambda b, i: (b, i, 0)), whole(kab), whole(v1),
                  pl.BlockSpec(m.shape, lambda b, i: (0, 0, 0))],
        out_specs=[pl.BlockSpec((1, tq, D), lambda b, i: (b, i, 0)), pl.BlockSpec((1, tq, LANES), lambda b, i: (b, i, 0))],
        out_shape=[jax.ShapeDtypeStruct((B, T, D), F32), jax.ShapeDtypeStruct((B, T, LANES), BF16)],
        compiler_params=_cparams(("parallel", "arbitrary")), name="cmp_sel_attn",
    )(q, kab, v1, m)


def _flash_pairs(lhs, kab_ref, v1_ref, kt_lo, kt_hi, tk, bias_fn, bounded):
    rows = lhs[0].shape[0]

    def body(kt, carry):
        k0 = pl.multiple_of(kt * tk, tk)
        biases = bias_fn(k0)
        out = []
        for g in range(N_KV_HEADS):
            vv = v1_ref[0, pl.ds(k0, tk), g * LANES:(g + 1) * LANES]
            for half in range(2):
                idx = 2 * g + half
                s = _mm(lhs[g], kab_ref[0, pl.ds(k0, tk), idx * LANES:(idx + 1) * LANES], "nt") + biases[g]
                if bounded:
                    out.append(carry[idx] + _mm(jnp.exp2(s), vv))
                else:
                    m, acc = carry[idx]
                    m_new = jnp.maximum(m, jnp.max(s, -1, keepdims=True))
                    out.append((m_new, jnp.exp2(m - m_new) * acc + _mm(jnp.exp2(s - m_new), vv)))
        return tuple(out)

    zero = jnp.zeros((rows, LANES), F32)
    if bounded:
        return lax.fori_loop(kt_lo, kt_hi, body, (zero,) * (2 * N_KV_HEADS))
    start = (jnp.full((rows, 1), M_INIT, F32), zero)
    return tuple(acc for _, acc in lax.fori_loop(kt_lo, kt_hi, body, (start,) * (2 * N_KV_HEADS)))


def _slc_win_attn_kernel(q_ref, sel_ref, skab_ref, sv1_ref, wkab_ref, wv1_ref, os_o, ow_o, *, tk, bounded):
    tq = q_ref.shape[1]
    nsp = sel_ref.shape[2] // N_KV_HEADS
    q0 = pl.program_id(1) * tq
    q_pos = q0 + lax.broadcasted_iota(jnp.int32, (tq, tk), 0)
    k_off = lax.broadcasted_iota(jnp.int32, (tq, tk), 1)
    sel_lane = lax.broadcasted_iota(jnp.int32, (N_KV_HEADS * nsp, tk), 0)
    blk_k = lax.broadcasted_iota(jnp.int32, (N_KV_HEADS * nsp, tk), 1)
    drop = jnp.where(sel_ref[0].astype(F32) > 0.5, 0.0, MASKED).astype(BF16)
    q = q_ref[0]
    kt_hi = (q0 + tq + tk - 1) // tk
    kt_lo_win = jnp.maximum(q0 - WINDOW + 1, 0) // tk

    twice = lambda b: jnp.concatenate([b, b], axis=0)

    def win_bias(k0):
        d = q_pos - (k_off + k0)
        return [twice(jnp.where((d >= 0) & (d < WINDOW), 0.0, MASKED))] * N_KV_HEADS

    def slc_bias(k0):
        causal = jnp.where(k_off + k0 <= q_pos, 0.0, MASKED)
        key_blk = (blk_k + k0) // SLC_BLOCK
        out = []
        for g in range(N_KV_HEADS):
            expand = jnp.where(g * nsp + key_blk == sel_lane, 1.0, 0.0).astype(BF16)
            out.append(twice(jnp.dot(drop, expand, preferred_element_type=F32) + causal))
        return out

    lhs = [_pair_rows(q, g) for g in range(N_KV_HEADS)]
    acc_s = _flash_pairs(lhs, skab_ref, sv1_ref, 0, kt_hi, tk, slc_bias, bounded)
    acc_w = _flash_pairs(lhs, wkab_ref, wv1_ref, kt_lo_win, kt_hi, tk, win_bias, bounded)
    for g in range(N_KV_HEADS):
        for p_ in range(2):
            col = g * 2 * LANES + p_ * LANES
            os_o[0, :, col:col + LANES] = _finish_pair(acc_s[2 * g], acc_s[2 * g + 1], p_, tq)
            ow_o[0, :, col:col + LANES] = _finish_pair(acc_w[2 * g], acc_w[2 * g + 1], p_, tq)


def _slc_win_attn(q, sel, skab, sv1, wkab, wv1, *, tq, tk, bounded):
    B, T, D = q.shape
    whole = lambda a: pl.BlockSpec((1,) + a.shape[1:], lambda b, i: (b, 0, 0))
    row = lambda w: pl.BlockSpec((1, tq, w), lambda b, i: (b, i, 0))
    return pl.pallas_call(
        functools.partial(_slc_win_attn_kernel, tk=tk, bounded=bounded), grid=(B, T // tq),
        in_specs=[row(D), row(sel.shape[2]), whole(skab), whole(sv1), whole(wkab), whole(wv1)],
        out_specs=[row(D), row(D)],
        out_shape=[jax.ShapeDtypeStruct((B, T, D), F32)] * 2,
        compiler_params=_cparams(("parallel", "arbitrary")), name="slc_win_attn",
    )(q, sel, skab, sv1, wkab, wv1)


def _pair_keys(k):
    z = jnp.zeros_like(k)
    return jnp.concatenate([k, z, z, k], axis=-1).reshape(k.shape[0], k.shape[1], -1)


def _ones_values(v):
    return jnp.concatenate([v, jnp.ones_like(v)], axis=-1).reshape(v.shape[0], v.shape[1], -1)


def _sample_attn_kernel(q_ref, kvs_ref, kpos_ref, win_ref, os_o, ow_o, *, n_q, pos_base, win_base, n_win, hp):
    tq = q_ref.shape[1]
    rows = HEADS_PER_KV * tq
    nk = kvs_ref.shape[3]
    nwp = win_ref.shape[1]
    q = q_ref[0]
    row_q = lax.broadcasted_iota(jnp.int32, (rows, 1), 0) % tq
    w_id = lax.broadcasted_iota(jnp.int32, (rows, nwp), 1)
    d_win = (pos_base + row_q) - (win_base + w_id)
    win_bias = jnp.where((d_win >= 0) & (d_win < WINDOW) & (win_base + w_id >= 0) & (w_id < n_win), 0.0, MASKED)

    def attend(s, v):
        m = jnp.maximum(jnp.max(s, -1, keepdims=True), M_INIT)
        e = jnp.exp(s - m)
        p = e / jnp.maximum(jnp.sum(e, -1, keepdims=True), 1e-30)
        return _mm(p, v, hp=hp)

    for g in range(N_KV_HEADS):
        qs = _stack_heads(q, g, tq)
        o_s = jnp.zeros((rows, HEAD_DIM), F32)
        for qi in range(n_q):
            kv = kvs_ref[0, qi, g]
            kpos = kpos_ref[0, qi * N_KV_HEADS + g:qi * N_KV_HEADS + g + 1, :]
            bias = jnp.broadcast_to(jnp.where(kpos <= pos_base + qi, 0.0, MASKED), (rows, nk))
            o_qi = attend(_mm(qs, kv[:, :HEAD_DIM], "nt", hp) + bias, kv[:, HEAD_DIM:])
            o_s = jnp.where(row_q == qi, o_qi, o_s)
        o_w = attend(_mm(qs, win_ref[0, :, g * HEAD_DIM:(g + 1) * HEAD_DIM], "nt", hp) + win_bias,
                     win_ref[0, :, KV_W + g * HEAD_DIM:KV_W + (g + 1) * HEAD_DIM])
        for i in range(HEADS_PER_KV):
            col = (HEADS_PER_KV * g + i) * HEAD_DIM
            os_o[0, :, col:col + HEAD_DIM] = o_s[i * tq:(i + 1) * tq]
            ow_o[0, :, col:col + HEAD_DIM] = o_w[i * tq:(i + 1) * tq]


def _sample_attn(q, kv_sel, kpos, win_all, *, n_q, pos_base, win_base, n_win, hp):
    B, tq, D = q.shape
    kern = functools.partial(_sample_attn_kernel, n_q=n_q, pos_base=pos_base, win_base=win_base, n_win=n_win, hp=hp)
    blk = lambda a: pl.BlockSpec((1,) + a.shape[1:], lambda b: (b,) + (0,) * (a.ndim - 1))
    return pl.pallas_call(
        kern, grid=(B,),
        in_specs=[blk(q), blk(kv_sel), blk(kpos), blk(win_all)],
        out_specs=[blk(q), blk(q)],
        out_shape=[jax.ShapeDtypeStruct((B, tq, D), F32)] * 2,
        compiler_params=_cparams(("parallel",)), name="sample_slc_win_attn",
    )(q, kv_sel, kpos, win_all)


def _pick_tile(n, cap=512):
    return next(t for t in (512, 256, 128, 64, 32, 16, 8) if t <= cap and n % t == 0)


def _pad_rows(x, t_pad):
    return jnp.pad(x, ((0, 0), (0, t_pad - x.shape[1]), (0, 0)))


def _rwkv_layer(x, shift0, wkv0, p, *, tm, hp, inv_hp):
    B, T, D = x.shape
    t_pad = -(-T // tm) * tm
    xp = _pad_rows(x, t_pad)
    r, lw, k, v, a, g, shift = _rwkv_proj(xp, shift0[:, None, :], p["g"], p["mix"], p["vec"], p["wrkv"], p["dw1"],
                                          p["dw2"], p["aw1"], p["aw2"], p["gw1"], p["gw2"], t_real=T, tm=tm, hp=hp)
    if t_pad != T:
        live = (jnp.arange(t_pad) < T)[None, :, None]
        r, lw, k, v, a = (jnp.where(live, t, 0.0) for t in (r, lw, k, v, a))
    y, s_fin = _wkv(r, lw, k, v, a, p["vec"], wkv0, hp=hp, inv_hp=inv_hp)
    n = B * t_pad
    out = _row_call(functools.partial(_rwkv_out_kernel, hp=hp), [xp.reshape(n, D), y.reshape(n, D), g.reshape(n, D)],
                    [p["wo"]], [D], [F32], _pick_tile(n), "rwkv_out")[0]
    return out.reshape(B, t_pad, D)[:, :T], shift[:, 0], s_fin


def _sub_blocks(rows):
    S, L = rows.shape[:2]
    n_sub = L // CMP_STRIDE
    r = rows[:, :n_sub * CMP_STRIDE].reshape(S, n_sub, CMP_STRIDE, 2, N_KV_HEADS, HEAD_DIM)
    r = jnp.transpose(r, (0, 3, 4, 1, 2, 5)).reshape(S, 2, N_KV_HEADS, n_sub, CMP_STRIDE * HEAD_DIM)
    r_pad = -(-n_sub // SUBLANES) * SUBLANES
    return jnp.pad(r, ((0, 0),) * 3 + ((0, r_pad - n_sub), (0, 0))), n_sub - CMP_BLOCK // CMP_STRIDE + 1


def _compressed_kv(rows, p, *, hp):
    xsub, n_cmp = _sub_blocks(rows)
    out = _compress(xsub, p["cmp_pe"], p["cmp_w1"], p["cmp_w2"], p["k_norm_g"][0:1], hp=hp)
    S, _, G, R, dh = out.shape
    out = jnp.transpose(out, (1, 0, 3, 2, 4)).reshape(2, S, R, G * dh).astype(F32 if hp else BF16)
    return out[0], out[1], n_cmp


def _overlap_matrix(ncp, n_cmp, n_slc, nsp):
    c0 = np.arange(ncp)[:, None] * CMP_STRIDE
    s0 = np.arange(nsp)[None, :] * SLC_BLOCK
    m = (c0 < s0 + SLC_BLOCK) & (c0 + CMP_BLOCK > s0) & (np.arange(ncp)[:, None] < n_cmp) & (np.arange(nsp)[None, :] < n_slc)
    return jnp.asarray(m.astype(np.float32), dtype=BF16)


def _moe(h, p, *, tm, tf, hp):
    n, D = h.shape
    xn, logits = _row_call(_router_kernel, [h], [p["g_moe"], p["w_router"]], [D, LANES], [F32 if hp else BF16, F32],
                           _pick_tile(n), "moe_router")
    top_v, top_e = lax.top_k(logits[:, :N_EXPERTS], TOP_K)
    gates = jax.nn.softmax(top_v, -1)
    n_pairs = n * TOP_K
    flat_e = top_e.reshape(n_pairs)
    order = jnp.argsort(flat_e)
    se = flat_e[order]
    counts = jnp.zeros((N_EXPERTS,), jnp.int32).at[flat_e].add(1)
    padded = (counts + tm - 1) // tm * tm
    pad_end = jnp.cumsum(padded)
    start = jnp.cumsum(counts) - counts
    dest_sorted = (pad_end - padded)[se] + jnp.arange(n_pairs, dtype=jnp.int32) - start[se]
    n_blocks = -(-n_pairs // tm) + N_EXPERTS
    cap = n_blocks * tm
    buf_tok = jnp.full((cap,), n, jnp.int32).at[dest_sorted].set((order // TOP_K).astype(jnp.int32))
    dest = jnp.zeros((n_pairs,), jnp.int32).at[order].set(dest_sorted).reshape(n, TOP_K)
    block_e = jnp.minimum(jnp.searchsorted(pad_end, jnp.arange(n_blocks) * tm, side="right"), N_EXPERTS - 1).astype(jnp.int32)
    n_used = (pad_end[-1] // tm).astype(jnp.int32).reshape(1)
    xs = jnp.concatenate([xn, jnp.zeros((1, D), xn.dtype)], 0)[buf_tok]
    ys = _moe_ffn(block_e, n_used, xs, p["moe_wg"], p["moe_wu"], p["moe_wd"], tm=tm, tf=tf, hp=hp)
    y = ys[dest[:, 0]] * gates[:, 0:1] + ys[dest[:, 1]] * gates[:, 1:2]
    return h + y


def kernel(x_prompt, x_sample, cache_cmp_kv, cache_slc_kv, state_win_kv, state_wkv, state_shift, page_table, norm_g, rw_mix, rw_vec, rw_w_rkv, rw_w_o, rw_decay_w1, rw_decay_w2, rw_iclr_w1, rw_iclr_w2, rw_gate_w1, rw_gate_w2, ffn_w_gate, ffn_w_up, ffn_w_down, moe_router, moe_w_gate, moe_w_up, moe_w_down, kv_norm_g, w_kv, k_norm_g, cmp_pe, cmp_w1, cmp_w2, w_qg, q_norm_g, w_o):
    B, T, D = x_prompt.shape
    SB, S, _ = x_sample.shape
    assert norm_g.shape[0] == 2 and rw_mix.shape[0] == 1 and w_qg.shape[0] == 1, "one RWKV-7 layer + one NSA layer"
    assert D == N_HEADS * HEAD_DIM and T % 512 == 0 and S <= SUBLANES
    page = cache_cmp_kv.shape[1]
    past = page_table.shape[1] * page
    wb = state_win_kv.shape[1]
    assert past % SLC_BLOCK == 0 and page % SLC_BLOCK == 0 and S <= SLC_BLOCK
    bf = lambda w: w.astype(BF16)
    pad8 = lambda m: jnp.pad(m, ((0, SUBLANES - m.shape[0]), (0, 0)))
    dff = ffn_w_gate.shape[2]
    tf = dff // 2 if (dff // 2) % LANES == 0 else dff
    n_p, n_s = B * T, SB * S
    both = lambda w: (bf(w), w)
    pick = lambda d, hp: {k: (v[hp] if isinstance(v, tuple) else v) for k, v in d.items()}

    rw = dict(g=norm_g[0, 0][None], mix=pad8(rw_mix[0]), vec=pad8(rw_vec[0]), wrkv=both(rw_w_rkv[0]), wo=both(rw_w_o[0]),
              dw1=both(rw_decay_w1[0]), dw2=both(rw_decay_w2[0]), aw1=both(rw_iclr_w1[0]), aw2=both(rw_iclr_w2[0]),
              gw1=both(rw_gate_w1[0]), gw2=both(rw_gate_w2[0]))
    H = D // RW_HEAD_DIM
    hp_, shift_p, wkv_p = _rwkv_layer(x_prompt, jnp.zeros((B, D), F32), jnp.zeros((B, H, RW_HEAD_DIM, RW_HEAD_DIM), F32),
                                      pick(rw, 0), tm=256, hp=False, inv_hp=True)
    hs, shift_s, wkv_s = _rwkv_layer(x_sample, state_shift[0], state_wkv[0], pick(rw, 1), tm=WKV_CHUNK, hp=True, inv_hp=True)
    ffn_g = norm_g[0, 1][None]
    hp_ = _ffn(hp_.reshape(n_p, D), ffn_g, bf(ffn_w_gate[0]), bf(ffn_w_up[0]), bf(ffn_w_down[0]), tm=512, tf=tf, hp=False)
    hs = _ffn(hs.reshape(n_s, D), ffn_g, ffn_w_gate[0], ffn_w_up[0], ffn_w_down[0], tm=n_s, tf=tf, hp=True)

    kv_consts = lambda hp: [kv_norm_g[None], both(w_kv)[hp], jnp.tile(k_norm_g, (1, N_KV_HEADS)),
                            _block_diag(KV_W, HEAD_DIM, 1.0 / HEAD_DIM)]
    kv_rows = lambda h, tm, hp: _row_call(functools.partial(_kv_proj_kernel, hp=hp), [h], kv_consts(hp),
                                          [2 * KV_W] * 3, [F32] * 3, tm, "kv_proj")
    as_rows = lambda t, b: t.reshape(b, -1, 2, N_KV_HEADS, HEAD_DIM)
    cmp_p, slc_p, win_p = (as_rows(t, B) for t in kv_rows(hp_, 512, False))
    cmp_s, slc_s, win_s = (as_rows(t, SB) for t in kv_rows(hs, n_s, True))
    win_all_s = jnp.concatenate([state_win_kv, win_s], axis=1)

    cp = dict(cmp_pe=cmp_pe.reshape(2, CMP_BLOCK // CMP_STRIDE, 1, CMP_STRIDE * HEAD_DIM),
              cmp_w1=both(cmp_w1.reshape(2, CMP_BLOCK // CMP_STRIDE, CMP_STRIDE * HEAD_DIM, HEAD_DIM)),
              cmp_w2=both(cmp_w2), k_norm_g=k_norm_g)
    kc_p, vc_p, ncmp_p = _compressed_kv(cmp_p, pick(cp, 0), hp=False)
    past_rows = cache_cmp_kv[page_table].reshape(SB, past, 2, N_KV_HEADS, HEAD_DIM)
    kc_s, vc_s, ncmp_s = _compressed_kv(jnp.concatenate([past_rows, cmp_s], axis=1), pick(cp, 1), hp=True)

    ge = np.zeros((LANES, N_BRANCH * D), np.float32)
    for br in range(N_BRANCH):
        for hd in range(N_HEADS):
            ge[br * N_HEADS + hd, br * D + hd * HEAD_DIM: br * D + (hd + 1) * HEAD_DIM] = 1.0
    nsa = dict(g_attn=norm_g[1, 0][None], wq=both(w_qg[0, :, :D]),
               wgate=both(jnp.pad(w_qg[0, :, D:], ((0, 0), (0, LANES - N_BRANCH * N_HEADS)))),
               q_norm_g=jnp.tile(q_norm_g[0], N_HEADS)[None], bd_q=_block_diag(D, HEAD_DIM, 1.0 / HEAD_DIM),
               w_o=both(w_o[0]), gate_expand=jnp.asarray(ge, dtype=BF16))

    def query_side(h, tm, hp, q_scale):
        c = pick(nsa, hp)
        return _row_call(functools.partial(_q_proj_kernel, hp=hp, q_scale=q_scale), [h],
                         [c["g_attn"], c["wq"], c["wgate"], c["q_norm_g"], c["bd_q"]], [D, LANES],
                         [F32 if hp else BF16, F32], tm, "q_proj")

    def merge(h, gates, o_cmp, o_slc, o_win, tm, hp):
        c = pick(nsa, hp)
        return _row_call(functools.partial(_merge_kernel, hp=hp), [h, gates, o_cmp, o_slc, o_win],
                         [c["gate_expand"], c["w_o"]], [D], [F32], tm, "nsa_merge")[0]

    q_p, gates_p = query_side(hp_, 512, False, ATTN_SCALE * LOG2E)
    q_p = q_p.reshape(B, T, D)
    n_slc_p = T // SLC_BLOCK
    nsp_p = LANES // N_KV_HEADS
    assert n_slc_p <= nsp_p, "the packed block-selection layout holds at most 32 selection blocks per kv group"
    m_one = _overlap_matrix(kc_p.shape[1], ncmp_p, n_slc_p, nsp_p)
    m_p = jnp.stack([jnp.pad(m_one, ((0, 0), (g * nsp_p, LANES - (g + 1) * nsp_p))) for g in range(N_KV_HEADS)])
    heads = lambda t: t.reshape(t.shape[0], t.shape[1], N_KV_HEADS, HEAD_DIM)
    kcab, vc1 = _pair_keys(heads(kc_p)), _ones_values(heads(vc_p))
    skab, sv1 = _pair_keys(bf(slc_p[:, :, 0])), _ones_values(bf(slc_p[:, :, 1]))
    wkab, wv1 = _pair_keys(bf(win_p[:, :, 0])), _ones_values(bf(win_p[:, :, 1]))

    def prompt_branches(bounded):
        o_cmp, sel = _cmp_sel(q_p, kcab, vc1, m_p, n_cmp=ncmp_p, n_slc=n_slc_p, tq=256, bounded=bounded)
        return (o_cmp,) + tuple(_slc_win_attn(q_p, sel, skab, sv1, wkab, wv1, tq=128, tk=256, bounded=bounded))

    score_bound = HEAD_DIM * ATTN_SCALE * jnp.max(jnp.abs(q_norm_g[0])) * jnp.max(jnp.abs(k_norm_g))
    o_cmp_p, o_slc_p, o_win_p = lax.cond(score_bound <= SCORE_BOUND_MAX, lambda: prompt_branches(True),
                                         lambda: prompt_branches(False))
    hp_ = merge(hp_, gates_p, o_cmp_p.reshape(n_p, D), o_slc_p.reshape(n_p, D), o_win_p.reshape(n_p, D), 512, False)

    kv_lanes = lambda t, dt: t.reshape(t.shape[0], t.shape[1], 2 * KV_W).astype(dt)
    q_s, gates_s = query_side(hs, n_s, True, ATTN_SCALE)
    q_s = _pad_rows(q_s.reshape(SB, S, D), SUBLANES)
    n_slc_s = (past + S - 1) // SLC_BLOCK + 1
    nsp_s = -(-n_slc_s // LANES) * LANES
    m_s = _overlap_matrix(kc_s.shape[1], ncmp_s, n_slc_s, nsp_s)
    o_cmp_s, sel_s = _cmp_attn(q_s, kc_s, vc_s, m_s, n_cmp=ncmp_s, n_slc=n_slc_s, pos_base=past, per_tile=False,
                               tq=SUBLANES, hp=True)
    n_sel = min(SLC_TOP, n_slc_s)
    sel_s = sel_s.reshape(SB, SUBLANES, N_KV_HEADS, nsp_s)[:, :S, :, :n_slc_s] > 0.5
    slot = jnp.cumsum(sel_s, axis=-1) - 1
    idx_s = jnp.sum(jnp.where(sel_s[..., None] & (slot[..., None] == jnp.arange(n_sel)),
                              jnp.arange(n_slc_s)[:, None], 0), axis=-2).astype(jnp.int32)
    in_pool = idx_s < past // SLC_BLOCK
    per_page = page // SLC_BLOCK
    blk_c = jnp.minimum(idx_s, past // SLC_BLOCK - 1)
    pages = jnp.take_along_axis(page_table, (blk_c // per_page).reshape(SB, -1), axis=1).reshape(idx_s.shape)
    pool_blk = pages * per_page + blk_c % per_page
    g_ix = jnp.arange(N_KV_HEADS)[None, None, :, None]
    cache_blocks = cache_slc_kv.reshape(-1, SLC_BLOCK, 2, N_KV_HEADS, HEAD_DIM)
    from_pool = cache_blocks[pool_blk, :, :, g_ix, :]
    new_rows = slc_s[:, jnp.clip(jnp.arange(SLC_BLOCK), 0, S - 1)]
    from_new = jnp.transpose(new_rows, (0, 3, 1, 2, 4))[:, None, :, None]
    kv_sel = jnp.where(in_pool[..., None, None, None], from_pool, from_new)
    kv_sel = kv_sel.reshape(SB, S, N_KV_HEADS, n_sel * SLC_BLOCK, 2 * HEAD_DIM)
    kpos = (idx_s[..., None] * SLC_BLOCK + jnp.arange(SLC_BLOCK)).reshape(SB, S * N_KV_HEADS, n_sel * SLC_BLOCK).astype(jnp.int32)
    kpos = jnp.pad(kpos, ((0, 0), (0, -(-S * N_KV_HEADS // SUBLANES) * SUBLANES - S * N_KV_HEADS), (0, 0)))
    n_win = wb + S
    nwp = -(-n_win // SUBLANES) * SUBLANES
    win_lanes = _pad_rows(kv_lanes(win_all_s, F32), nwp)
    o_slc_s, o_win_s = _sample_attn(q_s, kv_sel, kpos, win_lanes, n_q=S, pos_base=past, win_base=past - wb,
                                    n_win=n_win, hp=True)
    unpad = lambda t: t[:, :S].reshape(n_s, D)
    hs = merge(hs, gates_s, unpad(o_cmp_s), unpad(o_slc_s), unpad(o_win_s), n_s, True)

    moe = dict(g_moe=norm_g[1, 1][None], w_router=jnp.pad(moe_router[0], ((0, 0), (0, LANES - N_EXPERTS))),
               moe_wg=both(moe_w_gate[0]), moe_wu=both(moe_w_up[0]), moe_wd=both(moe_w_down[0]))
    hp_ = _moe(hp_, pick(moe, 0), tm=512, tf=tf, hp=False).reshape(B, T, D)
    hs = _moe(hs, pick(moe, 1), tm=128, tf=tf, hp=True).reshape(SB, S, D)

    keep_p = min(WINDOW, T)
    return (hp_, hs, cmp_p, cmp_s, slc_p, slc_s, win_p[:, T - keep_p:], win_all_s[:, win_all_s.shape[1] - wb:],
            wkv_p[None], wkv_s[None], shift_p[None], shift_s[None])
```

```python
import functools

import numpy as np
import jax
import jax.numpy as jnp
from jax import lax
from jax.experimental import pallas as pl
from jax.experimental.pallas import tpu as pltpu

F32 = jnp.float32
BF16 = jnp.bfloat16

RW_HEAD_DIM = 64
RW_GN_EPS = 64e-5
N_HEADS = 16
HEAD_DIM = 64
N_KV_HEADS = 4
HEADS_PER_KV = N_HEADS // N_KV_HEADS
KV_W = N_KV_HEADS * HEAD_DIM
N_BRANCH = 3
CMP_BLOCK = 32
CMP_STRIDE = 16
SLC_BLOCK = 64
SLC_TOP = 16
WINDOW = 512
ATTN_SCALE = HEAD_DIM ** -0.5
N_EXPERTS = 8
TOP_K = 2
RMS_EPS = 1e-6
MASKED = -1e30
M_INIT = -1e20
BIG = 1e30
LOG2E = 1.4426950408889634
SCORE_BOUND_MAX = 40.0

SUBLANES = 8
LANES = 128
VMEM_LIMIT = 56 * 1024 * 1024
WKV_CHUNK = RW_HEAD_DIM


def _cparams(sem):
    return pltpu.CompilerParams(dimension_semantics=sem, vmem_limit_bytes=VMEM_LIMIT)


_DIMS = {"nn": (((1,), (0,)), ((), ())), "nt": (((1,), (1,)), ((), ())), "tn": (((0,), (0,)), ((), ()))}


def _split2(x):
    hi = x.astype(BF16)
    lo = (x - hi.astype(F32)).astype(BF16)
    return hi, lo


def _mm(a, b, form="nn", hp=False):
    d = lambda s, t: lax.dot_general(s, t, _DIMS[form], preferred_element_type=F32)
    if not hp:
        return d(a.astype(BF16), b.astype(BF16))
    ah, al = _split2(a.astype(F32))
    bh, bl = _split2(b.astype(F32))
    return d(ah, bh) + d(ah, bl) + d(al, bh)


def _dot_hilo(x, m):
    hi, lo = _split2(x)
    return jnp.dot(hi, m, preferred_element_type=F32) + jnp.dot(lo, m, preferred_element_type=F32)


def _dot_exact_rhs(m, x):
    hi = x.astype(BF16)
    r1 = x - hi.astype(F32)
    mid = r1.astype(BF16)
    lo = (r1 - mid.astype(F32)).astype(BF16)
    d = lambda t: jnp.dot(m, t, preferred_element_type=F32)
    return d(hi) + d(mid) + d(lo)


def _rms(x, g):
    return x * lax.rsqrt(jnp.mean(x * x, -1, keepdims=True) + RMS_EPS) * g


def _silu(x):
    return x * jax.nn.sigmoid(x)


def _softplus(z):
    return jnp.maximum(z, 0.0) + jnp.log(1.0 + jnp.exp(-jnp.abs(z)))


def _block_diag(n, seg, value):
    i = np.arange(n)
    return jnp.asarray((i[:, None] // seg == i[None, :] // seg).astype(np.float32) * value, dtype=BF16)


def _rwkv_proj_kernel(x_ref, halo_ref, sh_ref, g_ref, mix_ref, vec_ref, wrkv_ref, dw1_ref, dw2_ref, aw1_ref, aw2_ref,
                      gw1_ref, gw2_ref, r_o, lw_o, k_o, v_o, a_o, g_o, shift_o, *, last_tile, last_row, hp):
    i = pl.program_id(1)
    mm = functools.partial(_mm, hp=hp)
    g = g_ref[...]
    xn = _rms(x_ref[0], g)
    hn = _rms(halo_ref[0, SUBLANES - 1:SUBLANES, :], g)
    prev_last = jnp.where(i == 0, sh_ref[0], hn)
    row = lax.broadcasted_iota(jnp.int32, xn.shape, 0)
    prev = jnp.where(row == 0, prev_last, pltpu.roll(xn, 1, 0))
    xx = prev - xn
    mixed = lambda j: xn + xx * mix_ref[j:j + 1, :]
    r_o[0] = mm(mixed(0), wrkv_ref[0])
    k_o[0] = mm(mixed(2), wrkv_ref[1])
    v_o[0] = mm(mixed(3), wrkv_ref[2])
    w0 = vec_ref[0:1, :]
    a0 = vec_ref[1:2, :]
    log_w = -_softplus(-(w0 + mm(jnp.tanh(mm(mixed(1), dw1_ref[...])), dw2_ref[...]))) - 0.5
    lw_o[0] = -jnp.exp(log_w)
    a_o[0] = jax.nn.sigmoid(a0 + mm(mm(mixed(4), aw1_ref[...]), aw2_ref[...]))
    g_o[0] = mm(jax.nn.sigmoid(mm(mixed(5), gw1_ref[...])), gw2_ref[...])

    @pl.when(i == last_tile)
    def _():
        shift_o[0] = xn[last_row:last_row + 1, :]


def _rwkv_proj(x, shift0, g, mix, vec, wrkv, dw1, dw2, aw1, aw2, gw1, gw2, *, t_real, tm, hp):
    B, T, D = x.shape
    nt = T // tm
    row_spec = pl.BlockSpec((1, tm, D), lambda b, i: (b, i, 0))
    halo_spec = pl.BlockSpec((1, SUBLANES, D), lambda b, i: (b, jnp.maximum(i * (tm // SUBLANES) - 1, 0), 0))
    vec_spec = pl.BlockSpec((1, 1, D), lambda b, i: (b, 0, 0))
    full = lambda a: pl.BlockSpec(a.shape, lambda b, i: (0,) * a.ndim)
    consts = (g, mix, vec, wrkv, dw1, dw2, aw1, aw2, gw1, gw2)
    out_sd = jax.ShapeDtypeStruct((B, T, D), F32)
    kern = functools.partial(_rwkv_proj_kernel, last_tile=(t_real - 1) // tm, last_row=(t_real - 1) % tm, hp=hp)
    return pl.pallas_call(
        kern, grid=(B, nt),
        in_specs=[row_spec, halo_spec, vec_spec] + [full(c) for c in consts],
        out_specs=[row_spec] * 6 + [vec_spec],
        out_shape=[out_sd] * 6 + [jax.ShapeDtypeStruct((B, 1, D), F32)],
        compiler_params=_cparams(("parallel", "arbitrary")), name="rwkv_proj",
    )(x, x, shift0, *consts)


def _wkv_kernel(r_ref, lw_ref, k_ref, v_ref, a_ref, vec_ref, ones_ref, s0_ref, y_o, sT_o, s_sc, *, n_pairs, hp, inv_hp):
    c = pl.program_id(1)
    C = r_ref.shape[1]
    W = 2 * C
    mm = functools.partial(_mm, hp=hp)
    mm_inv = functools.partial(_mm, hp=inv_hp)

    @pl.when(c == 0)
    def _():
        s_sc[...] = s0_ref[0]

    lane = lax.broadcasted_iota(jnp.int32, (C, W), 1)
    row = lax.broadcasted_iota(jnp.int32, (C, W), 0)
    left = lane < C
    col = jnp.where(left, lane, lane - C)
    strict = col < row
    incl = col <= row
    diag_blocks = (lax.broadcasted_iota(jnp.int32, (W, W), 0) < C) == (lax.broadcasted_iota(jnp.int32, (W, W), 1) < C)
    bd = lambda x: jnp.concatenate([jnp.where(left, x, 0.0), jnp.where(left, 0.0, x)], axis=0)
    seg_sum = lambda x: _dot_hilo(x, ones_ref[...])
    rows = lambda x, p: x[p * C:(p + 1) * C]
    tri = jnp.where(lax.broadcasted_iota(jnp.int32, (C, C), 1) <= lax.broadcasted_iota(jnp.int32, (C, C), 0), 1.0, 0.0)
    cum_all = _dot_exact_rhs(tri.astype(BF16), lw_ref[0])
    pairs = range(n_pairs)
    ps = lambda p: slice(p * W, (p + 1) * W)

    r = [r_ref[0, :, ps(p)] for p in pairs]
    v = [v_ref[0, :, ps(p)] for p in pairs]
    kkr = [k_ref[0, :, ps(p)] * vec_ref[2:3, ps(p)] for p in pairs]
    ss = seg_sum(jnp.concatenate([x * x for x in kkr], axis=0))
    k2, at, rt, bt, kt, w_end = [], [], [], [], [], []
    for p in pairs:
        a = a_ref[0, :, ps(p)]
        cum = cum_all[:, ps(p)]
        kk = kkr[p] * lax.rsqrt(jnp.maximum(rows(ss, p), 1e-24))
        k2.append(k_ref[0, :, ps(p)] * (1.0 + (a - 1.0) * vec_ref[3:4, ps(p)]))
        e_pos = jnp.exp(cum)
        e_neg = jnp.exp(-cum)
        rt.append(r[p] * e_pos)
        at.append(-kk * jnp.exp(cum - lw_ref[0, :, ps(p)]))
        bt.append(kk * a * e_neg)
        kt.append(k2[p] * e_neg)
        w_end.append(e_pos[C - 1:C, :])

    P, U, m_rbk, x_r = [], [], [], []
    for p in pairs:
        ar = jnp.concatenate([at[p], rt[p]], axis=0)
        big = mm(ar, jnp.concatenate([bd(bt[p]), bd(kt[p])], axis=0), "nt")
        x = mm(ar, s_sc[p], "nt")
        P.append(jnp.where(strict, big[:C, :W], 0.0))
        l_ak = jnp.where(strict, big[:C, W:], 0.0)
        m_rbk.append(jnp.concatenate([jnp.where(incl, big[C:, :W], 0.0), jnp.where(incl, big[C:, W:], 0.0)], axis=1))
        U.append(x[:C] + mm(l_ak, bd(v[p])))
        x_r.append(x[C:])

    n_it = int(np.log2(C))
    for it in range(n_it):
        for p in pairs:
            if it == n_it - 1:
                U[p] = U[p] + mm_inv(P[p], bd(U[p]))
            else:
                res = mm_inv(P[p], jnp.concatenate([bd(P[p]), bd(U[p])], axis=1))
                U[p] = U[p] + res[:, W:]
                P[p] = res[:, :W]

    y = []
    for p in pairs:
        y.append(x_r[p] + mm(m_rbk[p], jnp.concatenate([bd(U[p]), bd(v[p])], axis=0)))
        upd = mm(jnp.concatenate([U[p], v[p]], axis=0),
                 jnp.concatenate([bt[p] * w_end[p], kt[p] * w_end[p]], axis=0), "tn")
        s_sc[p] = s_sc[p] * w_end[p] + jnp.where(diag_blocks, upd, 0.0)

    inv_n = 1.0 / RW_HEAD_DIM
    y_all = jnp.concatenate(y, axis=0)
    mu = seg_sum(y_all) * inv_n
    var = seg_sum(jnp.square(y_all - mu)) * inv_n
    rk = seg_sum(jnp.concatenate([r[p] * k2[p] * vec_ref[4:5, ps(p)] for p in pairs], axis=0))
    yn = (y_all - mu) * lax.rsqrt(var + RW_GN_EPS)
    for p in pairs:
        y_o[0, :, ps(p)] = rows(yn, p) * vec_ref[5:6, ps(p)] + vec_ref[6:7, ps(p)] + rows(rk, p) * v[p]

    @pl.when(c == pl.num_programs(1) - 1)
    def _():
        sT_o[0] = s_sc[...]


def _wkv(r, lw, k, v, a, vec, s0, *, hp, inv_hp):
    B, T, D = r.shape
    H = D // RW_HEAD_DIM
    C, W, n_pairs = WKV_CHUNK, 2 * RW_HEAD_DIM, H // 2
    s_bd = jnp.zeros((B, n_pairs, W, W), F32)
    s_bd = s_bd.at[:, :, :C, :C].set(s0[:, 0::2]).at[:, :, C:, C:].set(s0[:, 1::2])
    row_spec = pl.BlockSpec((1, C, D), lambda b, c: (b, c, 0))
    st_spec = pl.BlockSpec((1, n_pairs, W, W), lambda b, c: (b, 0, 0, 0))
    ones = _block_diag(W, RW_HEAD_DIM, 1.0)
    kern = functools.partial(_wkv_kernel, n_pairs=n_pairs, hp=hp, inv_hp=inv_hp)
    y, s_out = pl.pallas_call(
        kern, grid=(B, T // C),
        in_specs=[row_spec] * 5 + [pl.BlockSpec(vec.shape, lambda b, c: (0, 0)), pl.BlockSpec((W, W), lambda b, c: (0, 0)),
                                   st_spec],
        out_specs=[row_spec, st_spec],
        out_shape=[jax.ShapeDtypeStruct((B, T, D), F32), jax.ShapeDtypeStruct(s_bd.shape, F32)],
        scratch_shapes=[pltpu.VMEM((n_pairs, W, W), F32)],
        compiler_params=_cparams(("parallel", "arbitrary")), name="wkv_chunk",
    )(r, lw, k, v, a, vec, ones, s_bd)
    s_fin = jnp.stack([s_out[:, :, :C, :C], s_out[:, :, C:, C:]], axis=2).reshape(B, H, RW_HEAD_DIM, RW_HEAD_DIM)
    return y, s_fin


def _row_call(kernel, rows, consts, out_widths, out_dtypes, tm, name):
    n = rows[0].shape[0]
    row_spec = lambda w: pl.BlockSpec((tm, w), lambda i: (i, 0))
    full = lambda a: pl.BlockSpec(a.shape, lambda i: (0,) * a.ndim)
    return pl.pallas_call(
        kernel, grid=(n // tm,),
        in_specs=[row_spec(a.shape[1]) for a in rows] + [full(c) for c in consts],
        out_specs=[row_spec(w) for w in out_widths],
        out_shape=[jax.ShapeDtypeStruct((n, w), dt) for w, dt in zip(out_widths, out_dtypes)],
        compiler_params=_cparams(("parallel",)), name=name,
    )(*rows, *consts)


def _rwkv_out_kernel(x_ref, y_ref, g_ref, wo_ref, o_ref, *, hp):
    o_ref[...] = x_ref[...] + _mm(y_ref[...] * g_ref[...], wo_ref[...], hp=hp)


def _kv_proj_kernel(h_ref, g_ref, wkv_ref, kg_ref, bd_ref, cmp_o, slc_o, win_o, *, hp):
    kv = _mm(_rms(h_ref[...], g_ref[...]), wkv_ref[...], hp=hp)
    cmp_o[...] = kv[:, :2 * KV_W]
    for br, out in ((1, slc_o), (2, win_o)):
        kraw = kv[:, br * 2 * KV_W: br * 2 * KV_W + KV_W]
        ms = _dot_hilo(kraw * kraw, bd_ref[...])
        out[:, :KV_W] = kraw * lax.rsqrt(ms + RMS_EPS) * kg_ref[br:br + 1, :]
        out[:, KV_W:] = kv[:, br * 2 * KV_W + KV_W: (br + 1) * 2 * KV_W]


def _q_proj_kernel(h_ref, g_ref, wq_ref, wg_ref, qg_ref, bd_ref, q_o, gate_o, *, hp, q_scale):
    hn = _rms(h_ref[...], g_ref[...])
    q = _mm(hn, wq_ref[...], hp=hp)
    ms = _dot_hilo(q * q, bd_ref[...])
    q_o[...] = (q * lax.rsqrt(ms + RMS_EPS) * qg_ref[...] * q_scale).astype(q_o.dtype)
    gate_o[...] = jax.nn.sigmoid(_mm(hn, wg_ref[...], hp=hp))


def _merge_kernel(h_ref, gate_ref, oc_ref, os_ref, ow_ref, eg_ref, wo_ref, o_ref, *, hp):
    D = h_ref.shape[1]
    ge = _dot_hilo(gate_ref[...], eg_ref[...])
    o = ge[:, :D] * oc_ref[...] + ge[:, D:2 * D] * os_ref[...] + ge[:, 2 * D:] * ow_ref[...]
    o_ref[...] = h_ref[...] + _mm(o, wo_ref[...], hp=hp)


def _router_kernel(h_ref, g_ref, wr_ref, xn_o, logit_o):
    xn = _rms(h_ref[...], g_ref[...])
    xn_o[...] = xn.astype(xn_o.dtype)
    logit_o[...] = _mm(xn, wr_ref[...], hp=True)


def _ffn_kernel(x_ref, g_ref, wg_ref, wu_ref, wd_ref, o_ref, xn_sc, acc_sc, *, hp):
    f = pl.program_id(1)

    @pl.when(f == 0)
    def _():
        xn_sc[...] = _rms(x_ref[...], g_ref[...]).astype(xn_sc.dtype)
        acc_sc[...] = jnp.zeros_like(acc_sc)

    xn = xn_sc[...]
    hid = _silu(_mm(xn, wg_ref[...], hp=hp)) * _mm(xn, wu_ref[...], hp=hp)
    acc_sc[...] += _mm(hid, wd_ref[...], hp=hp)

    @pl.when(f == pl.num_programs(1) - 1)
    def _():
        o_ref[...] = x_ref[...] + acc_sc[...]


def _ffn(x, g, wg, wu, wd, *, tm, tf, hp):
    n, D = x.shape
    dff = wg.shape[1]
    return pl.pallas_call(
        functools.partial(_ffn_kernel, hp=hp), grid=(n // tm, dff // tf),
        in_specs=[pl.BlockSpec((tm, D), lambda i, f: (i, 0)), pl.BlockSpec((1, D), lambda i, f: (0, 0)),
                  pl.BlockSpec((D, tf), lambda i, f: (0, f)), pl.BlockSpec((D, tf), lambda i, f: (0, f)),
                  pl.BlockSpec((tf, D), lambda i, f: (f, 0))],
        out_specs=pl.BlockSpec((tm, D), lambda i, f: (i, 0)),
        out_shape=jax.ShapeDtypeStruct((n, D), F32),
        scratch_shapes=[pltpu.VMEM((tm, D), F32 if hp else BF16), pltpu.VMEM((tm, D), F32)],
        compiler_params=_cparams(("parallel", "arbitrary")), name="ffn_dense",
    )(x, g, wg, wu, wd)


def _moe_ffn_kernel(be_ref, nb_ref, x_ref, wg_ref, wu_ref, wd_ref, o_ref, acc_sc, *, hp):
    i = pl.program_id(0)
    f = pl.program_id(1)

    @pl.when(i < nb_ref[0])
    def _():
        @pl.when(f == 0)
        def _():
            acc_sc[...] = jnp.zeros_like(acc_sc)

        x = x_ref[...]
        hid = _silu(_mm(x, wg_ref[0], hp=hp)) * _mm(x, wu_ref[0], hp=hp)
        acc_sc[...] += _mm(hid, wd_ref[0], hp=hp)

        @pl.when(f == pl.num_programs(1) - 1)
        def _():
            o_ref[...] = acc_sc[...]

    @pl.when((i >= nb_ref[0]) & (f == pl.num_programs(1) - 1))
    def _():
        o_ref[...] = jnp.zeros_like(o_ref)


def _moe_ffn(block_e, n_used, xs, wg, wu, wd, *, tm, tf, hp):
    cap, D = xs.shape
    dff = wg.shape[2]
    w_idx = lambda i, f, be, nb: jnp.where(i < nb[0], f, dff // tf - 1)
    grid_spec = pltpu.PrefetchScalarGridSpec(
        num_scalar_prefetch=2, grid=(cap // tm, dff // tf),
        in_specs=[pl.BlockSpec((tm, D), lambda i, f, be, nb: (i, 0)),
                  pl.BlockSpec((1, D, tf), lambda i, f, be, nb: (be[i], 0, w_idx(i, f, be, nb))),
                  pl.BlockSpec((1, D, tf), lambda i, f, be, nb: (be[i], 0, w_idx(i, f, be, nb))),
                  pl.BlockSpec((1, tf, D), lambda i, f, be, nb: (be[i], w_idx(i, f, be, nb), 0))],
        out_specs=pl.BlockSpec((tm, D), lambda i, f, be, nb: (i, 0)),
        scratch_shapes=[pltpu.VMEM((tm, D), F32)])
    return pl.pallas_call(
        functools.partial(_moe_ffn_kernel, hp=hp), grid_spec=grid_spec, out_shape=jax.ShapeDtypeStruct((cap, D), F32),
        compiler_params=_cparams(("arbitrary", "arbitrary")), name="moe_ffn",
    )(block_e, n_used, xs, wg, wu, wd)


def _compress_kernel(x_ref, pe_ref, w1_ref, w2_ref, kg_ref, o_ref, *, hp):
    c = pl.program_id(1)
    x = x_ref[0, 0, 0]
    R = x.shape[0]
    p0 = _mm(x + pe_ref[0, 0], w1_ref[0, 0], hp=hp)
    p1 = _mm(x + pe_ref[0, 1], w1_ref[0, 1], hp=hp)
    pre = p0 + pltpu.roll(p1, R - 1, 0)
    out = _mm(_silu(pre), w2_ref[0], hp=hp)
    o_ref[0, 0, 0] = jnp.where(c == 0, _rms(out, kg_ref[...]), out)


def _compress(xsub, pe, w1, w2, kg, *, hp):
    S, _, G, R, W = xsub.shape
    return pl.pallas_call(
        functools.partial(_compress_kernel, hp=hp), grid=(S, 2, G),
        in_specs=[pl.BlockSpec((1, 1, 1, R, W), lambda s, c, g: (s, c, g, 0, 0)),
                  pl.BlockSpec((1, 2, 1, W), lambda s, c, g: (c, 0, 0, 0)),
                  pl.BlockSpec((1, 2, W, HEAD_DIM), lambda s, c, g: (c, 0, 0, 0)),
                  pl.BlockSpec((1, HEAD_DIM, HEAD_DIM), lambda s, c, g: (c, 0, 0)),
                  pl.BlockSpec((1, HEAD_DIM), lambda s, c, g: (0, 0))],
        out_specs=pl.BlockSpec((1, 1, 1, R, HEAD_DIM), lambda s, c, g: (s, c, g, 0, 0)),
        out_shape=jax.ShapeDtypeStruct((S, 2, G, R, HEAD_DIM), F32),
        compiler_params=_cparams(("parallel", "arbitrary", "arbitrary")), name="kv_compress",
    )(xsub, pe, w1, w2, kg)


def _paged_sub_proj_kernel(pt_ref, cache_hbm, w_ref, o_ref, buf, sem, *, pages_per_step, page, hp):
    lin = pl.program_id(0) * pl.num_programs(1) + pl.program_id(1)
    total = pl.num_programs(0) * pl.num_programs(1)
    slot = lin % 2

    n_lane_pairs = buf.shape[1]

    def page_copies(step, into):
        return [pltpu.make_async_copy(cache_hbm.at[pt_ref[step * pages_per_step + p], pl.ds(0, page), pl.ds(pair * LANES, LANES)],
                                      buf.at[into, pair, pl.ds(p * page, page)], sem.at[into])
                for p in range(pages_per_step) for pair in range(n_lane_pairs)]

    @pl.when(lin == 0)
    def _():
        for cp in page_copies(0, 0):
            cp.start()

    @pl.when(lin + 1 < total)
    def _():
        for cp in page_copies(lin + 1, 1 - slot):
            cp.start()

    for cp in page_copies(lin, slot):
        cp.wait()
    n = pages_per_step * page // CMP_STRIDE
    for pair in range(n_lane_pairs):
        c = pair // (N_KV_HEADS // 2)
        rows = buf.at[slot, pair]
        x2 = jnp.concatenate([rows[pl.ds(s, n, stride=CMP_STRIDE), :] for s in range(CMP_STRIDE)], axis=1)
        o_ref[0, pair] = _mm(x2, w_ref[c], hp=hp and c == 0)


def _paged_sub_proj(page_table, cache, w2x, *, pages_per_step, hp):
    S, n_pages = page_table.shape
    page, width = cache.shape[1:]
    n = pages_per_step * page // CMP_STRIDE
    n_t = n_pages // pages_per_step
    grid_spec = pltpu.PrefetchScalarGridSpec(
        num_scalar_prefetch=1, grid=(S, n_t),
        in_specs=[pl.BlockSpec(memory_space=pl.ANY), pl.BlockSpec(w2x.shape, lambda s, t, pt: (0, 0, 0))],
        out_specs=pl.BlockSpec((1, 4, n, w2x.shape[2]), lambda s, t, pt: (s, 0, t, 0)),
        scratch_shapes=[pltpu.VMEM((2, width // LANES, pages_per_step * page, LANES), F32), pltpu.SemaphoreType.DMA((2,))])
    kern = functools.partial(_paged_sub_proj_kernel, pages_per_step=pages_per_step, page=page, hp=hp)
    return pl.pallas_call(
        kern, grid_spec=grid_spec, out_shape=jax.ShapeDtypeStruct((S, 4, n_t * n, w2x.shape[2]), F32),
        compiler_params=_cparams(("arbitrary", "arbitrary")), name="paged_sub_proj",
    )(page_table.reshape(-1), cache, w2x)


def _compress_tail_kernel(p_ref, pe_ref, w1_ref, w2_ref, kg_ref, o_ref, *, hp):
    R = p_ref.shape[2]
    half_groups = N_KV_HEADS // 2
    for c in range(2):
        bias = (_mm(jnp.broadcast_to(pe_ref[c, 0], (SUBLANES, pe_ref.shape[3])), w1_ref[c, 0], hp=True)
                + _mm(jnp.broadcast_to(pe_ref[c, 1], (SUBLANES, pe_ref.shape[3])), w1_ref[c, 1], hp=True))[0:1]
        outs = []
        for g in range(N_KV_HEADS):
            p = p_ref[0, c * half_groups + g // 2]
            e = g % 2
            p0 = p[:, e * HEAD_DIM:(e + 1) * HEAD_DIM]
            p1 = p[:, (2 + e) * HEAD_DIM:(3 + e) * HEAD_DIM]
            out = _mm(_silu(p0 + pltpu.roll(p1, R - 1, 0) + bias), w2_ref[c], hp=hp)
            outs.append(_rms(out, kg_ref[...]) if c == 0 else out)
        o_ref[0, c] = jnp.concatenate(outs, axis=1)


def _compress_tail(p, pe, w1, w2, kg, *, hp):
    S, _, R, _ = p.shape
    full = lambda a: pl.BlockSpec(a.shape, lambda s: (0,) * a.ndim)
    return pl.pallas_call(
        functools.partial(_compress_tail_kernel, hp=hp), grid=(S,),
        in_specs=[pl.BlockSpec((1,) + p.shape[1:], lambda s: (s, 0, 0, 0)), full(pe), full(w1), full(w2), full(kg)],
        out_specs=pl.BlockSpec((1, 2, R, KV_W), lambda s: (s, 0, 0, 0)),
        out_shape=jax.ShapeDtypeStruct((S, 2, R, KV_W), F32),
        compiler_params=_cparams(("parallel",)), name="kv_compress_tail",
    )(p, pe, w1, w2, kg)


def _stack_heads(q, g, tq):
    return jnp.concatenate([q[:, (HEADS_PER_KV * g + i) * HEAD_DIM:(HEADS_PER_KV * g + i + 1) * HEAD_DIM]
                            for i in range(HEADS_PER_KV)], axis=0)


def _cmp_attn_kernel(q_ref, kc_ref, vc_ref, m_ref, o_o, sel_o, *, n_cmp, n_slc, pos_base, per_tile, hp):
    tq = q_ref.shape[1]
    ncp = kc_ref.shape[1]
    nsp = m_ref.shape[1]
    q0 = pos_base + (pl.program_id(1) * tq if per_tile else 0)
    rows = HEADS_PER_KV * tq
    q_pos = q0 + lax.broadcasted_iota(jnp.int32, (rows, ncp), 0) % tq
    n_id = lax.broadcasted_iota(jnp.int32, (rows, ncp), 1)
    bias = jnp.where((n_id * CMP_STRIDE + CMP_BLOCK - 1 <= q_pos) & (n_id < n_cmp), 0.0, MASKED)
    blk = lax.broadcasted_iota(jnp.int32, (tq, nsp), 1)
    cur = (q0 + lax.broadcasted_iota(jnp.int32, (tq, nsp), 0)) // SLC_BLOCK
    forced = (blk == 0) | (blk == cur) | (blk == cur - 1)
    q = q_ref[0]
    scores = []
    for g in range(N_KV_HEADS):
        qs = _stack_heads(q, g, tq)
        s = _mm(qs, kc_ref[0, :, g * HEAD_DIM:(g + 1) * HEAD_DIM], "nt", hp) + bias
        m = jnp.maximum(jnp.max(s, -1, keepdims=True), M_INIT)
        e = jnp.exp(s - m)
        p = e / jnp.maximum(jnp.sum(e, -1, keepdims=True), 1e-30)
        o = _mm(p, vc_ref[0, :, g * HEAD_DIM:(g + 1) * HEAD_DIM], hp=hp)
        psum = p[0:tq]
        for i in range(HEADS_PER_KV):
            col = (HEADS_PER_KV * g + i) * HEAD_DIM
            o_o[0, :, col:col + HEAD_DIM] = o[i * tq:(i + 1) * tq]
            if i:
                psum = psum + p[i * tq:(i + 1) * tq]
        imp = _dot_hilo(psum, m_ref[...])
        scores.append(jnp.where(blk <= cur, jnp.where(forced, BIG, imp), -BIG))
    score = jnp.concatenate(scores, axis=0)
    blk_r = lax.broadcasted_iota(jnp.int32, score.shape, 1)

    def count_ahead(i, rank):
        s_i = jnp.sum(jnp.where(blk_r == i, score, 0.0), axis=-1, keepdims=True)
        return rank + jnp.where(s_i > score, 1.0, jnp.where((s_i == score) & (i < blk_r), 1.0, 0.0))

    rank = lax.fori_loop(0, n_slc, count_ahead, jnp.zeros(score.shape, F32), unroll=8)
    for g in range(N_KV_HEADS):
        sel_o[0, :, g * nsp:(g + 1) * nsp] = jnp.where(rank[g * tq:(g + 1) * tq] < min(SLC_TOP, n_slc), 1.0, 0.0).astype(sel_o.dtype)


def _cmp_attn(q, kc, vc, m, *, n_cmp, n_slc, pos_base, per_tile, tq, hp):
    B, T, D = q.shape
    ncp, nsp = m.shape
    kv_spec = pl.BlockSpec((1, ncp, KV_W), lambda b, i: (b, 0, 0))
    kern = functools.partial(_cmp_attn_kernel, n_cmp=n_cmp, n_slc=n_slc, pos_base=pos_base, per_tile=per_tile, hp=hp)
    return pl.pallas_call(
        kern, grid=(B, T // tq),
        in_specs=[pl.BlockSpec((1, tq, D), lambda b, i: (b, i, 0)), kv_spec, kv_spec,
                  pl.BlockSpec((ncp, nsp), lambda b, i: (0, 0))],
        out_specs=[pl.BlockSpec((1, tq, D), lambda b, i: (b, i, 0)),
                   pl.BlockSpec((1, tq, N_KV_HEADS * nsp), lambda b, i: (b, i, 0))],
        out_shape=[jax.ShapeDtypeStruct((B, T, D), F32), jax.ShapeDtypeStruct((B, T, N_KV_HEADS * nsp), BF16)],
        compiler_params=_cparams(("parallel", "arbitrary")), name="cmp_attn",
    )(q, kc, vc, m)


def _pair_rows(q, g):
    base = g * HEADS_PER_KV * HEAD_DIM
    return jnp.concatenate([q[:, base:base + LANES], q[:, base + LANES:base + 2 * LANES]], axis=0)


def _finish_pair(acc_even, acc_odd, p, tq):
    a0 = acc_even[p * tq:(p + 1) * tq]
    a1 = acc_odd[p * tq:(p + 1) * tq]
    o0 = a0 / jnp.maximum(a0[:, HEAD_DIM:HEAD_DIM + 1], 1e-30)
    o1 = a1 / jnp.maximum(a1[:, HEAD_DIM:HEAD_DIM + 1], 1e-30)
    lane = lax.broadcasted_iota(jnp.int32, (tq, LANES), 1)
    return jnp.where(lane < HEAD_DIM, o0, pltpu.roll(o1, HEAD_DIM, 1))


def _cmp_sel_kernel(q_ref, kab_ref, v1_ref, m_ref, o_o, sel_o, *, n_cmp, n_slc, bounded):
    tq = q_ref.shape[1]
    ncp = kab_ref.shape[1]
    nsp = LANES // N_KV_HEADS
    q0 = pl.program_id(1) * tq
    q_pos = q0 + lax.broadcasted_iota(jnp.int32, (tq, ncp), 0)
    n_id = lax.broadcasted_iota(jnp.int32, (tq, ncp), 1)
    b = jnp.where((n_id * CMP_STRIDE + CMP_BLOCK - 1 <= q_pos) & (n_id < n_cmp), 0.0, MASKED)
    b2 = jnp.concatenate([b, b], axis=0)
    q = q_ref[0]
    imp = jnp.zeros((tq, LANES), F32)
    for g in range(N_KV_HEADS):
        lhs = _pair_rows(q, g)
        vv = v1_ref[0, :, g * LANES:(g + 1) * LANES]
        accs, psum = [], None
        for half in range(2):
            s = _mm(lhs, kab_ref[0, :, (2 * g + half) * LANES:(2 * g + half + 1) * LANES], "nt") + b2
            e = jnp.exp2(s) if bounded else jnp.exp2(s - jnp.maximum(jnp.max(s, -1, keepdims=True), M_INIT))
            acc = _mm(e, vv)
            accs.append(acc)
            p = e / jnp.maximum(acc[:, HEAD_DIM:HEAD_DIM + 1], 1e-30)
            ph = p[:tq] + p[tq:]
            psum = ph if psum is None else psum + ph
        imp = imp + _dot_hilo(psum, m_ref[g])
        for p_ in range(2):
            col = g * 2 * LANES + p_ * LANES
            o_o[0, :, col:col + LANES] = _finish_pair(accs[0], accs[1], p_, tq)
    lane = lax.broadcasted_iota(jnp.int32, (tq, LANES), 1)
    blk = lane % nsp
    cur = (q0 + lax.broadcasted_iota(jnp.int32, (tq, LANES), 0)) // SLC_BLOCK
    forced = (blk == 0) | (blk == cur) | (blk == cur - 1)
    score = jnp.where(blk <= cur, jnp.where(forced, BIG, imp), -BIG)
    rank = jnp.zeros((tq, LANES), F32)
    for d in range(1, nsp):
        wrapped = blk + d >= nsp
        partner = jnp.where(wrapped, pltpu.roll(score, nsp - d, 1), pltpu.roll(score, LANES - d, 1))
        rank = rank + jnp.where(partner > score, 1.0, jnp.where((partner == score) & wrapped, 1.0, 0.0))
    sel_o[0] = jnp.where(rank < min(SLC_TOP, n_slc), 1.0, 0.0).astype(sel_o.dtype)


def _cmp_sel(q, kab, v1, m, *, n_cmp, n_slc, tq, bounded):
    B, T, D = q.shape
    ncp = kab.shape[1]
    assert m.shape == (N_KV_HEADS, ncp, LANES)
    whole = lambda a: pl.BlockSpec((1,) + a.shape[1:], lambda b, i: (b,) + (0,) * (a.ndim - 1))
    kern = functools.partial(_cmp_sel_kernel, n_cmp=n_cmp, n_slc=n_slc, bounded=bounded)
    return pl.pallas_call(
        kern, grid=(B, T // tq),
        in_specs=[pl.BlockSpec((1, tq, D), lambda b, i: (b, i, 0)), whole(kab), whole(v1),
                  pl.BlockSpec(m.shape, lambda b, i: (0, 0, 0))],
        out_specs=[pl.BlockSpec((1, tq, D), lambda b, i: (b, i, 0)), pl.BlockSpec((1, tq, LANES), lambda b, i: (b, i, 0))],
        out_shape=[jax.ShapeDtypeStruct((B, T, D), F32), jax.ShapeDtypeStruct((B, T, LANES), BF16)],
        compiler_params=_cparams(("parallel", "arbitrary")), name="cmp_sel_attn",
    )(q, kab, v1, m)


def _flash_pairs(lhs, kab_ref, v1_ref, kt_lo, kt_hi, tk, bias_fn, bounded):
    rows = lhs[0].shape[0]

    def body(kt, carry):
        k0 = pl.multiple_of(kt * tk, tk)
        biases = bias_fn(k0)
        out = []
        for g in range(N_KV_HEADS):
            vv = v1_ref[0, pl.ds(k0, tk), g * LANES:(g + 1) * LANES]
            for half in range(2):
                idx = 2 * g + half
                s = _mm(lhs[g], kab_ref[0, pl.ds(k0, tk), idx * LANES:(idx + 1) * LANES], "nt") + biases[g]
                if bounded:
                    out.append(carry[idx] + _mm(jnp.exp2(s), vv))
                else:
                    m, acc = carry[idx]
                    m_new = jnp.maximum(m, jnp.max(s, -1, keepdims=True))
                    out.append((m_new, jnp.exp2(m - m_new) * acc + _mm(jnp.exp2(s - m_new), vv)))
        return tuple(out)

    zero = jnp.zeros((rows, LANES), F32)
    if bounded:
        return lax.fori_loop(kt_lo, kt_hi, body, (zero,) * (2 * N_KV_HEADS))
    start = (jnp.full((rows, 1), M_INIT, F32), zero)
    return tuple(acc for _, acc in lax.fori_loop(kt_lo, kt_hi, body, (start,) * (2 * N_KV_HEADS)))


def _slc_win_attn_kernel(q_ref, sel_ref, skab_ref, sv1_ref, wkab_ref, wv1_ref, os_o, ow_o, *, tk, bounded):
    tq = q_ref.shape[1]
    nsp = sel_ref.shape[2] // N_KV_HEADS
    q0 = pl.program_id(1) * tq
    q_pos = q0 + lax.broadcasted_iota(jnp.int32, (tq, tk), 0)
    k_off = lax.broadcasted_iota(jnp.int32, (tq, tk), 1)
    sel_lane = lax.broadcasted_iota(jnp.int32, (N_KV_HEADS * nsp, tk), 0)
    blk_k = lax.broadcasted_iota(jnp.int32, (N_KV_HEADS * nsp, tk), 1)
    drop = jnp.where(sel_ref[0].astype(F32) > 0.5, 0.0, MASKED).astype(BF16)
    q = q_ref[0]
    kt_hi = (q0 + tq + tk - 1) // tk
    kt_lo_win = jnp.maximum(q0 - WINDOW + 1, 0) // tk

    twice = lambda b: jnp.concatenate([b, b], axis=0)

    def win_bias(k0):
        d = q_pos - (k_off + k0)
        return [twice(jnp.where((d >= 0) & (d < WINDOW), 0.0, MASKED))] * N_KV_HEADS

    def slc_bias(k0):
        causal = jnp.where(k_off + k0 <= q_pos, 0.0, MASKED)
        key_blk = (blk_k + k0) // SLC_BLOCK
        out = []
        for g in range(N_KV_HEADS):
            expand = jnp.where(g * nsp + key_blk == sel_lane, 1.0, 0.0).astype(BF16)
            out.append(twice(jnp.dot(drop, expand, preferred_element_type=F32) + causal))
        return out

    lhs = [_pair_rows(q, g) for g in range(N_KV_HEADS)]
    acc_s = _flash_pairs(lhs, skab_ref, sv1_ref, 0, kt_hi, tk, slc_bias, bounded)
    acc_w = _flash_pairs(lhs, wkab_ref, wv1_ref, kt_lo_win, kt_hi, tk, win_bias, bounded)
    for g in range(N_KV_HEADS):
        for p_ in range(2):
            col = g * 2 * LANES + p_ * LANES
            os_o[0, :, col:col + LANES] = _finish_pair(acc_s[2 * g], acc_s[2 * g + 1], p_, tq)
            ow_o[0, :, col:col + LANES] = _finish_pair(acc_w[2 * g], acc_w[2 * g + 1], p_, tq)


def _slc_win_attn(q, sel, skab, sv1, wkab, wv1, *, tq, tk, bounded):
    B, T, D = q.shape
    whole = lambda a: pl.BlockSpec((1,) + a.shape[1:], lambda b, i: (b, 0, 0))
    row = lambda w: pl.BlockSpec((1, tq, w), lambda b, i: (b, i, 0))
    return pl.pallas_call(
        functools.partial(_slc_win_attn_kernel, tk=tk, bounded=bounded), grid=(B, T // tq),
        in_specs=[row(D), row(sel.shape[2]), whole(skab), whole(sv1), whole(wkab), whole(wv1)],
        out_specs=[row(D), row(D)],
        out_shape=[jax.ShapeDtypeStruct((B, T, D), F32)] * 2,
        compiler_params=_cparams(("parallel", "arbitrary")), name="slc_win_attn",
    )(q, sel, skab, sv1, wkab, wv1)


def _pair_keys(k):
    z = jnp.zeros_like(k)
    return jnp.concatenate([k, z, z, k], axis=-1).reshape(k.shape[0], k.shape[1], -1)


def _ones_values(v):
    return jnp.concatenate([v, jnp.ones_like(v)], axis=-1).reshape(v.shape[0], v.shape[1], -1)


def _sample_attn_kernel(blk_ref, q_ref, kpos_ref, new_ref, win_ref, cache_hbm, os_o, ow_o, kbuf, vbuf, sem, *,
                        n_q, n_sel, pos_base, win_base, n_win, hp):
    b = pl.program_id(0)
    slot = b % 2
    per_seq = n_q * N_KV_HEADS * n_sel

    def pool_copies(step, into, start):
        def one(i, carry):
            blk = blk_ref[step * per_seq + i]
            qg = i // n_sel
            lane0 = pl.multiple_of(((qg % N_KV_HEADS) // 2) * LANES, LANES)

            @pl.when(blk >= 0)
            def _():
                for c, dst in ((0, kbuf), (1, vbuf)):
                    cp = pltpu.make_async_copy(cache_hbm.at[blk, :, pl.ds(c * KV_W + lane0, LANES)],
                                               dst.at[into, qg, pl.ds((i % n_sel) * SLC_BLOCK, SLC_BLOCK)], sem.at[into])
                    cp.start() if start else cp.wait()
            return carry
        lax.fori_loop(0, per_seq, one, 0)

    @pl.when(b == 0)
    def _():
        pool_copies(0, 0, True)

    @pl.when(b + 1 < pl.num_programs(0))
    def _():
        pool_copies(b + 1, 1 - slot, True)

    pool_copies(b, slot, False)

    def new_rows(i, carry):
        qg = i // n_sel
        win = (qg % N_KV_HEADS) // 2

        @pl.when(blk_ref[b * per_seq + i] < 0)
        def _():
            kbuf[slot, qg, pl.ds((i % n_sel) * SLC_BLOCK, SLC_BLOCK), :] = new_ref[0, win]
            vbuf[slot, qg, pl.ds((i % n_sel) * SLC_BLOCK, SLC_BLOCK), :] = new_ref[0, 2 + win]
        return carry
    lax.fori_loop(0, per_seq, new_rows, 0)

    tq = q_ref.shape[1]
    rows = HEADS_PER_KV * tq
    nk = n_sel * SLC_BLOCK
    nwp = win_ref.shape[1]
    q = q_ref[0]
    row_q = lax.broadcasted_iota(jnp.int32, (rows, 1), 0) % tq
    w_id = lax.broadcasted_iota(jnp.int32, (rows, nwp), 1)
    d_win = (pos_base + row_q) - (win_base + w_id)
    win_bias = jnp.where((d_win >= 0) & (d_win < WINDOW) & (win_base + w_id >= 0) & (w_id < n_win), 0.0, MASKED)

    def attend(s, v):
        m = jnp.maximum(jnp.max(s, -1, keepdims=True), M_INIT)
        e = jnp.exp(s - m)
        p = e / jnp.maximum(jnp.sum(e, -1, keepdims=True), 1e-30)
        return _mm(p, v, hp=hp)

    for g in range(N_KV_HEADS):
        qs = _stack_heads(q, g, tq)
        zeros = jnp.zeros_like(qs)
        qs_pair = jnp.concatenate([qs, zeros] if g % 2 == 0 else [zeros, qs], axis=1)
        lanes_g = slice((g % 2) * HEAD_DIM, (g % 2 + 1) * HEAD_DIM)
        o_s = jnp.zeros((rows, HEAD_DIM), F32)
        for qi in range(n_q):
            qg = qi * N_KV_HEADS + g
            kpos = kpos_ref[0, qg:qg + 1, :]
            bias = jnp.broadcast_to(jnp.where(kpos <= pos_base + qi, 0.0, MASKED), (rows, nk))
            o_qi = attend(_mm(qs_pair, kbuf[slot, qg], "nt", hp) + bias, vbuf[slot, qg])[:, lanes_g]
            o_s = jnp.where(row_q == qi, o_qi, o_s)
        o_w = attend(_mm(qs, win_ref[0, :, g * HEAD_DIM:(g + 1) * HEAD_DIM], "nt", hp) + win_bias,
                     win_ref[0, :, KV_W + g * HEAD_DIM:KV_W + (g + 1) * HEAD_DIM])
        for i in range(HEADS_PER_KV):
            col = (HEADS_PER_KV * g + i) * HEAD_DIM
            os_o[0, :, col:col + HEAD_DIM] = o_s[i * tq:(i + 1) * tq]
            ow_o[0, :, col:col + HEAD_DIM] = o_w[i * tq:(i + 1) * tq]


def _sample_attn(pool_blk, q, kpos, new_win, win_all, cache_blocks, *, n_q, n_sel, pos_base, win_base, n_win, hp):
    B, tq, D = q.shape
    kern = functools.partial(_sample_attn_kernel, n_q=n_q, n_sel=n_sel, pos_base=pos_base, win_base=win_base,
                             n_win=n_win, hp=hp)
    blk = lambda a: pl.BlockSpec((1,) + a.shape[1:], lambda b, pb: (b,) + (0,) * (a.ndim - 1))
    buf = pltpu.VMEM((2, n_q * N_KV_HEADS, n_sel * SLC_BLOCK, LANES), F32)
    grid_spec = pltpu.PrefetchScalarGridSpec(
        num_scalar_prefetch=1, grid=(B,),
        in_specs=[blk(q), blk(kpos), blk(new_win), blk(win_all), pl.BlockSpec(memory_space=pl.ANY)],
        out_specs=[blk(q), blk(q)],
        scratch_shapes=[buf, buf, pltpu.SemaphoreType.DMA((2,))])
    return pl.pallas_call(
        kern, grid_spec=grid_spec, out_shape=[jax.ShapeDtypeStruct((B, tq, D), F32)] * 2,
        compiler_params=_cparams(("arbitrary",)), name="sample_slc_win_attn",
    )(pool_blk.reshape(-1), q, kpos, new_win, win_all, cache_blocks)


def _pick_tile(n, cap=512):
    return next(t for t in (512, 256, 128, 64, 32, 16, 8) if t <= cap and n % t == 0)


def _pad_rows(x, t_pad):
    return jnp.pad(x, ((0, 0), (0, t_pad - x.shape[1]), (0, 0)))


def _rwkv_layer(x, shift0, wkv0, p, *, tm, hp, inv_hp):
    B, T, D = x.shape
    t_pad = -(-T // tm) * tm
    xp = _pad_rows(x, t_pad)
    r, lw, k, v, a, g, shift = _rwkv_proj(xp, shift0[:, None, :], p["g"], p["mix"], p["vec"], p["wrkv"], p["dw1"],
                                          p["dw2"], p["aw1"], p["aw2"], p["gw1"], p["gw2"], t_real=T, tm=tm, hp=hp)
    if t_pad != T:
        live = (jnp.arange(t_pad) < T)[None, :, None]
        r, lw, k, v, a = (jnp.where(live, t, 0.0) for t in (r, lw, k, v, a))
    y, s_fin = _wkv(r, lw, k, v, a, p["vec"], wkv0, hp=hp, inv_hp=inv_hp)
    n = B * t_pad
    out = _row_call(functools.partial(_rwkv_out_kernel, hp=hp), [xp.reshape(n, D), y.reshape(n, D), g.reshape(n, D)],
                    [p["wo"]], [D], [F32], _pick_tile(n), "rwkv_out")[0]
    return out.reshape(B, t_pad, D)[:, :T], shift[:, 0], s_fin


def _sub_blocks(rows):
    S, L = rows.shape[:2]
    n_sub = L // CMP_STRIDE
    r = rows[:, :n_sub * CMP_STRIDE].reshape(S, n_sub, CMP_STRIDE, 2, N_KV_HEADS, HEAD_DIM)
    r = jnp.transpose(r, (0, 3, 4, 1, 2, 5)).reshape(S, 2, N_KV_HEADS, n_sub, CMP_STRIDE * HEAD_DIM)
    r_pad = -(-n_sub // SUBLANES) * SUBLANES
    return jnp.pad(r, ((0, 0),) * 3 + ((0, r_pad - n_sub), (0, 0))), n_sub - CMP_BLOCK // CMP_STRIDE + 1


def _compressed_kv(rows, p, *, hp):
    xsub, n_cmp = _sub_blocks(rows)
    out = _compress(xsub, p["cmp_pe"], p["cmp_w1"], p["cmp_w2"], p["k_norm_g"][0:1], hp=hp)
    S, _, G, R, dh = out.shape
    out = jnp.transpose(out, (1, 0, 3, 2, 4)).reshape(2, S, R, G * dh).astype(F32 if hp else BF16)
    return out[0], out[1], n_cmp


def _overlap_matrix(ncp, n_cmp, n_slc, nsp):
    c0 = np.arange(ncp)[:, None] * CMP_STRIDE
    s0 = np.arange(nsp)[None, :] * SLC_BLOCK
    m = (c0 < s0 + SLC_BLOCK) & (c0 + CMP_BLOCK > s0) & (np.arange(ncp)[:, None] < n_cmp) & (np.arange(nsp)[None, :] < n_slc)
    return jnp.asarray(m.astype(np.float32), dtype=BF16)


def _moe(h, p, *, tm, tf, hp):
    n, D = h.shape
    xn, logits = _row_call(_router_kernel, [h], [p["g_moe"], p["w_router"]], [D, LANES], [F32 if hp else BF16, F32],
                           _pick_tile(n), "moe_router")
    top_v, top_e = lax.top_k(logits[:, :N_EXPERTS], TOP_K)
    gates = jax.nn.softmax(top_v, -1)
    n_pairs = n * TOP_K
    flat_e = top_e.reshape(n_pairs)
    order = jnp.argsort(flat_e)
    se = flat_e[order]
    counts = jnp.zeros((N_EXPERTS,), jnp.int32).at[flat_e].add(1)
    padded = (counts + tm - 1) // tm * tm
    pad_end = jnp.cumsum(padded)
    start = jnp.cumsum(counts) - counts
    dest_sorted = (pad_end - padded)[se] + jnp.arange(n_pairs, dtype=jnp.int32) - start[se]
    n_blocks = -(-n_pairs // tm) + N_EXPERTS
    cap = n_blocks * tm
    buf_tok = jnp.full((cap,), n, jnp.int32).at[dest_sorted].set((order // TOP_K).astype(jnp.int32))
    dest = jnp.zeros((n_pairs,), jnp.int32).at[order].set(dest_sorted).reshape(n, TOP_K)
    block_e = jnp.minimum(jnp.searchsorted(pad_end, jnp.arange(n_blocks) * tm, side="right"), N_EXPERTS - 1).astype(jnp.int32)
    n_used = (pad_end[-1] // tm).astype(jnp.int32).reshape(1)
    xs = jnp.concatenate([xn, jnp.zeros((1, D), xn.dtype)], 0)[buf_tok]
    ys = _moe_ffn(block_e, n_used, xs, p["moe_wg"], p["moe_wu"], p["moe_wd"], tm=tm, tf=tf, hp=hp)
    y = ys[dest[:, 0]] * gates[:, 0:1] + ys[dest[:, 1]] * gates[:, 1:2]
    return h + y


def kernel(x_prompt, x_sample, cache_cmp_kv, cache_slc_kv, state_win_kv, state_wkv, state_shift, page_table, norm_g, rw_mix, rw_vec, rw_w_rkv, rw_w_o, rw_decay_w1, rw_decay_w2, rw_iclr_w1, rw_iclr_w2, rw_gate_w1, rw_gate_w2, ffn_w_gate, ffn_w_up, ffn_w_down, moe_router, moe_w_gate, moe_w_up, moe_w_down, kv_norm_g, w_kv, k_norm_g, cmp_pe, cmp_w1, cmp_w2, w_qg, q_norm_g, w_o):
    B, T, D = x_prompt.shape
    SB, S, _ = x_sample.shape
    assert norm_g.shape[0] == 2 and rw_mix.shape[0] == 1 and w_qg.shape[0] == 1, "one RWKV-7 layer + one NSA layer"
    assert D == N_HEADS * HEAD_DIM and T % 512 == 0 and S <= SUBLANES
    page = cache_cmp_kv.shape[1]
    past = page_table.shape[1] * page
    wb = state_win_kv.shape[1]
    assert past % SLC_BLOCK == 0 and page % SLC_BLOCK == 0 and S <= SLC_BLOCK
    bf = lambda w: w.astype(BF16)
    pad8 = lambda m: jnp.pad(m, ((0, SUBLANES - m.shape[0]), (0, 0)))
    dff = ffn_w_gate.shape[2]
    tf = dff // 2 if (dff // 2) % LANES == 0 else dff
    n_p, n_s = B * T, SB * S
    both = lambda w: (bf(w), w)
    pick = lambda d, hp: {k: (v[hp] if isinstance(v, tuple) else v) for k, v in d.items()}

    rw = dict(g=norm_g[0, 0][None], mix=pad8(rw_mix[0]), vec=pad8(rw_vec[0]), wrkv=both(rw_w_rkv[0]), wo=both(rw_w_o[0]),
              dw1=both(rw_decay_w1[0]), dw2=both(rw_decay_w2[0]), aw1=both(rw_iclr_w1[0]), aw2=both(rw_iclr_w2[0]),
              gw1=both(rw_gate_w1[0]), gw2=both(rw_gate_w2[0]))
    H = D // RW_HEAD_DIM
    hp_, shift_p, wkv_p = _rwkv_layer(x_prompt, jnp.zeros((B, D), F32), jnp.zeros((B, H, RW_HEAD_DIM, RW_HEAD_DIM), F32),
                                      pick(rw, 0), tm=256, hp=False, inv_hp=True)
    hs, shift_s, wkv_s = _rwkv_layer(x_sample, state_shift[0], state_wkv[0], pick(rw, 1), tm=WKV_CHUNK, hp=True, inv_hp=True)
    ffn_g = norm_g[0, 1][None]
    hp_ = _ffn(hp_.reshape(n_p, D), ffn_g, bf(ffn_w_gate[0]), bf(ffn_w_up[0]), bf(ffn_w_down[0]), tm=512, tf=tf, hp=False)
    hs = _ffn(hs.reshape(n_s, D), ffn_g, ffn_w_gate[0], ffn_w_up[0], ffn_w_down[0], tm=n_s, tf=tf, hp=True)

    kv_consts = lambda hp: [kv_norm_g[None], both(w_kv)[hp], jnp.tile(k_norm_g, (1, N_KV_HEADS)),
                            _block_diag(KV_W, HEAD_DIM, 1.0 / HEAD_DIM)]
    kv_rows = lambda h, tm, hp: _row_call(functools.partial(_kv_proj_kernel, hp=hp), [h], kv_consts(hp),
                                          [2 * KV_W] * 3, [F32] * 3, tm, "kv_proj")
    as_rows = lambda t, b: t.reshape(b, -1, 2, N_KV_HEADS, HEAD_DIM)
    cmp_p, slc_p, win_p = (as_rows(t, B) for t in kv_rows(hp_, 512, False))
    cmp_s, slc_s, win_s = (as_rows(t, SB) for t in kv_rows(hs, n_s, True))
    win_all_s = jnp.concatenate([state_win_kv, win_s], axis=1)

    cp = dict(cmp_pe=cmp_pe.reshape(2, CMP_BLOCK // CMP_STRIDE, 1, CMP_STRIDE * HEAD_DIM),
              cmp_w1=both(cmp_w1.reshape(2, CMP_BLOCK // CMP_STRIDE, CMP_STRIDE * HEAD_DIM, HEAD_DIM)),
              cmp_w2=both(cmp_w2), k_norm_g=k_norm_g)
    kc_p, vc_p, ncmp_p = _compressed_kv(cmp_p, pick(cp, 0), hp=False)
    assert S < CMP_STRIDE and page % CMP_STRIDE == 0
    n_ratio = CMP_BLOCK // CMP_STRIDE
    w1r = cmp_w1.reshape(2, n_ratio, CMP_STRIDE, HEAD_DIM, HEAD_DIM)
    w2x = jnp.einsum("cjsdh,ef->csedjfh", w1r, jnp.eye(2, dtype=F32)).reshape(2, CMP_STRIDE * LANES, n_ratio * LANES)
    n_pages = page_table.shape[1]
    pages_per_step = next(pp for pp in (32, 16, 8, 4, 2, 1) if n_pages % pp == 0)
    part_s = _paged_sub_proj(page_table, cache_cmp_kv.reshape(-1, page, 2 * KV_W), w2x, pages_per_step=pages_per_step, hp=True)
    kcv_s = _compress_tail(part_s, cp["cmp_pe"], cp["cmp_w1"][1], cmp_w2, k_norm_g[0:1], hp=True)
    kc_s, vc_s, ncmp_s = kcv_s[:, 0], kcv_s[:, 1], past // CMP_STRIDE - n_ratio + 1

    ge = np.zeros((LANES, N_BRANCH * D), np.float32)
    for br in range(N_BRANCH):
        for hd in range(N_HEADS):
            ge[br * N_HEADS + hd, br * D + hd * HEAD_DIM: br * D + (hd + 1) * HEAD_DIM] = 1.0
    nsa = dict(g_attn=norm_g[1, 0][None], wq=both(w_qg[0, :, :D]),
               wgate=both(jnp.pad(w_qg[0, :, D:], ((0, 0), (0, LANES - N_BRANCH * N_HEADS)))),
               q_norm_g=jnp.tile(q_norm_g[0], N_HEADS)[None], bd_q=_block_diag(D, HEAD_DIM, 1.0 / HEAD_DIM),
               w_o=both(w_o[0]), gate_expand=jnp.asarray(ge, dtype=BF16))

    def query_side(h, tm, hp, q_scale):
        c = pick(nsa, hp)
        return _row_call(functools.partial(_q_proj_kernel, hp=hp, q_scale=q_scale), [h],
                         [c["g_attn"], c["wq"], c["wgate"], c["q_norm_g"], c["bd_q"]], [D, LANES],
                         [F32 if hp else BF16, F32], tm, "q_proj")

    def merge(h, gates, o_cmp, o_slc, o_win, tm, hp):
        c = pick(nsa, hp)
        return _row_call(functools.partial(_merge_kernel, hp=hp), [h, gates, o_cmp, o_slc, o_win],
                         [c["gate_expand"], c["w_o"]], [D], [F32], tm, "nsa_merge")[0]

    q_p, gates_p = query_side(hp_, 512, False, ATTN_SCALE * LOG2E)
    q_p = q_p.reshape(B, T, D)
    n_slc_p = T // SLC_BLOCK
    nsp_p = LANES // N_KV_HEADS
    assert n_slc_p <= nsp_p, "the packed block-selection layout holds at most 32 selection blocks per kv group"
    m_one = _overlap_matrix(kc_p.shape[1], ncmp_p, n_slc_p, nsp_p)
    m_p = jnp.stack([jnp.pad(m_one, ((0, 0), (g * nsp_p, LANES - (g + 1) * nsp_p))) for g in range(N_KV_HEADS)])
    heads = lambda t: t.reshape(t.shape[0], t.shape[1], N_KV_HEADS, HEAD_DIM)
    kcab, vc1 = _pair_keys(heads(kc_p)), _ones_values(heads(vc_p))
    skab, sv1 = _pair_keys(bf(slc_p[:, :, 0])), _ones_values(bf(slc_p[:, :, 1]))
    wkab, wv1 = _pair_keys(bf(win_p[:, :, 0])), _ones_values(bf(win_p[:, :, 1]))

    def prompt_branches(bounded):
        o_cmp, sel = _cmp_sel(q_p, kcab, vc1, m_p, n_cmp=ncmp_p, n_slc=n_slc_p, tq=256, bounded=bounded)
        return (o_cmp,) + tuple(_slc_win_attn(q_p, sel, skab, sv1, wkab, wv1, tq=128, tk=256, bounded=bounded))

    score_bound = HEAD_DIM * ATTN_SCALE * jnp.max(jnp.abs(q_norm_g[0])) * jnp.max(jnp.abs(k_norm_g))
    o_cmp_p, o_slc_p, o_win_p = lax.cond(score_bound <= SCORE_BOUND_MAX, lambda: prompt_branches(True),
                                         lambda: prompt_branches(False))
    hp_ = merge(hp_, gates_p, o_cmp_p.reshape(n_p, D), o_slc_p.reshape(n_p, D), o_win_p.reshape(n_p, D), 512, False)

    kv_lanes = lambda t, dt: t.reshape(t.shape[0], t.shape[1], 2 * KV_W).astype(dt)
    q_s, gates_s = query_side(hs, n_s, True, ATTN_SCALE)
    q_s = _pad_rows(q_s.reshape(SB, S, D), SUBLANES)
    n_slc_s = (past + S - 1) // SLC_BLOCK + 1
    nsp_s = -(-n_slc_s // LANES) * LANES
    m_s = _overlap_matrix(kc_s.shape[1], ncmp_s, n_slc_s, nsp_s)
    o_cmp_s, sel_s = _cmp_attn(q_s, kc_s, vc_s, m_s, n_cmp=ncmp_s, n_slc=n_slc_s, pos_base=past, per_tile=False,
                               tq=SUBLANES, hp=True)
    n_sel = min(SLC_TOP, n_slc_s)
    sel_s = sel_s.reshape(SB, SUBLANES, N_KV_HEADS, nsp_s)[:, :S, :, :n_slc_s] > 0.5
    slot = jnp.cumsum(sel_s, axis=-1) - 1
    idx_s = jnp.sum(jnp.where(sel_s[..., None] & (slot[..., None] == jnp.arange(n_sel)),
                              jnp.arange(n_slc_s)[:, None], 0), axis=-2).astype(jnp.int32)
    in_pool = idx_s < past // SLC_BLOCK
    per_page = page // SLC_BLOCK
    blk_c = jnp.minimum(idx_s, past // SLC_BLOCK - 1)
    pages = jnp.take_along_axis(page_table, (blk_c // per_page).reshape(SB, -1), axis=1).reshape(idx_s.shape)
    pool_blk = jnp.where(in_pool, pages * per_page + blk_c % per_page, -1).astype(jnp.int32)
    cache_blocks = cache_slc_kv.reshape(-1, SLC_BLOCK, 2 * KV_W)
    new_rows = slc_s[:, jnp.clip(jnp.arange(SLC_BLOCK), 0, S - 1)]
    new_win = jnp.transpose(new_rows.reshape(SB, SLC_BLOCK, 2, N_KV_HEADS // 2, LANES), (0, 2, 3, 1, 4))
    new_win = new_win.reshape(SB, 2 * (N_KV_HEADS // 2), SLC_BLOCK, LANES)
    kpos = (idx_s[..., None] * SLC_BLOCK + jnp.arange(SLC_BLOCK)).reshape(SB, S * N_KV_HEADS, n_sel * SLC_BLOCK).astype(jnp.int32)
    kpos = jnp.pad(kpos, ((0, 0), (0, -(-S * N_KV_HEADS // SUBLANES) * SUBLANES - S * N_KV_HEADS), (0, 0)))
    n_win = wb + S
    nwp = -(-n_win // SUBLANES) * SUBLANES
    win_lanes = _pad_rows(kv_lanes(win_all_s, F32), nwp)
    o_slc_s, o_win_s = _sample_attn(pool_blk, q_s, kpos, new_win, win_lanes, cache_blocks, n_q=S, n_sel=n_sel,
                                    pos_base=past, win_base=past - wb, n_win=n_win, hp=True)
    unpad = lambda t: t[:, :S].reshape(n_s, D)
    hs = merge(hs, gates_s, unpad(o_cmp_s), unpad(o_slc_s), unpad(o_win_s), n_s, True)

    moe = dict(g_moe=norm_g[1, 1][None], w_router=jnp.pad(moe_router[0], ((0, 0), (0, LANES - N_EXPERTS))),
               moe_wg=both(moe_w_gate[0]), moe_wu=both(moe_w_up[0]), moe_wd=both(moe_w_down[0]))
    hp_ = _moe(hp_, pick(moe, 0), tm=512, tf=tf, hp=False).reshape(B, T, D)
    hs = _moe(hs, pick(moe, 1), tm=128, tf=tf, hp=True).reshape(SB, S, D)

    keep_p = min(WINDOW, T)
    return (hp_, hs, cmp_p, cmp_s, slc_p, slc_s, win_p[:, T - keep_p:], win_all_s[:, win_all_s.shape[1] - wb:],
            wkv_p[None], wkv_s[None], shift_p[None], shift_s[None])
```

```python
import functools

import numpy as np
import jax
import jax.numpy as jnp
from jax import lax
from jax.experimental import pallas as pl
from jax.experimental.pallas import tpu as pltpu

F32 = jnp.float32
BF16 = jnp.bfloat16

RW_HEAD_DIM = 64
RW_GN_EPS = 64e-5
N_HEADS = 16
HEAD_DIM = 64
N_KV_HEADS = 4
HEADS_PER_KV = N_HEADS // N_KV_HEADS
KV_W = N_KV_HEADS * HEAD_DIM
N_BRANCH = 3
CMP_BLOCK = 32
CMP_STRIDE = 16
SLC_BLOCK = 64
SLC_TOP = 16
WINDOW = 512
ATTN_SCALE = HEAD_DIM ** -0.5
N_EXPERTS = 8
TOP_K = 2
RMS_EPS = 1e-6
MASKED = -1e30
M_INIT = -1e20
BIG = 1e30
LOG2E = 1.4426950408889634
SCORE_BOUND_MAX = 40.0

SUBLANES = 8
LANES = 128
VMEM_LIMIT = 56 * 1024 * 1024
WKV_CHUNK = RW_HEAD_DIM


def _cparams(sem):
    return pltpu.CompilerParams(dimension_semantics=sem, vmem_limit_bytes=VMEM_LIMIT)


_DIMS = {"nn": (((1,), (0,)), ((), ())), "nt": (((1,), (1,)), ((), ())), "tn": (((0,), (0,)), ((), ()))}


def _split2(x):
    hi = x.astype(BF16)
    lo = (x - hi.astype(F32)).astype(BF16)
    return hi, lo


def _mm(a, b, form="nn", hp=False):
    d = lambda s, t: lax.dot_general(s, t, _DIMS[form], preferred_element_type=F32)
    if not hp:
        return d(a.astype(BF16), b.astype(BF16))
    ah, al = _split2(a.astype(F32))
    bh, bl = _split2(b.astype(F32))
    return d(ah, bh) + d(ah, bl) + d(al, bh)


def _dot_hilo(x, m):
    hi, lo = _split2(x)
    return jnp.dot(hi, m, preferred_element_type=F32) + jnp.dot(lo, m, preferred_element_type=F32)


def _dot_exact_rhs(m, x):
    hi = x.astype(BF16)
    r1 = x - hi.astype(F32)
    mid = r1.astype(BF16)
    lo = (r1 - mid.astype(F32)).astype(BF16)
    d = lambda t: jnp.dot(m, t, preferred_element_type=F32)
    return d(hi) + d(mid) + d(lo)


def _rms(x, g):
    return x * lax.rsqrt(jnp.mean(x * x, -1, keepdims=True) + RMS_EPS) * g


def _silu(x):
    return x * jax.nn.sigmoid(x)


def _softplus(z):
    return jnp.maximum(z, 0.0) + jnp.log(1.0 + jnp.exp(-jnp.abs(z)))


def _block_diag(n, seg, value):
    i = np.arange(n)
    return jnp.asarray((i[:, None] // seg == i[None, :] // seg).astype(np.float32) * value, dtype=BF16)


def _rwkv_proj_kernel(x_ref, halo_ref, sh_ref, g_ref, mix_ref, vec_ref, wrkv_ref, dw1_ref, dw2_ref, aw1_ref, aw2_ref,
                      gw1_ref, gw2_ref, r_o, lw_o, k_o, v_o, a_o, g_o, shift_o, *, last_tile, last_row, hp):
    i = pl.program_id(1)
    mm = functools.partial(_mm, hp=hp)
    g = g_ref[...]
    xn = _rms(x_ref[0], g)
    hn = _rms(halo_ref[0, SUBLANES - 1:SUBLANES, :], g)
    prev_last = jnp.where(i == 0, sh_ref[0], hn)
    row = lax.broadcasted_iota(jnp.int32, xn.shape, 0)
    prev = jnp.where(row == 0, prev_last, pltpu.roll(xn, 1, 0))
    xx = prev - xn
    mixed = lambda j: xn + xx * mix_ref[j:j + 1, :]
    r_o[0] = mm(mixed(0), wrkv_ref[0])
    k_o[0] = mm(mixed(2), wrkv_ref[1])
    v_o[0] = mm(mixed(3), wrkv_ref[2])
    w0 = vec_ref[0:1, :]
    a0 = vec_ref[1:2, :]
    log_w = -_softplus(-(w0 + mm(jnp.tanh(mm(mixed(1), dw1_ref[...])), dw2_ref[...]))) - 0.5
    lw_o[0] = -jnp.exp(log_w)
    a_o[0] = jax.nn.sigmoid(a0 + mm(mm(mixed(4), aw1_ref[...]), aw2_ref[...]))
    g_o[0] = mm(jax.nn.sigmoid(mm(mixed(5), gw1_ref[...])), gw2_ref[...])

    @pl.when(i == last_tile)
    def _():
        shift_o[0] = xn[last_row:last_row + 1, :]


def _rwkv_proj(x, shift0, g, mix, vec, wrkv, dw1, dw2, aw1, aw2, gw1, gw2, *, t_real, tm, hp):
    B, T, D = x.shape
    nt = T // tm
    row_spec = pl.BlockSpec((1, tm, D), lambda b, i: (b, i, 0))
    halo_spec = pl.BlockSpec((1, SUBLANES, D), lambda b, i: (b, jnp.maximum(i * (tm // SUBLANES) - 1, 0), 0))
    vec_spec = pl.BlockSpec((1, 1, D), lambda b, i: (b, 0, 0))
    full = lambda a: pl.BlockSpec(a.shape, lambda b, i: (0,) * a.ndim)
    consts = (g, mix, vec, wrkv, dw1, dw2, aw1, aw2, gw1, gw2)
    out_sd = jax.ShapeDtypeStruct((B, T, D), F32)
    kern = functools.partial(_rwkv_proj_kernel, last_tile=(t_real - 1) // tm, last_row=(t_real - 1) % tm, hp=hp)
    return pl.pallas_call(
        kern, grid=(B, nt),
        in_specs=[row_spec, halo_spec, vec_spec] + [full(c) for c in consts],
        out_specs=[row_spec] * 6 + [vec_spec],
        out_shape=[out_sd] * 6 + [jax.ShapeDtypeStruct((B, 1, D), F32)],
        compiler_params=_cparams(("parallel", "arbitrary")), name="rwkv_proj",
    )(x, x, shift0, *consts)


def _wkv_kernel(r_ref, lw_ref, k_ref, v_ref, a_ref, vec_ref, ones_ref, s0_ref, y_o, sT_o, s_sc, *, n_pairs, hp, inv_hp):
    c = pl.program_id(1)
    C = r_ref.shape[1]
    W = 2 * C
    mm = functools.partial(_mm, hp=hp)
    mm_inv = functools.partial(_mm, hp=inv_hp)

    @pl.when(c == 0)
    def _():
        s_sc[...] = s0_ref[0]

    lane = lax.broadcasted_iota(jnp.int32, (C, W), 1)
    row = lax.broadcasted_iota(jnp.int32, (C, W), 0)
    left = lane < C
    col = jnp.where(left, lane, lane - C)
    strict = col < row
    incl = col <= row
    diag_blocks = (lax.broadcasted_iota(jnp.int32, (W, W), 0) < C) == (lax.broadcasted_iota(jnp.int32, (W, W), 1) < C)
    bd = lambda x: jnp.concatenate([jnp.where(left, x, 0.0), jnp.where(left, 0.0, x)], axis=0)
    seg_sum = lambda x: _dot_hilo(x, ones_ref[...])
    rows = lambda x, p: x[p * C:(p + 1) * C]
    tri = jnp.where(lax.broadcasted_iota(jnp.int32, (C, C), 1) <= lax.broadcasted_iota(jnp.int32, (C, C), 0), 1.0, 0.0)
    cum_all = _dot_exact_rhs(tri.astype(BF16), lw_ref[0])
    pairs = range(n_pairs)
    ps = lambda p: slice(p * W, (p + 1) * W)

    r = [r_ref[0, :, ps(p)] for p in pairs]
    v = [v_ref[0, :, ps(p)] for p in pairs]
    kkr = [k_ref[0, :, ps(p)] * vec_ref[2:3, ps(p)] for p in pairs]
    ss = seg_sum(jnp.concatenate([x * x for x in kkr], axis=0))
    k2, at, rt, bt, kt, w_end = [], [], [], [], [], []
    for p in pairs:
        a = a_ref[0, :, ps(p)]
        cum = cum_all[:, ps(p)]
        kk = kkr[p] * lax.rsqrt(jnp.maximum(rows(ss, p), 1e-24))
        k2.append(k_ref[0, :, ps(p)] * (1.0 + (a - 1.0) * vec_ref[3:4, ps(p)]))
        e_pos = jnp.exp(cum)
        e_neg = jnp.exp(-cum)
        rt.append(r[p] * e_pos)
        at.append(-kk * jnp.exp(cum - lw_ref[0, :, ps(p)]))
        bt.append(kk * a * e_neg)
        kt.append(k2[p] * e_neg)
        w_end.append(e_pos[C - 1:C, :])

    P, U, m_rbk, x_r = [], [], [], []
    for p in pairs:
        ar = jnp.concatenate([at[p], rt[p]], axis=0)
        big = mm(ar, jnp.concatenate([bd(bt[p]), bd(kt[p])], axis=0), "nt")
        x = mm(ar, s_sc[p], "nt")
        P.append(jnp.where(strict, big[:C, :W], 0.0))
        l_ak = jnp.where(strict, big[:C, W:], 0.0)
        m_rbk.append(jnp.concatenate([jnp.where(incl, big[C:, :W], 0.0), jnp.where(incl, big[C:, W:], 0.0)], axis=1))
        U.append(x[:C] + mm(l_ak, bd(v[p])))
        x_r.append(x[C:])

    n_it = int(np.log2(C))
    for it in range(n_it):
        for p in pairs:
            if it == n_it - 1:
                U[p] = U[p] + mm_inv(P[p], bd(U[p]))
            else:
                res = mm_inv(P[p], jnp.concatenate([bd(P[p]), bd(U[p])], axis=1))
                U[p] = U[p] + res[:, W:]
                P[p] = res[:, :W]

    y = []
    for p in pairs:
        y.append(x_r[p] + mm(m_rbk[p], jnp.concatenate([bd(U[p]), bd(v[p])], axis=0)))
        upd = mm(jnp.concatenate([U[p], v[p]], axis=0),
                 jnp.concatenate([bt[p] * w_end[p], kt[p] * w_end[p]], axis=0), "tn")
        s_sc[p] = s_sc[p] * w_end[p] + jnp.where(diag_blocks, upd, 0.0)

    inv_n = 1.0 / RW_HEAD_DIM
    y_all = jnp.concatenate(y, axis=0)
    mu = seg_sum(y_all) * inv_n
    var = seg_sum(jnp.square(y_all - mu)) * inv_n
    rk = seg_sum(jnp.concatenate([r[p] * k2[p] * vec_ref[4:5, ps(p)] for p in pairs], axis=0))
    yn = (y_all - mu) * lax.rsqrt(var + RW_GN_EPS)
    for p in pairs:
        y_o[0, :, ps(p)] = rows(yn, p) * vec_ref[5:6, ps(p)] + vec_ref[6:7, ps(p)] + rows(rk, p) * v[p]

    @pl.when(c == pl.num_programs(1) - 1)
    def _():
        sT_o[0] = s_sc[...]


def _wkv(r, lw, k, v, a, vec, s0, *, hp, inv_hp):
    B, T, D = r.shape
    H = D // RW_HEAD_DIM
    C, W, n_pairs = WKV_CHUNK, 2 * RW_HEAD_DIM, H // 2
    s_bd = jnp.zeros((B, n_pairs, W, W), F32)
    s_bd = s_bd.at[:, :, :C, :C].set(s0[:, 0::2]).at[:, :, C:, C:].set(s0[:, 1::2])
    row_spec = pl.BlockSpec((1, C, D), lambda b, c: (b, c, 0))
    st_spec = pl.BlockSpec((1, n_pairs, W, W), lambda b, c: (b, 0, 0, 0))
    ones = _block_diag(W, RW_HEAD_DIM, 1.0)
    kern = functools.partial(_wkv_kernel, n_pairs=n_pairs, hp=hp, inv_hp=inv_hp)
    y, s_out = pl.pallas_call(
        kern, grid=(B, T // C),
        in_specs=[row_spec] * 5 + [pl.BlockSpec(vec.shape, lambda b, c: (0, 0)), pl.BlockSpec((W, W), lambda b, c: (0, 0)),
                                   st_spec],
        out_specs=[row_spec, st_spec],
        out_shape=[jax.ShapeDtypeStruct((B, T, D), F32), jax.ShapeDtypeStruct(s_bd.shape, F32)],
        scratch_shapes=[pltpu.VMEM((n_pairs, W, W), F32)],
        compiler_params=_cparams(("parallel", "arbitrary")), name="wkv_chunk",
    )(r, lw, k, v, a, vec, ones, s_bd)
    s_fin = jnp.stack([s_out[:, :, :C, :C], s_out[:, :, C:, C:]], axis=2).reshape(B, H, RW_HEAD_DIM, RW_HEAD_DIM)
    return y, s_fin


def _row_call(kernel, rows, consts, out_widths, out_dtypes, tm, name):
    n = rows[0].shape[0]
    row_spec = lambda w: pl.BlockSpec((tm, w), lambda i: (i, 0))
    full = lambda a: pl.BlockSpec(a.shape, lambda i: (0,) * a.ndim)
    return pl.pallas_call(
        kernel, grid=(n // tm,),
        in_specs=[row_spec(a.shape[1]) for a in rows] + [full(c) for c in consts],
        out_specs=[row_spec(w) for w in out_widths],
        out_shape=[jax.ShapeDtypeStruct((n, w), dt) for w, dt in zip(out_widths, out_dtypes)],
        compiler_params=_cparams(("parallel",)), name=name,
    )(*rows, *consts)


def _rwkv_out_kernel(x_ref, y_ref, g_ref, wo_ref, o_ref, *, hp):
    o_ref[...] = x_ref[...] + _mm(y_ref[...] * g_ref[...], wo_ref[...], hp=hp)


def _kv_proj_kernel(h_ref, g_ref, wkv_ref, kg_ref, bd_ref, cmp_o, slc_o, win_o, *, hp):
    kv = _mm(_rms(h_ref[...], g_ref[...]), wkv_ref[...], hp=hp)
    cmp_o[...] = kv[:, :2 * KV_W]
    for br, out in ((1, slc_o), (2, win_o)):
        kraw = kv[:, br * 2 * KV_W: br * 2 * KV_W + KV_W]
        ms = _dot_hilo(kraw * kraw, bd_ref[...])
        out[:, :KV_W] = kraw * lax.rsqrt(ms + RMS_EPS) * kg_ref[br:br + 1, :]
        out[:, KV_W:] = kv[:, br * 2 * KV_W + KV_W: (br + 1) * 2 * KV_W]


def _q_proj_kernel(h_ref, g_ref, wq_ref, wg_ref, qg_ref, bd_ref, q_o, gate_o, *, hp, q_scale):
    hn = _rms(h_ref[...], g_ref[...])
    q = _mm(hn, wq_ref[...], hp=hp)
    ms = _dot_hilo(q * q, bd_ref[...])
    q_o[...] = (q * lax.rsqrt(ms + RMS_EPS) * qg_ref[...] * q_scale).astype(q_o.dtype)
    gate_o[...] = jax.nn.sigmoid(_mm(hn, wg_ref[...], hp=hp))


def _merge_kernel(h_ref, gate_ref, oc_ref, os_ref, ow_ref, eg_ref, wo_ref, o_ref, *, hp):
    D = h_ref.shape[1]
    ge = _dot_hilo(gate_ref[...], eg_ref[...])
    o = ge[:, :D] * oc_ref[...] + ge[:, D:2 * D] * os_ref[...] + ge[:, 2 * D:] * ow_ref[...]
    o_ref[...] = h_ref[...] + _mm(o, wo_ref[...], hp=hp)


def _router_kernel(h_ref, g_ref, wr_ref, xn_o, logit_o):
    xn = _rms(h_ref[...], g_ref[...])
    xn_o[...] = xn.astype(xn_o.dtype)
    logit_o[...] = _mm(xn, wr_ref[...], hp=True)


def _ffn_kernel(x_ref, g_ref, wg_ref, wu_ref, wd_ref, o_ref, xn_sc, acc_sc, *, hp):
    f = pl.program_id(1)

    @pl.when(f == 0)
    def _():
        xn_sc[...] = _rms(x_ref[...], g_ref[...]).astype(xn_sc.dtype)
        acc_sc[...] = jnp.zeros_like(acc_sc)

    xn = xn_sc[...]
    hid = _silu(_mm(xn, wg_ref[...], hp=hp)) * _mm(xn, wu_ref[...], hp=hp)
    acc_sc[...] += _mm(hid, wd_ref[...], hp=hp)

    @pl.when(f == pl.num_programs(1) - 1)
    def _():
        o_ref[...] = x_ref[...] + acc_sc[...]


def _ffn(x, g, wg, wu, wd, *, tm, tf, hp):
    n, D = x.shape
    dff = wg.shape[1]
    return pl.pallas_call(
        functools.partial(_ffn_kernel, hp=hp), grid=(n // tm, dff // tf),
        in_specs=[pl.BlockSpec((tm, D), lambda i, f: (i, 0)), pl.BlockSpec((1, D), lambda i, f: (0, 0)),
                  pl.BlockSpec((D, tf), lambda i, f: (0, f)), pl.BlockSpec((D, tf), lambda i, f: (0, f)),
                  pl.BlockSpec((tf, D), lambda i, f: (f, 0))],
        out_specs=pl.BlockSpec((tm, D), lambda i, f: (i, 0)),
        out_shape=jax.ShapeDtypeStruct((n, D), F32),
        scratch_shapes=[pltpu.VMEM((tm, D), F32 if hp else BF16), pltpu.VMEM((tm, D), F32)],
        compiler_params=_cparams(("parallel", "arbitrary")), name="ffn_dense",
    )(x, g, wg, wu, wd)


def _moe_ffn_kernel(be_ref, nb_ref, x_ref, wg_ref, wu_ref, wd_ref, o_ref, acc_sc, *, hp):
    i = pl.program_id(0)
    f = pl.program_id(1)

    @pl.when(i < nb_ref[0])
    def _():
        @pl.when(f == 0)
        def _():
            acc_sc[...] = jnp.zeros_like(acc_sc)

        x = x_ref[...]
        hid = _silu(_mm(x, wg_ref[0], hp=hp)) * _mm(x, wu_ref[0], hp=hp)
        acc_sc[...] += _mm(hid, wd_ref[0], hp=hp)

        @pl.when(f == pl.num_programs(1) - 1)
        def _():
            o_ref[...] = acc_sc[...]

    @pl.when((i >= nb_ref[0]) & (f == pl.num_programs(1) - 1))
    def _():
        o_ref[...] = jnp.zeros_like(o_ref)


def _moe_ffn(block_e, n_used, xs, wg, wu, wd, *, tm, tf, hp):
    cap, D = xs.shape
    dff = wg.shape[2]
    w_idx = lambda i, f, be, nb: jnp.where(i < nb[0], f, dff // tf - 1)
    grid_spec = pltpu.PrefetchScalarGridSpec(
        num_scalar_prefetch=2, grid=(cap // tm, dff // tf),
        in_specs=[pl.BlockSpec((tm, D), lambda i, f, be, nb: (i, 0)),
                  pl.BlockSpec((1, D, tf), lambda i, f, be, nb: (be[i], 0, w_idx(i, f, be, nb))),
                  pl.BlockSpec((1, D, tf), lambda i, f, be, nb: (be[i], 0, w_idx(i, f, be, nb))),
                  pl.BlockSpec((1, tf, D), lambda i, f, be, nb: (be[i], w_idx(i, f, be, nb), 0))],
        out_specs=pl.BlockSpec((tm, D), lambda i, f, be, nb: (i, 0)),
        scratch_shapes=[pltpu.VMEM((tm, D), F32)])
    return pl.pallas_call(
        functools.partial(_moe_ffn_kernel, hp=hp), grid_spec=grid_spec, out_shape=jax.ShapeDtypeStruct((cap, D), F32),
        compiler_params=_cparams(("arbitrary", "arbitrary")), name="moe_ffn",
    )(block_e, n_used, xs, wg, wu, wd)


def _compress_kernel(x_ref, pe_ref, w1_ref, w2_ref, kg_ref, o_ref, *, hp):
    c = pl.program_id(1)
    x = x_ref[0, 0, 0]
    R = x.shape[0]
    p0 = _mm(x + pe_ref[0, 0], w1_ref[0, 0], hp=hp)
    p1 = _mm(x + pe_ref[0, 1], w1_ref[0, 1], hp=hp)
    pre = p0 + pltpu.roll(p1, R - 1, 0)
    out = _mm(_silu(pre), w2_ref[0], hp=hp)
    o_ref[0, 0, 0] = jnp.where(c == 0, _rms(out, kg_ref[...]), out)


def _compress(xsub, pe, w1, w2, kg, *, hp):
    S, _, G, R, W = xsub.shape
    return pl.pallas_call(
        functools.partial(_compress_kernel, hp=hp), grid=(S, 2, G),
        in_specs=[pl.BlockSpec((1, 1, 1, R, W), lambda s, c, g: (s, c, g, 0, 0)),
                  pl.BlockSpec((1, 2, 1, W), lambda s, c, g: (c, 0, 0, 0)),
                  pl.BlockSpec((1, 2, W, HEAD_DIM), lambda s, c, g: (c, 0, 0, 0)),
                  pl.BlockSpec((1, HEAD_DIM, HEAD_DIM), lambda s, c, g: (c, 0, 0)),
                  pl.BlockSpec((1, HEAD_DIM), lambda s, c, g: (0, 0))],
        out_specs=pl.BlockSpec((1, 1, 1, R, HEAD_DIM), lambda s, c, g: (s, c, g, 0, 0)),
        out_shape=jax.ShapeDtypeStruct((S, 2, G, R, HEAD_DIM), F32),
        compiler_params=_cparams(("parallel", "arbitrary", "arbitrary")), name="kv_compress",
    )(xsub, pe, w1, w2, kg)


def _paged_sub_proj_kernel(pt_ref, cache_hbm, w_ref, o_ref, buf, sem, *, pages_per_step, page, hp):
    lin = pl.program_id(0) * pl.num_programs(1) + pl.program_id(1)
    total = pl.num_programs(0) * pl.num_programs(1)
    slot = lin % 2

    n_lane_pairs = buf.shape[1]

    def page_copies(step, into):
        return [pltpu.make_async_copy(cache_hbm.at[pt_ref[step * pages_per_step + p], pl.ds(0, page), pl.ds(pair * LANES, LANES)],
                                      buf.at[into, pair, pl.ds(p * page, page)], sem.at[into])
                for p in range(pages_per_step) for pair in range(n_lane_pairs)]

    @pl.when(lin == 0)
    def _():
        for cp in page_copies(0, 0):
            cp.start()

    @pl.when(lin + 1 < total)
    def _():
        for cp in page_copies(lin + 1, 1 - slot):
            cp.start()

    for cp in page_copies(lin, slot):
        cp.wait()
    n = pages_per_step * page // CMP_STRIDE
    for pair in range(n_lane_pairs):
        c = pair // (N_KV_HEADS // 2)
        rows = buf.at[slot, pair]
        x2 = jnp.concatenate([rows[pl.ds(s, n, stride=CMP_STRIDE), :] for s in range(CMP_STRIDE)], axis=1)
        o_ref[0, pair] = _mm(x2, w_ref[c], hp=hp and c == 0)


def _paged_sub_proj(page_table, cache, w2x, *, pages_per_step, hp):
    S, n_pages = page_table.shape
    page, width = cache.shape[1:]
    n = pages_per_step * page // CMP_STRIDE
    n_t = n_pages // pages_per_step
    grid_spec = pltpu.PrefetchScalarGridSpec(
        num_scalar_prefetch=1, grid=(S, n_t),
        in_specs=[pl.BlockSpec(memory_space=pl.ANY), pl.BlockSpec(w2x.shape, lambda s, t, pt: (0, 0, 0))],
        out_specs=pl.BlockSpec((1, 4, n, w2x.shape[2]), lambda s, t, pt: (s, 0, t, 0)),
        scratch_shapes=[pltpu.VMEM((2, width // LANES, pages_per_step * page, LANES), F32), pltpu.SemaphoreType.DMA((2,))])
    kern = functools.partial(_paged_sub_proj_kernel, pages_per_step=pages_per_step, page=page, hp=hp)
    return pl.pallas_call(
        kern, grid_spec=grid_spec, out_shape=jax.ShapeDtypeStruct((S, 4, n_t * n, w2x.shape[2]), F32),
        compiler_params=_cparams(("arbitrary", "arbitrary")), name="paged_sub_proj",
    )(page_table.reshape(-1), cache, w2x)


def _compress_tail_kernel(p_ref, pe_ref, w1_ref, w2_ref, kg_ref, o_ref, *, hp):
    R = p_ref.shape[2]
    half_groups = N_KV_HEADS // 2
    for c in range(2):
        bias = (_mm(jnp.broadcast_to(pe_ref[c, 0], (SUBLANES, pe_ref.shape[3])), w1_ref[c, 0], hp=True)
                + _mm(jnp.broadcast_to(pe_ref[c, 1], (SUBLANES, pe_ref.shape[3])), w1_ref[c, 1], hp=True))[0:1]
        outs = []
        for g in range(N_KV_HEADS):
            p = p_ref[0, c * half_groups + g // 2]
            e = g % 2
            p0 = p[:, e * HEAD_DIM:(e + 1) * HEAD_DIM]
            p1 = p[:, (2 + e) * HEAD_DIM:(3 + e) * HEAD_DIM]
            out = _mm(_silu(p0 + pltpu.roll(p1, R - 1, 0) + bias), w2_ref[c], hp=hp)
            outs.append(_rms(out, kg_ref[...]) if c == 0 else out)
        o_ref[0, c] = jnp.concatenate(outs, axis=1)


def _compress_tail(p, pe, w1, w2, kg, *, hp):
    S, _, R, _ = p.shape
    full = lambda a: pl.BlockSpec(a.shape, lambda s: (0,) * a.ndim)
    return pl.pallas_call(
        functools.partial(_compress_tail_kernel, hp=hp), grid=(S,),
        in_specs=[pl.BlockSpec((1,) + p.shape[1:], lambda s: (s, 0, 0, 0)), full(pe), full(w1), full(w2), full(kg)],
        out_specs=pl.BlockSpec((1, 2, R, KV_W), lambda s: (s, 0, 0, 0)),
        out_shape=jax.ShapeDtypeStruct((S, 2, R, KV_W), F32),
        compiler_params=_cparams(("parallel",)), name="kv_compress_tail",
    )(p, pe, w1, w2, kg)


def _stack_heads(q, g, tq):
    return jnp.concatenate([q[:, (HEADS_PER_KV * g + i) * HEAD_DIM:(HEADS_PER_KV * g + i + 1) * HEAD_DIM]
                            for i in range(HEADS_PER_KV)], axis=0)


def _cmp_attn_kernel(q_ref, kc_ref, vc_ref, m_ref, o_o, sel_o, *, n_cmp, n_slc, pos_base, per_tile, hp):
    tq = q_ref.shape[1]
    ncp = kc_ref.shape[1]
    nsp = m_ref.shape[1]
    q0 = pos_base + (pl.program_id(1) * tq if per_tile else 0)
    rows = HEADS_PER_KV * tq
    q_pos = q0 + lax.broadcasted_iota(jnp.int32, (rows, ncp), 0) % tq
    n_id = lax.broadcasted_iota(jnp.int32, (rows, ncp), 1)
    bias = jnp.where((n_id * CMP_STRIDE + CMP_BLOCK - 1 <= q_pos) & (n_id < n_cmp), 0.0, MASKED)
    blk = lax.broadcasted_iota(jnp.int32, (tq, nsp), 1)
    cur = (q0 + lax.broadcasted_iota(jnp.int32, (tq, nsp), 0)) // SLC_BLOCK
    forced = (blk == 0) | (blk == cur) | (blk == cur - 1)
    q = q_ref[0]
    scores = []
    for g in range(N_KV_HEADS):
        qs = _stack_heads(q, g, tq)
        s = _mm(qs, kc_ref[0, :, g * HEAD_DIM:(g + 1) * HEAD_DIM], "nt", hp) + bias
        m = jnp.maximum(jnp.max(s, -1, keepdims=True), M_INIT)
        e = jnp.exp(s - m)
        p = e / jnp.maximum(jnp.sum(e, -1, keepdims=True), 1e-30)
        o = _mm(p, vc_ref[0, :, g * HEAD_DIM:(g + 1) * HEAD_DIM], hp=hp)
        psum = p[0:tq]
        for i in range(HEADS_PER_KV):
            col = (HEADS_PER_KV * g + i) * HEAD_DIM
            o_o[0, :, col:col + HEAD_DIM] = o[i * tq:(i + 1) * tq]
            if i:
                psum = psum + p[i * tq:(i + 1) * tq]
        imp = _dot_hilo(psum, m_ref[...])
        scores.append(jnp.where(blk <= cur, jnp.where(forced, BIG, imp), -BIG))
    score = jnp.concatenate(scores, axis=0)
    blk_r = lax.broadcasted_iota(jnp.int32, score.shape, 1)

    def count_ahead(i, rank):
        s_i = jnp.sum(jnp.where(blk_r == i, score, 0.0), axis=-1, keepdims=True)
        return rank + jnp.where(s_i > score, 1.0, jnp.where((s_i == score) & (i < blk_r), 1.0, 0.0))

    rank = lax.fori_loop(0, n_slc, count_ahead, jnp.zeros(score.shape, F32), unroll=8)
    for g in range(N_KV_HEADS):
        sel_o[0, :, g * nsp:(g + 1) * nsp] = jnp.where(rank[g * tq:(g + 1) * tq] < min(SLC_TOP, n_slc), 1.0, 0.0).astype(sel_o.dtype)


def _cmp_attn(q, kc, vc, m, *, n_cmp, n_slc, pos_base, per_tile, tq, hp):
    B, T, D = q.shape
    ncp, nsp = m.shape
    kv_spec = pl.BlockSpec((1, ncp, KV_W), lambda b, i: (b, 0, 0))
    kern = functools.partial(_cmp_attn_kernel, n_cmp=n_cmp, n_slc=n_slc, pos_base=pos_base, per_tile=per_tile, hp=hp)
    return pl.pallas_call(
        kern, grid=(B, T // tq),
        in_specs=[pl.BlockSpec((1, tq, D), lambda b, i: (b, i, 0)), kv_spec, kv_spec,
                  pl.BlockSpec((ncp, nsp), lambda b, i: (0, 0))],
        out_specs=[pl.BlockSpec((1, tq, D), lambda b, i: (b, i, 0)),
                   pl.BlockSpec((1, tq, N_KV_HEADS * nsp), lambda b, i: (b, i, 0))],
        out_shape=[jax.ShapeDtypeStruct((B, T, D), F32), jax.ShapeDtypeStruct((B, T, N_KV_HEADS * nsp), BF16)],
        compiler_params=_cparams(("parallel", "arbitrary")), name="cmp_attn",
    )(q, kc, vc, m)


def _pair_rows(q, g):
    base = g * HEADS_PER_KV * HEAD_DIM
    return jnp.concatenate([q[:, base:base + LANES], q[:, base + LANES:base + 2 * LANES]], axis=0)


def _finish_pair(acc_even, acc_odd, p, tq):
    a0 = acc_even[p * tq:(p + 1) * tq]
    a1 = acc_odd[p * tq:(p + 1) * tq]
    o0 = a0 / jnp.maximum(a0[:, HEAD_DIM:HEAD_DIM + 1], 1e-30)
    o1 = a1 / jnp.maximum(a1[:, HEAD_DIM:HEAD_DIM + 1], 1e-30)
    lane = lax.broadcasted_iota(jnp.int32, (tq, LANES), 1)
    return jnp.where(lane < HEAD_DIM, o0, pltpu.roll(o1, HEAD_DIM, 1))


def _cmp_sel_kernel(q_ref, kab_ref, v1_ref, m_ref, o_o, sel_o, *, n_cmp, n_slc, bounded):
    tq = q_ref.shape[1]
    ncp = kab_ref.shape[1]
    nsp = LANES // N_KV_HEADS
    q0 = pl.program_id(1) * tq
    q_pos = q0 + lax.broadcasted_iota(jnp.int32, (tq, ncp), 0)
    n_id = lax.broadcasted_iota(jnp.int32, (tq, ncp), 1)
    b = jnp.where((n_id * CMP_STRIDE + CMP_BLOCK - 1 <= q_pos) & (n_id < n_cmp), 0.0, MASKED)
    b2 = jnp.concatenate([b, b], axis=0)
    q = q_ref[0]
    imp = jnp.zeros((tq, LANES), F32)
    for g in range(N_KV_HEADS):
        lhs = _pair_rows(q, g)
        vv = v1_ref[0, :, g * LANES:(g + 1) * LANES]
        accs, psum = [], None
        for half in range(2):
            s = _mm(lhs, kab_ref[0, :, (2 * g + half) * LANES:(2 * g + half + 1) * LANES], "nt") + b2
            e = jnp.exp2(s) if bounded else jnp.exp2(s - jnp.maximum(jnp.max(s, -1, keepdims=True), M_INIT))
            acc = _mm(e, vv)
            accs.append(acc)
            p = e / jnp.maximum(acc[:, HEAD_DIM:HEAD_DIM + 1], 1e-30)
            ph = p[:tq] + p[tq:]
            psum = ph if psum is None else psum + ph
        imp = imp + _dot_hilo(psum, m_ref[g])
        for p_ in range(2):
            col = g * 2 * LANES + p_ * LANES
            o_o[0, :, col:col + LANES] = _finish_pair(accs[0], accs[1], p_, tq)
    lane = lax.broadcasted_iota(jnp.int32, (tq, LANES), 1)
    blk = lane % nsp
    cur = (q0 + lax.broadcasted_iota(jnp.int32, (tq, LANES), 0)) // SLC_BLOCK
    forced = (blk == 0) | (blk == cur) | (blk == cur - 1)
    score = jnp.where(blk <= cur, jnp.where(forced, BIG, imp), -BIG)
    rank = jnp.zeros((tq, LANES), F32)
    for d in range(1, nsp):
        wrapped = blk + d >= nsp
        partner = jnp.where(wrapped, pltpu.roll(score, nsp - d, 1), pltpu.roll(score, LANES - d, 1))
        rank = rank + jnp.where(partner > score, 1.0, jnp.where((partner == score) & wrapped, 1.0, 0.0))
    sel_o[0] = jnp.where(rank < min(SLC_TOP, n_slc), 1.0, 0.0).astype(sel_o.dtype)


def _cmp_sel(q, kab, v1, m, *, n_cmp, n_slc, tq, bounded):
    B, T, D = q.shape
    ncp = kab.shape[1]
    assert m.shape == (N_KV_HEADS, ncp, LANES)
    whole = lambda a: pl.BlockSpec((1,) + a.shape[1:], lambda b, i: (b,) + (0,) * (a.ndim - 1))
    kern = functools.partial(_cmp_sel_kernel, n_cmp=n_cmp, n_slc=n_slc, bounded=bounded)
    return pl.pallas_call(
        kern, grid=(B, T // tq),
        in_specs=[pl.BlockSpec((1, tq, D), lambda b, i: (b, i, 0)), whole(kab), whole(v1),
                  pl.BlockSpec(m.shape, lambda b, i: (0, 0, 0))],
        out_specs=[pl.BlockSpec((1, tq, D), lambda b, i: (b, i, 0)), pl.BlockSpec((1, tq, LANES), lambda b, i: (b, i, 0))],
        out_shape=[jax.ShapeDtypeStruct((B, T, D), F32), jax.ShapeDtypeStruct((B, T, LANES), BF16)],
        compiler_params=_cparams(("parallel", "arbitrary")), name="cmp_sel_attn",
    )(q, kab, v1, m)


def _flash_pairs(lhs, kab_ref, v1_ref, kt_lo, kt_hi, tk, bias_fn, bounded):
    rows = lhs[0].shape[0]

    def body(kt, carry):
        k0 = pl.multiple_of(kt * tk, tk)
        biases = bias_fn(k0)
        out = []
        for g in range(N_KV_HEADS):
            vv = v1_ref[0, pl.ds(k0, tk), g * LANES:(g + 1) * LANES]
            for half in range(2):
                idx = 2 * g + half
                s = _mm(lhs[g], kab_ref[0, pl.ds(k0, tk), idx * LANES:(idx + 1) * LANES], "nt") + biases[g]
                if bounded:
                    out.append(carry[idx] + _mm(jnp.exp2(s), vv))
                else:
                    m, acc = carry[idx]
                    m_new = jnp.maximum(m, jnp.max(s, -1, keepdims=True))
                    out.append((m_new, jnp.exp2(m - m_new) * acc + _mm(jnp.exp2(s - m_new), vv)))
        return tuple(out)

    zero = jnp.zeros((rows, LANES), F32)
    if bounded:
        return lax.fori_loop(kt_lo, kt_hi, body, (zero,) * (2 * N_KV_HEADS))
    start = (jnp.full((rows, 1), M_INIT, F32), zero)
    return tuple(acc for _, acc in lax.fori_loop(kt_lo, kt_hi, body, (start,) * (2 * N_KV_HEADS)))


def _slc_win_attn_kernel(q_ref, sel_ref, skab_ref, sv1_ref, wkab_ref, wv1_ref, os_o, ow_o, *, tk, bounded):
    tq = q_ref.shape[1]
    nsp = sel_ref.shape[2] // N_KV_HEADS
    q0 = pl.program_id(1) * tq
    q_pos = q0 + lax.broadcasted_iota(jnp.int32, (tq, tk), 0)
    k_off = lax.broadcasted_iota(jnp.int32, (tq, tk), 1)
    sel_lane = lax.broadcasted_iota(jnp.int32, (N_KV_HEADS * nsp, tk), 0)
    blk_k = lax.broadcasted_iota(jnp.int32, (N_KV_HEADS * nsp, tk), 1)
    drop = jnp.where(sel_ref[0].astype(F32) > 0.5, 0.0, MASKED).astype(BF16)
    q = q_ref[0]
    kt_hi = (q0 + tq + tk - 1) // tk
    kt_lo_win = jnp.maximum(q0 - WINDOW + 1, 0) // tk

    twice = lambda b: jnp.concatenate([b, b], axis=0)

    def win_bias(k0):
        d = q_pos - (k_off + k0)
        return [twice(jnp.where((d >= 0) & (d < WINDOW), 0.0, MASKED))] * N_KV_HEADS

    def slc_bias(k0):
        causal = jnp.where(k_off + k0 <= q_pos, 0.0, MASKED)
        key_blk = (blk_k + k0) // SLC_BLOCK
        out = []
        for g in range(N_KV_HEADS):
            expand = jnp.where(g * nsp + key_blk == sel_lane, 1.0, 0.0).astype(BF16)
            out.append(twice(jnp.dot(drop, expand, preferred_element_type=F32) + causal))
        return out

    lhs = [_pair_rows(q, g) for g in range(N_KV_HEADS)]
    acc_s = _flash_pairs(lhs, skab_ref, sv1_ref, 0, kt_hi, tk, slc_bias, bounded)
    acc_w = _flash_pairs(lhs, wkab_ref, wv1_ref, kt_lo_win, kt_hi, tk, win_bias, bounded)
    for g in range(N_KV_HEADS):
        for p_ in range(2):
            col = g * 2 * LANES + p_ * LANES
            os_o[0, :, col:col + LANES] = _finish_pair(acc_s[2 * g], acc_s[2 * g + 1], p_, tq)
            ow_o[0, :, col:col + LANES] = _finish_pair(acc_w[2 * g], acc_w[2 * g + 1], p_, tq)


def _slc_win_attn(q, sel, skab, sv1, wkab, wv1, *, tq, tk, bounded):
    B, T, D = q.shape
    whole = lambda a: pl.BlockSpec((1,) + a.shape[1:], lambda b, i: (b, 0, 0))
    row = lambda w: pl.BlockSpec((1, tq, w), lambda b, i: (b, i, 0))
    return pl.pallas_call(
        functools.partial(_slc_win_attn_kernel, tk=tk, bounded=bounded), grid=(B, T // tq),
        in_specs=[row(D), row(sel.shape[2]), whole(skab), whole(sv1), whole(wkab), whole(wv1)],
        out_specs=[row(D), row(D)],
        out_shape=[jax.ShapeDtypeStruct((B, T, D), F32)] * 2,
        compiler_params=_cparams(("parallel", "arbitrary")), name="slc_win_attn",
    )(q, sel, skab, sv1, wkab, wv1)


def _pair_keys(k):
    z = jnp.zeros_like(k)
    return jnp.concatenate([k, z, z, k], axis=-1).reshape(k.shape[0], k.shape[1], -1)


def _ones_values(v):
    return jnp.concatenate([v, jnp.ones_like(v)], axis=-1).reshape(v.shape[0], v.shape[1], -1)


def _sample_attn_kernel(blk_ref, q_ref, kpos_ref, new_ref, win_ref, cache_hbm, os_o, ow_o, kbuf, vbuf, sem, *,
                        n_q, n_sel, pos_base, win_base, n_win, hp):
    b = pl.program_id(0)
    slot = b % 2
    per_seq = n_q * N_KV_HEADS * n_sel
    per_page = cache_hbm.shape[1] // SLC_BLOCK

    def pool_copies(step, into, start):
        def one(i, carry):
            blk = blk_ref[step * per_seq + i]
            qg = i // n_sel

            @pl.when(blk >= 0)
            def _():
                src_rows = pl.ds(pl.multiple_of((blk % per_page) * SLC_BLOCK, SLC_BLOCK), SLC_BLOCK)
                for c, dst in ((0, kbuf), (1, vbuf)):
                    cp = pltpu.make_async_copy(cache_hbm.at[blk // per_page, src_rows, c, qg % N_KV_HEADS],
                                               dst.at[into, qg, pl.ds((i % n_sel) * SLC_BLOCK, SLC_BLOCK)], sem.at[into])
                    cp.start() if start else cp.wait()
            return carry
        lax.fori_loop(0, per_seq, one, 0)

    @pl.when(b == 0)
    def _():
        pool_copies(0, 0, True)

    @pl.when(b + 1 < pl.num_programs(0))
    def _():
        pool_copies(b + 1, 1 - slot, True)

    pool_copies(b, slot, False)

    def new_rows(i, carry):
        qg = i // n_sel

        @pl.when(blk_ref[b * per_seq + i] < 0)
        def _():
            kbuf[slot, qg, pl.ds((i % n_sel) * SLC_BLOCK, SLC_BLOCK), :] = new_ref[0, qg % N_KV_HEADS]
            vbuf[slot, qg, pl.ds((i % n_sel) * SLC_BLOCK, SLC_BLOCK), :] = new_ref[0, N_KV_HEADS + qg % N_KV_HEADS]
        return carry
    lax.fori_loop(0, per_seq, new_rows, 0)

    tq = q_ref.shape[1]
    rows = HEADS_PER_KV * tq
    nk = n_sel * SLC_BLOCK
    nwp = win_ref.shape[1]
    q = q_ref[0]
    row_q = lax.broadcasted_iota(jnp.int32, (rows, 1), 0) % tq
    w_id = lax.broadcasted_iota(jnp.int32, (rows, nwp), 1)
    d_win = (pos_base + row_q) - (win_base + w_id)
    win_bias = jnp.where((d_win >= 0) & (d_win < WINDOW) & (win_base + w_id >= 0) & (w_id < n_win), 0.0, MASKED)

    def attend(s, v):
        m = jnp.maximum(jnp.max(s, -1, keepdims=True), M_INIT)
        e = jnp.exp(s - m)
        p = e / jnp.maximum(jnp.sum(e, -1, keepdims=True), 1e-30)
        return _mm(p, v, hp=hp)

    for g in range(N_KV_HEADS):
        qs = _stack_heads(q, g, tq)
        o_s = jnp.zeros((rows, HEAD_DIM), F32)
        for qi in range(n_q):
            qg = qi * N_KV_HEADS + g
            kpos = kpos_ref[0, qg:qg + 1, :]
            bias = jnp.broadcast_to(jnp.where(kpos <= pos_base + qi, 0.0, MASKED), (rows, nk))
            o_qi = attend(_mm(qs, kbuf[slot, qg], "nt", hp) + bias, vbuf[slot, qg])
            o_s = jnp.where(row_q == qi, o_qi, o_s)
        o_w = attend(_mm(qs, win_ref[0, :, g * HEAD_DIM:(g + 1) * HEAD_DIM], "nt", hp) + win_bias,
                     win_ref[0, :, KV_W + g * HEAD_DIM:KV_W + (g + 1) * HEAD_DIM])
        for i in range(HEADS_PER_KV):
            col = (HEADS_PER_KV * g + i) * HEAD_DIM
            os_o[0, :, col:col + HEAD_DIM] = o_s[i * tq:(i + 1) * tq]
            ow_o[0, :, col:col + HEAD_DIM] = o_w[i * tq:(i + 1) * tq]


def _sample_attn(pool_blk, q, kpos, new_win, win_all, cache_blocks, *, n_q, n_sel, pos_base, win_base, n_win, hp):
    B, tq, D = q.shape
    kern = functools.partial(_sample_attn_kernel, n_q=n_q, n_sel=n_sel, pos_base=pos_base, win_base=win_base,
                             n_win=n_win, hp=hp)
    blk = lambda a: pl.BlockSpec((1,) + a.shape[1:], lambda b, pb: (b,) + (0,) * (a.ndim - 1))
    buf = pltpu.VMEM((2, n_q * N_KV_HEADS, n_sel * SLC_BLOCK, HEAD_DIM), F32)
    grid_spec = pltpu.PrefetchScalarGridSpec(
        num_scalar_prefetch=1, grid=(B,),
        in_specs=[blk(q), blk(kpos), blk(new_win), blk(win_all), pl.BlockSpec(memory_space=pl.ANY)],
        out_specs=[blk(q), blk(q)],
        scratch_shapes=[buf, buf, pltpu.SemaphoreType.DMA((2,))])
    return pl.pallas_call(
        kern, grid_spec=grid_spec, out_shape=[jax.ShapeDtypeStruct((B, tq, D), F32)] * 2,
        compiler_params=_cparams(("arbitrary",)), name="sample_slc_win_attn",
    )(pool_blk.reshape(-1), q, kpos, new_win, win_all, cache_blocks)


def _pick_tile(n, cap=512):
    return next(t for t in (512, 256, 128, 64, 32, 16, 8) if t <= cap and n % t == 0)


def _pad_rows(x, t_pad):
    return jnp.pad(x, ((0, 0), (0, t_pad - x.shape[1]), (0, 0)))


def _rwkv_layer(x, shift0, wkv0, p, *, tm, hp, inv_hp):
    B, T, D = x.shape
    t_pad = -(-T // tm) * tm
    xp = _pad_rows(x, t_pad)
    r, lw, k, v, a, g, shift = _rwkv_proj(xp, shift0[:, None, :], p["g"], p["mix"], p["vec"], p["wrkv"], p["dw1"],
                                          p["dw2"], p["aw1"], p["aw2"], p["gw1"], p["gw2"], t_real=T, tm=tm, hp=hp)
    if t_pad != T:
        live = (jnp.arange(t_pad) < T)[None, :, None]
        r, lw, k, v, a = (jnp.where(live, t, 0.0) for t in (r, lw, k, v, a))
    y, s_fin = _wkv(r, lw, k, v, a, p["vec"], wkv0, hp=hp, inv_hp=inv_hp)
    n = B * t_pad
    out = _row_call(functools.partial(_rwkv_out_kernel, hp=hp), [xp.reshape(n, D), y.reshape(n, D), g.reshape(n, D)],
                    [p["wo"]], [D], [F32], _pick_tile(n), "rwkv_out")[0]
    return out.reshape(B, t_pad, D)[:, :T], shift[:, 0], s_fin


def _sub_blocks(rows):
    S, L = rows.shape[:2]
    n_sub = L // CMP_STRIDE
    r = rows[:, :n_sub * CMP_STRIDE].reshape(S, n_sub, CMP_STRIDE, 2, N_KV_HEADS, HEAD_DIM)
    r = jnp.transpose(r, (0, 3, 4, 1, 2, 5)).reshape(S, 2, N_KV_HEADS, n_sub, CMP_STRIDE * HEAD_DIM)
    r_pad = -(-n_sub // SUBLANES) * SUBLANES
    return jnp.pad(r, ((0, 0),) * 3 + ((0, r_pad - n_sub), (0, 0))), n_sub - CMP_BLOCK // CMP_STRIDE + 1


def _compressed_kv(rows, p, *, hp):
    xsub, n_cmp = _sub_blocks(rows)
    out = _compress(xsub, p["cmp_pe"], p["cmp_w1"], p["cmp_w2"], p["k_norm_g"][0:1], hp=hp)
    S, _, G, R, dh = out.shape
    out = jnp.transpose(out, (1, 0, 3, 2, 4)).reshape(2, S, R, G * dh).astype(F32 if hp else BF16)
    return out[0], out[1], n_cmp


def _overlap_matrix(ncp, n_cmp, n_slc, nsp):
    c0 = np.arange(ncp)[:, None] * CMP_STRIDE
    s0 = np.arange(nsp)[None, :] * SLC_BLOCK
    m = (c0 < s0 + SLC_BLOCK) & (c0 + CMP_BLOCK > s0) & (np.arange(ncp)[:, None] < n_cmp) & (np.arange(nsp)[None, :] < n_slc)
    return jnp.asarray(m.astype(np.float32), dtype=BF16)


def _moe(h, p, *, tm, tf, hp):
    n, D = h.shape
    xn, logits = _row_call(_router_kernel, [h], [p["g_moe"], p["w_router"]], [D, LANES], [F32 if hp else BF16, F32],
                           _pick_tile(n), "moe_router")
    top_v, top_e = lax.top_k(logits[:, :N_EXPERTS], TOP_K)
    gates = jax.nn.softmax(top_v, -1)
    n_pairs = n * TOP_K
    flat_e = top_e.reshape(n_pairs)
    order = jnp.argsort(flat_e)
    seen = jnp.cumsum((flat_e[:, None] == jnp.arange(N_EXPERTS)).astype(jnp.int32), axis=0)
    counts = seen[-1]
    padded = (counts + tm - 1) // tm * tm
    pad_end = jnp.cumsum(padded)
    pad_start = pad_end - padded
    start = jnp.cumsum(counts) - counts
    rank = jnp.take_along_axis(seen, flat_e[:, None], axis=1)[:, 0] - 1
    dest = (pad_start[flat_e] + rank).reshape(n, TOP_K)
    n_blocks = -(-n_pairs // tm) + N_EXPERTS
    cap = n_blocks * tm
    block_e = jnp.minimum(jnp.searchsorted(pad_end, jnp.arange(n_blocks) * tm, side="right"), N_EXPERTS - 1).astype(jnp.int32)
    row_e = jnp.repeat(block_e, tm)
    k_in_e = jnp.arange(cap, dtype=jnp.int32) - pad_start[row_e]
    src = order[jnp.clip(start[row_e] + k_in_e, 0, n_pairs - 1)] // TOP_K
    buf_tok = jnp.where((k_in_e >= 0) & (k_in_e < counts[row_e]), src, n).astype(jnp.int32)
    n_used = (pad_end[-1] // tm).astype(jnp.int32).reshape(1)
    xs = jnp.concatenate([xn, jnp.zeros((1, D), xn.dtype)], 0)[buf_tok]
    ys = _moe_ffn(block_e, n_used, xs, p["moe_wg"], p["moe_wu"], p["moe_wd"], tm=tm, tf=tf, hp=hp)
    y = ys[dest[:, 0]] * gates[:, 0:1] + ys[dest[:, 1]] * gates[:, 1:2]
    return h + y


def kernel(x_prompt, x_sample, cache_cmp_kv, cache_slc_kv, state_win_kv, state_wkv, state_shift, page_table, norm_g, rw_mix, rw_vec, rw_w_rkv, rw_w_o, rw_decay_w1, rw_decay_w2, rw_iclr_w1, rw_iclr_w2, rw_gate_w1, rw_gate_w2, ffn_w_gate, ffn_w_up, ffn_w_down, moe_router, moe_w_gate, moe_w_up, moe_w_down, kv_norm_g, w_kv, k_norm_g, cmp_pe, cmp_w1, cmp_w2, w_qg, q_norm_g, w_o):
    B, T, D = x_prompt.shape
    SB, S, _ = x_sample.shape
    assert norm_g.shape[0] == 2 and rw_mix.shape[0] == 1 and w_qg.shape[0] == 1, "one RWKV-7 layer + one NSA layer"
    assert D == N_HEADS * HEAD_DIM and T % 512 == 0 and S <= SUBLANES
    page = cache_cmp_kv.shape[1]
    past = page_table.shape[1] * page
    wb = state_win_kv.shape[1]
    assert past % SLC_BLOCK == 0 and page % SLC_BLOCK == 0 and S <= SLC_BLOCK
    bf = lambda w: w.astype(BF16)
    pad8 = lambda m: jnp.pad(m, ((0, SUBLANES - m.shape[0]), (0, 0)))
    dff = ffn_w_gate.shape[2]
    tf = dff // 2 if (dff // 2) % LANES == 0 else dff
    n_p, n_s = B * T, SB * S
    both = lambda w: (bf(w), w)
    pick = lambda d, hp: {k: (v[hp] if isinstance(v, tuple) else v) for k, v in d.items()}

    rw = dict(g=norm_g[0, 0][None], mix=pad8(rw_mix[0]), vec=pad8(rw_vec[0]), wrkv=both(rw_w_rkv[0]), wo=both(rw_w_o[0]),
              dw1=both(rw_decay_w1[0]), dw2=both(rw_decay_w2[0]), aw1=both(rw_iclr_w1[0]), aw2=both(rw_iclr_w2[0]),
              gw1=both(rw_gate_w1[0]), gw2=both(rw_gate_w2[0]))
    H = D // RW_HEAD_DIM
    hp_, shift_p, wkv_p = _rwkv_layer(x_prompt, jnp.zeros((B, D), F32), jnp.zeros((B, H, RW_HEAD_DIM, RW_HEAD_DIM), F32),
                                      pick(rw, 0), tm=256, hp=False, inv_hp=True)
    hs, shift_s, wkv_s = _rwkv_layer(x_sample, state_shift[0], state_wkv[0], pick(rw, 1), tm=WKV_CHUNK, hp=True, inv_hp=True)
    ffn_g = norm_g[0, 1][None]
    hp_ = _ffn(hp_.reshape(n_p, D), ffn_g, bf(ffn_w_gate[0]), bf(ffn_w_up[0]), bf(ffn_w_down[0]), tm=512, tf=tf, hp=False)
    hs = _ffn(hs.reshape(n_s, D), ffn_g, ffn_w_gate[0], ffn_w_up[0], ffn_w_down[0], tm=n_s, tf=tf, hp=True)

    kv_consts = lambda hp: [kv_norm_g[None], both(w_kv)[hp], jnp.tile(k_norm_g, (1, N_KV_HEADS)),
                            _block_diag(KV_W, HEAD_DIM, 1.0 / HEAD_DIM)]
    kv_rows = lambda h, tm, hp: _row_call(functools.partial(_kv_proj_kernel, hp=hp), [h], kv_consts(hp),
                                          [2 * KV_W] * 3, [F32] * 3, tm, "kv_proj")
    as_rows = lambda t, b: t.reshape(b, -1, 2, N_KV_HEADS, HEAD_DIM)
    cmp_p, slc_p, win_p = (as_rows(t, B) for t in kv_rows(hp_, 512, False))
    cmp_s, slc_s, win_s = (as_rows(t, SB) for t in kv_rows(hs, n_s, True))
    win_all_s = jnp.concatenate([state_win_kv, win_s], axis=1)

    cp = dict(cmp_pe=cmp_pe.reshape(2, CMP_BLOCK // CMP_STRIDE, 1, CMP_STRIDE * HEAD_DIM),
              cmp_w1=both(cmp_w1.reshape(2, CMP_BLOCK // CMP_STRIDE, CMP_STRIDE * HEAD_DIM, HEAD_DIM)),
              cmp_w2=both(cmp_w2), k_norm_g=k_norm_g)
    kc_p, vc_p, ncmp_p = _compressed_kv(cmp_p, pick(cp, 0), hp=False)
    assert S < CMP_STRIDE and page % CMP_STRIDE == 0
    n_ratio = CMP_BLOCK // CMP_STRIDE
    w1r = cmp_w1.reshape(2, n_ratio, CMP_STRIDE, HEAD_DIM, HEAD_DIM)
    w2x = jnp.einsum("cjsdh,ef->csedjfh", w1r, jnp.eye(2, dtype=F32)).reshape(2, CMP_STRIDE * LANES, n_ratio * LANES)
    n_pages = page_table.shape[1]
    pages_per_step = next(pp for pp in (32, 16, 8, 4, 2, 1) if n_pages % pp == 0)
    part_s = _paged_sub_proj(page_table, cache_cmp_kv.reshape(-1, page, 2 * KV_W), w2x, pages_per_step=pages_per_step, hp=True)
    kcv_s = _compress_tail(part_s, cp["cmp_pe"], cp["cmp_w1"][1], cmp_w2, k_norm_g[0:1], hp=True)
    kc_s, vc_s, ncmp_s = kcv_s[:, 0], kcv_s[:, 1], past // CMP_STRIDE - n_ratio + 1

    ge = np.zeros((LANES, N_BRANCH * D), np.float32)
    for br in range(N_BRANCH):
        for hd in range(N_HEADS):
            ge[br * N_HEADS + hd, br * D + hd * HEAD_DIM: br * D + (hd + 1) * HEAD_DIM] = 1.0
    nsa = dict(g_attn=norm_g[1, 0][None], wq=both(w_qg[0, :, :D]),
               wgate=both(jnp.pad(w_qg[0, :, D:], ((0, 0), (0, LANES - N_BRANCH * N_HEADS)))),
               q_norm_g=jnp.tile(q_norm_g[0], N_HEADS)[None], bd_q=_block_diag(D, HEAD_DIM, 1.0 / HEAD_DIM),
               w_o=both(w_o[0]), gate_expand=jnp.asarray(ge, dtype=BF16))

    def query_side(h, tm, hp, q_scale):
        c = pick(nsa, hp)
        return _row_call(functools.partial(_q_proj_kernel, hp=hp, q_scale=q_scale), [h],
                         [c["g_attn"], c["wq"], c["wgate"], c["q_norm_g"], c["bd_q"]], [D, LANES],
                         [F32 if hp else BF16, F32], tm, "q_proj")

    def merge(h, gates, o_cmp, o_slc, o_win, tm, hp):
        c = pick(nsa, hp)
        return _row_call(functools.partial(_merge_kernel, hp=hp), [h, gates, o_cmp, o_slc, o_win],
                         [c["gate_expand"], c["w_o"]], [D], [F32], tm, "nsa_merge")[0]

    q_p, gates_p = query_side(hp_, 512, False, ATTN_SCALE * LOG2E)
    q_p = q_p.reshape(B, T, D)
    n_slc_p = T // SLC_BLOCK
    nsp_p = LANES // N_KV_HEADS
    assert n_slc_p <= nsp_p, "the packed block-selection layout holds at most 32 selection blocks per kv group"
    m_one = _overlap_matrix(kc_p.shape[1], ncmp_p, n_slc_p, nsp_p)
    m_p = jnp.stack([jnp.pad(m_one, ((0, 0), (g * nsp_p, LANES - (g + 1) * nsp_p))) for g in range(N_KV_HEADS)])
    heads = lambda t: t.reshape(t.shape[0], t.shape[1], N_KV_HEADS, HEAD_DIM)
    kcab, vc1 = _pair_keys(heads(kc_p)), _ones_values(heads(vc_p))
    skab, sv1 = _pair_keys(bf(slc_p[:, :, 0])), _ones_values(bf(slc_p[:, :, 1]))
    wkab, wv1 = _pair_keys(bf(win_p[:, :, 0])), _ones_values(bf(win_p[:, :, 1]))

    def prompt_branches(bounded):
        o_cmp, sel = _cmp_sel(q_p, kcab, vc1, m_p, n_cmp=ncmp_p, n_slc=n_slc_p, tq=256, bounded=bounded)
        return (o_cmp,) + tuple(_slc_win_attn(q_p, sel, skab, sv1, wkab, wv1, tq=256, tk=256, bounded=bounded))

    score_bound = HEAD_DIM * ATTN_SCALE * jnp.max(jnp.abs(q_norm_g[0])) * jnp.max(jnp.abs(k_norm_g))
    o_cmp_p, o_slc_p, o_win_p = lax.cond(score_bound <= SCORE_BOUND_MAX, lambda: prompt_branches(True),
                                         lambda: prompt_branches(False))
    hp_ = merge(hp_, gates_p, o_cmp_p.reshape(n_p, D), o_slc_p.reshape(n_p, D), o_win_p.reshape(n_p, D), 512, False)

    kv_lanes = lambda t, dt: t.reshape(t.shape[0], t.shape[1], 2 * KV_W).astype(dt)
    q_s, gates_s = query_side(hs, n_s, True, ATTN_SCALE)
    q_s = _pad_rows(q_s.reshape(SB, S, D), SUBLANES)
    n_slc_s = (past + S - 1) // SLC_BLOCK + 1
    nsp_s = -(-n_slc_s // LANES) * LANES
    m_s = _overlap_matrix(kc_s.shape[1], ncmp_s, n_slc_s, nsp_s)
    o_cmp_s, sel_s = _cmp_attn(q_s, kc_s, vc_s, m_s, n_cmp=ncmp_s, n_slc=n_slc_s, pos_base=past, per_tile=False,
                               tq=SUBLANES, hp=True)
    n_sel = min(SLC_TOP, n_slc_s)
    sel_s = sel_s.reshape(SB, SUBLANES, N_KV_HEADS, nsp_s)[:, :S, :, :n_slc_s] > 0.5
    slot = jnp.cumsum(sel_s, axis=-1) - 1
    idx_s = jnp.sum(jnp.where(sel_s[..., None] & (slot[..., None] == jnp.arange(n_sel)),
                              jnp.arange(n_slc_s)[:, None], 0), axis=-2).astype(jnp.int32)
    in_pool = idx_s < past // SLC_BLOCK
    per_page = page // SLC_BLOCK
    blk_c = jnp.minimum(idx_s, past // SLC_BLOCK - 1)
    pages = jnp.take_along_axis(page_table, (blk_c // per_page).reshape(SB, -1), axis=1).reshape(idx_s.shape)
    pool_blk = jnp.where(in_pool, pages * per_page + blk_c % per_page, -1).astype(jnp.int32)
    new_rows = slc_s[:, jnp.clip(jnp.arange(SLC_BLOCK), 0, S - 1)]
    new_win = jnp.transpose(new_rows, (0, 2, 3, 1, 4)).reshape(SB, 2 * N_KV_HEADS, SLC_BLOCK, HEAD_DIM)
    kpos = (idx_s[..., None] * SLC_BLOCK + jnp.arange(SLC_BLOCK)).reshape(SB, S * N_KV_HEADS, n_sel * SLC_BLOCK).astype(jnp.int32)
    kpos = jnp.pad(kpos, ((0, 0), (0, -(-S * N_KV_HEADS // SUBLANES) * SUBLANES - S * N_KV_HEADS), (0, 0)))
    n_win = wb + S
    nwp = -(-n_win // SUBLANES) * SUBLANES
    win_lanes = _pad_rows(kv_lanes(win_all_s, F32), nwp)
    o_slc_s, o_win_s = _sample_attn(pool_blk, q_s, kpos, new_win, win_lanes, cache_slc_kv, n_q=S, n_sel=n_sel,
                                    pos_base=past, win_base=past - wb, n_win=n_win, hp=True)
    unpad = lambda t: t[:, :S].reshape(n_s, D)
    hs = merge(hs, gates_s, unpad(o_cmp_s), unpad(o_slc_s), unpad(o_win_s), n_s, True)

    moe = dict(g_moe=norm_g[1, 1][None], w_router=jnp.pad(moe_router[0], ((0, 0), (0, LANES - N_EXPERTS))),
               moe_wg=both(moe_w_gate[0]), moe_wu=both(moe_w_up[0]), moe_wd=both(moe_w_down[0]))
    hp_ = _moe(hp_, pick(moe, 0), tm=512, tf=tf, hp=False).reshape(B, T, D)
    hs = _moe(hs, pick(moe, 1), tm=128, tf=tf, hp=True).reshape(SB, S, D)

    keep_p = min(WINDOW, T)
    return (hp_, hs, cmp_p, cmp_s, slc_p, slc_s, win_p[:, T - keep_p:], win_all_s[:, win_all_s.shape[1] - wb:],
            wkv_p[None], wkv_s[None], shift_p[None], shift_s[None])
```

```python
import functools

import numpy as np
import jax
import jax.numpy as jnp
from jax import lax
from jax.experimental import pallas as pl
from jax.experimental.pallas import tpu as pltpu

F32 = jnp.float32
BF16 = jnp.bfloat16

RW_HEAD_DIM = 64
RW_GN_EPS = 64e-5
N_HEADS = 16
HEAD_DIM = 64
N_KV_HEADS = 4
HEADS_PER_KV = N_HEADS // N_KV_HEADS
KV_W = N_KV_HEADS * HEAD_DIM
N_BRANCH = 3
CMP_BLOCK = 32
CMP_STRIDE = 16
SLC_BLOCK = 64
SLC_TOP = 16
WINDOW = 512
ATTN_SCALE = HEAD_DIM ** -0.5
N_EXPERTS = 8
TOP_K = 2
RMS_EPS = 1e-6
MASKED = -1e30
M_INIT = -1e20
BIG = 1e30
LOG2E = 1.4426950408889634
SCORE_BOUND_MAX = 40.0

SUBLANES = 8
LANES = 128
VMEM_LIMIT = 56 * 1024 * 1024
WKV_CHUNK = RW_HEAD_DIM


def _cparams(sem):
    return pltpu.CompilerParams(dimension_semantics=sem, vmem_limit_bytes=VMEM_LIMIT)


_DIMS = {"nn": (((1,), (0,)), ((), ())), "nt": (((1,), (1,)), ((), ())), "tn": (((0,), (0,)), ((), ()))}


def _split2(x):
    hi = x.astype(BF16)
    lo = (x - hi.astype(F32)).astype(BF16)
    return hi, lo


def _mm(a, b, form="nn", hp=False):
    d = lambda s, t: lax.dot_general(s, t, _DIMS[form], preferred_element_type=F32)
    if not hp:
        return d(a.astype(BF16), b.astype(BF16))
    ah, al = _split2(a.astype(F32))
    bh, bl = _split2(b.astype(F32))
    return d(ah, bh) + d(ah, bl) + d(al, bh)


def _dot_hilo(x, m):
    hi, lo = _split2(x)
    return jnp.dot(hi, m, preferred_element_type=F32) + jnp.dot(lo, m, preferred_element_type=F32)


def _dot_exact_rhs(m, x):
    hi = x.astype(BF16)
    r1 = x - hi.astype(F32)
    mid = r1.astype(BF16)
    lo = (r1 - mid.astype(F32)).astype(BF16)
    d = lambda t: jnp.dot(m, t, preferred_element_type=F32)
    return d(hi) + d(mid) + d(lo)


def _rms(x, g):
    return x * lax.rsqrt(jnp.mean(x * x, -1, keepdims=True) + RMS_EPS) * g


def _silu(x):
    return x * jax.nn.sigmoid(x)


def _softplus(z):
    return jnp.maximum(z, 0.0) + jnp.log(1.0 + jnp.exp(-jnp.abs(z)))


def _block_diag(n, seg, value):
    i = np.arange(n)
    return jnp.asarray((i[:, None] // seg == i[None, :] // seg).astype(np.float32) * value, dtype=BF16)


def _rwkv_proj_kernel(x_ref, halo_ref, sh_ref, g_ref, mix_ref, vec_ref, wrkv_ref, dw1_ref, dw2_ref, aw1_ref, aw2_ref,
                      gw1_ref, gw2_ref, r_o, lw_o, k_o, v_o, a_o, g_o, shift_o, *, last_tile, last_row, hp):
    i = pl.program_id(1)
    mm = functools.partial(_mm, hp=hp)
    g = g_ref[...]
    xn = _rms(x_ref[0], g)
    hn = _rms(halo_ref[0, SUBLANES - 1:SUBLANES, :], g)
    prev_last = jnp.where(i == 0, sh_ref[0], hn)
    row = lax.broadcasted_iota(jnp.int32, xn.shape, 0)
    prev = jnp.where(row == 0, prev_last, pltpu.roll(xn, 1, 0))
    xx = prev - xn
    mixed = lambda j: xn + xx * mix_ref[j:j + 1, :]
    r_o[0] = mm(mixed(0), wrkv_ref[0])
    k_o[0] = mm(mixed(2), wrkv_ref[1])
    v_o[0] = mm(mixed(3), wrkv_ref[2])
    w0 = vec_ref[0:1, :]
    a0 = vec_ref[1:2, :]
    log_w = -_softplus(-(w0 + mm(jnp.tanh(mm(mixed(1), dw1_ref[...])), dw2_ref[...]))) - 0.5
    lw_o[0] = -jnp.exp(log_w)
    a_o[0] = jax.nn.sigmoid(a0 + mm(mm(mixed(4), aw1_ref[...]), aw2_ref[...]))
    g_o[0] = mm(jax.nn.sigmoid(mm(mixed(5), gw1_ref[...])), gw2_ref[...])

    @pl.when(i == last_tile)
    def _():
        shift_o[0] = xn[last_row:last_row + 1, :]


def _rwkv_proj(x, shift0, g, mix, vec, wrkv, dw1, dw2, aw1, aw2, gw1, gw2, *, t_real, tm, hp):
    B, T, D = x.shape
    nt = T // tm
    row_spec = pl.BlockSpec((1, tm, D), lambda b, i: (b, i, 0))
    halo_spec = pl.BlockSpec((1, SUBLANES, D), lambda b, i: (b, jnp.maximum(i * (tm // SUBLANES) - 1, 0), 0))
    vec_spec = pl.BlockSpec((1, 1, D), lambda b, i: (b, 0, 0))
    full = lambda a: pl.BlockSpec(a.shape, lambda b, i: (0,) * a.ndim)
    consts = (g, mix, vec, wrkv, dw1, dw2, aw1, aw2, gw1, gw2)
    out_sd = jax.ShapeDtypeStruct((B, T, D), F32)
    kern = functools.partial(_rwkv_proj_kernel, last_tile=(t_real - 1) // tm, last_row=(t_real - 1) % tm, hp=hp)
    return pl.pallas_call(
        kern, grid=(B, nt),
        in_specs=[row_spec, halo_spec, vec_spec] + [full(c) for c in consts],
        out_specs=[row_spec] * 6 + [vec_spec],
        out_shape=[out_sd] * 6 + [jax.ShapeDtypeStruct((B, 1, D), F32)],
        compiler_params=_cparams(("parallel", "arbitrary")), name="rwkv_proj",
    )(x, x, shift0, *consts)


def _wkv_kernel(r_ref, lw_ref, k_ref, v_ref, a_ref, vec_ref, ones_ref, s0_ref, y_o, sT_o, s_sc, *, n_pairs, hp, inv_hp):
    c = pl.program_id(1)
    C = r_ref.shape[1]
    W = 2 * C
    mm = functools.partial(_mm, hp=hp)
    mm_inv = functools.partial(_mm, hp=inv_hp)

    @pl.when(c == 0)
    def _():
        s_sc[...] = s0_ref[0]

    lane = lax.broadcasted_iota(jnp.int32, (C, W), 1)
    row = lax.broadcasted_iota(jnp.int32, (C, W), 0)
    left = lane < C
    col = jnp.where(left, lane, lane - C)
    strict = col < row
    incl = col <= row
    diag_blocks = (lax.broadcasted_iota(jnp.int32, (W, W), 0) < C) == (lax.broadcasted_iota(jnp.int32, (W, W), 1) < C)
    bd = lambda x: jnp.concatenate([jnp.where(left, x, 0.0), jnp.where(left, 0.0, x)], axis=0)
    seg_sum = lambda x: _dot_hilo(x, ones_ref[...])
    rows = lambda x, p: x[p * C:(p + 1) * C]
    tri = jnp.where(lax.broadcasted_iota(jnp.int32, (C, C), 1) <= lax.broadcasted_iota(jnp.int32, (C, C), 0), 1.0, 0.0)
    cum_all = _dot_exact_rhs(tri.astype(BF16), lw_ref[0])
    pairs = range(n_pairs)
    ps = lambda p: slice(p * W, (p + 1) * W)

    r = [r_ref[0, :, ps(p)] for p in pairs]
    v = [v_ref[0, :, ps(p)] for p in pairs]
    kkr = [k_ref[0, :, ps(p)] * vec_ref[2:3, ps(p)] for p in pairs]
    ss = seg_sum(jnp.concatenate([x * x for x in kkr], axis=0))
    k2, at, rt, bt, kt, w_end = [], [], [], [], [], []
    for p in pairs:
        a = a_ref[0, :, ps(p)]
        cum = cum_all[:, ps(p)]
        kk = kkr[p] * lax.rsqrt(jnp.maximum(rows(ss, p), 1e-24))
        k2.append(k_ref[0, :, ps(p)] * (1.0 + (a - 1.0) * vec_ref[3:4, ps(p)]))
        e_pos = jnp.exp(cum)
        e_neg = jnp.exp(-cum)
        rt.append(r[p] * e_pos)
        at.append(-kk * jnp.exp(cum - lw_ref[0, :, ps(p)]))
        bt.append(kk * a * e_neg)
        kt.append(k2[p] * e_neg)
        w_end.append(e_pos[C - 1:C, :])

    P, U, m_rbk, x_r = [], [], [], []
    for p in pairs:
        ar = jnp.concatenate([at[p], rt[p]], axis=0)
        big = mm(ar, jnp.concatenate([bd(bt[p]), bd(kt[p])], axis=0), "nt")
        x = mm(ar, s_sc[p], "nt")
        P.append(jnp.where(strict, big[:C, :W], 0.0))
        l_ak = jnp.where(strict, big[:C, W:], 0.0)
        m_rbk.append(jnp.concatenate([jnp.where(incl, big[C:, :W], 0.0), jnp.where(incl, big[C:, W:], 0.0)], axis=1))
        U.append(x[:C] + mm(l_ak, bd(v[p])))
        x_r.append(x[C:])

    n_it = int(np.log2(C))
    for it in range(n_it):
        for p in pairs:
            if it == n_it - 1:
                U[p] = U[p] + mm_inv(P[p], bd(U[p]))
            else:
                res = mm_inv(P[p], jnp.concatenate([bd(P[p]), bd(U[p])], axis=1))
                U[p] = U[p] + res[:, W:]
                P[p] = res[:, :W]

    y = []
    for p in pairs:
        y.append(x_r[p] + mm(m_rbk[p], jnp.concatenate([bd(U[p]), bd(v[p])], axis=0)))
        upd = mm(jnp.concatenate([U[p], v[p]], axis=0),
                 jnp.concatenate([bt[p] * w_end[p], kt[p] * w_end[p]], axis=0), "tn")
        s_sc[p] = s_sc[p] * w_end[p] + jnp.where(diag_blocks, upd, 0.0)

    inv_n = 1.0 / RW_HEAD_DIM
    y_all = jnp.concatenate(y, axis=0)
    mu = seg_sum(y_all) * inv_n
    var = seg_sum(jnp.square(y_all - mu)) * inv_n
    rk = seg_sum(jnp.concatenate([r[p] * k2[p] * vec_ref[4:5, ps(p)] for p in pairs], axis=0))
    yn = (y_all - mu) * lax.rsqrt(var + RW_GN_EPS)
    for p in pairs:
        y_o[0, :, ps(p)] = rows(yn, p) * vec_ref[5:6, ps(p)] + vec_ref[6:7, ps(p)] + rows(rk, p) * v[p]

    @pl.when(c == pl.num_programs(1) - 1)
    def _():
        sT_o[0] = s_sc[...]


def _wkv(r, lw, k, v, a, vec, s0, *, hp, inv_hp):
    B, T, D = r.shape
    H = D // RW_HEAD_DIM
    C, W, n_pairs = WKV_CHUNK, 2 * RW_HEAD_DIM, H // 2
    s_bd = jnp.zeros((B, n_pairs, W, W), F32)
    s_bd = s_bd.at[:, :, :C, :C].set(s0[:, 0::2]).at[:, :, C:, C:].set(s0[:, 1::2])
    row_spec = pl.BlockSpec((1, C, D), lambda b, c: (b, c, 0))
    st_spec = pl.BlockSpec((1, n_pairs, W, W), lambda b, c: (b, 0, 0, 0))
    ones = _block_diag(W, RW_HEAD_DIM, 1.0)
    kern = functools.partial(_wkv_kernel, n_pairs=n_pairs, hp=hp, inv_hp=inv_hp)
    y, s_out = pl.pallas_call(
        kern, grid=(B, T // C),
        in_specs=[row_spec] * 5 + [pl.BlockSpec(vec.shape, lambda b, c: (0, 0)), pl.BlockSpec((W, W), lambda b, c: (0, 0)),
                                   st_spec],
        out_specs=[row_spec, st_spec],
        out_shape=[jax.ShapeDtypeStruct((B, T, D), F32), jax.ShapeDtypeStruct(s_bd.shape, F32)],
        scratch_shapes=[pltpu.VMEM((n_pairs, W, W), F32)],
        compiler_params=_cparams(("parallel", "arbitrary")), name="wkv_chunk",
    )(r, lw, k, v, a, vec, ones, s_bd)
    s_fin = jnp.stack([s_out[:, :, :C, :C], s_out[:, :, C:, C:]], axis=2).reshape(B, H, RW_HEAD_DIM, RW_HEAD_DIM)
    return y, s_fin


def _row_call(kernel, rows, consts, out_widths, out_dtypes, tm, name):
    n = rows[0].shape[0]
    row_spec = lambda w: pl.BlockSpec((tm, w), lambda i: (i, 0))
    full = lambda a: pl.BlockSpec(a.shape, lambda i: (0,) * a.ndim)
    return pl.pallas_call(
        kernel, grid=(n // tm,),
        in_specs=[row_spec(a.shape[1]) for a in rows] + [full(c) for c in consts],
        out_specs=[row_spec(w) for w in out_widths],
        out_shape=[jax.ShapeDtypeStruct((n, w), dt) for w, dt in zip(out_widths, out_dtypes)],
        compiler_params=_cparams(("parallel",)), name=name,
    )(*rows, *consts)


def _rwkv_out_kernel(x_ref, y_ref, g_ref, wo_ref, o_ref, *, hp):
    o_ref[...] = x_ref[...] + _mm(y_ref[...] * g_ref[...], wo_ref[...], hp=hp)


def _kv_proj_kernel(h_ref, g_ref, wkv_ref, kg_ref, bd_ref, cmp_o, slc_o, win_o, *, hp):
    kv = _mm(_rms(h_ref[...], g_ref[...]), wkv_ref[...], hp=hp)
    cmp_o[...] = kv[:, :2 * KV_W]
    for br, out in ((1, slc_o), (2, win_o)):
        kraw = kv[:, br * 2 * KV_W: br * 2 * KV_W + KV_W]
        ms = _dot_hilo(kraw * kraw, bd_ref[...])
        out[:, :KV_W] = kraw * lax.rsqrt(ms + RMS_EPS) * kg_ref[br:br + 1, :]
        out[:, KV_W:] = kv[:, br * 2 * KV_W + KV_W: (br + 1) * 2 * KV_W]


def _q_proj_kernel(h_ref, g_ref, wq_ref, wg_ref, qg_ref, bd_ref, q_o, gate_o, *, hp, q_scale):
    hn = _rms(h_ref[...], g_ref[...])
    q = _mm(hn, wq_ref[...], hp=hp)
    ms = _dot_hilo(q * q, bd_ref[...])
    q_o[...] = (q * lax.rsqrt(ms + RMS_EPS) * qg_ref[...] * q_scale).astype(q_o.dtype)
    gate_o[...] = jax.nn.sigmoid(_mm(hn, wg_ref[...], hp=hp))


def _merge_kernel(h_ref, gate_ref, oc_ref, os_ref, ow_ref, eg_ref, wo_ref, o_ref, *, hp):
    D = h_ref.shape[1]
    ge = _dot_hilo(gate_ref[...], eg_ref[...])
    o = ge[:, :D] * oc_ref[...] + ge[:, D:2 * D] * os_ref[...] + ge[:, 2 * D:] * ow_ref[...]
    o_ref[...] = h_ref[...] + _mm(o, wo_ref[...], hp=hp)


def _router_kernel(h_ref, g_ref, wr_ref, xn_o, logit_o):
    xn = _rms(h_ref[...], g_ref[...])
    xn_o[...] = xn.astype(xn_o.dtype)
    logit_o[...] = _mm(xn, wr_ref[...], hp=True)


def _ffn_kernel(x_ref, g_ref, wg_ref, wu_ref, wd_ref, o_ref, xn_sc, acc_sc, *, hp):
    f = pl.program_id(1)

    @pl.when(f == 0)
    def _():
        xn_sc[...] = _rms(x_ref[...], g_ref[...]).astype(xn_sc.dtype)
        acc_sc[...] = jnp.zeros_like(acc_sc)

    xn = xn_sc[...]
    hid = _silu(_mm(xn, wg_ref[...], hp=hp)) * _mm(xn, wu_ref[...], hp=hp)
    acc_sc[...] += _mm(hid, wd_ref[...], hp=hp)

    @pl.when(f == pl.num_programs(1) - 1)
    def _():
        o_ref[...] = x_ref[...] + acc_sc[...]


def _ffn(x, g, wg, wu, wd, *, tm, tf, hp):
    n, D = x.shape
    dff = wg.shape[1]
    return pl.pallas_call(
        functools.partial(_ffn_kernel, hp=hp), grid=(n // tm, dff // tf),
        in_specs=[pl.BlockSpec((tm, D), lambda i, f: (i, 0)), pl.BlockSpec((1, D), lambda i, f: (0, 0)),
                  pl.BlockSpec((D, tf), lambda i, f: (0, f)), pl.BlockSpec((D, tf), lambda i, f: (0, f)),
                  pl.BlockSpec((tf, D), lambda i, f: (f, 0))],
        out_specs=pl.BlockSpec((tm, D), lambda i, f: (i, 0)),
        out_shape=jax.ShapeDtypeStruct((n, D), F32),
        scratch_shapes=[pltpu.VMEM((tm, D), F32 if hp else BF16), pltpu.VMEM((tm, D), F32)],
        compiler_params=_cparams(("parallel", "arbitrary")), name="ffn_dense",
    )(x, g, wg, wu, wd)


def _moe_ffn_kernel(be_ref, nb_ref, x_ref, wg_ref, wu_ref, wd_ref, o_ref, acc_sc, *, hp):
    i = pl.program_id(0)
    f = pl.program_id(1)

    @pl.when(i < nb_ref[0])
    def _():
        @pl.when(f == 0)
        def _():
            acc_sc[...] = jnp.zeros_like(acc_sc)

        x = x_ref[...]
        hid = _silu(_mm(x, wg_ref[0], hp=hp)) * _mm(x, wu_ref[0], hp=hp)
        acc_sc[...] += _mm(hid, wd_ref[0], hp=hp)

        @pl.when(f == pl.num_programs(1) - 1)
        def _():
            o_ref[...] = acc_sc[...]

    @pl.when((i >= nb_ref[0]) & (f == pl.num_programs(1) - 1))
    def _():
        o_ref[...] = jnp.zeros_like(o_ref)


def _moe_ffn(block_e, n_used, xs, wg, wu, wd, *, tm, tf, hp):
    cap, D = xs.shape
    dff = wg.shape[2]
    w_idx = lambda i, f, be, nb: jnp.where(i < nb[0], f, dff // tf - 1)
    grid_spec = pltpu.PrefetchScalarGridSpec(
        num_scalar_prefetch=2, grid=(cap // tm, dff // tf),
        in_specs=[pl.BlockSpec((tm, D), lambda i, f, be, nb: (i, 0)),
                  pl.BlockSpec((1, D, tf), lambda i, f, be, nb: (be[i], 0, w_idx(i, f, be, nb))),
                  pl.BlockSpec((1, D, tf), lambda i, f, be, nb: (be[i], 0, w_idx(i, f, be, nb))),
                  pl.BlockSpec((1, tf, D), lambda i, f, be, nb: (be[i], w_idx(i, f, be, nb), 0))],
        out_specs=pl.BlockSpec((tm, D), lambda i, f, be, nb: (i, 0)),
        scratch_shapes=[pltpu.VMEM((tm, D), F32)])
    return pl.pallas_call(
        functools.partial(_moe_ffn_kernel, hp=hp), grid_spec=grid_spec, out_shape=jax.ShapeDtypeStruct((cap, D), F32),
        compiler_params=_cparams(("arbitrary", "arbitrary")), name="moe_ffn",
    )(block_e, n_used, xs, wg, wu, wd)


def _compress_kernel(x_ref, pe_ref, w1_ref, w2_ref, kg_ref, o_ref, *, hp):
    c = pl.program_id(1)
    x = x_ref[0, 0, 0]
    R = x.shape[0]
    p0 = _mm(x + pe_ref[0, 0], w1_ref[0, 0], hp=hp)
    p1 = _mm(x + pe_ref[0, 1], w1_ref[0, 1], hp=hp)
    pre = p0 + pltpu.roll(p1, R - 1, 0)
    out = _mm(_silu(pre), w2_ref[0], hp=hp)
    o_ref[0, 0, 0] = jnp.where(c == 0, _rms(out, kg_ref[...]), out)


def _compress(xsub, pe, w1, w2, kg, *, hp):
    S, _, G, R, W = xsub.shape
    return pl.pallas_call(
        functools.partial(_compress_kernel, hp=hp), grid=(S, 2, G),
        in_specs=[pl.BlockSpec((1, 1, 1, R, W), lambda s, c, g: (s, c, g, 0, 0)),
                  pl.BlockSpec((1, 2, 1, W), lambda s, c, g: (c, 0, 0, 0)),
                  pl.BlockSpec((1, 2, W, HEAD_DIM), lambda s, c, g: (c, 0, 0, 0)),
                  pl.BlockSpec((1, HEAD_DIM, HEAD_DIM), lambda s, c, g: (c, 0, 0)),
                  pl.BlockSpec((1, HEAD_DIM), lambda s, c, g: (0, 0))],
        out_specs=pl.BlockSpec((1, 1, 1, R, HEAD_DIM), lambda s, c, g: (s, c, g, 0, 0)),
        out_shape=jax.ShapeDtypeStruct((S, 2, G, R, HEAD_DIM), F32),
        compiler_params=_cparams(("parallel", "arbitrary", "arbitrary")), name="kv_compress",
    )(xsub, pe, w1, w2, kg)


def _paged_sub_proj_kernel(pt_ref, cache_hbm, w_ref, o_ref, buf, sem, *, pages_per_step, page, hp):
    lin = pl.program_id(0) * pl.num_programs(1) + pl.program_id(1)
    total = pl.num_programs(0) * pl.num_programs(1)
    slot = lin % 2

    n_lane_pairs = buf.shape[1]

    def page_copies(step, into):
        return [pltpu.make_async_copy(cache_hbm.at[pt_ref[step * pages_per_step + p], pl.ds(0, page), pl.ds(pair * LANES, LANES)],
                                      buf.at[into, pair, pl.ds(p * page, page)], sem.at[into])
                for p in range(pages_per_step) for pair in range(n_lane_pairs)]

    @pl.when(lin == 0)
    def _():
        for cp in page_copies(0, 0):
            cp.start()

    @pl.when(lin + 1 < total)
    def _():
        for cp in page_copies(lin + 1, 1 - slot):
            cp.start()

    for cp in page_copies(lin, slot):
        cp.wait()
    n = pages_per_step * page // CMP_STRIDE
    for pair in range(n_lane_pairs):
        c = pair // (N_KV_HEADS // 2)
        rows = buf.at[slot, pair]
        x2 = jnp.concatenate([rows[pl.ds(s, n, stride=CMP_STRIDE), :] for s in range(CMP_STRIDE)], axis=1)
        o_ref[0, pair] = _mm(x2, w_ref[c], hp=hp and c == 0)


def _paged_sub_proj(page_table, cache, w2x, *, pages_per_step, hp):
    S, n_pages = page_table.shape
    page, width = cache.shape[1:]
    n = pages_per_step * page // CMP_STRIDE
    n_t = n_pages // pages_per_step
    grid_spec = pltpu.PrefetchScalarGridSpec(
        num_scalar_prefetch=1, grid=(S, n_t),
        in_specs=[pl.BlockSpec(memory_space=pl.ANY), pl.BlockSpec(w2x.shape, lambda s, t, pt: (0, 0, 0))],
        out_specs=pl.BlockSpec((1, 4, n, w2x.shape[2]), lambda s, t, pt: (s, 0, t, 0)),
        scratch_shapes=[pltpu.VMEM((2, width // LANES, pages_per_step * page, LANES), F32), pltpu.SemaphoreType.DMA((2,))])
    kern = functools.partial(_paged_sub_proj_kernel, pages_per_step=pages_per_step, page=page, hp=hp)
    return pl.pallas_call(
        kern, grid_spec=grid_spec, out_shape=jax.ShapeDtypeStruct((S, 4, n_t * n, w2x.shape[2]), F32),
        compiler_params=_cparams(("arbitrary", "arbitrary")), name="paged_sub_proj",
    )(page_table.reshape(-1), cache, w2x)


def _compress_tail_kernel(p_ref, pe_ref, w1_ref, w2_ref, kg_ref, o_ref, *, hp):
    R = p_ref.shape[2]
    half_groups = N_KV_HEADS // 2
    for c in range(2):
        bias = (_mm(jnp.broadcast_to(pe_ref[c, 0], (SUBLANES, pe_ref.shape[3])), w1_ref[c, 0], hp=True)
                + _mm(jnp.broadcast_to(pe_ref[c, 1], (SUBLANES, pe_ref.shape[3])), w1_ref[c, 1], hp=True))[0:1]
        outs = []
        for g in range(N_KV_HEADS):
            p = p_ref[0, c * half_groups + g // 2]
            e = g % 2
            p0 = p[:, e * HEAD_DIM:(e + 1) * HEAD_DIM]
            p1 = p[:, (2 + e) * HEAD_DIM:(3 + e) * HEAD_DIM]
            out = _mm(_silu(p0 + pltpu.roll(p1, R - 1, 0) + bias), w2_ref[c], hp=hp)
            outs.append(_rms(out, kg_ref[...]) if c == 0 else out)
        o_ref[0, c] = jnp.concatenate(outs, axis=1)


def _compress_tail(p, pe, w1, w2, kg, *, hp):
    S, _, R, _ = p.shape
    full = lambda a: pl.BlockSpec(a.shape, lambda s: (0,) * a.ndim)
    return pl.pallas_call(
        functools.partial(_compress_tail_kernel, hp=hp), grid=(S,),
        in_specs=[pl.BlockSpec((1,) + p.shape[1:], lambda s: (s, 0, 0, 0)), full(pe), full(w1), full(w2), full(kg)],
        out_specs=pl.BlockSpec((1, 2, R, KV_W), lambda s: (s, 0, 0, 0)),
        out_shape=jax.ShapeDtypeStruct((S, 2, R, KV_W), F32),
        compiler_params=_cparams(("parallel",)), name="kv_compress_tail",
    )(p, pe, w1, w2, kg)


def _stack_heads(q, g, tq):
    return jnp.concatenate([q[:, (HEADS_PER_KV * g + i) * HEAD_DIM:(HEADS_PER_KV * g + i + 1) * HEAD_DIM]
                            for i in range(HEADS_PER_KV)], axis=0)


def _cmp_attn_kernel(q_ref, kc_ref, vc_ref, m_ref, o_o, sel_o, *, n_cmp, n_slc, pos_base, per_tile, hp):
    tq = q_ref.shape[1]
    ncp = kc_ref.shape[1]
    nsp = m_ref.shape[1]
    q0 = pos_base + (pl.program_id(1) * tq if per_tile else 0)
    rows = HEADS_PER_KV * tq
    q_pos = q0 + lax.broadcasted_iota(jnp.int32, (rows, ncp), 0) % tq
    n_id = lax.broadcasted_iota(jnp.int32, (rows, ncp), 1)
    bias = jnp.where((n_id * CMP_STRIDE + CMP_BLOCK - 1 <= q_pos) & (n_id < n_cmp), 0.0, MASKED)
    blk = lax.broadcasted_iota(jnp.int32, (tq, nsp), 1)
    cur = (q0 + lax.broadcasted_iota(jnp.int32, (tq, nsp), 0)) // SLC_BLOCK
    forced = (blk == 0) | (blk == cur) | (blk == cur - 1)
    q = q_ref[0]
    scores = []
    for g in range(N_KV_HEADS):
        qs = _stack_heads(q, g, tq)
        s = _mm(qs, kc_ref[0, :, g * HEAD_DIM:(g + 1) * HEAD_DIM], "nt", hp) + bias
        m = jnp.maximum(jnp.max(s, -1, keepdims=True), M_INIT)
        e = jnp.exp(s - m)
        p = e / jnp.maximum(jnp.sum(e, -1, keepdims=True), 1e-30)
        o = _mm(p, vc_ref[0, :, g * HEAD_DIM:(g + 1) * HEAD_DIM], hp=hp)
        psum = p[0:tq]
        for i in range(HEADS_PER_KV):
            col = (HEADS_PER_KV * g + i) * HEAD_DIM
            o_o[0, :, col:col + HEAD_DIM] = o[i * tq:(i + 1) * tq]
            if i:
                psum = psum + p[i * tq:(i + 1) * tq]
        imp = _dot_hilo(psum, m_ref[...])
        scores.append(jnp.where(blk <= cur, jnp.where(forced, BIG, imp), -BIG))
    score = jnp.concatenate(scores, axis=0)
    blk_r = lax.broadcasted_iota(jnp.int32, score.shape, 1)

    def count_ahead(i, rank):
        s_i = jnp.sum(jnp.where(blk_r == i, score, 0.0), axis=-1, keepdims=True)
        return rank + jnp.where(s_i > score, 1.0, jnp.where((s_i == score) & (i < blk_r), 1.0, 0.0))

    rank = lax.fori_loop(0, n_slc, count_ahead, jnp.zeros(score.shape, F32), unroll=8)
    for g in range(N_KV_HEADS):
        sel_o[0, :, g * nsp:(g + 1) * nsp] = jnp.where(rank[g * tq:(g + 1) * tq] < min(SLC_TOP, n_slc), 1.0, 0.0).astype(sel_o.dtype)


def _cmp_attn(q, kc, vc, m, *, n_cmp, n_slc, pos_base, per_tile, tq, hp):
    B, T, D = q.shape
    ncp, nsp = m.shape
    kv_spec = pl.BlockSpec((1, ncp, KV_W), lambda b, i: (b, 0, 0))
    kern = functools.partial(_cmp_attn_kernel, n_cmp=n_cmp, n_slc=n_slc, pos_base=pos_base, per_tile=per_tile, hp=hp)
    return pl.pallas_call(
        kern, grid=(B, T // tq),
        in_specs=[pl.BlockSpec((1, tq, D), lambda b, i: (b, i, 0)), kv_spec, kv_spec,
                  pl.BlockSpec((ncp, nsp), lambda b, i: (0, 0))],
        out_specs=[pl.BlockSpec((1, tq, D), lambda b, i: (b, i, 0)),
                   pl.BlockSpec((1, tq, N_KV_HEADS * nsp), lambda b, i: (b, i, 0))],
        out_shape=[jax.ShapeDtypeStruct((B, T, D), F32), jax.ShapeDtypeStruct((B, T, N_KV_HEADS * nsp), BF16)],
        compiler_params=_cparams(("parallel", "arbitrary")), name="cmp_attn",
    )(q, kc, vc, m)


def _pair_rows(q, g):
    base = g * HEADS_PER_KV * HEAD_DIM
    return jnp.concatenate([q[:, base:base + LANES], q[:, base + LANES:base + 2 * LANES]], axis=0)


def _finish_pair(acc_even, acc_odd, p, tq):
    a0 = acc_even[p * tq:(p + 1) * tq]
    a1 = acc_odd[p * tq:(p + 1) * tq]
    o0 = a0 / jnp.maximum(a0[:, HEAD_DIM:HEAD_DIM + 1], 1e-30)
    o1 = a1 / jnp.maximum(a1[:, HEAD_DIM:HEAD_DIM + 1], 1e-30)
    lane = lax.broadcasted_iota(jnp.int32, (tq, LANES), 1)
    return jnp.where(lane < HEAD_DIM, o0, pltpu.roll(o1, HEAD_DIM, 1))


def _cmp_sel_kernel(q_ref, kab_ref, v1_ref, m_ref, o_o, sel_o, *, n_cmp, n_slc, bounded):
    tq = q_ref.shape[1]
    ncp = kab_ref.shape[1]
    nsp = LANES // N_KV_HEADS
    q0 = pl.program_id(1) * tq
    q_pos = q0 + lax.broadcasted_iota(jnp.int32, (tq, ncp), 0)
    n_id = lax.broadcasted_iota(jnp.int32, (tq, ncp), 1)
    b = jnp.where((n_id * CMP_STRIDE + CMP_BLOCK - 1 <= q_pos) & (n_id < n_cmp), 0.0, MASKED)
    b2 = jnp.concatenate([b, b], axis=0)
    q = q_ref[0]
    imp = jnp.zeros((tq, LANES), F32)
    for g in range(N_KV_HEADS):
        lhs = _pair_rows(q, g)
        vv = v1_ref[0, :, g * LANES:(g + 1) * LANES]
        accs, psum = [], None
        for half in range(2):
            s = _mm(lhs, kab_ref[0, :, (2 * g + half) * LANES:(2 * g + half + 1) * LANES], "nt") + b2
            e = jnp.exp2(s) if bounded else jnp.exp2(s - jnp.maximum(jnp.max(s, -1, keepdims=True), M_INIT))
            acc = _mm(e, vv)
            accs.append(acc)
            p = e / jnp.maximum(acc[:, HEAD_DIM:HEAD_DIM + 1], 1e-30)
            ph = p[:tq] + p[tq:]
            psum = ph if psum is None else psum + ph
        imp = imp + _dot_hilo(psum, m_ref[g])
        for p_ in range(2):
            col = g * 2 * LANES + p_ * LANES
            o_o[0, :, col:col + LANES] = _finish_pair(accs[0], accs[1], p_, tq)
    lane = lax.broadcasted_iota(jnp.int32, (tq, LANES), 1)
    blk = lane % nsp
    cur = (q0 + lax.broadcasted_iota(jnp.int32, (tq, LANES), 0)) // SLC_BLOCK
    forced = (blk == 0) | (blk == cur) | (blk == cur - 1)
    score = jnp.where(blk <= cur, jnp.where(forced, BIG, imp), -BIG)
    rank = jnp.zeros((tq, LANES), F32)
    for d in range(1, nsp):
        wrapped = blk + d >= nsp
        partner = jnp.where(wrapped, pltpu.roll(score, nsp - d, 1), pltpu.roll(score, LANES - d, 1))
        rank = rank + jnp.where(partner > score, 1.0, jnp.where((partner == score) & wrapped, 1.0, 0.0))
    sel_o[0] = jnp.where(rank < min(SLC_TOP, n_slc), 1.0, 0.0).astype(sel_o.dtype)


def _cmp_sel(q, kab, v1, m, *, n_cmp, n_slc, tq, bounded):
    B, T, D = q.shape
    ncp = kab.shape[1]
    assert m.shape == (N_KV_HEADS, ncp, LANES)
    whole = lambda a: pl.BlockSpec((1,) + a.shape[1:], lambda b, i: (b,) + (0,) * (a.ndim - 1))
    kern = functools.partial(_cmp_sel_kernel, n_cmp=n_cmp, n_slc=n_slc, bounded=bounded)
    return pl.pallas_call(
        kern, grid=(B, T // tq),
        in_specs=[pl.BlockSpec((1, tq, D), lambda b, i: (b, i, 0)), whole(kab), whole(v1),
                  pl.BlockSpec(m.shape, lambda b, i: (0, 0, 0))],
        out_specs=[pl.BlockSpec((1, tq, D), lambda b, i: (b, i, 0)), pl.BlockSpec((1, tq, LANES), lambda b, i: (b, i, 0))],
        out_shape=[jax.ShapeDtypeStruct((B, T, D), F32), jax.ShapeDtypeStruct((B, T, LANES), BF16)],
        compiler_params=_cparams(("parallel", "arbitrary")), name="cmp_sel_attn",
    )(q, kab, v1, m)


def _flash_pairs(lhs, kab_ref, v1_ref, kt_lo, kt_hi, tk, bias_fn, bounded):
    rows = lhs[0].shape[0]

    def body(kt, carry):
        k0 = pl.multiple_of(kt * tk, tk)
        biases = bias_fn(k0)
        out = []
        for g in range(N_KV_HEADS):
            vv = v1_ref[0, pl.ds(k0, tk), g * LANES:(g + 1) * LANES]
            for half in range(2):
                idx = 2 * g + half
                s = _mm(lhs[g], kab_ref[0, pl.ds(k0, tk), idx * LANES:(idx + 1) * LANES], "nt") + biases[g]
                if bounded:
                    out.append(carry[idx] + _mm(jnp.exp2(s), vv))
                else:
                    m, acc = carry[idx]
                    m_new = jnp.maximum(m, jnp.max(s, -1, keepdims=True))
                    out.append((m_new, jnp.exp2(m - m_new) * acc + _mm(jnp.exp2(s - m_new), vv)))
        return tuple(out)

    zero = jnp.zeros((rows, LANES), F32)
    if bounded:
        return lax.fori_loop(kt_lo, kt_hi, body, (zero,) * (2 * N_KV_HEADS))
    start = (jnp.full((rows, 1), M_INIT, F32), zero)
    return tuple(acc for _, acc in lax.fori_loop(kt_lo, kt_hi, body, (start,) * (2 * N_KV_HEADS)))


def _slc_win_attn_kernel(q_ref, sel_ref, skab_ref, sv1_ref, wkab_ref, wv1_ref, os_o, ow_o, *, tk, bounded):
    tq = q_ref.shape[1]
    nsp = sel_ref.shape[2] // N_KV_HEADS
    q0 = pl.program_id(1) * tq
    q_pos = q0 + lax.broadcasted_iota(jnp.int32, (tq, tk), 0)
    k_off = lax.broadcasted_iota(jnp.int32, (tq, tk), 1)
    sel_lane = lax.broadcasted_iota(jnp.int32, (N_KV_HEADS * nsp, tk), 0)
    blk_k = lax.broadcasted_iota(jnp.int32, (N_KV_HEADS * nsp, tk), 1)
    drop = jnp.where(sel_ref[0].astype(F32) > 0.5, 0.0, MASKED).astype(BF16)
    q = q_ref[0]
    kt_hi = (q0 + tq + tk - 1) // tk
    kt_lo_win = jnp.maximum(q0 - WINDOW + 1, 0) // tk

    twice = lambda b: jnp.concatenate([b, b], axis=0)

    def win_bias(k0):
        d = q_pos - (k_off + k0)
        return [twice(jnp.where((d >= 0) & (d < WINDOW), 0.0, MASKED))] * N_KV_HEADS

    def slc_bias(k0):
        causal = jnp.where(k_off + k0 <= q_pos, 0.0, MASKED)
        key_blk = (blk_k + k0) // SLC_BLOCK
        out = []
        for g in range(N_KV_HEADS):
            expand = jnp.where(g * nsp + key_blk == sel_lane, 1.0, 0.0).astype(BF16)
            out.append(twice(jnp.dot(drop, expand, preferred_element_type=F32) + causal))
        return out

    lhs = [_pair_rows(q, g) for g in range(N_KV_HEADS)]
    acc_s = _flash_pairs(lhs, skab_ref, sv1_ref, 0, kt_hi, tk, slc_bias, bounded)
    acc_w = _flash_pairs(lhs, wkab_ref, wv1_ref, kt_lo_win, kt_hi, tk, win_bias, bounded)
    for g in range(N_KV_HEADS):
        for p_ in range(2):
            col = g * 2 * LANES + p_ * LANES
            os_o[0, :, col:col + LANES] = _finish_pair(acc_s[2 * g], acc_s[2 * g + 1], p_, tq)
            ow_o[0, :, col:col + LANES] = _finish_pair(acc_w[2 * g], acc_w[2 * g + 1], p_, tq)


def _slc_win_attn(q, sel, skab, sv1, wkab, wv1, *, tq, tk, bounded):
    B, T, D = q.shape
    whole = lambda a: pl.BlockSpec((1,) + a.shape[1:], lambda b, i: (b, 0, 0))
    row = lambda w: pl.BlockSpec((1, tq, w), lambda b, i: (b, i, 0))
    return pl.pallas_call(
        functools.partial(_slc_win_attn_kernel, tk=tk, bounded=bounded), grid=(B, T // tq),
        in_specs=[row(D), row(sel.shape[2]), whole(skab), whole(sv1), whole(wkab), whole(wv1)],
        out_specs=[row(D), row(D)],
        out_shape=[jax.ShapeDtypeStruct((B, T, D), F32)] * 2,
        compiler_params=_cparams(("parallel", "arbitrary")), name="slc_win_attn",
    )(q, sel, skab, sv1, wkab, wv1)


def _pair_keys(k):
    z = jnp.zeros_like(k)
    return jnp.concatenate([k, z, z, k], axis=-1).reshape(k.shape[0], k.shape[1], -1)


def _ones_values(v):
    return jnp.concatenate([v, jnp.ones_like(v)], axis=-1).reshape(v.shape[0], v.shape[1], -1)


def _sample_attn_kernel(q_ref, kvs_ref, kpos_ref, new_ref, win_ref, os_o, ow_o, *, n_q, pos_base, win_base, n_win, hp):
    tq = q_ref.shape[1]
    rows = HEADS_PER_KV * tq
    nk = kpos_ref.shape[2]
    nwp = win_ref.shape[1]
    q = q_ref[0]
    row_q = lax.broadcasted_iota(jnp.int32, (rows, 1), 0) % tq
    w_id = lax.broadcasted_iota(jnp.int32, (rows, nwp), 1)
    d_win = (pos_base + row_q) - (win_base + w_id)
    win_bias = jnp.where((d_win >= 0) & (d_win < WINDOW) & (win_base + w_id >= 0) & (w_id < n_win), 0.0, MASKED)

    def attend(s, v):
        m = jnp.maximum(jnp.max(s, -1, keepdims=True), M_INIT)
        e = jnp.exp(s - m)
        p = e / jnp.maximum(jnp.sum(e, -1, keepdims=True), 1e-30)
        return _mm(p, v, hp=hp)

    for g in range(N_KV_HEADS):
        qs = _stack_heads(q, g, tq)
        o_s = jnp.zeros((rows, HEAD_DIM), F32)
        for qi in range(n_q):
            qg = qi * N_KV_HEADS + g
            planes = lambda c: jnp.concatenate([kvs_ref[0, qi, g, :, c].reshape(-1, HEAD_DIM),
                                                new_ref[0, c * N_KV_HEADS + g]], axis=0)
            kpos = kpos_ref[0, qg:qg + 1, :]
            bias = jnp.broadcast_to(jnp.where(kpos <= pos_base + qi, 0.0, MASKED), (rows, nk))
            o_qi = attend(_mm(qs, planes(0), "nt", hp) + bias, planes(1))
            o_s = jnp.where(row_q == qi, o_qi, o_s)
        o_w = attend(_mm(qs, win_ref[0, :, g * HEAD_DIM:(g + 1) * HEAD_DIM], "nt", hp) + win_bias,
                     win_ref[0, :, KV_W + g * HEAD_DIM:KV_W + (g + 1) * HEAD_DIM])
        for i in range(HEADS_PER_KV):
            col = (HEADS_PER_KV * g + i) * HEAD_DIM
            os_o[0, :, col:col + HEAD_DIM] = o_s[i * tq:(i + 1) * tq]
            ow_o[0, :, col:col + HEAD_DIM] = o_w[i * tq:(i + 1) * tq]


def _sample_attn(q, kv_sel, kpos, new_win, win_all, *, n_q, pos_base, win_base, n_win, hp):
    B, tq, D = q.shape
    kern = functools.partial(_sample_attn_kernel, n_q=n_q, pos_base=pos_base, win_base=win_base, n_win=n_win, hp=hp)
    blk = lambda a: pl.BlockSpec((1,) + a.shape[1:], lambda b: (b,) + (0,) * (a.ndim - 1))
    return pl.pallas_call(
        kern, grid=(B,),
        in_specs=[blk(q), blk(kv_sel), blk(kpos), blk(new_win), blk(win_all)],
        out_specs=[blk(q), blk(q)],
        out_shape=[jax.ShapeDtypeStruct((B, tq, D), F32)] * 2,
        compiler_params=_cparams(("parallel",)), name="sample_slc_win_attn",
    )(q, kv_sel, kpos, new_win, win_all)


def _pick_tile(n, cap=512):
    return next(t for t in (512, 256, 128, 64, 32, 16, 8) if t <= cap and n % t == 0)


def _pad_rows(x, t_pad):
    return jnp.pad(x, ((0, 0), (0, t_pad - x.shape[1]), (0, 0)))


def _rwkv_layer(x, shift0, wkv0, p, *, tm, hp, inv_hp):
    B, T, D = x.shape
    t_pad = -(-T // tm) * tm
    xp = _pad_rows(x, t_pad)
    r, lw, k, v, a, g, shift = _rwkv_proj(xp, shift0[:, None, :], p["g"], p["mix"], p["vec"], p["wrkv"], p["dw1"],
                                          p["dw2"], p["aw1"], p["aw2"], p["gw1"], p["gw2"], t_real=T, tm=tm, hp=hp)
    if t_pad != T:
        live = (jnp.arange(t_pad) < T)[None, :, None]
        r, lw, k, v, a = (jnp.where(live, t, 0.0) for t in (r, lw, k, v, a))
    y, s_fin = _wkv(r, lw, k, v, a, p["vec"], wkv0, hp=hp, inv_hp=inv_hp)
    n = B * t_pad
    out = _row_call(functools.partial(_rwkv_out_kernel, hp=hp), [xp.reshape(n, D), y.reshape(n, D), g.reshape(n, D)],
                    [p["wo"]], [D], [F32], _pick_tile(n), "rwkv_out")[0]
    return out.reshape(B, t_pad, D)[:, :T], shift[:, 0], s_fin


def _sub_blocks(rows):
    S, L = rows.shape[:2]
    n_sub = L // CMP_STRIDE
    r = rows[:, :n_sub * CMP_STRIDE].reshape(S, n_sub, CMP_STRIDE, 2, N_KV_HEADS, HEAD_DIM)
    r = jnp.transpose(r, (0, 3, 4, 1, 2, 5)).reshape(S, 2, N_KV_HEADS, n_sub, CMP_STRIDE * HEAD_DIM)
    r_pad = -(-n_sub // SUBLANES) * SUBLANES
    return jnp.pad(r, ((0, 0),) * 3 + ((0, r_pad - n_sub), (0, 0))), n_sub - CMP_BLOCK // CMP_STRIDE + 1


def _compressed_kv(rows, p, *, hp):
    xsub, n_cmp = _sub_blocks(rows)
    out = _compress(xsub, p["cmp_pe"], p["cmp_w1"], p["cmp_w2"], p["k_norm_g"][0:1], hp=hp)
    S, _, G, R, dh = out.shape
    out = jnp.transpose(out, (1, 0, 3, 2, 4)).reshape(2, S, R, G * dh).astype(F32 if hp else BF16)
    return out[0], out[1], n_cmp


def _overlap_matrix(ncp, n_cmp, n_slc, nsp):
    c0 = np.arange(ncp)[:, None] * CMP_STRIDE
    s0 = np.arange(nsp)[None, :] * SLC_BLOCK
    m = (c0 < s0 + SLC_BLOCK) & (c0 + CMP_BLOCK > s0) & (np.arange(ncp)[:, None] < n_cmp) & (np.arange(nsp)[None, :] < n_slc)
    return jnp.asarray(m.astype(np.float32), dtype=BF16)


def _moe(h, p, *, tm, tf, hp):
    n, D = h.shape
    xn, logits = _row_call(_router_kernel, [h], [p["g_moe"], p["w_router"]], [D, LANES], [F32 if hp else BF16, F32],
                           _pick_tile(n), "moe_router")
    top_v, top_e = lax.top_k(logits[:, :N_EXPERTS], TOP_K)
    gates = jax.nn.softmax(top_v, -1)
    n_pairs = n * TOP_K
    flat_e = top_e.reshape(n_pairs)
    order = jnp.argsort(flat_e)
    seen = jnp.cumsum((flat_e[:, None] == jnp.arange(N_EXPERTS)).astype(jnp.int32), axis=0)
    counts = seen[-1]
    padded = (counts + tm - 1) // tm * tm
    pad_end = jnp.cumsum(padded)
    pad_start = pad_end - padded
    start = jnp.cumsum(counts) - counts
    rank = jnp.take_along_axis(seen, flat_e[:, None], axis=1)[:, 0] - 1
    dest = (pad_start[flat_e] + rank).reshape(n, TOP_K)
    n_blocks = -(-n_pairs // tm) + N_EXPERTS
    cap = n_blocks * tm
    block_e = jnp.minimum(jnp.searchsorted(pad_end, jnp.arange(n_blocks) * tm, side="right"), N_EXPERTS - 1).astype(jnp.int32)
    row_e = jnp.repeat(block_e, tm)
    k_in_e = jnp.arange(cap, dtype=jnp.int32) - pad_start[row_e]
    src = order[jnp.clip(start[row_e] + k_in_e, 0, n_pairs - 1)] // TOP_K
    buf_tok = jnp.where((k_in_e >= 0) & (k_in_e < counts[row_e]), src, n).astype(jnp.int32)
    n_used = (pad_end[-1] // tm).astype(jnp.int32).reshape(1)
    xs = jnp.concatenate([xn, jnp.zeros((1, D), xn.dtype)], 0)[buf_tok]
    ys = _moe_ffn(block_e, n_used, xs, p["moe_wg"], p["moe_wu"], p["moe_wd"], tm=tm, tf=tf, hp=hp)
    y = ys[dest[:, 0]] * gates[:, 0:1] + ys[dest[:, 1]] * gates[:, 1:2]
    return h + y


def kernel(x_prompt, x_sample, cache_cmp_kv, cache_slc_kv, state_win_kv, state_wkv, state_shift, page_table, norm_g, rw_mix, rw_vec, rw_w_rkv, rw_w_o, rw_decay_w1, rw_decay_w2, rw_iclr_w1, rw_iclr_w2, rw_gate_w1, rw_gate_w2, ffn_w_gate, ffn_w_up, ffn_w_down, moe_router, moe_w_gate, moe_w_up, moe_w_down, kv_norm_g, w_kv, k_norm_g, cmp_pe, cmp_w1, cmp_w2, w_qg, q_norm_g, w_o):
    B, T, D = x_prompt.shape
    SB, S, _ = x_sample.shape
    assert norm_g.shape[0] == 2 and rw_mix.shape[0] == 1 and w_qg.shape[0] == 1, "one RWKV-7 layer + one NSA layer"
    assert D == N_HEADS * HEAD_DIM and T % 512 == 0 and S <= SUBLANES
    page = cache_cmp_kv.shape[1]
    past = page_table.shape[1] * page
    wb = state_win_kv.shape[1]
    assert past % SLC_BLOCK == 0 and page % SLC_BLOCK == 0 and S <= SLC_BLOCK
    bf = lambda w: w.astype(BF16)
    pad8 = lambda m: jnp.pad(m, ((0, SUBLANES - m.shape[0]), (0, 0)))
    dff = ffn_w_gate.shape[2]
    tf = dff // 2 if (dff // 2) % LANES == 0 else dff
    n_p, n_s = B * T, SB * S
    both = lambda w: (bf(w), w)
    pick = lambda d, hp: {k: (v[hp] if isinstance(v, tuple) else v) for k, v in d.items()}

    rw = dict(g=norm_g[0, 0][None], mix=pad8(rw_mix[0]), vec=pad8(rw_vec[0]), wrkv=both(rw_w_rkv[0]), wo=both(rw_w_o[0]),
              dw1=both(rw_decay_w1[0]), dw2=both(rw_decay_w2[0]), aw1=both(rw_iclr_w1[0]), aw2=both(rw_iclr_w2[0]),
              gw1=both(rw_gate_w1[0]), gw2=both(rw_gate_w2[0]))
    H = D // RW_HEAD_DIM
    hp_, shift_p, wkv_p = _rwkv_layer(x_prompt, jnp.zeros((B, D), F32), jnp.zeros((B, H, RW_HEAD_DIM, RW_HEAD_DIM), F32),
                                      pick(rw, 0), tm=256, hp=False, inv_hp=True)
    hs, shift_s, wkv_s = _rwkv_layer(x_sample, state_shift[0], state_wkv[0], pick(rw, 1), tm=WKV_CHUNK, hp=True, inv_hp=True)
    ffn_g = norm_g[0, 1][None]
    hp_ = _ffn(hp_.reshape(n_p, D), ffn_g, bf(ffn_w_gate[0]), bf(ffn_w_up[0]), bf(ffn_w_down[0]), tm=512, tf=tf, hp=False)
    hs = _ffn(hs.reshape(n_s, D), ffn_g, ffn_w_gate[0], ffn_w_up[0], ffn_w_down[0], tm=n_s, tf=tf, hp=True)

    kv_consts = lambda hp: [kv_norm_g[None], both(w_kv)[hp], jnp.tile(k_norm_g, (1, N_KV_HEADS)),
                            _block_diag(KV_W, HEAD_DIM, 1.0 / HEAD_DIM)]
    kv_rows = lambda h, tm, hp: _row_call(functools.partial(_kv_proj_kernel, hp=hp), [h], kv_consts(hp),
                                          [2 * KV_W] * 3, [F32] * 3, tm, "kv_proj")
    as_rows = lambda t, b: t.reshape(b, -1, 2, N_KV_HEADS, HEAD_DIM)
    cmp_p, slc_p, win_p = (as_rows(t, B) for t in kv_rows(hp_, 512, False))
    cmp_s, slc_s, win_s = (as_rows(t, SB) for t in kv_rows(hs, n_s, True))
    win_all_s = jnp.concatenate([state_win_kv, win_s], axis=1)

    cp = dict(cmp_pe=cmp_pe.reshape(2, CMP_BLOCK // CMP_STRIDE, 1, CMP_STRIDE * HEAD_DIM),
              cmp_w1=both(cmp_w1.reshape(2, CMP_BLOCK // CMP_STRIDE, CMP_STRIDE * HEAD_DIM, HEAD_DIM)),
              cmp_w2=both(cmp_w2), k_norm_g=k_norm_g)
    kc_p, vc_p, ncmp_p = _compressed_kv(cmp_p, pick(cp, 0), hp=False)
    assert S < CMP_STRIDE and page % CMP_STRIDE == 0
    n_ratio = CMP_BLOCK // CMP_STRIDE
    w1r = cmp_w1.reshape(2, n_ratio, CMP_STRIDE, HEAD_DIM, HEAD_DIM)
    w2x = jnp.einsum("cjsdh,ef->csedjfh", w1r, jnp.eye(2, dtype=F32)).reshape(2, CMP_STRIDE * LANES, n_ratio * LANES)
    n_pages = page_table.shape[1]
    pages_per_step = next(pp for pp in (32, 16, 8, 4, 2, 1) if n_pages % pp == 0)
    part_s = _paged_sub_proj(page_table, cache_cmp_kv.reshape(-1, page, 2 * KV_W), w2x, pages_per_step=pages_per_step, hp=True)
    kcv_s = _compress_tail(part_s, cp["cmp_pe"], cp["cmp_w1"][1], cmp_w2, k_norm_g[0:1], hp=True)
    kc_s, vc_s, ncmp_s = kcv_s[:, 0], kcv_s[:, 1], past // CMP_STRIDE - n_ratio + 1

    ge = np.zeros((LANES, N_BRANCH * D), np.float32)
    for br in range(N_BRANCH):
        for hd in range(N_HEADS):
            ge[br * N_HEADS + hd, br * D + hd * HEAD_DIM: br * D + (hd + 1) * HEAD_DIM] = 1.0
    nsa = dict(g_attn=norm_g[1, 0][None], wq=both(w_qg[0, :, :D]),
               wgate=both(jnp.pad(w_qg[0, :, D:], ((0, 0), (0, LANES - N_BRANCH * N_HEADS)))),
               q_norm_g=jnp.tile(q_norm_g[0], N_HEADS)[None], bd_q=_block_diag(D, HEAD_DIM, 1.0 / HEAD_DIM),
               w_o=both(w_o[0]), gate_expand=jnp.asarray(ge, dtype=BF16))

    def query_side(h, tm, hp, q_scale):
        c = pick(nsa, hp)
        return _row_call(functools.partial(_q_proj_kernel, hp=hp, q_scale=q_scale), [h],
                         [c["g_attn"], c["wq"], c["wgate"], c["q_norm_g"], c["bd_q"]], [D, LANES],
                         [F32 if hp else BF16, F32], tm, "q_proj")

    def merge(h, gates, o_cmp, o_slc, o_win, tm, hp):
        c = pick(nsa, hp)
        return _row_call(functools.partial(_merge_kernel, hp=hp), [h, gates, o_cmp, o_slc, o_win],
                         [c["gate_expand"], c["w_o"]], [D], [F32], tm, "nsa_merge")[0]

    q_p, gates_p = query_side(hp_, 512, False, ATTN_SCALE * LOG2E)
    q_p = q_p.reshape(B, T, D)
    n_slc_p = T // SLC_BLOCK
    nsp_p = LANES // N_KV_HEADS
    assert n_slc_p <= nsp_p, "the packed block-selection layout holds at most 32 selection blocks per kv group"
    m_one = _overlap_matrix(kc_p.shape[1], ncmp_p, n_slc_p, nsp_p)
    m_p = jnp.stack([jnp.pad(m_one, ((0, 0), (g * nsp_p, LANES - (g + 1) * nsp_p))) for g in range(N_KV_HEADS)])
    heads = lambda t: t.reshape(t.shape[0], t.shape[1], N_KV_HEADS, HEAD_DIM)
    kcab, vc1 = _pair_keys(heads(kc_p)), _ones_values(heads(vc_p))
    skab, sv1 = _pair_keys(bf(slc_p[:, :, 0])), _ones_values(bf(slc_p[:, :, 1]))
    wkab, wv1 = _pair_keys(bf(win_p[:, :, 0])), _ones_values(bf(win_p[:, :, 1]))

    def prompt_branches(bounded):
        o_cmp, sel = _cmp_sel(q_p, kcab, vc1, m_p, n_cmp=ncmp_p, n_slc=n_slc_p, tq=256, bounded=bounded)
        return (o_cmp,) + tuple(_slc_win_attn(q_p, sel, skab, sv1, wkab, wv1, tq=256, tk=256, bounded=bounded))

    score_bound = HEAD_DIM * ATTN_SCALE * jnp.max(jnp.abs(q_norm_g[0])) * jnp.max(jnp.abs(k_norm_g))
    o_cmp_p, o_slc_p, o_win_p = lax.cond(score_bound <= SCORE_BOUND_MAX, lambda: prompt_branches(True),
                                         lambda: prompt_branches(False))
    hp_ = merge(hp_, gates_p, o_cmp_p.reshape(n_p, D), o_slc_p.reshape(n_p, D), o_win_p.reshape(n_p, D), 512, False)

    kv_lanes = lambda t, dt: t.reshape(t.shape[0], t.shape[1], 2 * KV_W).astype(dt)
    q_s, gates_s = query_side(hs, n_s, True, ATTN_SCALE)
    q_s = _pad_rows(q_s.reshape(SB, S, D), SUBLANES)
    n_slc_s = (past + S - 1) // SLC_BLOCK + 1
    nsp_s = -(-n_slc_s // LANES) * LANES
    m_s = _overlap_matrix(kc_s.shape[1], ncmp_s, n_slc_s, nsp_s)
    o_cmp_s, sel_s = _cmp_attn(q_s, kc_s, vc_s, m_s, n_cmp=ncmp_s, n_slc=n_slc_s, pos_base=past, per_tile=False,
                               tq=SUBLANES, hp=True)
    n_sel = min(SLC_TOP, n_slc_s)
    sel_s = sel_s.reshape(SB, SUBLANES, N_KV_HEADS, nsp_s)[:, :S, :, :n_slc_s] > 0.5
    slot = jnp.cumsum(sel_s, axis=-1) - 1
    idx_s = jnp.sum(jnp.where(sel_s[..., None] & (slot[..., None] == jnp.arange(n_sel)),
                              jnp.arange(n_slc_s)[:, None], 0), axis=-2).astype(jnp.int32)
    in_pool = idx_s < past // SLC_BLOCK
    per_page = page // SLC_BLOCK
    blk_c = jnp.minimum(idx_s, past // SLC_BLOCK - 1)
    pages = jnp.take_along_axis(page_table, (blk_c // per_page).reshape(SB, -1), axis=1).reshape(idx_s.shape)
    pool_blk = pages * per_page + blk_c % per_page
    cache_blocks = cache_slc_kv.reshape(-1, SLC_BLOCK, 2, N_KV_HEADS, HEAD_DIM)
    g_ix = jnp.arange(N_KV_HEADS)[None, None, :, None, None]
    kv_sel = cache_blocks[pool_blk[..., None], :, jnp.arange(2), g_ix, :]
    never = jnp.int32(2 ** 30)
    new_rows = slc_s[:, jnp.clip(jnp.arange(SLC_BLOCK), 0, S - 1)]
    new_win = jnp.transpose(new_rows, (0, 2, 3, 1, 4)).reshape(SB, 2 * N_KV_HEADS, SLC_BLOCK, HEAD_DIM)
    kpos_pool = jnp.where(in_pool[..., None], idx_s[..., None] * SLC_BLOCK + jnp.arange(SLC_BLOCK), never)
    kpos_new = jnp.where(jnp.any(~in_pool, axis=-1, keepdims=True), past + jnp.arange(SLC_BLOCK), never)
    kpos = jnp.concatenate([kpos_pool.reshape(SB, S, N_KV_HEADS, n_sel * SLC_BLOCK), kpos_new], axis=-1)
    kpos = kpos.reshape(SB, S * N_KV_HEADS, (n_sel + 1) * SLC_BLOCK).astype(jnp.int32)
    kpos = jnp.pad(kpos, ((0, 0), (0, -(-S * N_KV_HEADS // SUBLANES) * SUBLANES - S * N_KV_HEADS), (0, 0)))
    n_win = wb + S
    nwp = -(-n_win // SUBLANES) * SUBLANES
    win_lanes = _pad_rows(kv_lanes(win_all_s, F32), nwp)
    o_slc_s, o_win_s = _sample_attn(q_s, kv_sel, kpos, new_win, win_lanes, n_q=S, pos_base=past, win_base=past - wb,
                                    n_win=n_win, hp=True)
    unpad = lambda t: t[:, :S].reshape(n_s, D)
    hs = merge(hs, gates_s, unpad(o_cmp_s), unpad(o_slc_s), unpad(o_win_s), n_s, True)

    moe = dict(g_moe=norm_g[1, 1][None], w_router=jnp.pad(moe_router[0], ((0, 0), (0, LANES - N_EXPERTS))),
               moe_wg=both(moe_w_gate[0]), moe_wu=both(moe_w_up[0]), moe_wd=both(moe_w_down[0]))
    hp_ = _moe(hp_, pick(moe, 0), tm=512, tf=tf, hp=False).reshape(B, T, D)
    hs = _moe(hs, pick(moe, 1), tm=128, tf=tf, hp=True).reshape(SB, S, D)

    keep_p = min(WINDOW, T)
    return (hp_, hs, cmp_p, cmp_s, slc_p, slc_s, win_p[:, T - keep_p:], win_all_s[:, win_all_s.shape[1] - wb:],
            wkv_p[None], wkv_s[None], shift_p[None], shift_s[None])
```

```python
import functools

import numpy as np
import jax
import jax.numpy as jnp
from jax import lax
from jax.experimental import pallas as pl
from jax.experimental.pallas import tpu as pltpu

F32 = jnp.float32
BF16 = jnp.bfloat16

RW_HEAD_DIM = 64
RW_GN_EPS = 64e-5
N_HEADS = 16
HEAD_DIM = 64
N_KV_HEADS = 4
HEADS_PER_KV = N_HEADS // N_KV_HEADS
KV_W = N_KV_HEADS * HEAD_DIM
N_BRANCH = 3
CMP_BLOCK = 32
CMP_STRIDE = 16
SLC_BLOCK = 64
SLC_TOP = 16
WINDOW = 512
ATTN_SCALE = HEAD_DIM ** -0.5
N_EXPERTS = 8
TOP_K = 2
RMS_EPS = 1e-6
MASKED = -1e30
M_INIT = -1e20
BIG = 1e30
LOG2E = 1.4426950408889634
SCORE_BOUND_MAX = 40.0

SUBLANES = 8
LANES = 128
VMEM_LIMIT = 56 * 1024 * 1024
WKV_CHUNK = RW_HEAD_DIM


def _cparams(sem):
    return pltpu.CompilerParams(dimension_semantics=sem, vmem_limit_bytes=VMEM_LIMIT)


_DIMS = {"nn": (((1,), (0,)), ((), ())), "nt": (((1,), (1,)), ((), ())), "tn": (((0,), (0,)), ((), ()))}


def _split2(x):
    hi = x.astype(BF16)
    lo = (x - hi.astype(F32)).astype(BF16)
    return hi, lo


def _mm(a, b, form="nn", hp=False):
    d = lambda s, t: lax.dot_general(s, t, _DIMS[form], preferred_element_type=F32)
    if not hp:
        return d(a.astype(BF16), b.astype(BF16))
    ah, al = _split2(a.astype(F32))
    bh, bl = _split2(b.astype(F32))
    return d(ah, bh) + d(ah, bl) + d(al, bh)


def _dot_hilo(x, m):
    hi, lo = _split2(x)
    return jnp.dot(hi, m, preferred_element_type=F32) + jnp.dot(lo, m, preferred_element_type=F32)


def _dot_exact_rhs(m, x):
    hi = x.astype(BF16)
    r1 = x - hi.astype(F32)
    mid = r1.astype(BF16)
    lo = (r1 - mid.astype(F32)).astype(BF16)
    d = lambda t: jnp.dot(m, t, preferred_element_type=F32)
    return d(hi) + d(mid) + d(lo)


def _rms(x, g):
    return x * lax.rsqrt(jnp.mean(x * x, -1, keepdims=True) + RMS_EPS) * g


def _silu(x):
    return x * jax.nn.sigmoid(x)


def _softplus(z):
    return jnp.maximum(z, 0.0) + jnp.log(1.0 + jnp.exp(-jnp.abs(z)))


def _block_diag(n, seg, value):
    i = np.arange(n)
    return jnp.asarray((i[:, None] // seg == i[None, :] // seg).astype(np.float32) * value, dtype=BF16)


def _rwkv_proj_kernel(x_ref, halo_ref, sh_ref, g_ref, mix_ref, vec_ref, wrkv_ref, dw1_ref, dw2_ref, aw1_ref, aw2_ref,
                      gw1_ref, gw2_ref, r_o, lw_o, k_o, v_o, a_o, g_o, shift_o, *, last_tile, last_row, hp):
    i = pl.program_id(1)
    mm = functools.partial(_mm, hp=hp)
    g = g_ref[...]
    xn = _rms(x_ref[0], g)
    hn = _rms(halo_ref[0, SUBLANES - 1:SUBLANES, :], g)
    prev_last = jnp.where(i == 0, sh_ref[0], hn)
    row = lax.broadcasted_iota(jnp.int32, xn.shape, 0)
    prev = jnp.where(row == 0, prev_last, pltpu.roll(xn, 1, 0))
    xx = prev - xn
    mixed = lambda j: xn + xx * mix_ref[j:j + 1, :]
    r_o[0] = mm(mixed(0), wrkv_ref[0])
    k_o[0] = mm(mixed(2), wrkv_ref[1])
    v_o[0] = mm(mixed(3), wrkv_ref[2])
    w0 = vec_ref[0:1, :]
    a0 = vec_ref[1:2, :]
    log_w = -_softplus(-(w0 + mm(jnp.tanh(mm(mixed(1), dw1_ref[...])), dw2_ref[...]))) - 0.5
    lw_o[0] = -jnp.exp(log_w)
    a_o[0] = jax.nn.sigmoid(a0 + mm(mm(mixed(4), aw1_ref[...]), aw2_ref[...]))
    g_o[0] = mm(jax.nn.sigmoid(mm(mixed(5), gw1_ref[...])), gw2_ref[...])

    @pl.when(i == last_tile)
    def _():
        shift_o[0] = xn[last_row:last_row + 1, :]


def _rwkv_proj(x, shift0, g, mix, vec, wrkv, dw1, dw2, aw1, aw2, gw1, gw2, *, t_real, tm, hp):
    B, T, D = x.shape
    nt = T // tm
    row_spec = pl.BlockSpec((1, tm, D), lambda b, i: (b, i, 0))
    halo_spec = pl.BlockSpec((1, SUBLANES, D), lambda b, i: (b, jnp.maximum(i * (tm // SUBLANES) - 1, 0), 0))
    vec_spec = pl.BlockSpec((1, 1, D), lambda b, i: (b, 0, 0))
    full = lambda a: pl.BlockSpec(a.shape, lambda b, i: (0,) * a.ndim)
    consts = (g, mix, vec, wrkv, dw1, dw2, aw1, aw2, gw1, gw2)
    out_sd = jax.ShapeDtypeStruct((B, T, D), F32)
    kern = functools.partial(_rwkv_proj_kernel, last_tile=(t_real - 1) // tm, last_row=(t_real - 1) % tm, hp=hp)
    return pl.pallas_call(
        kern, grid=(B, nt),
        in_specs=[row_spec, halo_spec, vec_spec] + [full(c) for c in consts],
        out_specs=[row_spec] * 6 + [vec_spec],
        out_shape=[out_sd] * 6 + [jax.ShapeDtypeStruct((B, 1, D), F32)],
        compiler_params=_cparams(("parallel", "arbitrary")), name="rwkv_proj",
    )(x, x, shift0, *consts)


def _wkv_kernel(r_ref, lw_ref, k_ref, v_ref, a_ref, vec_ref, ones_ref, s0_ref, y_o, sT_o, s_sc, *, n_pairs, hp, inv_hp):
    c = pl.program_id(1)
    C = r_ref.shape[1]
    W = 2 * C
    mm = functools.partial(_mm, hp=hp)
    mm_inv = functools.partial(_mm, hp=inv_hp)

    @pl.when(c == 0)
    def _():
        s_sc[...] = s0_ref[0]

    lane = lax.broadcasted_iota(jnp.int32, (C, W), 1)
    row = lax.broadcasted_iota(jnp.int32, (C, W), 0)
    left = lane < C
    col = jnp.where(left, lane, lane - C)
    strict = col < row
    incl = col <= row
    diag_blocks = (lax.broadcasted_iota(jnp.int32, (W, W), 0) < C) == (lax.broadcasted_iota(jnp.int32, (W, W), 1) < C)
    bd = lambda x: jnp.concatenate([jnp.where(left, x, 0.0), jnp.where(left, 0.0, x)], axis=0)
    seg_sum = lambda x: _dot_hilo(x, ones_ref[...])
    rows = lambda x, p: x[p * C:(p + 1) * C]
    tri = jnp.where(lax.broadcasted_iota(jnp.int32, (C, C), 1) <= lax.broadcasted_iota(jnp.int32, (C, C), 0), 1.0, 0.0)
    cum_all = _dot_exact_rhs(tri.astype(BF16), lw_ref[0])
    pairs = range(n_pairs)
    ps = lambda p: slice(p * W, (p + 1) * W)

    r = [r_ref[0, :, ps(p)] for p in pairs]
    v = [v_ref[0, :, ps(p)] for p in pairs]
    kkr = [k_ref[0, :, ps(p)] * vec_ref[2:3, ps(p)] for p in pairs]
    ss = seg_sum(jnp.concatenate([x * x for x in kkr], axis=0))
    k2, at, rt, bt, kt, w_end = [], [], [], [], [], []
    for p in pairs:
        a = a_ref[0, :, ps(p)]
        cum = cum_all[:, ps(p)]
        kk = kkr[p] * lax.rsqrt(jnp.maximum(rows(ss, p), 1e-24))
        k2.append(k_ref[0, :, ps(p)] * (1.0 + (a - 1.0) * vec_ref[3:4, ps(p)]))
        e_pos = jnp.exp(cum)
        e_neg = jnp.exp(-cum)
        rt.append(r[p] * e_pos)
        at.append(-kk * jnp.exp(cum - lw_ref[0, :, ps(p)]))
        bt.append(kk * a * e_neg)
        kt.append(k2[p] * e_neg)
        w_end.append(e_pos[C - 1:C, :])

    P, U, m_rbk, x_r = [], [], [], []
    for p in pairs:
        ar = jnp.concatenate([at[p], rt[p]], axis=0)
        big = mm(ar, jnp.concatenate([bd(bt[p]), bd(kt[p])], axis=0), "nt")
        x = mm(ar, s_sc[p], "nt")
        P.append(jnp.where(strict, big[:C, :W], 0.0))
        l_ak = jnp.where(strict, big[:C, W:], 0.0)
        m_rbk.append(jnp.concatenate([jnp.where(incl, big[C:, :W], 0.0), jnp.where(incl, big[C:, W:], 0.0)], axis=1))
        U.append(x[:C] + mm(l_ak, bd(v[p])))
        x_r.append(x[C:])

    n_it = int(np.log2(C))
    for it in range(n_it):
        for p in pairs:
            if it == n_it - 1:
                U[p] = U[p] + mm_inv(P[p], bd(U[p]))
            else:
                res = mm_inv(P[p], jnp.concatenate([bd(P[p]), bd(U[p])], axis=1))
                U[p] = U[p] + res[:, W:]
                P[p] = res[:, :W]

    y = []
    for p in pairs:
        y.append(x_r[p] + mm(m_rbk[p], jnp.concatenate([bd(U[p]), bd(v[p])], axis=0)))
        upd = mm(jnp.concatenate([U[p], v[p]], axis=0),
                 jnp.concatenate([bt[p] * w_end[p], kt[p] * w_end[p]], axis=0), "tn")
        s_sc[p] = s_sc[p] * w_end[p] + jnp.where(diag_blocks, upd, 0.0)

    inv_n = 1.0 / RW_HEAD_DIM
    y_all = jnp.concatenate(y, axis=0)
    mu = seg_sum(y_all) * inv_n
    var = seg_sum(jnp.square(y_all - mu)) * inv_n
    rk = seg_sum(jnp.concatenate([r[p] * k2[p] * vec_ref[4:5, ps(p)] for p in pairs], axis=0))
    yn = (y_all - mu) * lax.rsqrt(var + RW_GN_EPS)
    for p in pairs:
        y_o[0, :, ps(p)] = rows(yn, p) * vec_ref[5:6, ps(p)] + vec_ref[6:7, ps(p)] + rows(rk, p) * v[p]

    @pl.when(c == pl.num_programs(1) - 1)
    def _():
        sT_o[0] = s_sc[...]


def _wkv(r, lw, k, v, a, vec, s0, *, hp, inv_hp):
    B, T, D = r.shape
    H = D // RW_HEAD_DIM
    C, W, n_pairs = WKV_CHUNK, 2 * RW_HEAD_DIM, H // 2
    s_bd = jnp.zeros((B, n_pairs, W, W), F32)
    s_bd = s_bd.at[:, :, :C, :C].set(s0[:, 0::2]).at[:, :, C:, C:].set(s0[:, 1::2])
    row_spec = pl.BlockSpec((1, C, D), lambda b, c: (b, c, 0))
    st_spec = pl.BlockSpec((1, n_pairs, W, W), lambda b, c: (b, 0, 0, 0))
    ones = _block_diag(W, RW_HEAD_DIM, 1.0)
    kern = functools.partial(_wkv_kernel, n_pairs=n_pairs, hp=hp, inv_hp=inv_hp)
    y, s_out = pl.pallas_call(
        kern, grid=(B, T // C),
        in_specs=[row_spec] * 5 + [pl.BlockSpec(vec.shape, lambda b, c: (0, 0)), pl.BlockSpec((W, W), lambda b, c: (0, 0)),
                                   st_spec],
        out_specs=[row_spec, st_spec],
        out_shape=[jax.ShapeDtypeStruct((B, T, D), F32), jax.ShapeDtypeStruct(s_bd.shape, F32)],
        scratch_shapes=[pltpu.VMEM((n_pairs, W, W), F32)],
        compiler_params=_cparams(("parallel", "arbitrary")), name="wkv_chunk",
    )(r, lw, k, v, a, vec, ones, s_bd)
    s_fin = jnp.stack([s_out[:, :, :C, :C], s_out[:, :, C:, C:]], axis=2).reshape(B, H, RW_HEAD_DIM, RW_HEAD_DIM)
    return y, s_fin


def _row_call(kernel, rows, consts, out_widths, out_dtypes, tm, name):
    n = rows[0].shape[0]
    row_spec = lambda w: pl.BlockSpec((tm, w), lambda i: (i, 0))
    full = lambda a: pl.BlockSpec(a.shape, lambda i: (0,) * a.ndim)
    return pl.pallas_call(
        kernel, grid=(n // tm,),
        in_specs=[row_spec(a.shape[1]) for a in rows] + [full(c) for c in consts],
        out_specs=[row_spec(w) for w in out_widths],
        out_shape=[jax.ShapeDtypeStruct((n, w), dt) for w, dt in zip(out_widths, out_dtypes)],
        compiler_params=_cparams(("parallel",)), name=name,
    )(*rows, *consts)


def _rwkv_out_kernel(x_ref, y_ref, g_ref, wo_ref, o_ref, *, hp):
    o_ref[...] = x_ref[...] + _mm(y_ref[...] * g_ref[...], wo_ref[...], hp=hp)


def _kv_proj_kernel(h_ref, g_ref, wkv_ref, kg_ref, bd_ref, *rest, hp, pair_out):
    if pair_out:
        pk_ref, pv_ref, ones_ref, cmp_o, slc_o, win_o = rest[:6]
        pair_outs = {1: rest[6:8], 2: rest[8:10]}
    else:
        cmp_o, slc_o, win_o = rest
    kv = _mm(_rms(h_ref[...], g_ref[...]), wkv_ref[...], hp=hp)
    cmp_o[...] = kv[:, :2 * KV_W]
    for br, out in ((1, slc_o), (2, win_o)):
        kraw = kv[:, br * 2 * KV_W: br * 2 * KV_W + KV_W]
        ms = _dot_hilo(kraw * kraw, bd_ref[...])
        kn = kraw * lax.rsqrt(ms + RMS_EPS) * kg_ref[br:br + 1, :]
        v = kv[:, br * 2 * KV_W + KV_W: (br + 1) * 2 * KV_W]
        out[:, :KV_W] = kn
        out[:, KV_W:] = v
        if pair_out:
            kab_o, v1_o = pair_outs[br]
            kab_o[...] = jnp.dot(kn.astype(BF16), pk_ref[...], preferred_element_type=F32).astype(BF16)
            v1_o[...] = (jnp.dot(v.astype(BF16), pv_ref[...], preferred_element_type=F32) + ones_ref[...]).astype(BF16)


def _q_proj_kernel(h_ref, g_ref, wq_ref, wg_ref, qg_ref, bd_ref, q_o, gate_o, *, hp, q_scale):
    hn = _rms(h_ref[...], g_ref[...])
    q = _mm(hn, wq_ref[...], hp=hp)
    ms = _dot_hilo(q * q, bd_ref[...])
    q_o[...] = (q * lax.rsqrt(ms + RMS_EPS) * qg_ref[...] * q_scale).astype(q_o.dtype)
    gate_o[...] = jax.nn.sigmoid(_mm(hn, wg_ref[...], hp=hp))


def _merge_kernel(h_ref, gate_ref, oc_ref, os_ref, ow_ref, eg_ref, wo_ref, o_ref, *, hp):
    D = h_ref.shape[1]
    ge = _dot_hilo(gate_ref[...], eg_ref[...])
    o = ge[:, :D] * oc_ref[...] + ge[:, D:2 * D] * os_ref[...] + ge[:, 2 * D:] * ow_ref[...]
    o_ref[...] = h_ref[...] + _mm(o, wo_ref[...], hp=hp)


def _router_kernel(h_ref, g_ref, wr_ref, xn_o, logit_o):
    xn = _rms(h_ref[...], g_ref[...])
    xn_o[...] = xn.astype(xn_o.dtype)
    logit_o[...] = _mm(xn, wr_ref[...], hp=True)


def _ffn_kernel(x_ref, g_ref, wg_ref, wu_ref, wd_ref, o_ref, xn_sc, acc_sc, *, hp):
    f = pl.program_id(1)

    @pl.when(f == 0)
    def _():
        xn_sc[...] = _rms(x_ref[...], g_ref[...]).astype(xn_sc.dtype)
        acc_sc[...] = jnp.zeros_like(acc_sc)

    xn = xn_sc[...]
    hid = _silu(_mm(xn, wg_ref[...], hp=hp)) * _mm(xn, wu_ref[...], hp=hp)
    acc_sc[...] += _mm(hid, wd_ref[...], hp=hp)

    @pl.when(f == pl.num_programs(1) - 1)
    def _():
        o_ref[...] = x_ref[...] + acc_sc[...]


def _ffn(x, g, wg, wu, wd, *, tm, tf, hp):
    n, D = x.shape
    dff = wg.shape[1]
    return pl.pallas_call(
        functools.partial(_ffn_kernel, hp=hp), grid=(n // tm, dff // tf),
        in_specs=[pl.BlockSpec((tm, D), lambda i, f: (i, 0)), pl.BlockSpec((1, D), lambda i, f: (0, 0)),
                  pl.BlockSpec((D, tf), lambda i, f: (0, f)), pl.BlockSpec((D, tf), lambda i, f: (0, f)),
                  pl.BlockSpec((tf, D), lambda i, f: (f, 0))],
        out_specs=pl.BlockSpec((tm, D), lambda i, f: (i, 0)),
        out_shape=jax.ShapeDtypeStruct((n, D), F32),
        scratch_shapes=[pltpu.VMEM((tm, D), F32 if hp else BF16), pltpu.VMEM((tm, D), F32)],
        compiler_params=_cparams(("parallel", "arbitrary")), name="ffn_dense",
    )(x, g, wg, wu, wd)


def _moe_ffn_kernel(be_ref, nb_ref, x_ref, wg_ref, wu_ref, wd_ref, o_ref, acc_sc, *, hp):
    i = pl.program_id(0)
    f = pl.program_id(1)

    @pl.when(i < nb_ref[0])
    def _():
        @pl.when(f == 0)
        def _():
            acc_sc[...] = jnp.zeros_like(acc_sc)

        x = x_ref[...]
        hid = _silu(_mm(x, wg_ref[0], hp=hp)) * _mm(x, wu_ref[0], hp=hp)
        acc_sc[...] += _mm(hid, wd_ref[0], hp=hp)

        @pl.when(f == pl.num_programs(1) - 1)
        def _():
            o_ref[...] = acc_sc[...]

    @pl.when((i >= nb_ref[0]) & (f == pl.num_programs(1) - 1))
    def _():
        o_ref[...] = jnp.zeros_like(o_ref)


def _moe_ffn(block_e, n_used, xs, wg, wu, wd, *, tm, tf, hp):
    cap, D = xs.shape
    dff = wg.shape[2]
    w_idx = lambda i, f, be, nb: jnp.where(i < nb[0], f, dff // tf - 1)
    grid_spec = pltpu.PrefetchScalarGridSpec(
        num_scalar_prefetch=2, grid=(cap // tm, dff // tf),
        in_specs=[pl.BlockSpec((tm, D), lambda i, f, be, nb: (i, 0)),
                  pl.BlockSpec((1, D, tf), lambda i, f, be, nb: (be[i], 0, w_idx(i, f, be, nb))),
                  pl.BlockSpec((1, D, tf), lambda i, f, be, nb: (be[i], 0, w_idx(i, f, be, nb))),
                  pl.BlockSpec((1, tf, D), lambda i, f, be, nb: (be[i], w_idx(i, f, be, nb), 0))],
        out_specs=pl.BlockSpec((tm, D), lambda i, f, be, nb: (i, 0)),
        scratch_shapes=[pltpu.VMEM((tm, D), F32)])
    return pl.pallas_call(
        functools.partial(_moe_ffn_kernel, hp=hp), grid_spec=grid_spec, out_shape=jax.ShapeDtypeStruct((cap, D), F32),
        compiler_params=_cparams(("arbitrary", "arbitrary")), name="moe_ffn",
    )(block_e, n_used, xs, wg, wu, wd)


def _compress_kernel(x_ref, pe_ref, w1_ref, w2_ref, kg_ref, o_ref, *, hp):
    c = pl.program_id(1)
    x = x_ref[0, 0, 0]
    R = x.shape[0]
    p0 = _mm(x + pe_ref[0, 0], w1_ref[0, 0], hp=hp)
    p1 = _mm(x + pe_ref[0, 1], w1_ref[0, 1], hp=hp)
    pre = p0 + pltpu.roll(p1, R - 1, 0)
    out = _mm(_silu(pre), w2_ref[0], hp=hp)
    o_ref[0, 0, 0] = jnp.where(c == 0, _rms(out, kg_ref[...]), out)


def _compress(xsub, pe, w1, w2, kg, *, hp):
    S, _, G, R, W = xsub.shape
    return pl.pallas_call(
        functools.partial(_compress_kernel, hp=hp), grid=(S, 2, G),
        in_specs=[pl.BlockSpec((1, 1, 1, R, W), lambda s, c, g: (s, c, g, 0, 0)),
                  pl.BlockSpec((1, 2, 1, W), lambda s, c, g: (c, 0, 0, 0)),
                  pl.BlockSpec((1, 2, W, HEAD_DIM), lambda s, c, g: (c, 0, 0, 0)),
                  pl.BlockSpec((1, HEAD_DIM, HEAD_DIM), lambda s, c, g: (c, 0, 0)),
                  pl.BlockSpec((1, HEAD_DIM), lambda s, c, g: (0, 0))],
        out_specs=pl.BlockSpec((1, 1, 1, R, HEAD_DIM), lambda s, c, g: (s, c, g, 0, 0)),
        out_shape=jax.ShapeDtypeStruct((S, 2, G, R, HEAD_DIM), F32),
        compiler_params=_cparams(("parallel", "arbitrary", "arbitrary")), name="kv_compress",
    )(xsub, pe, w1, w2, kg)


def _paged_sub_proj_kernel(pt_ref, cache_hbm, w_ref, o_ref, buf, sem, *, pages_per_step, page, hp):
    lin = pl.program_id(0) * pl.num_programs(1) + pl.program_id(1)
    total = pl.num_programs(0) * pl.num_programs(1)
    slot = lin % 2

    n_lane_pairs = buf.shape[1]

    def page_copies(step, into):
        return [pltpu.make_async_copy(cache_hbm.at[pt_ref[step * pages_per_step + p], pl.ds(0, page), pl.ds(pair * LANES, LANES)],
                                      buf.at[into, pair, pl.ds(p * page, page)], sem.at[into])
                for p in range(pages_per_step) for pair in range(n_lane_pairs)]

    @pl.when(lin == 0)
    def _():
        for cp in page_copies(0, 0):
            cp.start()

    @pl.when(lin + 1 < total)
    def _():
        for cp in page_copies(lin + 1, 1 - slot):
            cp.start()

    for cp in page_copies(lin, slot):
        cp.wait()
    n = pages_per_step * page // CMP_STRIDE
    for pair in range(n_lane_pairs):
        c = pair // (N_KV_HEADS // 2)
        rows = buf.at[slot, pair]
        x2 = jnp.concatenate([rows[pl.ds(s, n, stride=CMP_STRIDE), :] for s in range(CMP_STRIDE)], axis=1)
        o_ref[0, pair] = _mm(x2, w_ref[c], hp=hp and c == 0)


def _paged_sub_proj(page_table, cache, w2x, *, pages_per_step, hp):
    S, n_pages = page_table.shape
    page, width = cache.shape[1:]
    n = pages_per_step * page // CMP_STRIDE
    n_t = n_pages // pages_per_step
    grid_spec = pltpu.PrefetchScalarGridSpec(
        num_scalar_prefetch=1, grid=(S, n_t),
        in_specs=[pl.BlockSpec(memory_space=pl.ANY), pl.BlockSpec(w2x.shape, lambda s, t, pt: (0, 0, 0))],
        out_specs=pl.BlockSpec((1, 4, n, w2x.shape[2]), lambda s, t, pt: (s, 0, t, 0)),
        scratch_shapes=[pltpu.VMEM((2, width // LANES, pages_per_step * page, LANES), F32), pltpu.SemaphoreType.DMA((2,))])
    kern = functools.partial(_paged_sub_proj_kernel, pages_per_step=pages_per_step, page=page, hp=hp)
    return pl.pallas_call(
        kern, grid_spec=grid_spec, out_shape=jax.ShapeDtypeStruct((S, 4, n_t * n, w2x.shape[2]), F32),
        compiler_params=_cparams(("arbitrary", "arbitrary")), name="paged_sub_proj",
    )(page_table.reshape(-1), cache, w2x)


def _compress_tail_kernel(p_ref, pe_ref, w1_ref, w2_ref, kg_ref, o_ref, *, hp):
    R = p_ref.shape[2]
    half_groups = N_KV_HEADS // 2
    for c in range(2):
        bias = (_mm(jnp.broadcast_to(pe_ref[c, 0], (SUBLANES, pe_ref.shape[3])), w1_ref[c, 0], hp=True)
                + _mm(jnp.broadcast_to(pe_ref[c, 1], (SUBLANES, pe_ref.shape[3])), w1_ref[c, 1], hp=True))[0:1]
        outs = []
        for g in range(N_KV_HEADS):
            p = p_ref[0, c * half_groups + g // 2]
            e = g % 2
            p0 = p[:, e * HEAD_DIM:(e + 1) * HEAD_DIM]
            p1 = p[:, (2 + e) * HEAD_DIM:(3 + e) * HEAD_DIM]
            out = _mm(_silu(p0 + pltpu.roll(p1, R - 1, 0) + bias), w2_ref[c], hp=hp)
            outs.append(_rms(out, kg_ref[...]) if c == 0 else out)
        o_ref[0, c] = jnp.concatenate(outs, axis=1)


def _compress_tail(p, pe, w1, w2, kg, *, hp):
    S, _, R, _ = p.shape
    full = lambda a: pl.BlockSpec(a.shape, lambda s: (0,) * a.ndim)
    return pl.pallas_call(
        functools.partial(_compress_tail_kernel, hp=hp), grid=(S,),
        in_specs=[pl.BlockSpec((1,) + p.shape[1:], lambda s: (s, 0, 0, 0)), full(pe), full(w1), full(w2), full(kg)],
        out_specs=pl.BlockSpec((1, 2, R, KV_W), lambda s: (s, 0, 0, 0)),
        out_shape=jax.ShapeDtypeStruct((S, 2, R, KV_W), F32),
        compiler_params=_cparams(("parallel",)), name="kv_compress_tail",
    )(p, pe, w1, w2, kg)


def _stack_heads(q, g, tq):
    return jnp.concatenate([q[:, (HEADS_PER_KV * g + i) * HEAD_DIM:(HEADS_PER_KV * g + i + 1) * HEAD_DIM]
                            for i in range(HEADS_PER_KV)], axis=0)


def _cmp_attn_kernel(q_ref, kc_ref, vc_ref, m_ref, o_o, sel_o, *, n_cmp, n_slc, pos_base, per_tile, hp):
    tq = q_ref.shape[1]
    ncp = kc_ref.shape[1]
    nsp = m_ref.shape[1]
    q0 = pos_base + (pl.program_id(1) * tq if per_tile else 0)
    rows = HEADS_PER_KV * tq
    q_pos = q0 + lax.broadcasted_iota(jnp.int32, (rows, ncp), 0) % tq
    n_id = lax.broadcasted_iota(jnp.int32, (rows, ncp), 1)
    bias = jnp.where((n_id * CMP_STRIDE + CMP_BLOCK - 1 <= q_pos) & (n_id < n_cmp), 0.0, MASKED)
    blk = lax.broadcasted_iota(jnp.int32, (tq, nsp), 1)
    cur = (q0 + lax.broadcasted_iota(jnp.int32, (tq, nsp), 0)) // SLC_BLOCK
    forced = (blk == 0) | (blk == cur) | (blk == cur - 1)
    q = q_ref[0]
    scores = []
    for g in range(N_KV_HEADS):
        qs = _stack_heads(q, g, tq)
        s = _mm(qs, kc_ref[0, :, g * HEAD_DIM:(g + 1) * HEAD_DIM], "nt", hp) + bias
        m = jnp.maximum(jnp.max(s, -1, keepdims=True), M_INIT)
        e = jnp.exp(s - m)
        p = e / jnp.maximum(jnp.sum(e, -1, keepdims=True), 1e-30)
        o = _mm(p, vc_ref[0, :, g * HEAD_DIM:(g + 1) * HEAD_DIM], hp=hp)
        psum = p[0:tq]
        for i in range(HEADS_PER_KV):
            col = (HEADS_PER_KV * g + i) * HEAD_DIM
            o_o[0, :, col:col + HEAD_DIM] = o[i * tq:(i + 1) * tq]
            if i:
                psum = psum + p[i * tq:(i + 1) * tq]
        imp = _dot_hilo(psum, m_ref[...])
        scores.append(jnp.where(blk <= cur, jnp.where(forced, BIG, imp), -BIG))
    score = jnp.concatenate(scores, axis=0)
    blk_r = lax.broadcasted_iota(jnp.int32, score.shape, 1)

    def count_ahead(i, rank):
        s_i = jnp.sum(jnp.where(blk_r == i, score, 0.0), axis=-1, keepdims=True)
        return rank + jnp.where(s_i > score, 1.0, jnp.where((s_i == score) & (i < blk_r), 1.0, 0.0))

    rank = lax.fori_loop(0, n_slc, count_ahead, jnp.zeros(score.shape, F32), unroll=8)
    for g in range(N_KV_HEADS):
        sel_o[0, :, g * nsp:(g + 1) * nsp] = jnp.where(rank[g * tq:(g + 1) * tq] < min(SLC_TOP, n_slc), 1.0, 0.0).astype(sel_o.dtype)


def _cmp_attn(q, kc, vc, m, *, n_cmp, n_slc, pos_base, per_tile, tq, hp):
    B, T, D = q.shape
    ncp, nsp = m.shape
    kv_spec = pl.BlockSpec((1, ncp, KV_W), lambda b, i: (b, 0, 0))
    kern = functools.partial(_cmp_attn_kernel, n_cmp=n_cmp, n_slc=n_slc, pos_base=pos_base, per_tile=per_tile, hp=hp)
    return pl.pallas_call(
        kern, grid=(B, T // tq),
        in_specs=[pl.BlockSpec((1, tq, D), lambda b, i: (b, i, 0)), kv_spec, kv_spec,
                  pl.BlockSpec((ncp, nsp), lambda b, i: (0, 0))],
        out_specs=[pl.BlockSpec((1, tq, D), lambda b, i: (b, i, 0)),
                   pl.BlockSpec((1, tq, N_KV_HEADS * nsp), lambda b, i: (b, i, 0))],
        out_shape=[jax.ShapeDtypeStruct((B, T, D), F32), jax.ShapeDtypeStruct((B, T, N_KV_HEADS * nsp), BF16)],
        compiler_params=_cparams(("parallel", "arbitrary")), name="cmp_attn",
    )(q, kc, vc, m)


def _pair_rows(q, g):
    base = g * HEADS_PER_KV * HEAD_DIM
    return jnp.concatenate([q[:, base:base + LANES], q[:, base + LANES:base + 2 * LANES]], axis=0)


def _finish_pair(acc_even, acc_odd, p, tq):
    a0 = acc_even[p * tq:(p + 1) * tq]
    a1 = acc_odd[p * tq:(p + 1) * tq]
    o0 = a0 / jnp.maximum(a0[:, HEAD_DIM:HEAD_DIM + 1], 1e-30)
    o1 = a1 / jnp.maximum(a1[:, HEAD_DIM:HEAD_DIM + 1], 1e-30)
    lane = lax.broadcasted_iota(jnp.int32, (tq, LANES), 1)
    return jnp.where(lane < HEAD_DIM, o0, pltpu.roll(o1, HEAD_DIM, 1))


def _cmp_sel_kernel(q_ref, kab_ref, v1_ref, m_ref, o_o, sel_o, *, n_cmp, n_slc, bounded):
    tq = q_ref.shape[1]
    ncp = kab_ref.shape[1]
    nsp = LANES // N_KV_HEADS
    q0 = pl.program_id(1) * tq
    q_pos = q0 + lax.broadcasted_iota(jnp.int32, (tq, ncp), 0)
    n_id = lax.broadcasted_iota(jnp.int32, (tq, ncp), 1)
    b = jnp.where((n_id * CMP_STRIDE + CMP_BLOCK - 1 <= q_pos) & (n_id < n_cmp), 0.0, MASKED)
    b2 = jnp.concatenate([b, b], axis=0)
    q = q_ref[0]
    imp = jnp.zeros((tq, LANES), F32)
    for g in range(N_KV_HEADS):
        lhs = _pair_rows(q, g)
        vv = v1_ref[0, :, g * LANES:(g + 1) * LANES]
        accs, psum = [], None
        for half in range(2):
            s = _mm(lhs, kab_ref[0, :, (2 * g + half) * LANES:(2 * g + half + 1) * LANES], "nt") + b2
            e = jnp.exp2(s) if bounded else jnp.exp2(s - jnp.maximum(jnp.max(s, -1, keepdims=True), M_INIT))
            acc = _mm(e, vv)
            accs.append(acc)
            p = e / jnp.maximum(acc[:, HEAD_DIM:HEAD_DIM + 1], 1e-30)
            ph = p[:tq] + p[tq:]
            psum = ph if psum is None else psum + ph
        imp = imp + _dot_hilo(psum, m_ref[g])
        for p_ in range(2):
            col = g * 2 * LANES + p_ * LANES
            o_o[0, :, col:col + LANES] = _finish_pair(accs[0], accs[1], p_, tq)
    lane = lax.broadcasted_iota(jnp.int32, (tq, LANES), 1)
    blk = lane % nsp
    cur = (q0 + lax.broadcasted_iota(jnp.int32, (tq, LANES), 0)) // SLC_BLOCK
    forced = (blk == 0) | (blk == cur) | (blk == cur - 1)
    score = jnp.where(blk <= cur, jnp.where(forced, BIG, imp), -BIG)
    rank = jnp.zeros((tq, LANES), F32)
    for d in range(1, nsp):
        wrapped = blk + d >= nsp
        partner = jnp.where(wrapped, pltpu.roll(score, nsp - d, 1), pltpu.roll(score, LANES - d, 1))
        rank = rank + jnp.where(partner > score, 1.0, jnp.where((partner == score) & wrapped, 1.0, 0.0))
    sel_o[0] = jnp.where(rank < min(SLC_TOP, n_slc), 1.0, 0.0).astype(sel_o.dtype)


def _cmp_sel(q, kab, v1, m, *, n_cmp, n_slc, tq, bounded):
    B, T, D = q.shape
    ncp = kab.shape[1]
    assert m.shape == (N_KV_HEADS, ncp, LANES)
    whole = lambda a: pl.BlockSpec((1,) + a.shape[1:], lambda b, i: (b,) + (0,) * (a.ndim - 1))
    kern = functools.partial(_cmp_sel_kernel, n_cmp=n_cmp, n_slc=n_slc, bounded=bounded)
    return pl.pallas_call(
        kern, grid=(B, T // tq),
        in_specs=[pl.BlockSpec((1, tq, D), lambda b, i: (b, i, 0)), whole(kab), whole(v1),
                  pl.BlockSpec(m.shape, lambda b, i: (0, 0, 0))],
        out_specs=[pl.BlockSpec((1, tq, D), lambda b, i: (b, i, 0)), pl.BlockSpec((1, tq, LANES), lambda b, i: (b, i, 0))],
        out_shape=[jax.ShapeDtypeStruct((B, T, D), F32), jax.ShapeDtypeStruct((B, T, LANES), BF16)],
        compiler_params=_cparams(("parallel", "arbitrary")), name="cmp_sel_attn",
    )(q, kab, v1, m)


def _flash_pairs(lhs, kab_ref, v1_ref, kt_lo, kt_hi, tk, bias_fn, bounded):
    rows = lhs[0].shape[0]

    def body(kt, carry):
        k0 = pl.multiple_of(kt * tk, tk)
        biases = bias_fn(k0)
        out = []
        for g in range(N_KV_HEADS):
            vv = v1_ref[0, pl.ds(k0, tk), g * LANES:(g + 1) * LANES]
            for half in range(2):
                idx = 2 * g + half
                s = _mm(lhs[g], kab_ref[0, pl.ds(k0, tk), idx * LANES:(idx + 1) * LANES], "nt") + biases[g]
                if bounded:
                    out.append(carry[idx] + _mm(jnp.exp2(s), vv))
                else:
                    m, acc = carry[idx]
                    m_new = jnp.maximum(m, jnp.max(s, -1, keepdims=True))
                    out.append((m_new, jnp.exp2(m - m_new) * acc + _mm(jnp.exp2(s - m_new), vv)))
        return tuple(out)

    zero = jnp.zeros((rows, LANES), F32)
    if bounded:
        return lax.fori_loop(kt_lo, kt_hi, body, (zero,) * (2 * N_KV_HEADS))
    start = (jnp.full((rows, 1), M_INIT, F32), zero)
    return tuple(acc for _, acc in lax.fori_loop(kt_lo, kt_hi, body, (start,) * (2 * N_KV_HEADS)))


def _slc_win_attn_kernel(q_ref, sel_ref, skab_ref, sv1_ref, wkab_ref, wv1_ref, os_o, ow_o, *, tk, bounded):
    tq = q_ref.shape[1]
    nsp = sel_ref.shape[2] // N_KV_HEADS
    q0 = pl.program_id(1) * tq
    q_pos = q0 + lax.broadcasted_iota(jnp.int32, (tq, tk), 0)
    k_off = lax.broadcasted_iota(jnp.int32, (tq, tk), 1)
    sel_lane = lax.broadcasted_iota(jnp.int32, (N_KV_HEADS * nsp, tk), 0)
    blk_k = lax.broadcasted_iota(jnp.int32, (N_KV_HEADS * nsp, tk), 1)
    drop = jnp.where(sel_ref[0].astype(F32) > 0.5, 0.0, MASKED).astype(BF16)
    q = q_ref[0]
    kt_hi = (q0 + tq + tk - 1) // tk
    kt_lo_win = jnp.maximum(q0 - WINDOW + 1, 0) // tk

    twice = lambda b: jnp.concatenate([b, b], axis=0)

    def win_bias(k0):
        d = q_pos - (k_off + k0)
        return [twice(jnp.where((d >= 0) & (d < WINDOW), 0.0, MASKED))] * N_KV_HEADS

    def slc_bias(k0):
        causal = jnp.where(k_off + k0 <= q_pos, 0.0, MASKED)
        key_blk = (blk_k + k0) // SLC_BLOCK
        out = []
        for g in range(N_KV_HEADS):
            expand = jnp.where(g * nsp + key_blk == sel_lane, 1.0, 0.0).astype(BF16)
            out.append(twice(jnp.dot(drop, expand, preferred_element_type=F32) + causal))
        return out

    lhs = [_pair_rows(q, g) for g in range(N_KV_HEADS)]
    acc_s = _flash_pairs(lhs, skab_ref, sv1_ref, 0, kt_hi, tk, slc_bias, bounded)
    acc_w = _flash_pairs(lhs, wkab_ref, wv1_ref, kt_lo_win, kt_hi, tk, win_bias, bounded)
    for g in range(N_KV_HEADS):
        for p_ in range(2):
            col = g * 2 * LANES + p_ * LANES
            os_o[0, :, col:col + LANES] = _finish_pair(acc_s[2 * g], acc_s[2 * g + 1], p_, tq)
            ow_o[0, :, col:col + LANES] = _finish_pair(acc_w[2 * g], acc_w[2 * g + 1], p_, tq)


def _slc_win_attn(q, sel, skab, sv1, wkab, wv1, *, tq, tk, bounded):
    B, T, D = q.shape
    whole = lambda a: pl.BlockSpec((1,) + a.shape[1:], lambda b, i: (b, 0, 0))
    row = lambda w: pl.BlockSpec((1, tq, w), lambda b, i: (b, i, 0))
    return pl.pallas_call(
        functools.partial(_slc_win_attn_kernel, tk=tk, bounded=bounded), grid=(B, T // tq),
        in_specs=[row(D), row(sel.shape[2]), whole(skab), whole(sv1), whole(wkab), whole(wv1)],
        out_specs=[row(D), row(D)],
        out_shape=[jax.ShapeDtypeStruct((B, T, D), F32)] * 2,
        compiler_params=_cparams(("parallel", "arbitrary")), name="slc_win_attn",
    )(q, sel, skab, sv1, wkab, wv1)


def _pair_keys(k):
    z = jnp.zeros_like(k)
    return jnp.concatenate([k, z, z, k], axis=-1).reshape(k.shape[0], k.shape[1], -1)


def _ones_values(v):
    return jnp.concatenate([v, jnp.ones_like(v)], axis=-1).reshape(v.shape[0], v.shape[1], -1)


def _sample_attn_kernel(blk_ref, q_ref, kpos_ref, new_ref, win_ref, cache_hbm, os_o, ow_o, kbuf, vbuf, sem, *,
                        n_q, n_sel, pos_base, win_base, n_win, hp):
    b = pl.program_id(0)
    slot = b % 2
    per_seq = n_q * N_KV_HEADS * n_sel
    per_page = cache_hbm.shape[1] // SLC_BLOCK

    def pool_copies(step, into, start):
        def one(i, carry):
            blk = blk_ref[step * per_seq + i]
            qg = i // n_sel

            @pl.when(blk >= 0)
            def _():
                src_rows = pl.ds(pl.multiple_of((blk % per_page) * SLC_BLOCK, SLC_BLOCK), SLC_BLOCK)
                for c, dst in ((0, kbuf), (1, vbuf)):
                    cp = pltpu.make_async_copy(cache_hbm.at[blk // per_page, src_rows, c * N_KV_HEADS + qg % N_KV_HEADS],
                                               dst.at[into, qg, pl.ds((i % n_sel) * SLC_BLOCK, SLC_BLOCK)], sem.at[into])
                    cp.start() if start else cp.wait()
            return carry
        lax.fori_loop(0, per_seq, one, 0)

    @pl.when(b == 0)
    def _():
        pool_copies(0, 0, True)

    @pl.when(b + 1 < pl.num_programs(0))
    def _():
        pool_copies(b + 1, 1 - slot, True)

    pool_copies(b, slot, False)

    def new_rows(i, carry):
        qg = i // n_sel

        @pl.when(blk_ref[b * per_seq + i] < 0)
        def _():
            kbuf[slot, qg, pl.ds((i % n_sel) * SLC_BLOCK, SLC_BLOCK), :] = new_ref[0, qg % N_KV_HEADS]
            vbuf[slot, qg, pl.ds((i % n_sel) * SLC_BLOCK, SLC_BLOCK), :] = new_ref[0, N_KV_HEADS + qg % N_KV_HEADS]
        return carry
    lax.fori_loop(0, per_seq, new_rows, 0)

    tq = q_ref.shape[1]
    rows = HEADS_PER_KV * tq
    nk = n_sel * SLC_BLOCK
    nwp = win_ref.shape[1]
    q = q_ref[0]
    row_q = lax.broadcasted_iota(jnp.int32, (rows, 1), 0) % tq
    w_id = lax.broadcasted_iota(jnp.int32, (rows, nwp), 1)
    d_win = (pos_base + row_q) - (win_base + w_id)
    win_bias = jnp.where((d_win >= 0) & (d_win < WINDOW) & (win_base + w_id >= 0) & (w_id < n_win), 0.0, MASKED)

    def attend(s, v):
        m = jnp.maximum(jnp.max(s, -1, keepdims=True), M_INIT)
        e = jnp.exp(s - m)
        p = e / jnp.maximum(jnp.sum(e, -1, keepdims=True), 1e-30)
        return _mm(p, v, hp=hp)

    for g in range(N_KV_HEADS):
        qs = _stack_heads(q, g, tq)
        o_s = jnp.zeros((rows, HEAD_DIM), F32)
        for qi in range(n_q):
            qg = qi * N_KV_HEADS + g
            kpos = kpos_ref[0, qg:qg + 1, :]
            bias = jnp.broadcast_to(jnp.where(kpos <= pos_base + qi, 0.0, MASKED), (rows, nk))
            o_qi = attend(_mm(qs, kbuf[slot, qg], "nt", hp) + bias, vbuf[slot, qg])
            o_s = jnp.where(row_q == qi, o_qi, o_s)
        o_w = attend(_mm(qs, win_ref[0, :, g * HEAD_DIM:(g + 1) * HEAD_DIM], "nt", hp) + win_bias,
                     win_ref[0, :, KV_W + g * HEAD_DIM:KV_W + (g + 1) * HEAD_DIM])
        for i in range(HEADS_PER_KV):
            col = (HEADS_PER_KV * g + i) * HEAD_DIM
            os_o[0, :, col:col + HEAD_DIM] = o_s[i * tq:(i + 1) * tq]
            ow_o[0, :, col:col + HEAD_DIM] = o_w[i * tq:(i + 1) * tq]


def _sample_attn(pool_blk, q, kpos, new_win, win_all, cache_rows, *, n_q, n_sel, pos_base, win_base, n_win, hp):
    B, tq, D = q.shape
    kern = functools.partial(_sample_attn_kernel, n_q=n_q, n_sel=n_sel, pos_base=pos_base, win_base=win_base,
                             n_win=n_win, hp=hp)
    blk = lambda a: pl.BlockSpec((1,) + a.shape[1:], lambda b, pb: (b,) + (0,) * (a.ndim - 1))
    buf = pltpu.VMEM((2, n_q * N_KV_HEADS, n_sel * SLC_BLOCK, HEAD_DIM), F32)
    grid_spec = pltpu.PrefetchScalarGridSpec(
        num_scalar_prefetch=1, grid=(B,),
        in_specs=[blk(q), blk(kpos), blk(new_win), blk(win_all), pl.BlockSpec(memory_space=pl.ANY)],
        out_specs=[blk(q), blk(q)],
        scratch_shapes=[buf, buf, pltpu.SemaphoreType.DMA((2,))])
    return pl.pallas_call(
        kern, grid_spec=grid_spec, out_shape=[jax.ShapeDtypeStruct((B, tq, D), F32)] * 2,
        compiler_params=_cparams(("arbitrary",)), name="sample_slc_win_attn",
    )(pool_blk.reshape(-1), q, kpos, new_win, win_all, cache_rows)


def _pick_tile(n, cap=512):
    return next(t for t in (512, 256, 128, 64, 32, 16, 8) if t <= cap and n % t == 0)


def _pad_rows(x, t_pad):
    return jnp.pad(x, ((0, 0), (0, t_pad - x.shape[1]), (0, 0)))


def _rwkv_layer(x, shift0, wkv0, p, *, tm, hp, inv_hp):
    B, T, D = x.shape
    t_pad = -(-T // tm) * tm
    xp = _pad_rows(x, t_pad)
    r, lw, k, v, a, g, shift = _rwkv_proj(xp, shift0[:, None, :], p["g"], p["mix"], p["vec"], p["wrkv"], p["dw1"],
                                          p["dw2"], p["aw1"], p["aw2"], p["gw1"], p["gw2"], t_real=T, tm=tm, hp=hp)
    if t_pad != T:
        live = (jnp.arange(t_pad) < T)[None, :, None]
        r, lw, k, v, a = (jnp.where(live, t, 0.0) for t in (r, lw, k, v, a))
    y, s_fin = _wkv(r, lw, k, v, a, p["vec"], wkv0, hp=hp, inv_hp=inv_hp)
    n = B * t_pad
    out = _row_call(functools.partial(_rwkv_out_kernel, hp=hp), [xp.reshape(n, D), y.reshape(n, D), g.reshape(n, D)],
                    [p["wo"]], [D], [F32], _pick_tile(n), "rwkv_out")[0]
    return out.reshape(B, t_pad, D)[:, :T], shift[:, 0], s_fin


def _sub_blocks(rows):
    S, L = rows.shape[:2]
    n_sub = L // CMP_STRIDE
    r = rows[:, :n_sub * CMP_STRIDE].reshape(S, n_sub, CMP_STRIDE, 2, N_KV_HEADS, HEAD_DIM)
    r = jnp.transpose(r, (0, 3, 4, 1, 2, 5)).reshape(S, 2, N_KV_HEADS, n_sub, CMP_STRIDE * HEAD_DIM)
    r_pad = -(-n_sub // SUBLANES) * SUBLANES
    return jnp.pad(r, ((0, 0),) * 3 + ((0, r_pad - n_sub), (0, 0))), n_sub - CMP_BLOCK // CMP_STRIDE + 1


def _compressed_kv(rows, p, *, hp):
    xsub, n_cmp = _sub_blocks(rows)
    out = _compress(xsub, p["cmp_pe"], p["cmp_w1"], p["cmp_w2"], p["k_norm_g"][0:1], hp=hp)
    S, _, G, R, dh = out.shape
    out = jnp.transpose(out, (1, 0, 3, 2, 4)).reshape(2, S, R, G * dh).astype(F32 if hp else BF16)
    return out[0], out[1], n_cmp


def _overlap_matrix(ncp, n_cmp, n_slc, nsp):
    c0 = np.arange(ncp)[:, None] * CMP_STRIDE
    s0 = np.arange(nsp)[None, :] * SLC_BLOCK
    m = (c0 < s0 + SLC_BLOCK) & (c0 + CMP_BLOCK > s0) & (np.arange(ncp)[:, None] < n_cmp) & (np.arange(nsp)[None, :] < n_slc)
    return jnp.asarray(m.astype(np.float32), dtype=BF16)


def _moe(h, p, *, tm, tf, hp):
    n, D = h.shape
    xn, logits = _row_call(_router_kernel, [h], [p["g_moe"], p["w_router"]], [D, LANES], [F32 if hp else BF16, F32],
                           _pick_tile(n), "moe_router")
    top_v, top_e = lax.top_k(logits[:, :N_EXPERTS], TOP_K)
    gates = jax.nn.softmax(top_v, -1)
    n_pairs = n * TOP_K
    flat_e = top_e.reshape(n_pairs)
    order = jnp.argsort(flat_e)
    seen = jnp.cumsum((flat_e[:, None] == jnp.arange(N_EXPERTS)).astype(jnp.int32), axis=0)
    counts = seen[-1]
    padded = (counts + tm - 1) // tm * tm
    pad_end = jnp.cumsum(padded)
    pad_start = pad_end - padded
    start = jnp.cumsum(counts) - counts
    rank = jnp.take_along_axis(seen, flat_e[:, None], axis=1)[:, 0] - 1
    dest = (pad_start[flat_e] + rank).reshape(n, TOP_K)
    n_blocks = -(-n_pairs // tm) + N_EXPERTS
    cap = n_blocks * tm
    block_e = jnp.minimum(jnp.searchsorted(pad_end, jnp.arange(n_blocks) * tm, side="right"), N_EXPERTS - 1).astype(jnp.int32)
    row_e = jnp.repeat(block_e, tm)
    k_in_e = jnp.arange(cap, dtype=jnp.int32) - pad_start[row_e]
    src = order[jnp.clip(start[row_e] + k_in_e, 0, n_pairs - 1)] // TOP_K
    buf_tok = jnp.where((k_in_e >= 0) & (k_in_e < counts[row_e]), src, n).astype(jnp.int32)
    n_used = (pad_end[-1] // tm).astype(jnp.int32).reshape(1)
    xs = jnp.concatenate([xn, jnp.zeros((1, D), xn.dtype)], 0)[buf_tok]
    ys = _moe_ffn(block_e, n_used, xs, p["moe_wg"], p["moe_wu"], p["moe_wd"], tm=tm, tf=tf, hp=hp)
    y = ys[dest[:, 0]] * gates[:, 0:1] + ys[dest[:, 1]] * gates[:, 1:2]
    return h + y


def kernel(x_prompt, x_sample, cache_cmp_kv, cache_slc_kv, state_win_kv, state_wkv, state_shift, page_table, norm_g, rw_mix, rw_vec, rw_w_rkv, rw_w_o, rw_decay_w1, rw_decay_w2, rw_iclr_w1, rw_iclr_w2, rw_gate_w1, rw_gate_w2, ffn_w_gate, ffn_w_up, ffn_w_down, moe_router, moe_w_gate, moe_w_up, moe_w_down, kv_norm_g, w_kv, k_norm_g, cmp_pe, cmp_w1, cmp_w2, w_qg, q_norm_g, w_o):
    B, T, D = x_prompt.shape
    SB, S, _ = x_sample.shape
    assert norm_g.shape[0] == 2 and rw_mix.shape[0] == 1 and w_qg.shape[0] == 1, "one RWKV-7 layer + one NSA layer"
    assert D == N_HEADS * HEAD_DIM and T % 512 == 0 and S <= SUBLANES
    page = cache_cmp_kv.shape[1]
    past = page_table.shape[1] * page
    wb = state_win_kv.shape[1]
    assert past % SLC_BLOCK == 0 and page % SLC_BLOCK == 0 and S <= SLC_BLOCK
    bf = lambda w: w.astype(BF16)
    pad8 = lambda m: jnp.pad(m, ((0, SUBLANES - m.shape[0]), (0, 0)))
    dff = ffn_w_gate.shape[2]
    tf = dff // 2 if (dff // 2) % LANES == 0 else dff
    n_p, n_s = B * T, SB * S
    both = lambda w: (bf(w), w)
    pick = lambda d, hp: {k: (v[hp] if isinstance(v, tuple) else v) for k, v in d.items()}

    rw = dict(g=norm_g[0, 0][None], mix=pad8(rw_mix[0]), vec=pad8(rw_vec[0]), wrkv=both(rw_w_rkv[0]), wo=both(rw_w_o[0]),
              dw1=both(rw_decay_w1[0]), dw2=both(rw_decay_w2[0]), aw1=both(rw_iclr_w1[0]), aw2=both(rw_iclr_w2[0]),
              gw1=both(rw_gate_w1[0]), gw2=both(rw_gate_w2[0]))
    H = D // RW_HEAD_DIM
    hp_, shift_p, wkv_p = _rwkv_layer(x_prompt, jnp.zeros((B, D), F32), jnp.zeros((B, H, RW_HEAD_DIM, RW_HEAD_DIM), F32),
                                      pick(rw, 0), tm=256, hp=False, inv_hp=False)
    hs, shift_s, wkv_s = _rwkv_layer(x_sample, state_shift[0], state_wkv[0], pick(rw, 1), tm=WKV_CHUNK, hp=True, inv_hp=True)
    ffn_g = norm_g[0, 1][None]
    hp_ = _ffn(hp_.reshape(n_p, D), ffn_g, bf(ffn_w_gate[0]), bf(ffn_w_up[0]), bf(ffn_w_down[0]), tm=512, tf=tf, hp=False)
    hs = _ffn(hs.reshape(n_s, D), ffn_g, ffn_w_gate[0], ffn_w_up[0], ffn_w_down[0], tm=n_s, tf=tf, hp=True)

    kv_consts = lambda hp: [kv_norm_g[None], both(w_kv)[hp], jnp.tile(k_norm_g, (1, N_KV_HEADS)),
                            _block_diag(KV_W, HEAD_DIM, 1.0 / HEAD_DIM)]
    place_k = np.zeros((KV_W, N_KV_HEADS * 4 * HEAD_DIM), np.float32)
    place_v = np.zeros((KV_W, N_KV_HEADS * 2 * HEAD_DIM), np.float32)
    ones_v = np.zeros((1, N_KV_HEADS * 2 * HEAD_DIM), np.float32)
    for g in range(N_KV_HEADS):
        for d in range(HEAD_DIM):
            place_k[g * HEAD_DIM + d, g * 4 * HEAD_DIM + d] = 1.0
            place_k[g * HEAD_DIM + d, g * 4 * HEAD_DIM + 3 * HEAD_DIM + d] = 1.0
            place_v[g * HEAD_DIM + d, g * 2 * HEAD_DIM + d] = 1.0
        ones_v[0, g * 2 * HEAD_DIM + HEAD_DIM:(g + 1) * 2 * HEAD_DIM] = 1.0
    pair_consts = [jnp.asarray(place_k, dtype=BF16), jnp.asarray(place_v, dtype=BF16), jnp.asarray(ones_v)]
    as_rows = lambda t, b: t.reshape(b, -1, 2, N_KV_HEADS, HEAD_DIM)
    kv_p = _row_call(functools.partial(_kv_proj_kernel, hp=False, pair_out=True), [hp_], kv_consts(False) + pair_consts,
                     [2 * KV_W] * 3 + [4 * KV_W, 2 * KV_W] * 2, [F32] * 3 + [BF16] * 4, 512, "kv_proj")
    cmp_p, slc_p, win_p = (as_rows(t, B) for t in kv_p[:3])
    skab, sv1, wkab, wv1 = (t.reshape(B, T, -1) for t in kv_p[3:])
    cmp_s, slc_s, win_s = (as_rows(t, SB) for t in _row_call(
        functools.partial(_kv_proj_kernel, hp=True, pair_out=False), [hs], kv_consts(True), [2 * KV_W] * 3, [F32] * 3,
        n_s, "kv_proj"))
    win_all_s = jnp.concatenate([state_win_kv, win_s], axis=1)

    cp = dict(cmp_pe=cmp_pe.reshape(2, CMP_BLOCK // CMP_STRIDE, 1, CMP_STRIDE * HEAD_DIM),
              cmp_w1=both(cmp_w1.reshape(2, CMP_BLOCK // CMP_STRIDE, CMP_STRIDE * HEAD_DIM, HEAD_DIM)),
              cmp_w2=both(cmp_w2), k_norm_g=k_norm_g)
    kc_p, vc_p, ncmp_p = _compressed_kv(cmp_p, pick(cp, 0), hp=False)
    assert S < CMP_STRIDE and page % CMP_STRIDE == 0
    n_ratio = CMP_BLOCK // CMP_STRIDE
    w1r = cmp_w1.reshape(2, n_ratio, CMP_STRIDE, HEAD_DIM, HEAD_DIM)
    w2x = jnp.einsum("cjsdh,ef->csedjfh", w1r, jnp.eye(2, dtype=F32)).reshape(2, CMP_STRIDE * LANES, n_ratio * LANES)
    n_pages = page_table.shape[1]
    pages_per_step = next(pp for pp in (32, 16, 8, 4, 2, 1) if n_pages % pp == 0)
    part_s = _paged_sub_proj(page_table, cache_cmp_kv.reshape(-1, page, 2 * KV_W), w2x, pages_per_step=pages_per_step, hp=True)
    kcv_s = _compress_tail(part_s, cp["cmp_pe"], cp["cmp_w1"][1], cmp_w2, k_norm_g[0:1], hp=True)
    kc_s, vc_s, ncmp_s = kcv_s[:, 0], kcv_s[:, 1], past // CMP_STRIDE - n_ratio + 1

    ge = np.zeros((LANES, N_BRANCH * D), np.float32)
    for br in range(N_BRANCH):
        for hd in range(N_HEADS):
            ge[br * N_HEADS + hd, br * D + hd * HEAD_DIM: br * D + (hd + 1) * HEAD_DIM] = 1.0
    nsa = dict(g_attn=norm_g[1, 0][None], wq=both(w_qg[0, :, :D]),
               wgate=both(jnp.pad(w_qg[0, :, D:], ((0, 0), (0, LANES - N_BRANCH * N_HEADS)))),
               q_norm_g=jnp.tile(q_norm_g[0], N_HEADS)[None], bd_q=_block_diag(D, HEAD_DIM, 1.0 / HEAD_DIM),
               w_o=both(w_o[0]), gate_expand=jnp.asarray(ge, dtype=BF16))

    def query_side(h, tm, hp, q_scale):
        c = pick(nsa, hp)
        return _row_call(functools.partial(_q_proj_kernel, hp=hp, q_scale=q_scale), [h],
                         [c["g_attn"], c["wq"], c["wgate"], c["q_norm_g"], c["bd_q"]], [D, LANES],
                         [F32 if hp else BF16, F32], tm, "q_proj")

    def merge(h, gates, o_cmp, o_slc, o_win, tm, hp):
        c = pick(nsa, hp)
        return _row_call(functools.partial(_merge_kernel, hp=hp), [h, gates, o_cmp, o_slc, o_win],
                         [c["gate_expand"], c["w_o"]], [D], [F32], tm, "nsa_merge")[0]

    q_p, gates_p = query_side(hp_, 512, False, ATTN_SCALE * LOG2E)
    q_p = q_p.reshape(B, T, D)
    n_slc_p = T // SLC_BLOCK
    nsp_p = LANES // N_KV_HEADS
    assert n_slc_p <= nsp_p, "the packed block-selection layout holds at most 32 selection blocks per kv group"
    m_one = _overlap_matrix(kc_p.shape[1], ncmp_p, n_slc_p, nsp_p)
    m_p = jnp.stack([jnp.pad(m_one, ((0, 0), (g * nsp_p, LANES - (g + 1) * nsp_p))) for g in range(N_KV_HEADS)])
    heads = lambda t: t.reshape(t.shape[0], t.shape[1], N_KV_HEADS, HEAD_DIM)
    kcab, vc1 = _pair_keys(heads(kc_p)), _ones_values(heads(vc_p))

    def prompt_branches(bounded):
        o_cmp, sel = _cmp_sel(q_p, kcab, vc1, m_p, n_cmp=ncmp_p, n_slc=n_slc_p, tq=256, bounded=bounded)
        return (o_cmp,) + tuple(_slc_win_attn(q_p, sel, skab, sv1, wkab, wv1, tq=256, tk=256, bounded=bounded))

    score_bound = HEAD_DIM * ATTN_SCALE * jnp.max(jnp.abs(q_norm_g[0])) * jnp.max(jnp.abs(k_norm_g))
    o_cmp_p, o_slc_p, o_win_p = lax.cond(score_bound <= SCORE_BOUND_MAX, lambda: prompt_branches(True),
                                         lambda: prompt_branches(False))
    hp_ = merge(hp_, gates_p, o_cmp_p.reshape(n_p, D), o_slc_p.reshape(n_p, D), o_win_p.reshape(n_p, D), 512, False)

    kv_lanes = lambda t, dt: t.reshape(t.shape[0], t.shape[1], 2 * KV_W).astype(dt)
    q_s, gates_s = query_side(hs, n_s, True, ATTN_SCALE)
    q_s = _pad_rows(q_s.reshape(SB, S, D), SUBLANES)
    n_slc_s = (past + S - 1) // SLC_BLOCK + 1
    nsp_s = -(-n_slc_s // LANES) * LANES
    m_s = _overlap_matrix(kc_s.shape[1], ncmp_s, n_slc_s, nsp_s)
    o_cmp_s, sel_s = _cmp_attn(q_s, kc_s, vc_s, m_s, n_cmp=ncmp_s, n_slc=n_slc_s, pos_base=past, per_tile=False,
                               tq=SUBLANES, hp=True)
    n_sel = min(SLC_TOP, n_slc_s)
    sel_s = sel_s.reshape(SB, SUBLANES, N_KV_HEADS, nsp_s)[:, :S, :, :n_slc_s] > 0.5
    slot = jnp.cumsum(sel_s, axis=-1) - 1
    idx_s = jnp.sum(jnp.where(sel_s[..., None] & (slot[..., None] == jnp.arange(n_sel)),
                              jnp.arange(n_slc_s)[:, None], 0), axis=-2).astype(jnp.int32)
    in_pool = idx_s < past // SLC_BLOCK
    per_page = page // SLC_BLOCK
    blk_c = jnp.minimum(idx_s, past // SLC_BLOCK - 1)
    pages = jnp.take_along_axis(page_table, (blk_c // per_page).reshape(SB, -1), axis=1).reshape(idx_s.shape)
    pool_blk = jnp.where(in_pool, pages * per_page + blk_c % per_page, -1).astype(jnp.int32)
    new_rows = slc_s[:, jnp.clip(jnp.arange(SLC_BLOCK), 0, S - 1)]
    new_win = jnp.transpose(new_rows, (0, 2, 3, 1, 4)).reshape(SB, 2 * N_KV_HEADS, SLC_BLOCK, HEAD_DIM)
    kpos = (idx_s[..., None] * SLC_BLOCK + jnp.arange(SLC_BLOCK)).reshape(SB, S * N_KV_HEADS, n_sel * SLC_BLOCK).astype(jnp.int32)
    kpos = jnp.pad(kpos, ((0, 0), (0, -(-S * N_KV_HEADS // SUBLANES) * SUBLANES - S * N_KV_HEADS), (0, 0)))
    cache_rows = cache_slc_kv.reshape(cache_slc_kv.shape[0], page, 2 * N_KV_HEADS, HEAD_DIM)
    n_win = wb + S
    nwp = -(-n_win // SUBLANES) * SUBLANES
    win_lanes = _pad_rows(kv_lanes(win_all_s, F32), nwp)
    o_slc_s, o_win_s = _sample_attn(pool_blk, q_s, kpos, new_win, win_lanes, cache_rows, n_q=S, n_sel=n_sel,
                                    pos_base=past, win_base=past - wb, n_win=n_win, hp=True)
    unpad = lambda t: t[:, :S].reshape(n_s, D)
    hs = merge(hs, gates_s, unpad(o_cmp_s), unpad(o_slc_s), unpad(o_win_s), n_s, True)

    moe = dict(g_moe=norm_g[1, 1][None], w_router=jnp.pad(moe_router[0], ((0, 0), (0, LANES - N_EXPERTS))),
               moe_wg=both(moe_w_gate[0]), moe_wu=both(moe_w_up[0]), moe_wd=both(moe_w_down[0]))
    hp_ = _moe(hp_, pick(moe, 0), tm=512, tf=tf, hp=False).reshape(B, T, D)
    hs = _moe(hs, pick(moe, 1), tm=128, tf=tf, hp=True).reshape(SB, S, D)

    keep_p = min(WINDOW, T)
    return (hp_, hs, cmp_p, cmp_s, slc_p, slc_s, win_p[:, T - keep_p:], win_all_s[:, win_all_s.shape[1] - wb:],
            wkv_p[None], wkv_s[None], shift_p[None], shift_s[None])
```

```python
import functools

import numpy as np
import jax
import jax.numpy as jnp
from jax import lax
from jax.experimental import pallas as pl
from jax.experimental.pallas import tpu as pltpu

F32 = jnp.float32
BF16 = jnp.bfloat16

RW_HEAD_DIM = 64
RW_GN_EPS = 64e-5
N_HEADS = 16
HEAD_DIM = 64
N_KV_HEADS = 4
HEADS_PER_KV = N_HEADS // N_KV_HEADS
KV_W = N_KV_HEADS * HEAD_DIM
N_BRANCH = 3
CMP_BLOCK = 32
CMP_STRIDE = 16
SLC_BLOCK = 64
SLC_TOP = 16
WINDOW = 512
ATTN_SCALE = HEAD_DIM ** -0.5
N_EXPERTS = 8
TOP_K = 2
RMS_EPS = 1e-6
MASKED = -1e30
M_INIT = -1e20
BIG = 1e30
LOG2E = 1.4426950408889634
SCORE_BOUND_MAX = 40.0

SUBLANES = 8
LANES = 128
VMEM_LIMIT = 56 * 1024 * 1024
WKV_CHUNK = RW_HEAD_DIM


def _cparams(sem):
    return pltpu.CompilerParams(dimension_semantics=sem, vmem_limit_bytes=VMEM_LIMIT)


_DIMS = {"nn": (((1,), (0,)), ((), ())), "nt": (((1,), (1,)), ((), ())), "tn": (((0,), (0,)), ((), ()))}


def _split2(x):
    hi = x.astype(BF16)
    lo = (x - hi.astype(F32)).astype(BF16)
    return hi, lo


def _mm(a, b, form="nn", hp=False):
    d = lambda s, t: lax.dot_general(s, t, _DIMS[form], preferred_element_type=F32)
    if not hp:
        return d(a.astype(BF16), b.astype(BF16))
    ah, al = _split2(a.astype(F32))
    bh, bl = _split2(b.astype(F32))
    return d(ah, bh) + d(ah, bl) + d(al, bh)


def _dot_hilo(x, m):
    hi, lo = _split2(x)
    return jnp.dot(hi, m, preferred_element_type=F32) + jnp.dot(lo, m, preferred_element_type=F32)


def _dot_exact_rhs(m, x):
    hi = x.astype(BF16)
    r1 = x - hi.astype(F32)
    mid = r1.astype(BF16)
    lo = (r1 - mid.astype(F32)).astype(BF16)
    d = lambda t: jnp.dot(m, t, preferred_element_type=F32)
    return d(hi) + d(mid) + d(lo)


def _rms(x, g):
    return x * lax.rsqrt(jnp.mean(x * x, -1, keepdims=True) + RMS_EPS) * g


def _silu(x):
    return x * jax.nn.sigmoid(x)


def _softplus(z):
    return jnp.maximum(z, 0.0) + jnp.log(1.0 + jnp.exp(-jnp.abs(z)))


def _block_diag(n, seg, value):
    i = np.arange(n)
    return jnp.asarray((i[:, None] // seg == i[None, :] // seg).astype(np.float32) * value, dtype=BF16)


def _rwkv_proj_kernel(x_ref, halo_ref, sh_ref, g_ref, mix_ref, vec_ref, wrkv_ref, dw1_ref, dw2_ref, aw1_ref, aw2_ref,
                      gw1_ref, gw2_ref, r_o, lw_o, k_o, v_o, a_o, g_o, shift_o, *, last_tile, last_row, hp):
    i = pl.program_id(1)
    mm = functools.partial(_mm, hp=hp)
    g = g_ref[...]
    xn = _rms(x_ref[0], g)
    hn = _rms(halo_ref[0, SUBLANES - 1:SUBLANES, :], g)
    prev_last = jnp.where(i == 0, sh_ref[0], hn)
    row = lax.broadcasted_iota(jnp.int32, xn.shape, 0)
    prev = jnp.where(row == 0, prev_last, pltpu.roll(xn, 1, 0))
    xx = prev - xn
    mixed = lambda j: xn + xx * mix_ref[j:j + 1, :]
    r_o[0] = mm(mixed(0), wrkv_ref[0])
    k_o[0] = mm(mixed(2), wrkv_ref[1])
    v_o[0] = mm(mixed(3), wrkv_ref[2])
    w0 = vec_ref[0:1, :]
    a0 = vec_ref[1:2, :]
    log_w = -_softplus(-(w0 + mm(jnp.tanh(mm(mixed(1), dw1_ref[...])), dw2_ref[...]))) - 0.5
    lw_o[0] = -jnp.exp(log_w)
    a_o[0] = jax.nn.sigmoid(a0 + mm(mm(mixed(4), aw1_ref[...]), aw2_ref[...]))
    g_o[0] = mm(jax.nn.sigmoid(mm(mixed(5), gw1_ref[...])), gw2_ref[...])

    @pl.when(i == last_tile)
    def _():
        shift_o[0] = xn[last_row:last_row + 1, :]


def _rwkv_proj(x, shift0, g, mix, vec, wrkv, dw1, dw2, aw1, aw2, gw1, gw2, *, t_real, tm, hp):
    B, T, D = x.shape
    nt = T // tm
    row_spec = pl.BlockSpec((1, tm, D), lambda b, i: (b, i, 0))
    halo_spec = pl.BlockSpec((1, SUBLANES, D), lambda b, i: (b, jnp.maximum(i * (tm // SUBLANES) - 1, 0), 0))
    vec_spec = pl.BlockSpec((1, 1, D), lambda b, i: (b, 0, 0))
    full = lambda a: pl.BlockSpec(a.shape, lambda b, i: (0,) * a.ndim)
    consts = (g, mix, vec, wrkv, dw1, dw2, aw1, aw2, gw1, gw2)
    out_sd = jax.ShapeDtypeStruct((B, T, D), F32)
    kern = functools.partial(_rwkv_proj_kernel, last_tile=(t_real - 1) // tm, last_row=(t_real - 1) % tm, hp=hp)
    return pl.pallas_call(
        kern, grid=(B, nt),
        in_specs=[row_spec, halo_spec, vec_spec] + [full(c) for c in consts],
        out_specs=[row_spec] * 6 + [vec_spec],
        out_shape=[out_sd] * 6 + [jax.ShapeDtypeStruct((B, 1, D), F32)],
        compiler_params=_cparams(("parallel", "arbitrary")), name="rwkv_proj",
    )(x, x, shift0, *consts)


def _wkv_kernel(r_ref, lw_ref, k_ref, v_ref, a_ref, vec_ref, ones_ref, s0_ref, y_o, sT_o, s_sc, *, n_pairs, hp, inv_hp):
    c = pl.program_id(1)
    C = r_ref.shape[1]
    W = 2 * C
    mm = functools.partial(_mm, hp=hp)
    mm_inv = functools.partial(_mm, hp=inv_hp)

    @pl.when(c == 0)
    def _():
        s_sc[...] = s0_ref[0]

    lane = lax.broadcasted_iota(jnp.int32, (C, W), 1)
    row = lax.broadcasted_iota(jnp.int32, (C, W), 0)
    left = lane < C
    col = jnp.where(left, lane, lane - C)
    strict = col < row
    incl = col <= row
    diag_blocks = (lax.broadcasted_iota(jnp.int32, (W, W), 0) < C) == (lax.broadcasted_iota(jnp.int32, (W, W), 1) < C)
    bd = lambda x: jnp.concatenate([jnp.where(left, x, 0.0), jnp.where(left, 0.0, x)], axis=0)
    seg_sum = lambda x: _dot_hilo(x, ones_ref[...])
    rows = lambda x, p: x[p * C:(p + 1) * C]
    tri = jnp.where(lax.broadcasted_iota(jnp.int32, (C, C), 1) <= lax.broadcasted_iota(jnp.int32, (C, C), 0), 1.0, 0.0)
    cum_all = _dot_exact_rhs(tri.astype(BF16), lw_ref[0])
    pairs = range(n_pairs)
    ps = lambda p: slice(p * W, (p + 1) * W)

    r = [r_ref[0, :, ps(p)] for p in pairs]
    v = [v_ref[0, :, ps(p)] for p in pairs]
    kkr = [k_ref[0, :, ps(p)] * vec_ref[2:3, ps(p)] for p in pairs]
    ss = seg_sum(jnp.concatenate([x * x for x in kkr], axis=0))
    k2, at, rt, bt, kt, w_end = [], [], [], [], [], []
    for p in pairs:
        a = a_ref[0, :, ps(p)]
        cum = cum_all[:, ps(p)]
        kk = kkr[p] * lax.rsqrt(jnp.maximum(rows(ss, p), 1e-24))
        k2.append(k_ref[0, :, ps(p)] * (1.0 + (a - 1.0) * vec_ref[3:4, ps(p)]))
        e_pos = jnp.exp(cum)
        e_neg = jnp.exp(-cum)
        rt.append(r[p] * e_pos)
        at.append(-kk * jnp.exp(cum - lw_ref[0, :, ps(p)]))
        bt.append(kk * a * e_neg)
        kt.append(k2[p] * e_neg)
        w_end.append(e_pos[C - 1:C, :])

    P, U, m_rbk, x_r = [], [], [], []
    for p in pairs:
        ar = jnp.concatenate([at[p], rt[p]], axis=0)
        big = mm(ar, jnp.concatenate([bd(bt[p]), bd(kt[p])], axis=0), "nt")
        x = mm(ar, s_sc[p], "nt")
        P.append(jnp.where(strict, big[:C, :W], 0.0))
        l_ak = jnp.where(strict, big[:C, W:], 0.0)
        m_rbk.append(jnp.concatenate([jnp.where(incl, big[C:, :W], 0.0), jnp.where(incl, big[C:, W:], 0.0)], axis=1))
        U.append(x[:C] + mm(l_ak, bd(v[p])))
        x_r.append(x[C:])

    n_it = int(np.log2(C))
    for it in range(n_it):
        for p in pairs:
            if it == n_it - 1:
                U[p] = U[p] + mm_inv(P[p], bd(U[p]))
            else:
                res = mm_inv(P[p], jnp.concatenate([bd(P[p]), bd(U[p])], axis=1))
                U[p] = U[p] + res[:, W:]
                P[p] = res[:, :W]

    y = []
    for p in pairs:
        y.append(x_r[p] + mm(m_rbk[p], jnp.concatenate([bd(U[p]), bd(v[p])], axis=0)))
        upd = mm(jnp.concatenate([U[p], v[p]], axis=0),
                 jnp.concatenate([bt[p] * w_end[p], kt[p] * w_end[p]], axis=0), "tn")
        s_sc[p] = s_sc[p] * w_end[p] + jnp.where(diag_blocks, upd, 0.0)

    inv_n = 1.0 / RW_HEAD_DIM
    y_all = jnp.concatenate(y, axis=0)
    mu = seg_sum(y_all) * inv_n
    var = seg_sum(jnp.square(y_all - mu)) * inv_n
    rk = seg_sum(jnp.concatenate([r[p] * k2[p] * vec_ref[4:5, ps(p)] for p in pairs], axis=0))
    yn = (y_all - mu) * lax.rsqrt(var + RW_GN_EPS)
    for p in pairs:
        y_o[0, :, ps(p)] = rows(yn, p) * vec_ref[5:6, ps(p)] + vec_ref[6:7, ps(p)] + rows(rk, p) * v[p]

    @pl.when(c == pl.num_programs(1) - 1)
    def _():
        sT_o[0] = s_sc[...]


def _wkv(r, lw, k, v, a, vec, s0, *, hp, inv_hp):
    B, T, D = r.shape
    H = D // RW_HEAD_DIM
    C, W, n_pairs = WKV_CHUNK, 2 * RW_HEAD_DIM, H // 2
    s_bd = jnp.zeros((B, n_pairs, W, W), F32)
    s_bd = s_bd.at[:, :, :C, :C].set(s0[:, 0::2]).at[:, :, C:, C:].set(s0[:, 1::2])
    row_spec = pl.BlockSpec((1, C, D), lambda b, c: (b, c, 0))
    st_spec = pl.BlockSpec((1, n_pairs, W, W), lambda b, c: (b, 0, 0, 0))
    ones = _block_diag(W, RW_HEAD_DIM, 1.0)
    kern = functools.partial(_wkv_kernel, n_pairs=n_pairs, hp=hp, inv_hp=inv_hp)
    y, s_out = pl.pallas_call(
        kern, grid=(B, T // C),
        in_specs=[row_spec] * 5 + [pl.BlockSpec(vec.shape, lambda b, c: (0, 0)), pl.BlockSpec((W, W), lambda b, c: (0, 0)),
                                   st_spec],
        out_specs=[row_spec, st_spec],
        out_shape=[jax.ShapeDtypeStruct((B, T, D), F32), jax.ShapeDtypeStruct(s_bd.shape, F32)],
        scratch_shapes=[pltpu.VMEM((n_pairs, W, W), F32)],
        compiler_params=_cparams(("parallel", "arbitrary")), name="wkv_chunk",
    )(r, lw, k, v, a, vec, ones, s_bd)
    s_fin = jnp.stack([s_out[:, :, :C, :C], s_out[:, :, C:, C:]], axis=2).reshape(B, H, RW_HEAD_DIM, RW_HEAD_DIM)
    return y, s_fin


def _row_call(kernel, rows, consts, out_widths, out_dtypes, tm, name):
    n = rows[0].shape[0]
    row_spec = lambda w: pl.BlockSpec((tm, w), lambda i: (i, 0))
    full = lambda a: pl.BlockSpec(a.shape, lambda i: (0,) * a.ndim)
    return pl.pallas_call(
        kernel, grid=(n // tm,),
        in_specs=[row_spec(a.shape[1]) for a in rows] + [full(c) for c in consts],
        out_specs=[row_spec(w) for w in out_widths],
        out_shape=[jax.ShapeDtypeStruct((n, w), dt) for w, dt in zip(out_widths, out_dtypes)],
        compiler_params=_cparams(("parallel",)), name=name,
    )(*rows, *consts)


def _rwkv_out_kernel(x_ref, y_ref, g_ref, wo_ref, o_ref, *, hp):
    o_ref[...] = x_ref[...] + _mm(y_ref[...] * g_ref[...], wo_ref[...], hp=hp)


def _kv_proj_kernel(h_ref, g_ref, wkv_ref, kg_ref, bd_ref, *rest, hp, pair_out):
    if pair_out:
        pk_ref, pv_ref, ones_ref, cmp_o, slc_o, win_o = rest[:6]
        pair_outs = {1: rest[6:8], 2: rest[8:10]}
    else:
        cmp_o, slc_o, win_o = rest
    kv = _mm(_rms(h_ref[...], g_ref[...]), wkv_ref[...], hp=hp)
    cmp_o[...] = kv[:, :2 * KV_W]
    for br, out in ((1, slc_o), (2, win_o)):
        kraw = kv[:, br * 2 * KV_W: br * 2 * KV_W + KV_W]
        ms = _dot_hilo(kraw * kraw, bd_ref[...])
        kn = kraw * lax.rsqrt(ms + RMS_EPS) * kg_ref[br:br + 1, :]
        v = kv[:, br * 2 * KV_W + KV_W: (br + 1) * 2 * KV_W]
        out[:, :KV_W] = kn
        out[:, KV_W:] = v
        if pair_out:
            kab_o, v1_o = pair_outs[br]
            kab_o[...] = jnp.dot(kn.astype(BF16), pk_ref[...], preferred_element_type=F32).astype(BF16)
            v1_o[...] = (jnp.dot(v.astype(BF16), pv_ref[...], preferred_element_type=F32) + ones_ref[...]).astype(BF16)


def _q_proj_kernel(h_ref, g_ref, wq_ref, wg_ref, qg_ref, bd_ref, q_o, gate_o, *, hp, q_scale):
    hn = _rms(h_ref[...], g_ref[...])
    q = _mm(hn, wq_ref[...], hp=hp)
    ms = _dot_hilo(q * q, bd_ref[...])
    q_o[...] = (q * lax.rsqrt(ms + RMS_EPS) * qg_ref[...] * q_scale).astype(q_o.dtype)
    gate_o[...] = jax.nn.sigmoid(_mm(hn, wg_ref[...], hp=hp))


def _merge_kernel(h_ref, gate_ref, oc_ref, os_ref, ow_ref, eg_ref, wo_ref, o_ref, *, hp):
    D = h_ref.shape[1]
    ge = _dot_hilo(gate_ref[...], eg_ref[...])
    o = ge[:, :D] * oc_ref[...] + ge[:, D:2 * D] * os_ref[...] + ge[:, 2 * D:] * ow_ref[...]
    o_ref[...] = h_ref[...] + _mm(o, wo_ref[...], hp=hp)


def _router_kernel(h_ref, g_ref, wr_ref, xn_o, logit_o):
    xn = _rms(h_ref[...], g_ref[...])
    xn_o[...] = xn.astype(xn_o.dtype)
    logit_o[...] = _mm(xn, wr_ref[...], hp=True)


def _ffn_kernel(x_ref, g_ref, wg_ref, wu_ref, wd_ref, o_ref, xn_sc, acc_sc, *, hp):
    f = pl.program_id(1)

    @pl.when(f == 0)
    def _():
        xn_sc[...] = _rms(x_ref[...], g_ref[...]).astype(xn_sc.dtype)
        acc_sc[...] = jnp.zeros_like(acc_sc)

    xn = xn_sc[...]
    hid = _silu(_mm(xn, wg_ref[...], hp=hp)) * _mm(xn, wu_ref[...], hp=hp)
    acc_sc[...] += _mm(hid, wd_ref[...], hp=hp)

    @pl.when(f == pl.num_programs(1) - 1)
    def _():
        o_ref[...] = x_ref[...] + acc_sc[...]


def _ffn(x, g, wg, wu, wd, *, tm, tf, hp):
    n, D = x.shape
    dff = wg.shape[1]
    return pl.pallas_call(
        functools.partial(_ffn_kernel, hp=hp), grid=(n // tm, dff // tf),
        in_specs=[pl.BlockSpec((tm, D), lambda i, f: (i, 0)), pl.BlockSpec((1, D), lambda i, f: (0, 0)),
                  pl.BlockSpec((D, tf), lambda i, f: (0, f)), pl.BlockSpec((D, tf), lambda i, f: (0, f)),
                  pl.BlockSpec((tf, D), lambda i, f: (f, 0))],
        out_specs=pl.BlockSpec((tm, D), lambda i, f: (i, 0)),
        out_shape=jax.ShapeDtypeStruct((n, D), F32),
        scratch_shapes=[pltpu.VMEM((tm, D), F32 if hp else BF16), pltpu.VMEM((tm, D), F32)],
        compiler_params=_cparams(("parallel", "arbitrary")), name="ffn_dense",
    )(x, g, wg, wu, wd)


def _moe_ffn_kernel(be_ref, nb_ref, x_ref, wg_ref, wu_ref, wd_ref, o_ref, acc_sc, *, hp):
    i = pl.program_id(0)
    f = pl.program_id(1)

    @pl.when(i < nb_ref[0])
    def _():
        @pl.when(f == 0)
        def _():
            acc_sc[...] = jnp.zeros_like(acc_sc)

        x = x_ref[...]
        hid = _silu(_mm(x, wg_ref[0], hp=hp)) * _mm(x, wu_ref[0], hp=hp)
        acc_sc[...] += _mm(hid, wd_ref[0], hp=hp)

        @pl.when(f == pl.num_programs(1) - 1)
        def _():
            o_ref[...] = acc_sc[...]

    @pl.when((i >= nb_ref[0]) & (f == pl.num_programs(1) - 1))
    def _():
        o_ref[...] = jnp.zeros_like(o_ref)


def _moe_ffn(block_e, n_used, xs, wg, wu, wd, *, tm, tf, hp):
    cap, D = xs.shape
    dff = wg.shape[2]
    w_idx = lambda i, f, be, nb: jnp.where(i < nb[0], f, dff // tf - 1)
    grid_spec = pltpu.PrefetchScalarGridSpec(
        num_scalar_prefetch=2, grid=(cap // tm, dff // tf),
        in_specs=[pl.BlockSpec((tm, D), lambda i, f, be, nb: (i, 0)),
                  pl.BlockSpec((1, D, tf), lambda i, f, be, nb: (be[i], 0, w_idx(i, f, be, nb))),
                  pl.BlockSpec((1, D, tf), lambda i, f, be, nb: (be[i], 0, w_idx(i, f, be, nb))),
                  pl.BlockSpec((1, tf, D), lambda i, f, be, nb: (be[i], w_idx(i, f, be, nb), 0))],
        out_specs=pl.BlockSpec((tm, D), lambda i, f, be, nb: (i, 0)),
        scratch_shapes=[pltpu.VMEM((tm, D), F32)])
    return pl.pallas_call(
        functools.partial(_moe_ffn_kernel, hp=hp), grid_spec=grid_spec, out_shape=jax.ShapeDtypeStruct((cap, D), F32),
        compiler_params=_cparams(("arbitrary", "arbitrary")), name="moe_ffn",
    )(block_e, n_used, xs, wg, wu, wd)


def _compress_kernel(x_ref, pe_ref, w1_ref, w2_ref, kg_ref, o_ref, *, hp):
    c = pl.program_id(1)
    x = x_ref[0, 0, 0]
    R = x.shape[0]
    p0 = _mm(x + pe_ref[0, 0], w1_ref[0, 0], hp=hp)
    p1 = _mm(x + pe_ref[0, 1], w1_ref[0, 1], hp=hp)
    pre = p0 + pltpu.roll(p1, R - 1, 0)
    out = _mm(_silu(pre), w2_ref[0], hp=hp)
    o_ref[0, 0, 0] = jnp.where(c == 0, _rms(out, kg_ref[...]), out)


def _compress(xsub, pe, w1, w2, kg, *, hp):
    S, _, G, R, W = xsub.shape
    return pl.pallas_call(
        functools.partial(_compress_kernel, hp=hp), grid=(S, 2, G),
        in_specs=[pl.BlockSpec((1, 1, 1, R, W), lambda s, c, g: (s, c, g, 0, 0)),
                  pl.BlockSpec((1, 2, 1, W), lambda s, c, g: (c, 0, 0, 0)),
                  pl.BlockSpec((1, 2, W, HEAD_DIM), lambda s, c, g: (c, 0, 0, 0)),
                  pl.BlockSpec((1, HEAD_DIM, HEAD_DIM), lambda s, c, g: (c, 0, 0)),
                  pl.BlockSpec((1, HEAD_DIM), lambda s, c, g: (0, 0))],
        out_specs=pl.BlockSpec((1, 1, 1, R, HEAD_DIM), lambda s, c, g: (s, c, g, 0, 0)),
        out_shape=jax.ShapeDtypeStruct((S, 2, G, R, HEAD_DIM), F32),
        compiler_params=_cparams(("parallel", "arbitrary", "arbitrary")), name="kv_compress",
    )(xsub, pe, w1, w2, kg)


def _paged_sub_proj_kernel(pt_ref, cache_hbm, w_ref, o_ref, buf, sem, *, pages_per_step, page, hp):
    lin = pl.program_id(0) * pl.num_programs(1) + pl.program_id(1)
    total = pl.num_programs(0) * pl.num_programs(1)
    slot = lin % 2

    n_lane_pairs = buf.shape[1]

    def page_copies(step, into):
        return [pltpu.make_async_copy(cache_hbm.at[pt_ref[step * pages_per_step + p], pl.ds(0, page), pl.ds(pair * LANES, LANES)],
                                      buf.at[into, pair, pl.ds(p * page, page)], sem.at[into])
                for p in range(pages_per_step) for pair in range(n_lane_pairs)]

    @pl.when(lin == 0)
    def _():
        for cp in page_copies(0, 0):
            cp.start()

    @pl.when(lin + 1 < total)
    def _():
        for cp in page_copies(lin + 1, 1 - slot):
            cp.start()

    for cp in page_copies(lin, slot):
        cp.wait()
    n = pages_per_step * page // CMP_STRIDE
    for pair in range(n_lane_pairs):
        c = pair // (N_KV_HEADS // 2)
        rows = buf.at[slot, pair]
        x2 = jnp.concatenate([rows[pl.ds(s, n, stride=CMP_STRIDE), :] for s in range(CMP_STRIDE)], axis=1)
        o_ref[0, pair] = _mm(x2, w_ref[c], hp=hp and c == 0)


def _paged_sub_proj(page_table, cache, w2x, *, pages_per_step, hp):
    S, n_pages = page_table.shape
    page, width = cache.shape[1:]
    n = pages_per_step * page // CMP_STRIDE
    n_t = n_pages // pages_per_step
    grid_spec = pltpu.PrefetchScalarGridSpec(
        num_scalar_prefetch=1, grid=(S, n_t),
        in_specs=[pl.BlockSpec(memory_space=pl.ANY), pl.BlockSpec(w2x.shape, lambda s, t, pt: (0, 0, 0))],
        out_specs=pl.BlockSpec((1, 4, n, w2x.shape[2]), lambda s, t, pt: (s, 0, t, 0)),
        scratch_shapes=[pltpu.VMEM((2, width // LANES, pages_per_step * page, LANES), F32), pltpu.SemaphoreType.DMA((2,))])
    kern = functools.partial(_paged_sub_proj_kernel, pages_per_step=pages_per_step, page=page, hp=hp)
    return pl.pallas_call(
        kern, grid_spec=grid_spec, out_shape=jax.ShapeDtypeStruct((S, 4, n_t * n, w2x.shape[2]), F32),
        compiler_params=_cparams(("arbitrary", "arbitrary")), name="paged_sub_proj",
    )(page_table.reshape(-1), cache, w2x)


def _compress_tail_kernel(p_ref, pe_ref, w1_ref, w2_ref, kg_ref, o_ref, *, hp):
    R = p_ref.shape[2]
    half_groups = N_KV_HEADS // 2
    for c in range(2):
        bias = (_mm(jnp.broadcast_to(pe_ref[c, 0], (SUBLANES, pe_ref.shape[3])), w1_ref[c, 0], hp=True)
                + _mm(jnp.broadcast_to(pe_ref[c, 1], (SUBLANES, pe_ref.shape[3])), w1_ref[c, 1], hp=True))[0:1]
        outs = []
        for g in range(N_KV_HEADS):
            p = p_ref[0, c * half_groups + g // 2]
            e = g % 2
            p0 = p[:, e * HEAD_DIM:(e + 1) * HEAD_DIM]
            p1 = p[:, (2 + e) * HEAD_DIM:(3 + e) * HEAD_DIM]
            out = _mm(_silu(p0 + pltpu.roll(p1, R - 1, 0) + bias), w2_ref[c], hp=hp)
            outs.append(_rms(out, kg_ref[...]) if c == 0 else out)
        o_ref[0, c] = jnp.concatenate(outs, axis=1)


def _compress_tail(p, pe, w1, w2, kg, *, hp):
    S, _, R, _ = p.shape
    full = lambda a: pl.BlockSpec(a.shape, lambda s: (0,) * a.ndim)
    return pl.pallas_call(
        functools.partial(_compress_tail_kernel, hp=hp), grid=(S,),
        in_specs=[pl.BlockSpec((1,) + p.shape[1:], lambda s: (s, 0, 0, 0)), full(pe), full(w1), full(w2), full(kg)],
        out_specs=pl.BlockSpec((1, 2, R, KV_W), lambda s: (s, 0, 0, 0)),
        out_shape=jax.ShapeDtypeStruct((S, 2, R, KV_W), F32),
        compiler_params=_cparams(("parallel",)), name="kv_compress_tail",
    )(p, pe, w1, w2, kg)


def _stack_heads(q, g, tq):
    return jnp.concatenate([q[:, (HEADS_PER_KV * g + i) * HEAD_DIM:(HEADS_PER_KV * g + i + 1) * HEAD_DIM]
                            for i in range(HEADS_PER_KV)], axis=0)


def _cmp_attn_kernel(q_ref, kc_ref, vc_ref, m_ref, o_o, sel_o, *, n_cmp, n_slc, pos_base, per_tile, hp):
    tq = q_ref.shape[1]
    ncp = kc_ref.shape[1]
    nsp = m_ref.shape[1]
    q0 = pos_base + (pl.program_id(1) * tq if per_tile else 0)
    rows = HEADS_PER_KV * tq
    q_pos = q0 + lax.broadcasted_iota(jnp.int32, (rows, ncp), 0) % tq
    n_id = lax.broadcasted_iota(jnp.int32, (rows, ncp), 1)
    bias = jnp.where((n_id * CMP_STRIDE + CMP_BLOCK - 1 <= q_pos) & (n_id < n_cmp), 0.0, MASKED)
    blk = lax.broadcasted_iota(jnp.int32, (tq, nsp), 1)
    cur = (q0 + lax.broadcasted_iota(jnp.int32, (tq, nsp), 0)) // SLC_BLOCK
    forced = (blk == 0) | (blk == cur) | (blk == cur - 1)
    q = q_ref[0]
    scores = []
    for g in range(N_KV_HEADS):
        qs = _stack_heads(q, g, tq)
        s = _mm(qs, kc_ref[0, :, g * HEAD_DIM:(g + 1) * HEAD_DIM], "nt", hp) + bias
        m = jnp.maximum(jnp.max(s, -1, keepdims=True), M_INIT)
        e = jnp.exp(s - m)
        p = e / jnp.maximum(jnp.sum(e, -1, keepdims=True), 1e-30)
        o = _mm(p, vc_ref[0, :, g * HEAD_DIM:(g + 1) * HEAD_DIM], hp=hp)
        psum = p[0:tq]
        for i in range(HEADS_PER_KV):
            col = (HEADS_PER_KV * g + i) * HEAD_DIM
            o_o[0, :, col:col + HEAD_DIM] = o[i * tq:(i + 1) * tq]
            if i:
                psum = psum + p[i * tq:(i + 1) * tq]
        imp = _dot_hilo(psum, m_ref[...])
        scores.append(jnp.where(blk <= cur, jnp.where(forced, BIG, imp), -BIG))
    score = jnp.concatenate(scores, axis=0)
    blk_r = lax.broadcasted_iota(jnp.int32, score.shape, 1)

    def count_ahead(i, rank):
        s_i = jnp.sum(jnp.where(blk_r == i, score, 0.0), axis=-1, keepdims=True)
        return rank + jnp.where(s_i > score, 1.0, jnp.where((s_i == score) & (i < blk_r), 1.0, 0.0))

    rank = lax.fori_loop(0, n_slc, count_ahead, jnp.zeros(score.shape, F32), unroll=8)
    for g in range(N_KV_HEADS):
        sel_o[0, :, g * nsp:(g + 1) * nsp] = jnp.where(rank[g * tq:(g + 1) * tq] < min(SLC_TOP, n_slc), 1.0, 0.0).astype(sel_o.dtype)


def _cmp_attn(q, kc, vc, m, *, n_cmp, n_slc, pos_base, per_tile, tq, hp):
    B, T, D = q.shape
    ncp, nsp = m.shape
    kv_spec = pl.BlockSpec((1, ncp, KV_W), lambda b, i: (b, 0, 0))
    kern = functools.partial(_cmp_attn_kernel, n_cmp=n_cmp, n_slc=n_slc, pos_base=pos_base, per_tile=per_tile, hp=hp)
    return pl.pallas_call(
        kern, grid=(B, T // tq),
        in_specs=[pl.BlockSpec((1, tq, D), lambda b, i: (b, i, 0)), kv_spec, kv_spec,
                  pl.BlockSpec((ncp, nsp), lambda b, i: (0, 0))],
        out_specs=[pl.BlockSpec((1, tq, D), lambda b, i: (b, i, 0)),
                   pl.BlockSpec((1, tq, N_KV_HEADS * nsp), lambda b, i: (b, i, 0))],
        out_shape=[jax.ShapeDtypeStruct((B, T, D), F32), jax.ShapeDtypeStruct((B, T, N_KV_HEADS * nsp), BF16)],
        compiler_params=_cparams(("parallel", "arbitrary")), name="cmp_attn",
    )(q, kc, vc, m)


def _pair_rows(q, g):
    base = g * HEADS_PER_KV * HEAD_DIM
    return jnp.concatenate([q[:, base:base + LANES], q[:, base + LANES:base + 2 * LANES]], axis=0)


def _finish_pair(acc_even, acc_odd, p, tq):
    a0 = acc_even[p * tq:(p + 1) * tq]
    a1 = acc_odd[p * tq:(p + 1) * tq]
    o0 = a0 / jnp.maximum(a0[:, HEAD_DIM:HEAD_DIM + 1], 1e-30)
    o1 = a1 / jnp.maximum(a1[:, HEAD_DIM:HEAD_DIM + 1], 1e-30)
    lane = lax.broadcasted_iota(jnp.int32, (tq, LANES), 1)
    return jnp.where(lane < HEAD_DIM, o0, pltpu.roll(o1, HEAD_DIM, 1))


def _cmp_sel_kernel(q_ref, kab_ref, v1_ref, m_ref, o_o, sel_o, *, n_cmp, n_slc, bounded):
    tq = q_ref.shape[1]
    ncp = kab_ref.shape[1]
    nsp = LANES // N_KV_HEADS
    q0 = pl.program_id(1) * tq
    q_pos = q0 + lax.broadcasted_iota(jnp.int32, (tq, ncp), 0)
    n_id = lax.broadcasted_iota(jnp.int32, (tq, ncp), 1)
    b = jnp.where((n_id * CMP_STRIDE + CMP_BLOCK - 1 <= q_pos) & (n_id < n_cmp), 0.0, MASKED)
    b2 = jnp.concatenate([b, b], axis=0)
    q = q_ref[0]
    imp = jnp.zeros((tq, LANES), F32)
    for g in range(N_KV_HEADS):
        lhs = _pair_rows(q, g)
        vv = v1_ref[0, :, g * LANES:(g + 1) * LANES]
        accs, psum = [], None
        for half in range(2):
            s = _mm(lhs, kab_ref[0, :, (2 * g + half) * LANES:(2 * g + half + 1) * LANES], "nt") + b2
            e = jnp.exp2(s) if bounded else jnp.exp2(s - jnp.maximum(jnp.max(s, -1, keepdims=True), M_INIT))
            acc = _mm(e, vv)
            accs.append(acc)
            p = e / jnp.maximum(acc[:, HEAD_DIM:HEAD_DIM + 1], 1e-30)
            ph = p[:tq] + p[tq:]
            psum = ph if psum is None else psum + ph
        imp = imp + _dot_hilo(psum, m_ref[g])
        for p_ in range(2):
            col = g * 2 * LANES + p_ * LANES
            o_o[0, :, col:col + LANES] = _finish_pair(accs[0], accs[1], p_, tq)
    lane = lax.broadcasted_iota(jnp.int32, (tq, LANES), 1)
    blk = lane % nsp
    cur = (q0 + lax.broadcasted_iota(jnp.int32, (tq, LANES), 0)) // SLC_BLOCK
    forced = (blk == 0) | (blk == cur) | (blk == cur - 1)
    score = jnp.where(blk <= cur, jnp.where(forced, BIG, imp), -BIG)
    rank = jnp.zeros((tq, LANES), F32)
    for d in range(1, nsp):
        wrapped = blk + d >= nsp
        partner = jnp.where(wrapped, pltpu.roll(score, nsp - d, 1), pltpu.roll(score, LANES - d, 1))
        rank = rank + jnp.where(partner > score, 1.0, jnp.where((partner == score) & wrapped, 1.0, 0.0))
    sel_o[0] = jnp.where(rank < min(SLC_TOP, n_slc), 1.0, 0.0).astype(sel_o.dtype)


def _cmp_sel(q, kab, v1, m, *, n_cmp, n_slc, tq, bounded):
    B, T, D = q.shape
    ncp = kab.shape[1]
    assert m.shape == (N_KV_HEADS, ncp, LANES)
    whole = lambda a: pl.BlockSpec((1,) + a.shape[1:], lambda b, i: (b,) + (0,) * (a.ndim - 1))
    kern = functools.partial(_cmp_sel_kernel, n_cmp=n_cmp, n_slc=n_slc, bounded=bounded)
    return pl.pallas_call(
        kern, grid=(B, T // tq),
        in_specs=[pl.BlockSpec((1, tq, D), lambda b, i: (b, i, 0)), whole(kab), whole(v1),
                  pl.BlockSpec(m.shape, lambda b, i: (0, 0, 0))],
        out_specs=[pl.BlockSpec((1, tq, D), lambda b, i: (b, i, 0)), pl.BlockSpec((1, tq, LANES), lambda b, i: (b, i, 0))],
        out_shape=[jax.ShapeDtypeStruct((B, T, D), F32), jax.ShapeDtypeStruct((B, T, LANES), BF16)],
        compiler_params=_cparams(("parallel", "arbitrary")), name="cmp_sel_attn",
    )(q, kab, v1, m)


def _flash_pairs(lhs, kab_ref, v1_ref, kt_lo, kt_hi, tk, bias_fn, bounded):
    rows = lhs[0].shape[0]

    def body(kt, carry):
        k0 = pl.multiple_of(kt * tk, tk)
        biases = bias_fn(k0)
        out = []
        for g in range(N_KV_HEADS):
            vv = v1_ref[0, pl.ds(k0, tk), g * LANES:(g + 1) * LANES]
            for half in range(2):
                idx = 2 * g + half
                s = _mm(lhs[g], kab_ref[0, pl.ds(k0, tk), idx * LANES:(idx + 1) * LANES], "nt") + biases[g]
                if bounded:
                    out.append(carry[idx] + _mm(jnp.exp2(s), vv))
                else:
                    m, acc = carry[idx]
                    m_new = jnp.maximum(m, jnp.max(s, -1, keepdims=True))
                    out.append((m_new, jnp.exp2(m - m_new) * acc + _mm(jnp.exp2(s - m_new), vv)))
        return tuple(out)

    zero = jnp.zeros((rows, LANES), F32)
    if bounded:
        return lax.fori_loop(kt_lo, kt_hi, body, (zero,) * (2 * N_KV_HEADS))
    start = (jnp.full((rows, 1), M_INIT, F32), zero)
    return tuple(acc for _, acc in lax.fori_loop(kt_lo, kt_hi, body, (start,) * (2 * N_KV_HEADS)))


def _slc_win_attn_kernel(q_ref, sel_ref, skab_ref, sv1_ref, wkab_ref, wv1_ref, os_o, ow_o, *, tk, bounded):
    tq = q_ref.shape[1]
    nsp = sel_ref.shape[2] // N_KV_HEADS
    q0 = pl.program_id(1) * tq
    q_pos = q0 + lax.broadcasted_iota(jnp.int32, (tq, tk), 0)
    k_off = lax.broadcasted_iota(jnp.int32, (tq, tk), 1)
    sel_lane = lax.broadcasted_iota(jnp.int32, (N_KV_HEADS * nsp, tk), 0)
    blk_k = lax.broadcasted_iota(jnp.int32, (N_KV_HEADS * nsp, tk), 1)
    drop = jnp.where(sel_ref[0].astype(F32) > 0.5, 0.0, MASKED).astype(BF16)
    q = q_ref[0]
    kt_hi = (q0 + tq + tk - 1) // tk
    kt_lo_win = jnp.maximum(q0 - WINDOW + 1, 0) // tk

    twice = lambda b: jnp.concatenate([b, b], axis=0)

    def win_bias(k0):
        d = q_pos - (k_off + k0)
        return [twice(jnp.where((d >= 0) & (d < WINDOW), 0.0, MASKED))] * N_KV_HEADS

    def slc_bias(k0):
        causal = jnp.where(k_off + k0 <= q_pos, 0.0, MASKED)
        key_blk = (blk_k + k0) // SLC_BLOCK
        out = []
        for g in range(N_KV_HEADS):
            expand = jnp.where(g * nsp + key_blk == sel_lane, 1.0, 0.0).astype(BF16)
            out.append(twice(jnp.dot(drop, expand, preferred_element_type=F32) + causal))
        return out

    lhs = [_pair_rows(q, g) for g in range(N_KV_HEADS)]
    acc_s = _flash_pairs(lhs, skab_ref, sv1_ref, 0, kt_hi, tk, slc_bias, bounded)
    acc_w = _flash_pairs(lhs, wkab_ref, wv1_ref, kt_lo_win, kt_hi, tk, win_bias, bounded)
    for g in range(N_KV_HEADS):
        for p_ in range(2):
            col = g * 2 * LANES + p_ * LANES
            os_o[0, :, col:col + LANES] = _finish_pair(acc_s[2 * g], acc_s[2 * g + 1], p_, tq)
            ow_o[0, :, col:col + LANES] = _finish_pair(acc_w[2 * g], acc_w[2 * g + 1], p_, tq)


def _slc_win_attn(q, sel, skab, sv1, wkab, wv1, *, tq, tk, bounded):
    B, T, D = q.shape
    whole = lambda a: pl.BlockSpec((1,) + a.shape[1:], lambda b, i: (b, 0, 0))
    row = lambda w: pl.BlockSpec((1, tq, w), lambda b, i: (b, i, 0))
    return pl.pallas_call(
        functools.partial(_slc_win_attn_kernel, tk=tk, bounded=bounded), grid=(B, T // tq),
        in_specs=[row(D), row(sel.shape[2]), whole(skab), whole(sv1), whole(wkab), whole(wv1)],
        out_specs=[row(D), row(D)],
        out_shape=[jax.ShapeDtypeStruct((B, T, D), F32)] * 2,
        compiler_params=_cparams(("parallel", "arbitrary")), name="slc_win_attn",
    )(q, sel, skab, sv1, wkab, wv1)


def _pair_keys(k):
    z = jnp.zeros_like(k)
    return jnp.concatenate([k, z, z, k], axis=-1).reshape(k.shape[0], k.shape[1], -1)


def _ones_values(v):
    return jnp.concatenate([v, jnp.ones_like(v)], axis=-1).reshape(v.shape[0], v.shape[1], -1)


def _sample_attn_kernel(blk_ref, q_ref, kpos_ref, new_ref, win_ref, cache_hbm, os_o, ow_o, kbuf, vbuf, sem, *,
                        n_q, n_sel, pos_base, win_base, n_win, hp):
    b = pl.program_id(0)
    slot = b % 2
    per_seq = n_q * N_KV_HEADS * n_sel

    def pool_copies(step, into, start):
        def one(i, carry):
            blk = blk_ref[step * per_seq + i]
            qg = i // n_sel
            lane0 = pl.multiple_of(((qg % N_KV_HEADS) // 2) * LANES, LANES)

            @pl.when(blk >= 0)
            def _():
                for c, dst in ((0, kbuf), (1, vbuf)):
                    cp = pltpu.make_async_copy(cache_hbm.at[blk, :, pl.ds(c * KV_W + lane0, LANES)],
                                               dst.at[into, qg, pl.ds((i % n_sel) * SLC_BLOCK, SLC_BLOCK)], sem.at[into])
                    cp.start() if start else cp.wait()
            return carry
        lax.fori_loop(0, per_seq, one, 0)

    @pl.when(b == 0)
    def _():
        pool_copies(0, 0, True)

    @pl.when(b + 1 < pl.num_programs(0))
    def _():
        pool_copies(b + 1, 1 - slot, True)

    pool_copies(b, slot, False)

    def new_rows(i, carry):
        qg = i // n_sel
        win = (qg % N_KV_HEADS) // 2

        @pl.when(blk_ref[b * per_seq + i] < 0)
        def _():
            kbuf[slot, qg, pl.ds((i % n_sel) * SLC_BLOCK, SLC_BLOCK), :] = new_ref[0, win]
            vbuf[slot, qg, pl.ds((i % n_sel) * SLC_BLOCK, SLC_BLOCK), :] = new_ref[0, N_KV_HEADS // 2 + win]
        return carry
    lax.fori_loop(0, per_seq, new_rows, 0)

    tq = q_ref.shape[1]
    rows = HEADS_PER_KV * tq
    nk = n_sel * SLC_BLOCK
    nwp = win_ref.shape[1]
    q = q_ref[0]
    row_q = lax.broadcasted_iota(jnp.int32, (rows, 1), 0) % tq
    w_id = lax.broadcasted_iota(jnp.int32, (rows, nwp), 1)
    d_win = (pos_base + row_q) - (win_base + w_id)
    win_bias = jnp.where((d_win >= 0) & (d_win < WINDOW) & (win_base + w_id >= 0) & (w_id < n_win), 0.0, MASKED)

    def attend(s, v):
        m = jnp.maximum(jnp.max(s, -1, keepdims=True), M_INIT)
        e = jnp.exp(s - m)
        p = e / jnp.maximum(jnp.sum(e, -1, keepdims=True), 1e-30)
        return _mm(p, v, hp=hp)

    for g in range(N_KV_HEADS):
        qs = _stack_heads(q, g, tq)
        zeros = jnp.zeros_like(qs)
        qs_pair = jnp.concatenate([qs, zeros] if g % 2 == 0 else [zeros, qs], axis=1)
        lanes_g = slice((g % 2) * HEAD_DIM, (g % 2 + 1) * HEAD_DIM)
        o_s = jnp.zeros((rows, HEAD_DIM), F32)
        for qi in range(n_q):
            qg = qi * N_KV_HEADS + g
            kpos = kpos_ref[0, qg:qg + 1, :]
            bias = jnp.broadcast_to(jnp.where(kpos <= pos_base + qi, 0.0, MASKED), (rows, nk))
            o_qi = attend(_mm(qs_pair, kbuf[slot, qg], "nt", hp) + bias, vbuf[slot, qg])[:, lanes_g]
            o_s = jnp.where(row_q == qi, o_qi, o_s)
        o_w = attend(_mm(qs, win_ref[0, :, g * HEAD_DIM:(g + 1) * HEAD_DIM], "nt", hp) + win_bias,
                     win_ref[0, :, KV_W + g * HEAD_DIM:KV_W + (g + 1) * HEAD_DIM])
        for i in range(HEADS_PER_KV):
            col = (HEADS_PER_KV * g + i) * HEAD_DIM
            os_o[0, :, col:col + HEAD_DIM] = o_s[i * tq:(i + 1) * tq]
            ow_o[0, :, col:col + HEAD_DIM] = o_w[i * tq:(i + 1) * tq]


def _sample_attn(pool_blk, q, kpos, new_win, win_all, cache_rows, *, n_q, n_sel, pos_base, win_base, n_win, hp):
    B, tq, D = q.shape
    kern = functools.partial(_sample_attn_kernel, n_q=n_q, n_sel=n_sel, pos_base=pos_base, win_base=win_base,
                             n_win=n_win, hp=hp)
    blk = lambda a: pl.BlockSpec((1,) + a.shape[1:], lambda b, pb: (b,) + (0,) * (a.ndim - 1))
    buf = pltpu.VMEM((2, n_q * N_KV_HEADS, n_sel * SLC_BLOCK, LANES), F32)
    grid_spec = pltpu.PrefetchScalarGridSpec(
        num_scalar_prefetch=1, grid=(B,),
        in_specs=[blk(q), blk(kpos), blk(new_win), blk(win_all), pl.BlockSpec(memory_space=pl.ANY)],
        out_specs=[blk(q), blk(q)],
        scratch_shapes=[buf, buf, pltpu.SemaphoreType.DMA((2,))])
    return pl.pallas_call(
        kern, grid_spec=grid_spec, out_shape=[jax.ShapeDtypeStruct((B, tq, D), F32)] * 2,
        compiler_params=_cparams(("arbitrary",)), name="sample_slc_win_attn",
    )(pool_blk.reshape(-1), q, kpos, new_win, win_all, cache_rows)


def _pick_tile(n, cap=512):
    return next(t for t in (512, 256, 128, 64, 32, 16, 8) if t <= cap and n % t == 0)


def _pad_rows(x, t_pad):
    return jnp.pad(x, ((0, 0), (0, t_pad - x.shape[1]), (0, 0)))


def _rwkv_layer(x, shift0, wkv0, p, *, tm, hp, inv_hp):
    B, T, D = x.shape
    t_pad = -(-T // tm) * tm
    xp = _pad_rows(x, t_pad)
    r, lw, k, v, a, g, shift = _rwkv_proj(xp, shift0[:, None, :], p["g"], p["mix"], p["vec"], p["wrkv"], p["dw1"],
                                          p["dw2"], p["aw1"], p["aw2"], p["gw1"], p["gw2"], t_real=T, tm=tm, hp=hp)
    if t_pad != T:
        live = (jnp.arange(t_pad) < T)[None, :, None]
        r, lw, k, v, a = (jnp.where(live, t, 0.0) for t in (r, lw, k, v, a))
    y, s_fin = _wkv(r, lw, k, v, a, p["vec"], wkv0, hp=hp, inv_hp=inv_hp)
    n = B * t_pad
    out = _row_call(functools.partial(_rwkv_out_kernel, hp=hp), [xp.reshape(n, D), y.reshape(n, D), g.reshape(n, D)],
                    [p["wo"]], [D], [F32], _pick_tile(n), "rwkv_out")[0]
    return out.reshape(B, t_pad, D)[:, :T], shift[:, 0], s_fin


def _sub_blocks(rows):
    S, L = rows.shape[:2]
    n_sub = L // CMP_STRIDE
    r = rows[:, :n_sub * CMP_STRIDE].reshape(S, n_sub, CMP_STRIDE, 2, N_KV_HEADS, HEAD_DIM)
    r = jnp.transpose(r, (0, 3, 4, 1, 2, 5)).reshape(S, 2, N_KV_HEADS, n_sub, CMP_STRIDE * HEAD_DIM)
    r_pad = -(-n_sub // SUBLANES) * SUBLANES
    return jnp.pad(r, ((0, 0),) * 3 + ((0, r_pad - n_sub), (0, 0))), n_sub - CMP_BLOCK // CMP_STRIDE + 1


def _compressed_kv(rows, p, *, hp):
    xsub, n_cmp = _sub_blocks(rows)
    out = _compress(xsub, p["cmp_pe"], p["cmp_w1"], p["cmp_w2"], p["k_norm_g"][0:1], hp=hp)
    S, _, G, R, dh = out.shape
    out = jnp.transpose(out, (1, 0, 3, 2, 4)).reshape(2, S, R, G * dh).astype(F32 if hp else BF16)
    return out[0], out[1], n_cmp


def _overlap_matrix(ncp, n_cmp, n_slc, nsp):
    c0 = np.arange(ncp)[:, None] * CMP_STRIDE
    s0 = np.arange(nsp)[None, :] * SLC_BLOCK
    m = (c0 < s0 + SLC_BLOCK) & (c0 + CMP_BLOCK > s0) & (np.arange(ncp)[:, None] < n_cmp) & (np.arange(nsp)[None, :] < n_slc)
    return jnp.asarray(m.astype(np.float32), dtype=BF16)


def _moe(h, p, *, tm, tf, hp):
    n, D = h.shape
    xn, logits = _row_call(_router_kernel, [h], [p["g_moe"], p["w_router"]], [D, LANES], [F32 if hp else BF16, F32],
                           _pick_tile(n), "moe_router")
    top_v, top_e = lax.top_k(logits[:, :N_EXPERTS], TOP_K)
    gates = jax.nn.softmax(top_v, -1)
    n_pairs = n * TOP_K
    flat_e = top_e.reshape(n_pairs)
    order = jnp.argsort(flat_e)
    seen = jnp.cumsum((flat_e[:, None] == jnp.arange(N_EXPERTS)).astype(jnp.int32), axis=0)
    counts = seen[-1]
    padded = (counts + tm - 1) // tm * tm
    pad_end = jnp.cumsum(padded)
    pad_start = pad_end - padded
    start = jnp.cumsum(counts) - counts
    rank = jnp.take_along_axis(seen, flat_e[:, None], axis=1)[:, 0] - 1
    dest = (pad_start[flat_e] + rank).reshape(n, TOP_K)
    n_blocks = -(-n_pairs // tm) + N_EXPERTS
    cap = n_blocks * tm
    block_e = jnp.minimum(jnp.searchsorted(pad_end, jnp.arange(n_blocks) * tm, side="right"), N_EXPERTS - 1).astype(jnp.int32)
    row_e = jnp.repeat(block_e, tm)
    k_in_e = jnp.arange(cap, dtype=jnp.int32) - pad_start[row_e]
    src = order[jnp.clip(start[row_e] + k_in_e, 0, n_pairs - 1)] // TOP_K
    buf_tok = jnp.where((k_in_e >= 0) & (k_in_e < counts[row_e]), src, n).astype(jnp.int32)
    n_used = (pad_end[-1] // tm).astype(jnp.int32).reshape(1)
    xs = jnp.concatenate([xn, jnp.zeros((1, D), xn.dtype)], 0)[buf_tok]
    ys = _moe_ffn(block_e, n_used, xs, p["moe_wg"], p["moe_wu"], p["moe_wd"], tm=tm, tf=tf, hp=hp)
    y = ys[dest[:, 0]] * gates[:, 0:1] + ys[dest[:, 1]] * gates[:, 1:2]
    return h + y


def kernel(x_prompt, x_sample, cache_cmp_kv, cache_slc_kv, state_win_kv, state_wkv, state_shift, page_table, norm_g, rw_mix, rw_vec, rw_w_rkv, rw_w_o, rw_decay_w1, rw_decay_w2, rw_iclr_w1, rw_iclr_w2, rw_gate_w1, rw_gate_w2, ffn_w_gate, ffn_w_up, ffn_w_down, moe_router, moe_w_gate, moe_w_up, moe_w_down, kv_norm_g, w_kv, k_norm_g, cmp_pe, cmp_w1, cmp_w2, w_qg, q_norm_g, w_o):
    B, T, D = x_prompt.shape
    SB, S, _ = x_sample.shape
    assert norm_g.shape[0] == 2 and rw_mix.shape[0] == 1 and w_qg.shape[0] == 1, "one RWKV-7 layer + one NSA layer"
    assert D == N_HEADS * HEAD_DIM and T % 512 == 0 and S <= SUBLANES
    page = cache_cmp_kv.shape[1]
    past = page_table.shape[1] * page
    wb = state_win_kv.shape[1]
    assert past % SLC_BLOCK == 0 and page % SLC_BLOCK == 0 and S <= SLC_BLOCK
    bf = lambda w: w.astype(BF16)
    pad8 = lambda m: jnp.pad(m, ((0, SUBLANES - m.shape[0]), (0, 0)))
    dff = ffn_w_gate.shape[2]
    tf = dff // 2 if (dff // 2) % LANES == 0 else dff
    n_p, n_s = B * T, SB * S
    both = lambda w: (bf(w), w)
    pick = lambda d, hp: {k: (v[hp] if isinstance(v, tuple) else v) for k, v in d.items()}

    rw = dict(g=norm_g[0, 0][None], mix=pad8(rw_mix[0]), vec=pad8(rw_vec[0]), wrkv=both(rw_w_rkv[0]), wo=both(rw_w_o[0]),
              dw1=both(rw_decay_w1[0]), dw2=both(rw_decay_w2[0]), aw1=both(rw_iclr_w1[0]), aw2=both(rw_iclr_w2[0]),
              gw1=both(rw_gate_w1[0]), gw2=both(rw_gate_w2[0]))
    H = D // RW_HEAD_DIM
    hp_, shift_p, wkv_p = _rwkv_layer(x_prompt, jnp.zeros((B, D), F32), jnp.zeros((B, H, RW_HEAD_DIM, RW_HEAD_DIM), F32),
                                      pick(rw, 0), tm=256, hp=False, inv_hp=False)
    hs, shift_s, wkv_s = _rwkv_layer(x_sample, state_shift[0], state_wkv[0], pick(rw, 1), tm=WKV_CHUNK, hp=True, inv_hp=True)
    ffn_g = norm_g[0, 1][None]
    hp_ = _ffn(hp_.reshape(n_p, D), ffn_g, bf(ffn_w_gate[0]), bf(ffn_w_up[0]), bf(ffn_w_down[0]), tm=512, tf=dff, hp=False)
    hs = _ffn(hs.reshape(n_s, D), ffn_g, ffn_w_gate[0], ffn_w_up[0], ffn_w_down[0], tm=n_s, tf=tf, hp=True)

    kv_consts = lambda hp: [kv_norm_g[None], both(w_kv)[hp], jnp.tile(k_norm_g, (1, N_KV_HEADS)),
                            _block_diag(KV_W, HEAD_DIM, 1.0 / HEAD_DIM)]
    place_k = np.zeros((KV_W, N_KV_HEADS * 4 * HEAD_DIM), np.float32)
    place_v = np.zeros((KV_W, N_KV_HEADS * 2 * HEAD_DIM), np.float32)
    ones_v = np.zeros((1, N_KV_HEADS * 2 * HEAD_DIM), np.float32)
    for g in range(N_KV_HEADS):
        for d in range(HEAD_DIM):
            place_k[g * HEAD_DIM + d, g * 4 * HEAD_DIM + d] = 1.0
            place_k[g * HEAD_DIM + d, g * 4 * HEAD_DIM + 3 * HEAD_DIM + d] = 1.0
            place_v[g * HEAD_DIM + d, g * 2 * HEAD_DIM + d] = 1.0
        ones_v[0, g * 2 * HEAD_DIM + HEAD_DIM:(g + 1) * 2 * HEAD_DIM] = 1.0
    pair_consts = [jnp.asarray(place_k, dtype=BF16), jnp.asarray(place_v, dtype=BF16), jnp.asarray(ones_v)]
    as_rows = lambda t, b: t.reshape(b, -1, 2, N_KV_HEADS, HEAD_DIM)
    kv_p = _row_call(functools.partial(_kv_proj_kernel, hp=False, pair_out=True), [hp_], kv_consts(False) + pair_consts,
                     [2 * KV_W] * 3 + [4 * KV_W, 2 * KV_W] * 2, [F32] * 3 + [BF16] * 4, 512, "kv_proj")
    cmp_p, slc_p, win_p = (as_rows(t, B) for t in kv_p[:3])
    skab, sv1, wkab, wv1 = (t.reshape(B, T, -1) for t in kv_p[3:])
    cmp_s, slc_s, win_s = (as_rows(t, SB) for t in _row_call(
        functools.partial(_kv_proj_kernel, hp=True, pair_out=False), [hs], kv_consts(True), [2 * KV_W] * 3, [F32] * 3,
        n_s, "kv_proj"))
    win_all_s = jnp.concatenate([state_win_kv, win_s], axis=1)

    cp = dict(cmp_pe=cmp_pe.reshape(2, CMP_BLOCK // CMP_STRIDE, 1, CMP_STRIDE * HEAD_DIM),
              cmp_w1=both(cmp_w1.reshape(2, CMP_BLOCK // CMP_STRIDE, CMP_STRIDE * HEAD_DIM, HEAD_DIM)),
              cmp_w2=both(cmp_w2), k_norm_g=k_norm_g)
    kc_p, vc_p, ncmp_p = _compressed_kv(cmp_p, pick(cp, 0), hp=False)
    assert S < CMP_STRIDE and page % CMP_STRIDE == 0
    n_ratio = CMP_BLOCK // CMP_STRIDE
    w1r = cmp_w1.reshape(2, n_ratio, CMP_STRIDE, HEAD_DIM, HEAD_DIM)
    w2x = jnp.einsum("cjsdh,ef->csedjfh", w1r, jnp.eye(2, dtype=F32)).reshape(2, CMP_STRIDE * LANES, n_ratio * LANES)
    n_pages = page_table.shape[1]
    pages_per_step = next(pp for pp in (32, 16, 8, 4, 2, 1) if n_pages % pp == 0)
    part_s = _paged_sub_proj(page_table, cache_cmp_kv.reshape(-1, page, 2 * KV_W), w2x, pages_per_step=pages_per_step, hp=True)
    kcv_s = _compress_tail(part_s, cp["cmp_pe"], cp["cmp_w1"][1], cmp_w2, k_norm_g[0:1], hp=True)
    kc_s, vc_s, ncmp_s = kcv_s[:, 0], kcv_s[:, 1], past // CMP_STRIDE - n_ratio + 1

    ge = np.zeros((LANES, N_BRANCH * D), np.float32)
    for br in range(N_BRANCH):
        for hd in range(N_HEADS):
            ge[br * N_HEADS + hd, br * D + hd * HEAD_DIM: br * D + (hd + 1) * HEAD_DIM] = 1.0
    nsa = dict(g_attn=norm_g[1, 0][None], wq=both(w_qg[0, :, :D]),
               wgate=both(jnp.pad(w_qg[0, :, D:], ((0, 0), (0, LANES - N_BRANCH * N_HEADS)))),
               q_norm_g=jnp.tile(q_norm_g[0], N_HEADS)[None], bd_q=_block_diag(D, HEAD_DIM, 1.0 / HEAD_DIM),
               w_o=both(w_o[0]), gate_expand=jnp.asarray(ge, dtype=BF16))

    def query_side(h, tm, hp, q_scale):
        c = pick(nsa, hp)
        return _row_call(functools.partial(_q_proj_kernel, hp=hp, q_scale=q_scale), [h],
                         [c["g_attn"], c["wq"], c["wgate"], c["q_norm_g"], c["bd_q"]], [D, LANES],
                         [F32 if hp else BF16, F32], tm, "q_proj")

    def merge(h, gates, o_cmp, o_slc, o_win, tm, hp):
        c = pick(nsa, hp)
        return _row_call(functools.partial(_merge_kernel, hp=hp), [h, gates, o_cmp, o_slc, o_win],
                         [c["gate_expand"], c["w_o"]], [D], [F32], tm, "nsa_merge")[0]

    q_p, gates_p = query_side(hp_, 512, False, ATTN_SCALE * LOG2E)
    q_p = q_p.reshape(B, T, D)
    n_slc_p = T // SLC_BLOCK
    nsp_p = LANES // N_KV_HEADS
    assert n_slc_p <= nsp_p, "the packed block-selection layout holds at most 32 selection blocks per kv group"
    m_one = _overlap_matrix(kc_p.shape[1], ncmp_p, n_slc_p, nsp_p)
    m_p = jnp.stack([jnp.pad(m_one, ((0, 0), (g * nsp_p, LANES - (g + 1) * nsp_p))) for g in range(N_KV_HEADS)])
    heads = lambda t: t.reshape(t.shape[0], t.shape[1], N_KV_HEADS, HEAD_DIM)
    kcab, vc1 = _pair_keys(heads(kc_p)), _ones_values(heads(vc_p))

    def prompt_branches(bounded):
        o_cmp, sel = _cmp_sel(q_p, kcab, vc1, m_p, n_cmp=ncmp_p, n_slc=n_slc_p, tq=256, bounded=bounded)
        return (o_cmp,) + tuple(_slc_win_attn(q_p, sel, skab, sv1, wkab, wv1, tq=256, tk=256, bounded=bounded))

    score_bound = HEAD_DIM * ATTN_SCALE * jnp.max(jnp.abs(q_norm_g[0])) * jnp.max(jnp.abs(k_norm_g))
    o_cmp_p, o_slc_p, o_win_p = lax.cond(score_bound <= SCORE_BOUND_MAX, lambda: prompt_branches(True),
                                         lambda: prompt_branches(False))
    hp_ = merge(hp_, gates_p, o_cmp_p.reshape(n_p, D), o_slc_p.reshape(n_p, D), o_win_p.reshape(n_p, D), 512, False)

    kv_lanes = lambda t, dt: t.reshape(t.shape[0], t.shape[1], 2 * KV_W).astype(dt)
    q_s, gates_s = query_side(hs, n_s, True, ATTN_SCALE)
    q_s = _pad_rows(q_s.reshape(SB, S, D), SUBLANES)
    n_slc_s = (past + S - 1) // SLC_BLOCK + 1
    nsp_s = -(-n_slc_s // LANES) * LANES
    m_s = _overlap_matrix(kc_s.shape[1], ncmp_s, n_slc_s, nsp_s)
    o_cmp_s, sel_s = _cmp_attn(q_s, kc_s, vc_s, m_s, n_cmp=ncmp_s, n_slc=n_slc_s, pos_base=past, per_tile=False,
                               tq=SUBLANES, hp=True)
    n_sel = min(SLC_TOP, n_slc_s)
    sel_s = sel_s.reshape(SB, SUBLANES, N_KV_HEADS, nsp_s)[:, :S, :, :n_slc_s] > 0.5
    slot = jnp.cumsum(sel_s, axis=-1) - 1
    idx_s = jnp.sum(jnp.where(sel_s[..., None] & (slot[..., None] == jnp.arange(n_sel)),
                              jnp.arange(n_slc_s)[:, None], 0), axis=-2).astype(jnp.int32)
    in_pool = idx_s < past // SLC_BLOCK
    per_page = page // SLC_BLOCK
    blk_c = jnp.minimum(idx_s, past // SLC_BLOCK - 1)
    pages = jnp.take_along_axis(page_table, (blk_c // per_page).reshape(SB, -1), axis=1).reshape(idx_s.shape)
    pool_blk = jnp.where(in_pool, pages * per_page + blk_c % per_page, -1).astype(jnp.int32)
    new_rows = slc_s[:, jnp.clip(jnp.arange(SLC_BLOCK), 0, S - 1)]
    new_win = jnp.transpose(new_rows.reshape(SB, SLC_BLOCK, 2, N_KV_HEADS // 2, LANES), (0, 2, 3, 1, 4))
    new_win = new_win.reshape(SB, N_KV_HEADS, SLC_BLOCK, LANES)
    kpos = (idx_s[..., None] * SLC_BLOCK + jnp.arange(SLC_BLOCK)).reshape(SB, S * N_KV_HEADS, n_sel * SLC_BLOCK).astype(jnp.int32)
    kpos = jnp.pad(kpos, ((0, 0), (0, -(-S * N_KV_HEADS // SUBLANES) * SUBLANES - S * N_KV_HEADS), (0, 0)))
    cache_rows = cache_slc_kv.reshape(-1, SLC_BLOCK, 2 * KV_W)
    n_win = wb + S
    nwp = -(-n_win // SUBLANES) * SUBLANES
    win_lanes = _pad_rows(kv_lanes(win_all_s, F32), nwp)
    o_slc_s, o_win_s = _sample_attn(pool_blk, q_s, kpos, new_win, win_lanes, cache_rows, n_q=S, n_sel=n_sel,
                                    pos_base=past, win_base=past - wb, n_win=n_win, hp=True)
    unpad = lambda t: t[:, :S].reshape(n_s, D)
    hs = merge(hs, gates_s, unpad(o_cmp_s), unpad(o_slc_s), unpad(o_win_s), n_s, True)

    moe = dict(g_moe=norm_g[1, 1][None], w_router=jnp.pad(moe_router[0], ((0, 0), (0, LANES - N_EXPERTS))),
               moe_wg=both(moe_w_gate[0]), moe_wu=both(moe_w_up[0]), moe_wd=both(moe_w_down[0]))
    hp_ = _moe(hp_, pick(moe, 0), tm=512, tf=dff, hp=False).reshape(B, T, D)
    hs = _moe(hs, pick(moe, 1), tm=128, tf=tf, hp=True).reshape(SB, S, D)

    keep_p = min(WINDOW, T)
    return (hp_, hs, cmp_p, cmp_s, slc_p, slc_s, win_p[:, T - keep_p:], win_all_s[:, win_all_s.shape[1] - wb:],
            wkv_p[None], wkv_s[None], shift_p[None], shift_s[None])
```

```python
import functools

import numpy as np
import jax
import jax.numpy as jnp
from jax import lax
from jax.experimental import pallas as pl
from jax.experimental.pallas import tpu as pltpu

F32 = jnp.float32
BF16 = jnp.bfloat16

RW_HEAD_DIM = 64
RW_GN_EPS = 64e-5
N_HEADS = 16
HEAD_DIM = 64
N_KV_HEADS = 4
HEADS_PER_KV = N_HEADS // N_KV_HEADS
KV_W = N_KV_HEADS * HEAD_DIM
N_BRANCH = 3
CMP_BLOCK = 32
CMP_STRIDE = 16
SLC_BLOCK = 64
SLC_TOP = 16
WINDOW = 512
ATTN_SCALE = HEAD_DIM ** -0.5
N_EXPERTS = 8
TOP_K = 2
RMS_EPS = 1e-6
MASKED = -1e30
M_INIT = -1e20
BIG = 1e30
LOG2E = 1.4426950408889634
SCORE_BOUND_MAX = 40.0

SUBLANES = 8
LANES = 128
VMEM_LIMIT = 56 * 1024 * 1024
WKV_CHUNK = RW_HEAD_DIM


def _cparams(sem):
    return pltpu.CompilerParams(dimension_semantics=sem, vmem_limit_bytes=VMEM_LIMIT)


_DIMS = {"nn": (((1,), (0,)), ((), ())), "nt": (((1,), (1,)), ((), ())), "tn": (((0,), (0,)), ((), ()))}


def _split2(x):
    hi = x.astype(BF16)
    lo = (x - hi.astype(F32)).astype(BF16)
    return hi, lo


def _mm(a, b, form="nn", hp=False):
    d = lambda s, t: lax.dot_general(s, t, _DIMS[form], preferred_element_type=F32)
    if not hp:
        return d(a.astype(BF16), b.astype(BF16))
    ah, al = _split2(a.astype(F32))
    bh, bl = _split2(b.astype(F32))
    return d(ah, bh) + d(ah, bl) + d(al, bh)


def _dot_hilo(x, m):
    hi, lo = _split2(x)
    return jnp.dot(hi, m, preferred_element_type=F32) + jnp.dot(lo, m, preferred_element_type=F32)


def _dot_exact_rhs(m, x):
    hi = x.astype(BF16)
    r1 = x - hi.astype(F32)
    mid = r1.astype(BF16)
    lo = (r1 - mid.astype(F32)).astype(BF16)
    d = lambda t: jnp.dot(m, t, preferred_element_type=F32)
    return d(hi) + d(mid) + d(lo)


def _rms(x, g):
    return x * lax.rsqrt(jnp.mean(x * x, -1, keepdims=True) + RMS_EPS) * g


def _silu(x):
    return x * jax.nn.sigmoid(x)


def _softplus(z):
    return jnp.maximum(z, 0.0) + jnp.log(1.0 + jnp.exp(-jnp.abs(z)))


def _block_diag(n, seg, value):
    i = np.arange(n)
    return jnp.asarray((i[:, None] // seg == i[None, :] // seg).astype(np.float32) * value, dtype=BF16)


def _rwkv_proj_kernel(x_ref, halo_ref, sh_ref, g_ref, mix_ref, vec_ref, wrkv_ref, dw1_ref, dw2_ref, aw1_ref, aw2_ref,
                      gw1_ref, gw2_ref, r_o, lw_o, k_o, v_o, a_o, g_o, shift_o, *, last_tile, last_row, hp):
    i = pl.program_id(1)
    mm = functools.partial(_mm, hp=hp)
    g = g_ref[...]
    xn = _rms(x_ref[0], g)
    hn = _rms(halo_ref[0, SUBLANES - 1:SUBLANES, :], g)
    prev_last = jnp.where(i == 0, sh_ref[0], hn)
    row = lax.broadcasted_iota(jnp.int32, xn.shape, 0)
    prev = jnp.where(row == 0, prev_last, pltpu.roll(xn, 1, 0))
    xx = prev - xn
    mixed = lambda j: xn + xx * mix_ref[j:j + 1, :]
    r_o[0] = mm(mixed(0), wrkv_ref[0])
    k_o[0] = mm(mixed(2), wrkv_ref[1])
    v_o[0] = mm(mixed(3), wrkv_ref[2])
    w0 = vec_ref[0:1, :]
    a0 = vec_ref[1:2, :]
    log_w = -_softplus(-(w0 + mm(jnp.tanh(mm(mixed(1), dw1_ref[...])), dw2_ref[...]))) - 0.5
    lw_o[0] = -jnp.exp(log_w)
    a_o[0] = jax.nn.sigmoid(a0 + mm(mm(mixed(4), aw1_ref[...]), aw2_ref[...]))
    g_o[0] = mm(jax.nn.sigmoid(mm(mixed(5), gw1_ref[...])), gw2_ref[...])

    @pl.when(i == last_tile)
    def _():
        shift_o[0] = xn[last_row:last_row + 1, :]


def _rwkv_proj(x, shift0, g, mix, vec, wrkv, dw1, dw2, aw1, aw2, gw1, gw2, *, t_real, tm, hp):
    B, T, D = x.shape
    nt = T // tm
    row_spec = pl.BlockSpec((1, tm, D), lambda b, i: (b, i, 0))
    halo_spec = pl.BlockSpec((1, SUBLANES, D), lambda b, i: (b, jnp.maximum(i * (tm // SUBLANES) - 1, 0), 0))
    vec_spec = pl.BlockSpec((1, 1, D), lambda b, i: (b, 0, 0))
    full = lambda a: pl.BlockSpec(a.shape, lambda b, i: (0,) * a.ndim)
    consts = (g, mix, vec, wrkv, dw1, dw2, aw1, aw2, gw1, gw2)
    out_sd = jax.ShapeDtypeStruct((B, T, D), F32)
    kern = functools.partial(_rwkv_proj_kernel, last_tile=(t_real - 1) // tm, last_row=(t_real - 1) % tm, hp=hp)
    return pl.pallas_call(
        kern, grid=(B, nt),
        in_specs=[row_spec, halo_spec, vec_spec] + [full(c) for c in consts],
        out_specs=[row_spec] * 6 + [vec_spec],
        out_shape=[out_sd] * 6 + [jax.ShapeDtypeStruct((B, 1, D), F32)],
        compiler_params=_cparams(("parallel", "arbitrary")), name="rwkv_proj",
    )(x, x, shift0, *consts)


def _wkv_kernel(r_ref, lw_ref, k_ref, v_ref, a_ref, vec_ref, ones_ref, s0_ref, y_o, sT_o, s_sc, *, n_pairs, hp, inv_hp):
    c = pl.program_id(1)
    C = r_ref.shape[1]
    W = 2 * C
    mm = functools.partial(_mm, hp=hp)
    mm_inv = functools.partial(_mm, hp=inv_hp)

    @pl.when(c == 0)
    def _():
        s_sc[...] = s0_ref[0]

    lane = lax.broadcasted_iota(jnp.int32, (C, W), 1)
    row = lax.broadcasted_iota(jnp.int32, (C, W), 0)
    left = lane < C
    col = jnp.where(left, lane, lane - C)
    strict = col < row
    incl = col <= row
    diag_blocks = (lax.broadcasted_iota(jnp.int32, (W, W), 0) < C) == (lax.broadcasted_iota(jnp.int32, (W, W), 1) < C)
    bd = lambda x: jnp.concatenate([jnp.where(left, x, 0.0), jnp.where(left, 0.0, x)], axis=0)
    seg_sum = lambda x: _dot_hilo(x, ones_ref[...])
    rows = lambda x, p: x[p * C:(p + 1) * C]
    tri = jnp.where(lax.broadcasted_iota(jnp.int32, (C, C), 1) <= lax.broadcasted_iota(jnp.int32, (C, C), 0), 1.0, 0.0)
    cum_all = _dot_exact_rhs(tri.astype(BF16), lw_ref[0])
    pairs = range(n_pairs)
    ps = lambda p: slice(p * W, (p + 1) * W)

    r = [r_ref[0, :, ps(p)] for p in pairs]
    v = [v_ref[0, :, ps(p)] for p in pairs]
    kkr = [k_ref[0, :, ps(p)] * vec_ref[2:3, ps(p)] for p in pairs]
    ss = seg_sum(jnp.concatenate([x * x for x in kkr], axis=0))
    k2, at, rt, bt, kt, w_end = [], [], [], [], [], []
    for p in pairs:
        a = a_ref[0, :, ps(p)]
        cum = cum_all[:, ps(p)]
        kk = kkr[p] * lax.rsqrt(jnp.maximum(rows(ss, p), 1e-24))
        k2.append(k_ref[0, :, ps(p)] * (1.0 + (a - 1.0) * vec_ref[3:4, ps(p)]))
        e_pos = jnp.exp(cum)
        e_neg = jnp.exp(-cum)
        rt.append(r[p] * e_pos)
        at.append(-kk * jnp.exp(cum - lw_ref[0, :, ps(p)]))
        bt.append(kk * a * e_neg)
        kt.append(k2[p] * e_neg)
        w_end.append(e_pos[C - 1:C, :])

    P, U, m_rbk, x_r = [], [], [], []
    for p in pairs:
        ar = jnp.concatenate([at[p], rt[p]], axis=0)
        big = mm(ar, jnp.concatenate([bd(bt[p]), bd(kt[p])], axis=0), "nt")
        x = mm(ar, s_sc[p], "nt")
        P.append(jnp.where(strict, big[:C, :W], 0.0))
        l_ak = jnp.where(strict, big[:C, W:], 0.0)
        m_rbk.append(jnp.concatenate([jnp.where(incl, big[C:, :W], 0.0), jnp.where(incl, big[C:, W:], 0.0)], axis=1))
        U.append(x[:C] + mm(l_ak, bd(v[p])))
        x_r.append(x[C:])

    n_it = int(np.log2(C))
    for it in range(n_it):
        for p in pairs:
            if it == n_it - 1:
                U[p] = U[p] + mm_inv(P[p], bd(U[p]))
            else:
                res = mm_inv(P[p], jnp.concatenate([bd(P[p]), bd(U[p])], axis=1))
                U[p] = U[p] + res[:, W:]
                P[p] = res[:, :W]

    y = []
    for p in pairs:
        y.append(x_r[p] + mm(m_rbk[p], jnp.concatenate([bd(U[p]), bd(v[p])], axis=0)))
        upd = mm(jnp.concatenate([U[p], v[p]], axis=0),
                 jnp.concatenate([bt[p] * w_end[p], kt[p] * w_end[p]], axis=0), "tn")
        s_sc[p] = s_sc[p] * w_end[p] + jnp.where(diag_blocks, upd, 0.0)

    inv_n = 1.0 / RW_HEAD_DIM
    y_all = jnp.concatenate(y, axis=0)
    mu = seg_sum(y_all) * inv_n
    var = seg_sum(jnp.square(y_all - mu)) * inv_n
    rk = seg_sum(jnp.concatenate([r[p] * k2[p] * vec_ref[4:5, ps(p)] for p in pairs], axis=0))
    yn = (y_all - mu) * lax.rsqrt(var + RW_GN_EPS)
    for p in pairs:
        y_o[0, :, ps(p)] = rows(yn, p) * vec_ref[5:6, ps(p)] + vec_ref[6:7, ps(p)] + rows(rk, p) * v[p]

    @pl.when(c == pl.num_programs(1) - 1)
    def _():
        sT_o[0] = s_sc[...]


def _wkv(r, lw, k, v, a, vec, s0, *, hp, inv_hp):
    B, T, D = r.shape
    H = D // RW_HEAD_DIM
    C, W, n_pairs = WKV_CHUNK, 2 * RW_HEAD_DIM, H // 2
    s_bd = jnp.zeros((B, n_pairs, W, W), F32)
    if s0 is not None:
        s_bd = s_bd.at[:, :, :C, :C].set(s0[:, 0::2]).at[:, :, C:, C:].set(s0[:, 1::2])
    row_spec = pl.BlockSpec((1, C, D), lambda b, c: (b, c, 0))
    st_spec = pl.BlockSpec((1, n_pairs, W, W), lambda b, c: (b, 0, 0, 0))
    ones = _block_diag(W, RW_HEAD_DIM, 1.0)
    kern = functools.partial(_wkv_kernel, n_pairs=n_pairs, hp=hp, inv_hp=inv_hp)
    y, s_out = pl.pallas_call(
        kern, grid=(B, T // C),
        in_specs=[row_spec] * 5 + [pl.BlockSpec(vec.shape, lambda b, c: (0, 0)), pl.BlockSpec((W, W), lambda b, c: (0, 0)),
                                   st_spec],
        out_specs=[row_spec, st_spec],
        out_shape=[jax.ShapeDtypeStruct((B, T, D), F32), jax.ShapeDtypeStruct(s_bd.shape, F32)],
        scratch_shapes=[pltpu.VMEM((n_pairs, W, W), F32)],
        compiler_params=_cparams(("parallel", "arbitrary")), name="wkv_chunk",
    )(r, lw, k, v, a, vec, ones, s_bd)
    s_fin = jnp.stack([s_out[:, :, :C, :C], s_out[:, :, C:, C:]], axis=2).reshape(B, H, RW_HEAD_DIM, RW_HEAD_DIM)
    return y, s_fin


def _row_call(kernel, rows, consts, out_widths, out_dtypes, tm, name):
    n = rows[0].shape[0]
    row_spec = lambda w: pl.BlockSpec((tm, w), lambda i: (i, 0))
    full = lambda a: pl.BlockSpec(a.shape, lambda i: (0,) * a.ndim)
    return pl.pallas_call(
        kernel, grid=(n // tm,),
        in_specs=[row_spec(a.shape[1]) for a in rows] + [full(c) for c in consts],
        out_specs=[row_spec(w) for w in out_widths],
        out_shape=[jax.ShapeDtypeStruct((n, w), dt) for w, dt in zip(out_widths, out_dtypes)],
        compiler_params=_cparams(("parallel",)), name=name,
    )(*rows, *consts)


def _rwkv_out_kernel(x_ref, y_ref, g_ref, wo_ref, o_ref, *, hp):
    o_ref[...] = x_ref[...] + _mm(y_ref[...] * g_ref[...], wo_ref[...], hp=hp)


def _kv_proj_kernel(h_ref, g_ref, wkv_ref, kg_ref, bd_ref, *rest, hp, pair_out):
    if pair_out:
        pk_ref, pv_ref, ones_ref, cmp_o, slc_o, win_o = rest[:6]
        pair_outs = {1: rest[6:8], 2: rest[8:10]}
    else:
        cmp_o, slc_o, win_o = rest
    kv = _mm(_rms(h_ref[...], g_ref[...]), wkv_ref[...], hp=hp)
    cmp_o[...] = kv[:, :2 * KV_W]
    for br, out in ((1, slc_o), (2, win_o)):
        kraw = kv[:, br * 2 * KV_W: br * 2 * KV_W + KV_W]
        ms = _dot_hilo(kraw * kraw, bd_ref[...])
        kn = kraw * lax.rsqrt(ms + RMS_EPS) * kg_ref[br:br + 1, :]
        v = kv[:, br * 2 * KV_W + KV_W: (br + 1) * 2 * KV_W]
        out[:, :KV_W] = kn
        out[:, KV_W:] = v
        if pair_out:
            kab_o, v1_o = pair_outs[br]
            kab_o[...] = jnp.dot(kn.astype(BF16), pk_ref[...], preferred_element_type=F32).astype(BF16)
            v1_o[...] = (jnp.dot(v.astype(BF16), pv_ref[...], preferred_element_type=F32) + ones_ref[...]).astype(BF16)


def _q_proj_kernel(h_ref, g_ref, wq_ref, wg_ref, qg_ref, bd_ref, q_o, gate_o, *, hp, q_scale):
    hn = _rms(h_ref[...], g_ref[...])
    q = _mm(hn, wq_ref[...], hp=hp)
    ms = _dot_hilo(q * q, bd_ref[...])
    q_o[...] = (q * lax.rsqrt(ms + RMS_EPS) * qg_ref[...] * q_scale).astype(q_o.dtype)
    gate_o[...] = jax.nn.sigmoid(_mm(hn, wg_ref[...], hp=hp))


def _merge_kernel(h_ref, gate_ref, oc_ref, os_ref, ow_ref, eg_ref, wo_ref, o_ref, *, hp):
    D = h_ref.shape[1]
    ge = _dot_hilo(gate_ref[...], eg_ref[...])
    o = ge[:, :D] * oc_ref[...] + ge[:, D:2 * D] * os_ref[...] + ge[:, 2 * D:] * ow_ref[...]
    o_ref[...] = h_ref[...] + _mm(o, wo_ref[...], hp=hp)


def _router_kernel(h_ref, g_ref, wr_ref, xn_o, logit_o):
    xn = _rms(h_ref[...], g_ref[...])
    xn_o[...] = xn.astype(xn_o.dtype)
    logit_o[...] = _mm(xn, wr_ref[...], hp=True)


def _ffn_kernel(x_ref, g_ref, wg_ref, wu_ref, wd_ref, o_ref, xn_sc, acc_sc, *, hp):
    f = pl.program_id(1)

    @pl.when(f == 0)
    def _():
        xn_sc[...] = _rms(x_ref[...], g_ref[...]).astype(xn_sc.dtype)
        acc_sc[...] = jnp.zeros_like(acc_sc)

    xn = xn_sc[...]
    hid = _silu(_mm(xn, wg_ref[...], hp=hp)) * _mm(xn, wu_ref[...], hp=hp)
    acc_sc[...] += _mm(hid, wd_ref[...], hp=hp)

    @pl.when(f == pl.num_programs(1) - 1)
    def _():
        o_ref[...] = x_ref[...] + acc_sc[...]


def _ffn(x, g, wg, wu, wd, *, tm, tf, hp):
    n, D = x.shape
    dff = wg.shape[1]
    return pl.pallas_call(
        functools.partial(_ffn_kernel, hp=hp), grid=(n // tm, dff // tf),
        in_specs=[pl.BlockSpec((tm, D), lambda i, f: (i, 0)), pl.BlockSpec((1, D), lambda i, f: (0, 0)),
                  pl.BlockSpec((D, tf), lambda i, f: (0, f)), pl.BlockSpec((D, tf), lambda i, f: (0, f)),
                  pl.BlockSpec((tf, D), lambda i, f: (f, 0))],
        out_specs=pl.BlockSpec((tm, D), lambda i, f: (i, 0)),
        out_shape=jax.ShapeDtypeStruct((n, D), F32),
        scratch_shapes=[pltpu.VMEM((tm, D), F32 if hp else BF16), pltpu.VMEM((tm, D), F32)],
        compiler_params=_cparams(("parallel", "arbitrary")), name="ffn_dense",
    )(x, g, wg, wu, wd)


def _moe_ffn_kernel(be_ref, nb_ref, x_ref, wg_ref, wu_ref, wd_ref, o_ref, acc_sc, *, hp):
    i = pl.program_id(0)
    f = pl.program_id(1)

    @pl.when(i < nb_ref[0])
    def _():
        @pl.when(f == 0)
        def _():
            acc_sc[...] = jnp.zeros_like(acc_sc)

        x = x_ref[...]
        hid = _silu(_mm(x, wg_ref[0], hp=hp)) * _mm(x, wu_ref[0], hp=hp)
        acc_sc[...] += _mm(hid, wd_ref[0], hp=hp)

        @pl.when(f == pl.num_programs(1) - 1)
        def _():
            o_ref[...] = acc_sc[...]

    @pl.when((i >= nb_ref[0]) & (f == pl.num_programs(1) - 1))
    def _():
        o_ref[...] = jnp.zeros_like(o_ref)


def _moe_ffn(block_e, n_used, xs, wg, wu, wd, *, tm, tf, hp):
    cap, D = xs.shape
    dff = wg.shape[2]
    w_idx = lambda i, f, be, nb: jnp.where(i < nb[0], f, dff // tf - 1)
    grid_spec = pltpu.PrefetchScalarGridSpec(
        num_scalar_prefetch=2, grid=(cap // tm, dff // tf),
        in_specs=[pl.BlockSpec((tm, D), lambda i, f, be, nb: (i, 0)),
                  pl.BlockSpec((1, D, tf), lambda i, f, be, nb: (be[i], 0, w_idx(i, f, be, nb))),
                  pl.BlockSpec((1, D, tf), lambda i, f, be, nb: (be[i], 0, w_idx(i, f, be, nb))),
                  pl.BlockSpec((1, tf, D), lambda i, f, be, nb: (be[i], w_idx(i, f, be, nb), 0))],
        out_specs=pl.BlockSpec((tm, D), lambda i, f, be, nb: (i, 0)),
        scratch_shapes=[pltpu.VMEM((tm, D), F32)])
    return pl.pallas_call(
        functools.partial(_moe_ffn_kernel, hp=hp), grid_spec=grid_spec, out_shape=jax.ShapeDtypeStruct((cap, D), F32),
        compiler_params=_cparams(("arbitrary", "arbitrary")), name="moe_ffn",
    )(block_e, n_used, xs, wg, wu, wd)


def _compress_kernel(x_ref, pe_ref, w1_ref, w2_ref, kg_ref, o_ref, *, hp):
    c = pl.program_id(1)
    x = x_ref[0, 0, 0]
    R = x.shape[0]
    p0 = _mm(x + pe_ref[0, 0], w1_ref[0, 0], hp=hp)
    p1 = _mm(x + pe_ref[0, 1], w1_ref[0, 1], hp=hp)
    pre = p0 + pltpu.roll(p1, R - 1, 0)
    out = _mm(_silu(pre), w2_ref[0], hp=hp)
    o_ref[0, 0, 0] = jnp.where(c == 0, _rms(out, kg_ref[...]), out)


def _compress(xsub, pe, w1, w2, kg, *, hp):
    S, _, G, R, W = xsub.shape
    return pl.pallas_call(
        functools.partial(_compress_kernel, hp=hp), grid=(S, 2, G),
        in_specs=[pl.BlockSpec((1, 1, 1, R, W), lambda s, c, g: (s, c, g, 0, 0)),
                  pl.BlockSpec((1, 2, 1, W), lambda s, c, g: (c, 0, 0, 0)),
                  pl.BlockSpec((1, 2, W, HEAD_DIM), lambda s, c, g: (c, 0, 0, 0)),
                  pl.BlockSpec((1, HEAD_DIM, HEAD_DIM), lambda s, c, g: (c, 0, 0)),
                  pl.BlockSpec((1, HEAD_DIM), lambda s, c, g: (0, 0))],
        out_specs=pl.BlockSpec((1, 1, 1, R, HEAD_DIM), lambda s, c, g: (s, c, g, 0, 0)),
        out_shape=jax.ShapeDtypeStruct((S, 2, G, R, HEAD_DIM), F32),
        compiler_params=_cparams(("parallel", "arbitrary", "arbitrary")), name="kv_compress",
    )(xsub, pe, w1, w2, kg)


def _paged_sub_proj_kernel(pt_ref, cache_hbm, w_ref, o_ref, buf, sem, *, pages_per_step, page, hp):
    lin = pl.program_id(0) * pl.num_programs(1) + pl.program_id(1)
    total = pl.num_programs(0) * pl.num_programs(1)
    slot = lin % 2

    n_lane_pairs = buf.shape[1]

    def page_copies(step, into):
        return [pltpu.make_async_copy(cache_hbm.at[pt_ref[step * pages_per_step + p], pl.ds(0, page), pl.ds(pair * LANES, LANES)],
                                      buf.at[into, pair, pl.ds(p * page, page)], sem.at[into])
                for p in range(pages_per_step) for pair in range(n_lane_pairs)]

    @pl.when(lin == 0)
    def _():
        for cp in page_copies(0, 0):
            cp.start()

    @pl.when(lin + 1 < total)
    def _():
        for cp in page_copies(lin + 1, 1 - slot):
            cp.start()

    for cp in page_copies(lin, slot):
        cp.wait()
    n = pages_per_step * page // CMP_STRIDE
    for pair in range(n_lane_pairs):
        c = pair // (N_KV_HEADS // 2)
        rows = buf.at[slot, pair]
        x2 = jnp.concatenate([rows[pl.ds(s, n, stride=CMP_STRIDE), :] for s in range(CMP_STRIDE)], axis=1)
        o_ref[0, pair] = _mm(x2, w_ref[c], hp=hp and c == 0)


def _paged_sub_proj(page_table, cache, w2x, *, pages_per_step, hp):
    S, n_pages = page_table.shape
    page, width = cache.shape[1:]
    n = pages_per_step * page // CMP_STRIDE
    n_t = n_pages // pages_per_step
    grid_spec = pltpu.PrefetchScalarGridSpec(
        num_scalar_prefetch=1, grid=(S, n_t),
        in_specs=[pl.BlockSpec(memory_space=pl.ANY), pl.BlockSpec(w2x.shape, lambda s, t, pt: (0, 0, 0))],
        out_specs=pl.BlockSpec((1, 4, n, w2x.shape[2]), lambda s, t, pt: (s, 0, t, 0)),
        scratch_shapes=[pltpu.VMEM((2, width // LANES, pages_per_step * page, LANES), F32), pltpu.SemaphoreType.DMA((2,))])
    kern = functools.partial(_paged_sub_proj_kernel, pages_per_step=pages_per_step, page=page, hp=hp)
    return pl.pallas_call(
        kern, grid_spec=grid_spec, out_shape=jax.ShapeDtypeStruct((S, 4, n_t * n, w2x.shape[2]), F32),
        compiler_params=_cparams(("arbitrary", "arbitrary")), name="paged_sub_proj",
    )(page_table.reshape(-1), cache, w2x)


def _compress_tail_kernel(p_ref, pe_ref, w1_ref, w2_ref, kg_ref, o_ref, *, hp):
    R = p_ref.shape[2]
    half_groups = N_KV_HEADS // 2
    for c in range(2):
        bias = (_mm(jnp.broadcast_to(pe_ref[c, 0], (SUBLANES, pe_ref.shape[3])), w1_ref[c, 0], hp=True)
                + _mm(jnp.broadcast_to(pe_ref[c, 1], (SUBLANES, pe_ref.shape[3])), w1_ref[c, 1], hp=True))[0:1]
        outs = []
        for g in range(N_KV_HEADS):
            p = p_ref[0, c * half_groups + g // 2]
            e = g % 2
            p0 = p[:, e * HEAD_DIM:(e + 1) * HEAD_DIM]
            p1 = p[:, (2 + e) * HEAD_DIM:(3 + e) * HEAD_DIM]
            out = _mm(_silu(p0 + pltpu.roll(p1, R - 1, 0) + bias), w2_ref[c], hp=hp)
            outs.append(_rms(out, kg_ref[...]) if c == 0 else out)
        o_ref[0, c] = jnp.concatenate(outs, axis=1)


def _compress_tail(p, pe, w1, w2, kg, *, hp):
    S, _, R, _ = p.shape
    full = lambda a: pl.BlockSpec(a.shape, lambda s: (0,) * a.ndim)
    return pl.pallas_call(
        functools.partial(_compress_tail_kernel, hp=hp), grid=(S,),
        in_specs=[pl.BlockSpec((1,) + p.shape[1:], lambda s: (s, 0, 0, 0)), full(pe), full(w1), full(w2), full(kg)],
        out_specs=pl.BlockSpec((1, 2, R, KV_W), lambda s: (s, 0, 0, 0)),
        out_shape=jax.ShapeDtypeStruct((S, 2, R, KV_W), F32),
        compiler_params=_cparams(("parallel",)), name="kv_compress_tail",
    )(p, pe, w1, w2, kg)


def _stack_heads(q, g, tq):
    return jnp.concatenate([q[:, (HEADS_PER_KV * g + i) * HEAD_DIM:(HEADS_PER_KV * g + i + 1) * HEAD_DIM]
                            for i in range(HEADS_PER_KV)], axis=0)


def _cmp_attn_kernel(q_ref, kc_ref, vc_ref, m_ref, o_o, sel_o, *, n_cmp, n_slc, pos_base, per_tile, hp):
    tq = q_ref.shape[1]
    ncp = kc_ref.shape[1]
    nsp = m_ref.shape[1]
    q0 = pos_base + (pl.program_id(1) * tq if per_tile else 0)
    rows = HEADS_PER_KV * tq
    q_pos = q0 + lax.broadcasted_iota(jnp.int32, (rows, ncp), 0) % tq
    n_id = lax.broadcasted_iota(jnp.int32, (rows, ncp), 1)
    bias = jnp.where((n_id * CMP_STRIDE + CMP_BLOCK - 1 <= q_pos) & (n_id < n_cmp), 0.0, MASKED)
    blk = lax.broadcasted_iota(jnp.int32, (tq, nsp), 1)
    cur = (q0 + lax.broadcasted_iota(jnp.int32, (tq, nsp), 0)) // SLC_BLOCK
    forced = (blk == 0) | (blk == cur) | (blk == cur - 1)
    q = q_ref[0]
    scores = []
    for g in range(N_KV_HEADS):
        qs = _stack_heads(q, g, tq)
        s = _mm(qs, kc_ref[0, :, g * HEAD_DIM:(g + 1) * HEAD_DIM], "nt", hp) + bias
        m = jnp.maximum(jnp.max(s, -1, keepdims=True), M_INIT)
        e = jnp.exp(s - m)
        p = e / jnp.maximum(jnp.sum(e, -1, keepdims=True), 1e-30)
        o = _mm(p, vc_ref[0, :, g * HEAD_DIM:(g + 1) * HEAD_DIM], hp=hp)
        psum = p[0:tq]
        for i in range(HEADS_PER_KV):
            col = (HEADS_PER_KV * g + i) * HEAD_DIM
            o_o[0, :, col:col + HEAD_DIM] = o[i * tq:(i + 1) * tq]
            if i:
                psum = psum + p[i * tq:(i + 1) * tq]
        imp = _dot_hilo(psum, m_ref[...])
        scores.append(jnp.where(blk <= cur, jnp.where(forced, BIG, imp), -BIG))
    score = jnp.concatenate(scores, axis=0)
    blk_r = lax.broadcasted_iota(jnp.int32, score.shape, 1)

    def count_ahead(i, rank):
        s_i = jnp.sum(jnp.where(blk_r == i, score, 0.0), axis=-1, keepdims=True)
        return rank + jnp.where(s_i > score, 1.0, jnp.where((s_i == score) & (i < blk_r), 1.0, 0.0))

    rank = lax.fori_loop(0, n_slc, count_ahead, jnp.zeros(score.shape, F32), unroll=8)
    for g in range(N_KV_HEADS):
        sel_o[0, :, g * nsp:(g + 1) * nsp] = jnp.where(rank[g * tq:(g + 1) * tq] < min(SLC_TOP, n_slc), 1.0, 0.0).astype(sel_o.dtype)


def _cmp_attn(q, kc, vc, m, *, n_cmp, n_slc, pos_base, per_tile, tq, hp):
    B, T, D = q.shape
    ncp, nsp = m.shape
    kv_spec = pl.BlockSpec((1, ncp, KV_W), lambda b, i: (b, 0, 0))
    kern = functools.partial(_cmp_attn_kernel, n_cmp=n_cmp, n_slc=n_slc, pos_base=pos_base, per_tile=per_tile, hp=hp)
    return pl.pallas_call(
        kern, grid=(B, T // tq),
        in_specs=[pl.BlockSpec((1, tq, D), lambda b, i: (b, i, 0)), kv_spec, kv_spec,
                  pl.BlockSpec((ncp, nsp), lambda b, i: (0, 0))],
        out_specs=[pl.BlockSpec((1, tq, D), lambda b, i: (b, i, 0)),
                   pl.BlockSpec((1, tq, N_KV_HEADS * nsp), lambda b, i: (b, i, 0))],
        out_shape=[jax.ShapeDtypeStruct((B, T, D), F32), jax.ShapeDtypeStruct((B, T, N_KV_HEADS * nsp), BF16)],
        compiler_params=_cparams(("parallel", "arbitrary")), name="cmp_attn",
    )(q, kc, vc, m)


def _pair_rows(q, g):
    base = g * HEADS_PER_KV * HEAD_DIM
    return jnp.concatenate([q[:, base:base + LANES], q[:, base + LANES:base + 2 * LANES]], axis=0)


def _finish_pair(acc_even, acc_odd, p, tq):
    a0 = acc_even[p * tq:(p + 1) * tq]
    a1 = acc_odd[p * tq:(p + 1) * tq]
    o0 = a0 / jnp.maximum(a0[:, HEAD_DIM:HEAD_DIM + 1], 1e-30)
    o1 = a1 / jnp.maximum(a1[:, HEAD_DIM:HEAD_DIM + 1], 1e-30)
    lane = lax.broadcasted_iota(jnp.int32, (tq, LANES), 1)
    return jnp.where(lane < HEAD_DIM, o0, pltpu.roll(o1, HEAD_DIM, 1))


def _cmp_sel_kernel(q_ref, kab_ref, v1_ref, m_ref, o_o, sel_o, *, n_cmp, n_slc, bounded):
    tq = q_ref.shape[1]
    ncp = kab_ref.shape[1]
    nsp = LANES // N_KV_HEADS
    q0 = pl.program_id(1) * tq
    q_pos = q0 + lax.broadcasted_iota(jnp.int32, (tq, ncp), 0)
    n_id = lax.broadcasted_iota(jnp.int32, (tq, ncp), 1)
    b = jnp.where((n_id * CMP_STRIDE + CMP_BLOCK - 1 <= q_pos) & (n_id < n_cmp), 0.0, MASKED)
    b2 = jnp.concatenate([b, b], axis=0)
    q = q_ref[0]
    imp = jnp.zeros((tq, LANES), F32)
    for g in range(N_KV_HEADS):
        lhs = _pair_rows(q, g)
        vv = v1_ref[0, :, g * LANES:(g + 1) * LANES]
        accs, psum = [], None
        for half in range(2):
            s = _mm(lhs, kab_ref[0, :, (2 * g + half) * LANES:(2 * g + half + 1) * LANES], "nt") + b2
            e = jnp.exp2(s) if bounded else jnp.exp2(s - jnp.maximum(jnp.max(s, -1, keepdims=True), M_INIT))
            acc = _mm(e, vv)
            accs.append(acc)
            p = e / jnp.maximum(acc[:, HEAD_DIM:HEAD_DIM + 1], 1e-30)
            ph = p[:tq] + p[tq:]
            psum = ph if psum is None else psum + ph
        imp = imp + _dot_hilo(psum, m_ref[g])
        for p_ in range(2):
            col = g * 2 * LANES + p_ * LANES
            o_o[0, :, col:col + LANES] = _finish_pair(accs[0], accs[1], p_, tq)
    lane = lax.broadcasted_iota(jnp.int32, (tq, LANES), 1)
    blk = lane % nsp
    cur = (q0 + lax.broadcasted_iota(jnp.int32, (tq, LANES), 0)) // SLC_BLOCK
    forced = (blk == 0) | (blk == cur) | (blk == cur - 1)
    score = jnp.where(blk <= cur, jnp.where(forced, BIG, imp), -BIG)
    rank = jnp.zeros((tq, LANES), F32)
    for d in range(1, nsp):
        wrapped = blk + d >= nsp
        partner = jnp.where(wrapped, pltpu.roll(score, nsp - d, 1), pltpu.roll(score, LANES - d, 1))
        rank = rank + jnp.where(partner > score, 1.0, jnp.where((partner == score) & wrapped, 1.0, 0.0))
    sel_o[0] = jnp.where(rank < min(SLC_TOP, n_slc), 1.0, 0.0).astype(sel_o.dtype)


def _cmp_sel(q, kab, v1, m, *, n_cmp, n_slc, tq, bounded):
    B, T, D = q.shape
    ncp = kab.shape[1]
    assert m.shape == (N_KV_HEADS, ncp, LANES)
    whole = lambda a: pl.BlockSpec((1,) + a.shape[1:], lambda b, i: (b,) + (0,) * (a.ndim - 1))
    kern = functools.partial(_cmp_sel_kernel, n_cmp=n_cmp, n_slc=n_slc, bounded=bounded)
    return pl.pallas_call(
        kern, grid=(B, T // tq),
        in_specs=[pl.BlockSpec((1, tq, D), lambda b, i: (b, i, 0)), whole(kab), whole(v1),
                  pl.BlockSpec(m.shape, lambda b, i: (0, 0, 0))],
        out_specs=[pl.BlockSpec((1, tq, D), lambda b, i: (b, i, 0)), pl.BlockSpec((1, tq, LANES), lambda b, i: (b, i, 0))],
        out_shape=[jax.ShapeDtypeStruct((B, T, D), F32), jax.ShapeDtypeStruct((B, T, LANES), BF16)],
        compiler_params=_cparams(("parallel", "arbitrary")), name="cmp_sel_attn",
    )(q, kab, v1, m)


def _flash_pairs(lhs, kab_ref, v1_ref, kt_lo, kt_hi, tk, bias_fn, bounded):
    rows = lhs[0].shape[0]

    def body(kt, carry):
        k0 = pl.multiple_of(kt * tk, tk)
        biases = bias_fn(k0)
        out = []
        for g in range(N_KV_HEADS):
            vv = v1_ref[0, pl.ds(k0, tk), g * LANES:(g + 1) * LANES]
            for half in range(2):
                idx = 2 * g + half
                s = _mm(lhs[g], kab_ref[0, pl.ds(k0, tk), idx * LANES:(idx + 1) * LANES], "nt") + biases[g]
                if bounded:
                    out.append(carry[idx] + _mm(jnp.exp2(s), vv))
                else:
                    m, acc = carry[idx]
                    m_new = jnp.maximum(m, jnp.max(s, -1, keepdims=True))
                    out.append((m_new, jnp.exp2(m - m_new) * acc + _mm(jnp.exp2(s - m_new), vv)))
        return tuple(out)

    zero = jnp.zeros((rows, LANES), F32)
    if bounded:
        return lax.fori_loop(kt_lo, kt_hi, body, (zero,) * (2 * N_KV_HEADS))
    start = (jnp.full((rows, 1), M_INIT, F32), zero)
    return tuple(acc for _, acc in lax.fori_loop(kt_lo, kt_hi, body, (start,) * (2 * N_KV_HEADS)))


def _slc_win_attn_kernel(q_ref, sel_ref, skab_ref, sv1_ref, wkab_ref, wv1_ref, os_o, ow_o, *, tk, bounded):
    tq = q_ref.shape[1]
    nsp = sel_ref.shape[2] // N_KV_HEADS
    q0 = pl.program_id(1) * tq
    q_pos = q0 + lax.broadcasted_iota(jnp.int32, (tq, tk), 0)
    k_off = lax.broadcasted_iota(jnp.int32, (tq, tk), 1)
    sel_lane = lax.broadcasted_iota(jnp.int32, (N_KV_HEADS * nsp, tk), 0)
    blk_k = lax.broadcasted_iota(jnp.int32, (N_KV_HEADS * nsp, tk), 1)
    drop = jnp.where(sel_ref[0].astype(F32) > 0.5, 0.0, MASKED).astype(BF16)
    q = q_ref[0]
    kt_hi = (q0 + tq + tk - 1) // tk
    kt_lo_win = jnp.maximum(q0 - WINDOW + 1, 0) // tk

    twice = lambda b: jnp.concatenate([b, b], axis=0)

    def win_bias(k0):
        d = q_pos - (k_off + k0)
        return [twice(jnp.where((d >= 0) & (d < WINDOW), 0.0, MASKED))] * N_KV_HEADS

    def slc_bias(k0):
        causal = jnp.where(k_off + k0 <= q_pos, 0.0, MASKED)
        key_blk = (blk_k + k0) // SLC_BLOCK
        out = []
        for g in range(N_KV_HEADS):
            expand = jnp.where(g * nsp + key_blk == sel_lane, 1.0, 0.0).astype(BF16)
            out.append(twice(jnp.dot(drop, expand, preferred_element_type=F32) + causal))
        return out

    lhs = [_pair_rows(q, g) for g in range(N_KV_HEADS)]
    acc_s = _flash_pairs(lhs, skab_ref, sv1_ref, 0, kt_hi, tk, slc_bias, bounded)
    acc_w = _flash_pairs(lhs, wkab_ref, wv1_ref, kt_lo_win, kt_hi, tk, win_bias, bounded)
    for g in range(N_KV_HEADS):
        for p_ in range(2):
            col = g * 2 * LANES + p_ * LANES
            os_o[0, :, col:col + LANES] = _finish_pair(acc_s[2 * g], acc_s[2 * g + 1], p_, tq)
            ow_o[0, :, col:col + LANES] = _finish_pair(acc_w[2 * g], acc_w[2 * g + 1], p_, tq)


def _slc_win_attn(q, sel, skab, sv1, wkab, wv1, *, tq, tk, bounded):
    B, T, D = q.shape
    whole = lambda a: pl.BlockSpec((1,) + a.shape[1:], lambda b, i: (b, 0, 0))
    row = lambda w: pl.BlockSpec((1, tq, w), lambda b, i: (b, i, 0))
    return pl.pallas_call(
        functools.partial(_slc_win_attn_kernel, tk=tk, bounded=bounded), grid=(B, T // tq),
        in_specs=[row(D), row(sel.shape[2]), whole(skab), whole(sv1), whole(wkab), whole(wv1)],
        out_specs=[row(D), row(D)],
        out_shape=[jax.ShapeDtypeStruct((B, T, D), F32)] * 2,
        compiler_params=_cparams(("parallel", "arbitrary")), name="slc_win_attn",
    )(q, sel, skab, sv1, wkab, wv1)


def _pair_keys(k):
    z = jnp.zeros_like(k)
    return jnp.concatenate([k, z, z, k], axis=-1).reshape(k.shape[0], k.shape[1], -1)


def _ones_values(v):
    return jnp.concatenate([v, jnp.ones_like(v)], axis=-1).reshape(v.shape[0], v.shape[1], -1)


def _sample_attn_kernel(blk_ref, q_ref, kpos_ref, new_ref, win_ref, cache_hbm, os_o, ow_o, kbuf, vbuf, sem, *,
                        n_q, n_sel, pos_base, win_base, n_win, hp):
    b = pl.program_id(0)
    slot = b % 2
    per_seq = n_q * N_KV_HEADS * n_sel

    def start_copies(step, into):
        def one(i, carry):
            blk = jnp.maximum(blk_ref[step * per_seq + i], 0)
            qg = i // n_sel
            lane0 = pl.multiple_of(((qg % N_KV_HEADS) // 2) * LANES, LANES)
            for c, dst in ((0, kbuf), (1, vbuf)):
                pltpu.make_async_copy(cache_hbm.at[blk, :, pl.ds(c * KV_W + lane0, LANES)],
                                      dst.at[into, qg, pl.ds((i % n_sel) * SLC_BLOCK, SLC_BLOCK)], sem.at[into]).start()
            return carry
        lax.fori_loop(0, per_seq, one, 0)

    @pl.when(b == 0)
    def _():
        start_copies(0, 0)

    @pl.when(b + 1 < pl.num_programs(0))
    def _():
        start_copies(b + 1, 1 - slot)

    for dst in (kbuf, vbuf):
        pltpu.make_async_copy(dst.at[slot], dst.at[slot], sem.at[slot]).wait()

    def new_rows(i, carry):
        qg = i // n_sel
        win = (qg % N_KV_HEADS) // 2

        @pl.when(blk_ref[b * per_seq + i] < 0)
        def _():
            kbuf[slot, qg, pl.ds((i % n_sel) * SLC_BLOCK, SLC_BLOCK), :] = new_ref[0, win]
            vbuf[slot, qg, pl.ds((i % n_sel) * SLC_BLOCK, SLC_BLOCK), :] = new_ref[0, N_KV_HEADS // 2 + win]
        return carry
    lax.fori_loop(0, per_seq, new_rows, 0)

    tq = q_ref.shape[1]
    rows = HEADS_PER_KV * tq
    nk = n_sel * SLC_BLOCK
    nwp = win_ref.shape[1]
    q = q_ref[0]
    row_q = lax.broadcasted_iota(jnp.int32, (rows, 1), 0) % tq
    w_id = lax.broadcasted_iota(jnp.int32, (rows, nwp), 1)
    d_win = (pos_base + row_q) - (win_base + w_id)
    win_bias = jnp.where((d_win >= 0) & (d_win < WINDOW) & (win_base + w_id >= 0) & (w_id < n_win), 0.0, MASKED)

    def attend(s, v):
        m = jnp.maximum(jnp.max(s, -1, keepdims=True), M_INIT)
        e = jnp.exp(s - m)
        p = e / jnp.maximum(jnp.sum(e, -1, keepdims=True), 1e-30)
        return _mm(p, v, hp=hp)

    for g in range(N_KV_HEADS):
        qs = _stack_heads(q, g, tq)
        zeros = jnp.zeros_like(qs)
        qs_pair = jnp.concatenate([qs, zeros] if g % 2 == 0 else [zeros, qs], axis=1)
        lanes_g = slice((g % 2) * HEAD_DIM, (g % 2 + 1) * HEAD_DIM)
        o_s = jnp.zeros((rows, HEAD_DIM), F32)
        for qi in range(n_q):
            qg = qi * N_KV_HEADS + g
            kpos = kpos_ref[0, qg:qg + 1, :]
            bias = jnp.broadcast_to(jnp.where(kpos <= pos_base + qi, 0.0, MASKED), (rows, nk))
            o_qi = attend(_mm(qs_pair, kbuf[slot, qg], "nt", hp) + bias, vbuf[slot, qg])[:, lanes_g]
            o_s = jnp.where(row_q == qi, o_qi, o_s)
        o_w = attend(_mm(qs, win_ref[0, :, g * HEAD_DIM:(g + 1) * HEAD_DIM], "nt", hp) + win_bias,
                     win_ref[0, :, KV_W + g * HEAD_DIM:KV_W + (g + 1) * HEAD_DIM])
        for i in range(HEADS_PER_KV):
            col = (HEADS_PER_KV * g + i) * HEAD_DIM
            os_o[0, :, col:col + HEAD_DIM] = o_s[i * tq:(i + 1) * tq]
            ow_o[0, :, col:col + HEAD_DIM] = o_w[i * tq:(i + 1) * tq]


def _sample_attn(pool_blk, q, kpos, new_win, win_all, cache_rows, *, n_q, n_sel, pos_base, win_base, n_win, hp):
    B, tq, D = q.shape
    kern = functools.partial(_sample_attn_kernel, n_q=n_q, n_sel=n_sel, pos_base=pos_base, win_base=win_base,
                             n_win=n_win, hp=hp)
    blk = lambda a: pl.BlockSpec((1,) + a.shape[1:], lambda b, pb: (b,) + (0,) * (a.ndim - 1))
    buf = pltpu.VMEM((2, n_q * N_KV_HEADS, n_sel * SLC_BLOCK, LANES), F32)
    grid_spec = pltpu.PrefetchScalarGridSpec(
        num_scalar_prefetch=1, grid=(B,),
        in_specs=[blk(q), blk(kpos), blk(new_win), blk(win_all), pl.BlockSpec(memory_space=pl.ANY)],
        out_specs=[blk(q), blk(q)],
        scratch_shapes=[buf, buf, pltpu.SemaphoreType.DMA((2,))])
    return pl.pallas_call(
        kern, grid_spec=grid_spec, out_shape=[jax.ShapeDtypeStruct((B, tq, D), F32)] * 2,
        compiler_params=_cparams(("arbitrary",)), name="sample_slc_win_attn",
    )(pool_blk.reshape(-1), q, kpos, new_win, win_all, cache_rows)


def _pick_tile(n, cap=512):
    return next(t for t in (512, 256, 128, 64, 32, 16, 8) if t <= cap and n % t == 0)


def _pad_rows(x, t_pad):
    return jnp.pad(x, ((0, 0), (0, t_pad - x.shape[1]), (0, 0)))


def _rwkv_layer(x, shift0, wkv0, p, *, tm, hp, inv_hp):
    B, T, D = x.shape
    t_pad = -(-T // tm) * tm
    xp = _pad_rows(x, t_pad)
    r, lw, k, v, a, g, shift = _rwkv_proj(xp, shift0[:, None, :], p["g"], p["mix"], p["vec"], p["wrkv"], p["dw1"],
                                          p["dw2"], p["aw1"], p["aw2"], p["gw1"], p["gw2"], t_real=T, tm=tm, hp=hp)
    if t_pad != T:
        live = (jnp.arange(t_pad) < T)[None, :, None]
        r, lw, k, v, a = (jnp.where(live, t, 0.0) for t in (r, lw, k, v, a))
    y, s_fin = _wkv(r, lw, k, v, a, p["vec"], wkv0, hp=hp, inv_hp=inv_hp)
    n = B * t_pad
    out = _row_call(functools.partial(_rwkv_out_kernel, hp=hp), [xp.reshape(n, D), y.reshape(n, D), g.reshape(n, D)],
                    [p["wo"]], [D], [F32], _pick_tile(n), "rwkv_out")[0]
    return out.reshape(B, t_pad, D)[:, :T], shift[:, 0], s_fin


def _sub_blocks(rows):
    S, L = rows.shape[:2]
    n_sub = L // CMP_STRIDE
    r = rows[:, :n_sub * CMP_STRIDE].reshape(S, n_sub, CMP_STRIDE, 2, N_KV_HEADS, HEAD_DIM)
    r = jnp.transpose(r, (0, 3, 4, 1, 2, 5)).reshape(S, 2, N_KV_HEADS, n_sub, CMP_STRIDE * HEAD_DIM)
    r_pad = -(-n_sub // SUBLANES) * SUBLANES
    return jnp.pad(r, ((0, 0),) * 3 + ((0, r_pad - n_sub), (0, 0))), n_sub - CMP_BLOCK // CMP_STRIDE + 1


def _compressed_kv(rows, p, *, hp):
    xsub, n_cmp = _sub_blocks(rows)
    out = _compress(xsub, p["cmp_pe"], p["cmp_w1"], p["cmp_w2"], p["k_norm_g"][0:1], hp=hp)
    S, _, G, R, dh = out.shape
    out = jnp.transpose(out, (1, 0, 3, 2, 4)).reshape(2, S, R, G * dh).astype(F32 if hp else BF16)
    return out[0], out[1], n_cmp


def _overlap_matrix(ncp, n_cmp, n_slc, nsp):
    c0 = np.arange(ncp)[:, None] * CMP_STRIDE
    s0 = np.arange(nsp)[None, :] * SLC_BLOCK
    m = (c0 < s0 + SLC_BLOCK) & (c0 + CMP_BLOCK > s0) & (np.arange(ncp)[:, None] < n_cmp) & (np.arange(nsp)[None, :] < n_slc)
    return jnp.asarray(m.astype(np.float32), dtype=BF16)


def _moe(h, p, *, tm, tf, hp):
    n, D = h.shape
    xn, logits = _row_call(_router_kernel, [h], [p["g_moe"], p["w_router"]], [D, LANES], [F32 if hp else BF16, F32],
                           _pick_tile(n), "moe_router")
    top_v, top_e = lax.top_k(logits[:, :N_EXPERTS], TOP_K)
    gates = jax.nn.softmax(top_v, -1)
    n_pairs = n * TOP_K
    flat_e = top_e.reshape(n_pairs)
    order = jnp.argsort(flat_e)
    seen = jnp.cumsum((flat_e[:, None] == jnp.arange(N_EXPERTS)).astype(jnp.int32), axis=0)
    counts = seen[-1]
    padded = (counts + tm - 1) // tm * tm
    pad_end = jnp.cumsum(padded)
    pad_start = pad_end - padded
    start = jnp.cumsum(counts) - counts
    rank = jnp.take_along_axis(seen, flat_e[:, None], axis=1)[:, 0] - 1
    dest = (pad_start[flat_e] + rank).reshape(n, TOP_K)
    n_blocks = -(-n_pairs // tm) + N_EXPERTS
    cap = n_blocks * tm
    block_e = jnp.minimum(jnp.searchsorted(pad_end, jnp.arange(n_blocks) * tm, side="right"), N_EXPERTS - 1).astype(jnp.int32)
    row_e = jnp.repeat(block_e, tm)
    k_in_e = jnp.arange(cap, dtype=jnp.int32) - pad_start[row_e]
    src = order[jnp.clip(start[row_e] + k_in_e, 0, n_pairs - 1)] // TOP_K
    buf_tok = jnp.where((k_in_e >= 0) & (k_in_e < counts[row_e]), src, n).astype(jnp.int32)
    n_used = (pad_end[-1] // tm).astype(jnp.int32).reshape(1)
    xs = xn[jnp.minimum(buf_tok, n - 1)]
    ys = _moe_ffn(block_e, n_used, xs, p["moe_wg"], p["moe_wu"], p["moe_wd"], tm=tm, tf=tf, hp=hp)
    y = ys[dest[:, 0]] * gates[:, 0:1] + ys[dest[:, 1]] * gates[:, 1:2]
    return h + y


def kernel(x_prompt, x_sample, cache_cmp_kv, cache_slc_kv, state_win_kv, state_wkv, state_shift, page_table, norm_g, rw_mix, rw_vec, rw_w_rkv, rw_w_o, rw_decay_w1, rw_decay_w2, rw_iclr_w1, rw_iclr_w2, rw_gate_w1, rw_gate_w2, ffn_w_gate, ffn_w_up, ffn_w_down, moe_router, moe_w_gate, moe_w_up, moe_w_down, kv_norm_g, w_kv, k_norm_g, cmp_pe, cmp_w1, cmp_w2, w_qg, q_norm_g, w_o):
    B, T, D = x_prompt.shape
    SB, S, _ = x_sample.shape
    assert norm_g.shape[0] == 2 and rw_mix.shape[0] == 1 and w_qg.shape[0] == 1, "one RWKV-7 layer + one NSA layer"
    assert D == N_HEADS * HEAD_DIM and T % 512 == 0 and S <= SUBLANES
    page = cache_cmp_kv.shape[1]
    past = page_table.shape[1] * page
    wb = state_win_kv.shape[1]
    assert past % SLC_BLOCK == 0 and page % SLC_BLOCK == 0 and S <= SLC_BLOCK
    bf = lambda w: w.astype(BF16)
    pad8 = lambda m: jnp.pad(m, ((0, SUBLANES - m.shape[0]), (0, 0)))
    dff = ffn_w_gate.shape[2]
    tf = dff // 2 if (dff // 2) % LANES == 0 else dff
    n_p, n_s = B * T, SB * S
    both = lambda w: (bf(w), w)
    pick = lambda d, hp: {k: (v[hp] if isinstance(v, tuple) else v) for k, v in d.items()}

    rw = dict(g=norm_g[0, 0][None], mix=pad8(rw_mix[0]), vec=pad8(rw_vec[0]), wrkv=both(rw_w_rkv[0]), wo=both(rw_w_o[0]),
              dw1=both(rw_decay_w1[0]), dw2=both(rw_decay_w2[0]), aw1=both(rw_iclr_w1[0]), aw2=both(rw_iclr_w2[0]),
              gw1=both(rw_gate_w1[0]), gw2=both(rw_gate_w2[0]))
    H = D // RW_HEAD_DIM
    hp_, shift_p, wkv_p = _rwkv_layer(x_prompt, jnp.zeros((B, D), F32), None,
                                      pick(rw, 0), tm=256, hp=False, inv_hp=False)
    hs, shift_s, wkv_s = _rwkv_layer(x_sample, state_shift[0], state_wkv[0], pick(rw, 1), tm=WKV_CHUNK, hp=True, inv_hp=True)
    ffn_g = norm_g[0, 1][None]
    hp_ = _ffn(hp_.reshape(n_p, D), ffn_g, bf(ffn_w_gate[0]), bf(ffn_w_up[0]), bf(ffn_w_down[0]), tm=512, tf=dff, hp=False)
    hs = _ffn(hs.reshape(n_s, D), ffn_g, ffn_w_gate[0], ffn_w_up[0], ffn_w_down[0], tm=n_s, tf=tf, hp=True)

    kv_consts = lambda hp: [kv_norm_g[None], both(w_kv)[hp], jnp.tile(k_norm_g, (1, N_KV_HEADS)),
                            _block_diag(KV_W, HEAD_DIM, 1.0 / HEAD_DIM)]
    place_k = np.zeros((KV_W, N_KV_HEADS * 4 * HEAD_DIM), np.float32)
    place_v = np.zeros((KV_W, N_KV_HEADS * 2 * HEAD_DIM), np.float32)
    ones_v = np.zeros((1, N_KV_HEADS * 2 * HEAD_DIM), np.float32)
    for g in range(N_KV_HEADS):
        for d in range(HEAD_DIM):
            place_k[g * HEAD_DIM + d, g * 4 * HEAD_DIM + d] = 1.0
            place_k[g * HEAD_DIM + d, g * 4 * HEAD_DIM + 3 * HEAD_DIM + d] = 1.0
            place_v[g * HEAD_DIM + d, g * 2 * HEAD_DIM + d] = 1.0
        ones_v[0, g * 2 * HEAD_DIM + HEAD_DIM:(g + 1) * 2 * HEAD_DIM] = 1.0
    pair_consts = [jnp.asarray(place_k, dtype=BF16), jnp.asarray(place_v, dtype=BF16), jnp.asarray(ones_v)]
    as_rows = lambda t, b: t.reshape(b, -1, 2, N_KV_HEADS, HEAD_DIM)
    kv_p = _row_call(functools.partial(_kv_proj_kernel, hp=False, pair_out=True), [hp_], kv_consts(False) + pair_consts,
                     [2 * KV_W] * 3 + [4 * KV_W, 2 * KV_W] * 2, [F32] * 3 + [BF16] * 4, 512, "kv_proj")
    cmp_p, slc_p, win_p = (as_rows(t, B) for t in kv_p[:3])
    skab, sv1, wkab, wv1 = (t.reshape(B, T, -1) for t in kv_p[3:])
    cmp_s, slc_s, win_s = (as_rows(t, SB) for t in _row_call(
        functools.partial(_kv_proj_kernel, hp=True, pair_out=False), [hs], kv_consts(True), [2 * KV_W] * 3, [F32] * 3,
        n_s, "kv_proj"))
    win_all_s = jnp.concatenate([state_win_kv, win_s], axis=1)

    cp = dict(cmp_pe=cmp_pe.reshape(2, CMP_BLOCK // CMP_STRIDE, 1, CMP_STRIDE * HEAD_DIM),
              cmp_w1=both(cmp_w1.reshape(2, CMP_BLOCK // CMP_STRIDE, CMP_STRIDE * HEAD_DIM, HEAD_DIM)),
              cmp_w2=both(cmp_w2), k_norm_g=k_norm_g)
    kc_p, vc_p, ncmp_p = _compressed_kv(cmp_p, pick(cp, 0), hp=False)
    assert S < CMP_STRIDE and page % CMP_STRIDE == 0
    n_ratio = CMP_BLOCK // CMP_STRIDE
    w1r = cmp_w1.reshape(2, n_ratio, CMP_STRIDE, HEAD_DIM, HEAD_DIM)
    w2x = jnp.einsum("cjsdh,ef->csedjfh", w1r, jnp.eye(2, dtype=F32)).reshape(2, CMP_STRIDE * LANES, n_ratio * LANES)
    n_pages = page_table.shape[1]
    pages_per_step = next(pp for pp in (32, 16, 8, 4, 2, 1) if n_pages % pp == 0)
    part_s = _paged_sub_proj(page_table, cache_cmp_kv.reshape(-1, page, 2 * KV_W), w2x, pages_per_step=pages_per_step, hp=True)
    kcv_s = _compress_tail(part_s, cp["cmp_pe"], cp["cmp_w1"][1], cmp_w2, k_norm_g[0:1], hp=True)
    kc_s, vc_s, ncmp_s = kcv_s[:, 0], kcv_s[:, 1], past // CMP_STRIDE - n_ratio + 1

    ge = np.zeros((LANES, N_BRANCH * D), np.float32)
    for br in range(N_BRANCH):
        for hd in range(N_HEADS):
            ge[br * N_HEADS + hd, br * D + hd * HEAD_DIM: br * D + (hd + 1) * HEAD_DIM] = 1.0
    nsa = dict(g_attn=norm_g[1, 0][None], wq=both(w_qg[0, :, :D]),
               wgate=both(jnp.pad(w_qg[0, :, D:], ((0, 0), (0, LANES - N_BRANCH * N_HEADS)))),
               q_norm_g=jnp.tile(q_norm_g[0], N_HEADS)[None], bd_q=_block_diag(D, HEAD_DIM, 1.0 / HEAD_DIM),
               w_o=both(w_o[0]), gate_expand=jnp.asarray(ge, dtype=BF16))

    def query_side(h, tm, hp, q_scale):
        c = pick(nsa, hp)
        return _row_call(functools.partial(_q_proj_kernel, hp=hp, q_scale=q_scale), [h],
                         [c["g_attn"], c["wq"], c["wgate"], c["q_norm_g"], c["bd_q"]], [D, LANES],
                         [F32 if hp else BF16, F32], tm, "q_proj")

    def merge(h, gates, o_cmp, o_slc, o_win, tm, hp):
        c = pick(nsa, hp)
        return _row_call(functools.partial(_merge_kernel, hp=hp), [h, gates, o_cmp, o_slc, o_win],
                         [c["gate_expand"], c["w_o"]], [D], [F32], tm, "nsa_merge")[0]

    q_p, gates_p = query_side(hp_, 512, False, ATTN_SCALE * LOG2E)
    q_p = q_p.reshape(B, T, D)
    n_slc_p = T // SLC_BLOCK
    nsp_p = LANES // N_KV_HEADS
    assert n_slc_p <= nsp_p, "the packed block-selection layout holds at most 32 selection blocks per kv group"
    m_one = _overlap_matrix(kc_p.shape[1], ncmp_p, n_slc_p, nsp_p)
    m_p = jnp.stack([jnp.pad(m_one, ((0, 0), (g * nsp_p, LANES - (g + 1) * nsp_p))) for g in range(N_KV_HEADS)])
    heads = lambda t: t.reshape(t.shape[0], t.shape[1], N_KV_HEADS, HEAD_DIM)
    kcab, vc1 = _pair_keys(heads(kc_p)), _ones_values(heads(vc_p))

    def prompt_branches(bounded):
        o_cmp, sel = _cmp_sel(q_p, kcab, vc1, m_p, n_cmp=ncmp_p, n_slc=n_slc_p, tq=256, bounded=bounded)
        return (o_cmp,) + tuple(_slc_win_attn(q_p, sel, skab, sv1, wkab, wv1, tq=256, tk=256, bounded=bounded))

    score_bound = HEAD_DIM * ATTN_SCALE * jnp.max(jnp.abs(q_norm_g[0])) * jnp.max(jnp.abs(k_norm_g))
    o_cmp_p, o_slc_p, o_win_p = lax.cond(score_bound <= SCORE_BOUND_MAX, lambda: prompt_branches(True),
                                         lambda: prompt_branches(False))
    hp_ = merge(hp_, gates_p, o_cmp_p.reshape(n_p, D), o_slc_p.reshape(n_p, D), o_win_p.reshape(n_p, D), 512, False)

    kv_lanes = lambda t, dt: t.reshape(t.shape[0], t.shape[1], 2 * KV_W).astype(dt)
    q_s, gates_s = query_side(hs, n_s, True, ATTN_SCALE)
    q_s = _pad_rows(q_s.reshape(SB, S, D), SUBLANES)
    n_slc_s = (past + S - 1) // SLC_BLOCK + 1
    nsp_s = -(-n_slc_s // LANES) * LANES
    m_s = _overlap_matrix(kc_s.shape[1], ncmp_s, n_slc_s, nsp_s)
    o_cmp_s, sel_s = _cmp_attn(q_s, kc_s, vc_s, m_s, n_cmp=ncmp_s, n_slc=n_slc_s, pos_base=past, per_tile=False,
                               tq=SUBLANES, hp=True)
    n_sel = min(SLC_TOP, n_slc_s)
    sel_s = sel_s.reshape(SB, SUBLANES, N_KV_HEADS, nsp_s)[:, :S, :, :n_slc_s] > 0.5
    slot = jnp.cumsum(sel_s, axis=-1) - 1
    idx_s = jnp.sum(jnp.where(sel_s[..., None] & (slot[..., None] == jnp.arange(n_sel)),
                              jnp.arange(n_slc_s)[:, None], 0), axis=-2).astype(jnp.int32)
    in_pool = idx_s < past // SLC_BLOCK
    per_page = page // SLC_BLOCK
    blk_c = jnp.minimum(idx_s, past // SLC_BLOCK - 1)
    pages = jnp.take_along_axis(page_table, (blk_c // per_page).reshape(SB, -1), axis=1).reshape(idx_s.shape)
    pool_blk = jnp.where(in_pool, pages * per_page + blk_c % per_page, -1).astype(jnp.int32)
    new_rows = slc_s[:, jnp.clip(jnp.arange(SLC_BLOCK), 0, S - 1)]
    new_win = jnp.transpose(new_rows.reshape(SB, SLC_BLOCK, 2, N_KV_HEADS // 2, LANES), (0, 2, 3, 1, 4))
    new_win = new_win.reshape(SB, N_KV_HEADS, SLC_BLOCK, LANES)
    kpos = (idx_s[..., None] * SLC_BLOCK + jnp.arange(SLC_BLOCK)).reshape(SB, S * N_KV_HEADS, n_sel * SLC_BLOCK).astype(jnp.int32)
    kpos = jnp.pad(kpos, ((0, 0), (0, -(-S * N_KV_HEADS // SUBLANES) * SUBLANES - S * N_KV_HEADS), (0, 0)))
    cache_rows = cache_slc_kv.reshape(-1, SLC_BLOCK, 2 * KV_W)
    n_win = wb + S
    nwp = -(-n_win // SUBLANES) * SUBLANES
    win_lanes = _pad_rows(kv_lanes(win_all_s, F32), nwp)
    o_slc_s, o_win_s = _sample_attn(pool_blk, q_s, kpos, new_win, win_lanes, cache_rows, n_q=S, n_sel=n_sel,
                                    pos_base=past, win_base=past - wb, n_win=n_win, hp=True)
    unpad = lambda t: t[:, :S].reshape(n_s, D)
    hs = merge(hs, gates_s, unpad(o_cmp_s), unpad(o_slc_s), unpad(o_win_s), n_s, True)

    moe = dict(g_moe=norm_g[1, 1][None], w_router=jnp.pad(moe_router[0], ((0, 0), (0, LANES - N_EXPERTS))),
               moe_wg=both(moe_w_gate[0]), moe_wu=both(moe_w_up[0]), moe_wd=both(moe_w_down[0]))
    hp_ = _moe(hp_, pick(moe, 0), tm=512, tf=dff, hp=False).reshape(B, T, D)
    hs = _moe(hs, pick(moe, 1), tm=128, tf=tf, hp=True).reshape(SB, S, D)

    keep_p = min(WINDOW, T)
    return (hp_, hs, cmp_p, cmp_s, slc_p, slc_s, win_p[:, T - keep_p:], win_all_s[:, win_all_s.shape[1] - wb:],
            wkv_p[None], wkv_s[None], shift_p[None], shift_s[None])
```

```python
import functools

import numpy as np
import jax
import jax.numpy as jnp
from jax import lax
from jax.experimental import pallas as pl
from jax.experimental.pallas import tpu as pltpu

F32 = jnp.float32
BF16 = jnp.bfloat16

RW_HEAD_DIM = 64
RW_GN_EPS = 64e-5
N_HEADS = 16
HEAD_DIM = 64
N_KV_HEADS = 4
HEADS_PER_KV = N_HEADS // N_KV_HEADS
KV_W = N_KV_HEADS * HEAD_DIM
N_BRANCH = 3
CMP_BLOCK = 32
CMP_STRIDE = 16
SLC_BLOCK = 64
SLC_TOP = 16
WINDOW = 512
ATTN_SCALE = HEAD_DIM ** -0.5
N_EXPERTS = 8
TOP_K = 2
RMS_EPS = 1e-6
MASKED = -1e30
M_INIT = -1e20
BIG = 1e30
LOG2E = 1.4426950408889634
SCORE_BOUND_MAX = 40.0

SUBLANES = 8
LANES = 128
VMEM_LIMIT = 56 * 1024 * 1024
WKV_CHUNK = RW_HEAD_DIM


def _cparams(sem):
    return pltpu.CompilerParams(dimension_semantics=sem, vmem_limit_bytes=VMEM_LIMIT)


_DIMS = {"nn": (((1,), (0,)), ((), ())), "nt": (((1,), (1,)), ((), ())), "tn": (((0,), (0,)), ((), ()))}


def _split2(x):
    hi = x.astype(BF16)
    lo = (x - hi.astype(F32)).astype(BF16)
    return hi, lo


def _mm(a, b, form="nn", hp=False):
    d = lambda s, t: lax.dot_general(s, t, _DIMS[form], preferred_element_type=F32)
    if not hp:
        return d(a.astype(BF16), b.astype(BF16))
    ah, al = _split2(a.astype(F32))
    bh, bl = _split2(b.astype(F32))
    return d(ah, bh) + d(ah, bl) + d(al, bh)


def _dot_hilo(x, m):
    hi, lo = _split2(x)
    return jnp.dot(hi, m, preferred_element_type=F32) + jnp.dot(lo, m, preferred_element_type=F32)


def _dot_exact_rhs(m, x):
    hi = x.astype(BF16)
    r1 = x - hi.astype(F32)
    mid = r1.astype(BF16)
    lo = (r1 - mid.astype(F32)).astype(BF16)
    d = lambda t: jnp.dot(m, t, preferred_element_type=F32)
    return d(hi) + d(mid) + d(lo)


def _rms(x, g):
    return x * lax.rsqrt(jnp.mean(x * x, -1, keepdims=True) + RMS_EPS) * g


def _silu(x):
    return x * jax.nn.sigmoid(x)


def _softplus(z):
    return jnp.maximum(z, 0.0) + jnp.log(1.0 + jnp.exp(-jnp.abs(z)))


def _block_diag(n, seg, value):
    i = np.arange(n)
    return jnp.asarray((i[:, None] // seg == i[None, :] // seg).astype(np.float32) * value, dtype=BF16)


def _rwkv_proj_kernel(x_ref, halo_ref, sh_ref, g_ref, mix_ref, vec_ref, wrkv_ref, dw1_ref, dw2_ref, aw1_ref, aw2_ref,
                      gw1_ref, gw2_ref, r_o, lw_o, k_o, v_o, a_o, g_o, shift_o, *, last_tile, last_row, hp):
    i = pl.program_id(1)
    mm = functools.partial(_mm, hp=hp)
    g = g_ref[...]
    xn = _rms(x_ref[0], g)
    hn = _rms(halo_ref[0, SUBLANES - 1:SUBLANES, :], g)
    prev_last = jnp.where(i == 0, sh_ref[0], hn)
    row = lax.broadcasted_iota(jnp.int32, xn.shape, 0)
    prev = jnp.where(row == 0, prev_last, pltpu.roll(xn, 1, 0))
    xx = prev - xn
    mixed = lambda j: xn + xx * mix_ref[j:j + 1, :]
    r_o[0] = mm(mixed(0), wrkv_ref[0])
    k_o[0] = mm(mixed(2), wrkv_ref[1])
    v_o[0] = mm(mixed(3), wrkv_ref[2])
    w0 = vec_ref[0:1, :]
    a0 = vec_ref[1:2, :]
    log_w = -_softplus(-(w0 + mm(jnp.tanh(mm(mixed(1), dw1_ref[...])), dw2_ref[...]))) - 0.5
    lw_o[0] = -jnp.exp(log_w)
    a_o[0] = jax.nn.sigmoid(a0 + mm(mm(mixed(4), aw1_ref[...]), aw2_ref[...]))
    g_o[0] = mm(jax.nn.sigmoid(mm(mixed(5), gw1_ref[...])), gw2_ref[...])

    @pl.when(i == last_tile)
    def _():
        shift_o[0] = xn[last_row:last_row + 1, :]


def _rwkv_proj(x, shift0, g, mix, vec, wrkv, dw1, dw2, aw1, aw2, gw1, gw2, *, t_real, tm, hp):
    B, T, D = x.shape
    nt = T // tm
    row_spec = pl.BlockSpec((1, tm, D), lambda b, i: (b, i, 0))
    halo_spec = pl.BlockSpec((1, SUBLANES, D), lambda b, i: (b, jnp.maximum(i * (tm // SUBLANES) - 1, 0), 0))
    vec_spec = pl.BlockSpec((1, 1, D), lambda b, i: (b, 0, 0))
    full = lambda a: pl.BlockSpec(a.shape, lambda b, i: (0,) * a.ndim)
    consts = (g, mix, vec, wrkv, dw1, dw2, aw1, aw2, gw1, gw2)
    out_sd = jax.ShapeDtypeStruct((B, T, D), F32)
    kern = functools.partial(_rwkv_proj_kernel, last_tile=(t_real - 1) // tm, last_row=(t_real - 1) % tm, hp=hp)
    return pl.pallas_call(
        kern, grid=(B, nt),
        in_specs=[row_spec, halo_spec, vec_spec] + [full(c) for c in consts],
        out_specs=[row_spec] * 6 + [vec_spec],
        out_shape=[out_sd] * 6 + [jax.ShapeDtypeStruct((B, 1, D), F32)],
        compiler_params=_cparams(("parallel", "arbitrary")), name="rwkv_proj",
    )(x, x, shift0, *consts)


def _wkv_kernel(r_ref, lw_ref, k_ref, v_ref, a_ref, vec_ref, ones_ref, s0_ref, y_o, sT_o, s_sc, *, n_pairs, hp, inv_hp):
    c = pl.program_id(1)
    C = r_ref.shape[1]
    W = 2 * C
    mm = functools.partial(_mm, hp=hp)
    mm_inv = functools.partial(_mm, hp=inv_hp)

    @pl.when(c == 0)
    def _():
        s_sc[...] = s0_ref[0]

    lane = lax.broadcasted_iota(jnp.int32, (C, W), 1)
    row = lax.broadcasted_iota(jnp.int32, (C, W), 0)
    left = lane < C
    col = jnp.where(left, lane, lane - C)
    strict = col < row
    incl = col <= row
    diag_blocks = (lax.broadcasted_iota(jnp.int32, (W, W), 0) < C) == (lax.broadcasted_iota(jnp.int32, (W, W), 1) < C)
    bd = lambda x: jnp.concatenate([jnp.where(left, x, 0.0), jnp.where(left, 0.0, x)], axis=0)
    seg_sum = lambda x: _dot_hilo(x, ones_ref[...])
    rows = lambda x, p: x[p * C:(p + 1) * C]
    tri = jnp.where(lax.broadcasted_iota(jnp.int32, (C, C), 1) <= lax.broadcasted_iota(jnp.int32, (C, C), 0), 1.0, 0.0)
    cum_all = _dot_exact_rhs(tri.astype(BF16), lw_ref[0])
    pairs = range(n_pairs)
    ps = lambda p: slice(p * W, (p + 1) * W)

    r = [r_ref[0, :, ps(p)] for p in pairs]
    v = [v_ref[0, :, ps(p)] for p in pairs]
    kkr = [k_ref[0, :, ps(p)] * vec_ref[2:3, ps(p)] for p in pairs]
    ss = seg_sum(jnp.concatenate([x * x for x in kkr], axis=0))
    k2, at, rt, bt, kt, w_end = [], [], [], [], [], []
    for p in pairs:
        a = a_ref[0, :, ps(p)]
        cum = cum_all[:, ps(p)]
        kk = kkr[p] * lax.rsqrt(jnp.maximum(rows(ss, p), 1e-24))
        k2.append(k_ref[0, :, ps(p)] * (1.0 + (a - 1.0) * vec_ref[3:4, ps(p)]))
        e_pos = jnp.exp(cum)
        e_neg = jnp.exp(-cum)
        rt.append(r[p] * e_pos)
        at.append(-kk * jnp.exp(cum - lw_ref[0, :, ps(p)]))
        bt.append(kk * a * e_neg)
        kt.append(k2[p] * e_neg)
        w_end.append(e_pos[C - 1:C, :])

    P, U, m_rbk, x_r = [], [], [], []
    for p in pairs:
        ar = jnp.concatenate([at[p], rt[p]], axis=0)
        big = mm(ar, jnp.concatenate([bd(bt[p]), bd(kt[p])], axis=0), "nt")
        x = mm(ar, s_sc[p], "nt")
        P.append(jnp.where(strict, big[:C, :W], 0.0))
        l_ak = jnp.where(strict, big[:C, W:], 0.0)
        m_rbk.append(jnp.concatenate([jnp.where(incl, big[C:, :W], 0.0), jnp.where(incl, big[C:, W:], 0.0)], axis=1))
        U.append(x[:C] + mm(l_ak, bd(v[p])))
        x_r.append(x[C:])

    n_it = int(np.log2(C))
    for it in range(n_it):
        for p in pairs:
            if it == n_it - 1:
                U[p] = U[p] + mm_inv(P[p], bd(U[p]))
            else:
                res = mm_inv(P[p], jnp.concatenate([bd(P[p]), bd(U[p])], axis=1))
                U[p] = U[p] + res[:, W:]
                P[p] = res[:, :W]

    y = []
    for p in pairs:
        y.append(x_r[p] + mm(m_rbk[p], jnp.concatenate([bd(U[p]), bd(v[p])], axis=0)))
        upd = mm(jnp.concatenate([U[p], v[p]], axis=0),
                 jnp.concatenate([bt[p] * w_end[p], kt[p] * w_end[p]], axis=0), "tn")
        s_sc[p] = s_sc[p] * w_end[p] + jnp.where(diag_blocks, upd, 0.0)

    inv_n = 1.0 / RW_HEAD_DIM
    y_all = jnp.concatenate(y, axis=0)
    mu = seg_sum(y_all) * inv_n
    var = seg_sum(jnp.square(y_all - mu)) * inv_n
    rk = seg_sum(jnp.concatenate([r[p] * k2[p] * vec_ref[4:5, ps(p)] for p in pairs], axis=0))
    yn = (y_all - mu) * lax.rsqrt(var + RW_GN_EPS)
    for p in pairs:
        y_o[0, :, ps(p)] = rows(yn, p) * vec_ref[5:6, ps(p)] + vec_ref[6:7, ps(p)] + rows(rk, p) * v[p]

    @pl.when(c == pl.num_programs(1) - 1)
    def _():
        sT_o[0] = s_sc[...]


def _wkv(r, lw, k, v, a, vec, s0, *, hp, inv_hp):
    B, T, D = r.shape
    H = D // RW_HEAD_DIM
    C, W, n_pairs = WKV_CHUNK, 2 * RW_HEAD_DIM, H // 2
    if s0 is None:
        s_bd = jnp.zeros((B, n_pairs, W, W), F32)
    else:
        z = jnp.zeros((B, n_pairs, C, C), F32)
        s_bd = jnp.concatenate([jnp.concatenate([s0[:, 0::2], z], axis=-1), jnp.concatenate([z, s0[:, 1::2]], axis=-1)], axis=-2)
    row_spec = pl.BlockSpec((1, C, D), lambda b, c: (b, c, 0))
    st_spec = pl.BlockSpec((1, n_pairs, W, W), lambda b, c: (b, 0, 0, 0))
    ones = _block_diag(W, RW_HEAD_DIM, 1.0)
    kern = functools.partial(_wkv_kernel, n_pairs=n_pairs, hp=hp, inv_hp=inv_hp)
    y, s_out = pl.pallas_call(
        kern, grid=(B, T // C),
        in_specs=[row_spec] * 5 + [pl.BlockSpec(vec.shape, lambda b, c: (0, 0)), pl.BlockSpec((W, W), lambda b, c: (0, 0)),
                                   st_spec],
        out_specs=[row_spec, st_spec],
        out_shape=[jax.ShapeDtypeStruct((B, T, D), F32), jax.ShapeDtypeStruct(s_bd.shape, F32)],
        scratch_shapes=[pltpu.VMEM((n_pairs, W, W), F32)],
        compiler_params=_cparams(("parallel", "arbitrary")), name="wkv_chunk",
    )(r, lw, k, v, a, vec, ones, s_bd)
    s_fin = jnp.stack([s_out[:, :, :C, :C], s_out[:, :, C:, C:]], axis=2).reshape(B, H, RW_HEAD_DIM, RW_HEAD_DIM)
    return y, s_fin


def _row_call(kernel, rows, consts, out_widths, out_dtypes, tm, name):
    n = rows[0].shape[0]
    row_spec = lambda w: pl.BlockSpec((tm, w), lambda i: (i, 0))
    full = lambda a: pl.BlockSpec(a.shape, lambda i: (0,) * a.ndim)
    return pl.pallas_call(
        kernel, grid=(n // tm,),
        in_specs=[row_spec(a.shape[1]) for a in rows] + [full(c) for c in consts],
        out_specs=[row_spec(w) for w in out_widths],
        out_shape=[jax.ShapeDtypeStruct((n, w), dt) for w, dt in zip(out_widths, out_dtypes)],
        compiler_params=_cparams(("parallel",)), name=name,
    )(*rows, *consts)


def _rwkv_out_kernel(x_ref, y_ref, g_ref, wo_ref, o_ref, *, hp):
    o_ref[...] = x_ref[...] + _mm(y_ref[...] * g_ref[...], wo_ref[...], hp=hp)


def _kv_proj_kernel(h_ref, g_ref, wkv_ref, kg_ref, bd_ref, *rest, hp, pair_out):
    if pair_out:
        pk_ref, pv_ref, ones_ref, cmp_o, slc_o, win_o = rest[:6]
        pair_outs = {1: rest[6:8], 2: rest[8:10]}
    else:
        cmp_o, slc_o, win_o = rest
    kv = _mm(_rms(h_ref[...], g_ref[...]), wkv_ref[...], hp=hp)
    cmp_o[...] = kv[:, :2 * KV_W]
    for br, out in ((1, slc_o), (2, win_o)):
        kraw = kv[:, br * 2 * KV_W: br * 2 * KV_W + KV_W]
        ms = _dot_hilo(kraw * kraw, bd_ref[...])
        kn = kraw * lax.rsqrt(ms + RMS_EPS) * kg_ref[br:br + 1, :]
        v = kv[:, br * 2 * KV_W + KV_W: (br + 1) * 2 * KV_W]
        out[:, :KV_W] = kn
        out[:, KV_W:] = v
        if pair_out:
            kab_o, v1_o = pair_outs[br]
            kab_o[...] = jnp.dot(kn.astype(BF16), pk_ref[...], preferred_element_type=F32).astype(BF16)
            v1_o[...] = (jnp.dot(v.astype(BF16), pv_ref[...], preferred_element_type=F32) + ones_ref[...]).astype(BF16)


def _q_proj_kernel(h_ref, g_ref, wq_ref, wg_ref, qg_ref, bd_ref, q_o, gate_o, *, hp, q_scale):
    hn = _rms(h_ref[...], g_ref[...])
    q = _mm(hn, wq_ref[...], hp=hp)
    ms = _dot_hilo(q * q, bd_ref[...])
    q_o[...] = (q * lax.rsqrt(ms + RMS_EPS) * qg_ref[...] * q_scale).astype(q_o.dtype)
    gate_o[...] = jax.nn.sigmoid(_mm(hn, wg_ref[...], hp=hp))


def _merge_kernel(h_ref, gate_ref, oc_ref, os_ref, ow_ref, eg_ref, wo_ref, o_ref, *, hp):
    D = h_ref.shape[1]
    ge = _dot_hilo(gate_ref[...], eg_ref[...])
    o = ge[:, :D] * oc_ref[...] + ge[:, D:2 * D] * os_ref[...] + ge[:, 2 * D:] * ow_ref[...]
    o_ref[...] = h_ref[...] + _mm(o, wo_ref[...], hp=hp)


def _router_kernel(h_ref, g_ref, wr_ref, xn_o, logit_o):
    xn = _rms(h_ref[...], g_ref[...])
    xn_o[...] = xn.astype(xn_o.dtype)
    logit_o[...] = _mm(xn, wr_ref[...], hp=True)


def _ffn_kernel(x_ref, g_ref, wg_ref, wu_ref, wd_ref, o_ref, xn_sc, acc_sc, *, hp):
    f = pl.program_id(1)

    @pl.when(f == 0)
    def _():
        xn_sc[...] = _rms(x_ref[...], g_ref[...]).astype(xn_sc.dtype)
        acc_sc[...] = jnp.zeros_like(acc_sc)

    xn = xn_sc[...]
    hid = _silu(_mm(xn, wg_ref[...], hp=hp)) * _mm(xn, wu_ref[...], hp=hp)
    acc_sc[...] += _mm(hid, wd_ref[...], hp=hp)

    @pl.when(f == pl.num_programs(1) - 1)
    def _():
        o_ref[...] = x_ref[...] + acc_sc[...]


def _ffn(x, g, wg, wu, wd, *, tm, tf, hp):
    n, D = x.shape
    dff = wg.shape[1]
    return pl.pallas_call(
        functools.partial(_ffn_kernel, hp=hp), grid=(n // tm, dff // tf),
        in_specs=[pl.BlockSpec((tm, D), lambda i, f: (i, 0)), pl.BlockSpec((1, D), lambda i, f: (0, 0)),
                  pl.BlockSpec((D, tf), lambda i, f: (0, f)), pl.BlockSpec((D, tf), lambda i, f: (0, f)),
                  pl.BlockSpec((tf, D), lambda i, f: (f, 0))],
        out_specs=pl.BlockSpec((tm, D), lambda i, f: (i, 0)),
        out_shape=jax.ShapeDtypeStruct((n, D), F32),
        scratch_shapes=[pltpu.VMEM((tm, D), F32 if hp else BF16), pltpu.VMEM((tm, D), F32)],
        compiler_params=_cparams(("parallel", "arbitrary")), name="ffn_dense",
    )(x, g, wg, wu, wd)


def _moe_ffn_kernel(be_ref, nb_ref, x_ref, wg_ref, wu_ref, wd_ref, o_ref, acc_sc, *, hp):
    i = pl.program_id(0)
    f = pl.program_id(1)

    @pl.when(i < nb_ref[0])
    def _():
        @pl.when(f == 0)
        def _():
            acc_sc[...] = jnp.zeros_like(acc_sc)

        x = x_ref[...]
        hid = _silu(_mm(x, wg_ref[0], hp=hp)) * _mm(x, wu_ref[0], hp=hp)
        acc_sc[...] += _mm(hid, wd_ref[0], hp=hp)

        @pl.when(f == pl.num_programs(1) - 1)
        def _():
            o_ref[...] = acc_sc[...]

    @pl.when((i >= nb_ref[0]) & (f == pl.num_programs(1) - 1))
    def _():
        o_ref[...] = jnp.zeros_like(o_ref)


def _moe_ffn(block_e, n_used, xs, wg, wu, wd, *, tm, tf, hp):
    cap, D = xs.shape
    dff = wg.shape[2]
    w_idx = lambda i, f, be, nb: jnp.where(i < nb[0], f, dff // tf - 1)
    grid_spec = pltpu.PrefetchScalarGridSpec(
        num_scalar_prefetch=2, grid=(cap // tm, dff // tf),
        in_specs=[pl.BlockSpec((tm, D), lambda i, f, be, nb: (i, 0)),
                  pl.BlockSpec((1, D, tf), lambda i, f, be, nb: (be[i], 0, w_idx(i, f, be, nb))),
                  pl.BlockSpec((1, D, tf), lambda i, f, be, nb: (be[i], 0, w_idx(i, f, be, nb))),
                  pl.BlockSpec((1, tf, D), lambda i, f, be, nb: (be[i], w_idx(i, f, be, nb), 0))],
        out_specs=pl.BlockSpec((tm, D), lambda i, f, be, nb: (i, 0)),
        scratch_shapes=[pltpu.VMEM((tm, D), F32)])
    return pl.pallas_call(
        functools.partial(_moe_ffn_kernel, hp=hp), grid_spec=grid_spec, out_shape=jax.ShapeDtypeStruct((cap, D), F32),
        compiler_params=_cparams(("arbitrary", "arbitrary")), name="moe_ffn",
    )(block_e, n_used, xs, wg, wu, wd)


def _compress_kernel(x_ref, pe_ref, w1_ref, w2_ref, kg_ref, o_ref, *, hp):
    c = pl.program_id(1)
    x = x_ref[0, 0, 0]
    R = x.shape[0]
    p0 = _mm(x + pe_ref[0, 0], w1_ref[0, 0], hp=hp)
    p1 = _mm(x + pe_ref[0, 1], w1_ref[0, 1], hp=hp)
    pre = p0 + pltpu.roll(p1, R - 1, 0)
    out = _mm(_silu(pre), w2_ref[0], hp=hp)
    o_ref[0, 0, 0] = jnp.where(c == 0, _rms(out, kg_ref[...]), out)


def _compress(xsub, pe, w1, w2, kg, *, hp):
    S, _, G, R, W = xsub.shape
    return pl.pallas_call(
        functools.partial(_compress_kernel, hp=hp), grid=(S, 2, G),
        in_specs=[pl.BlockSpec((1, 1, 1, R, W), lambda s, c, g: (s, c, g, 0, 0)),
                  pl.BlockSpec((1, 2, 1, W), lambda s, c, g: (c, 0, 0, 0)),
                  pl.BlockSpec((1, 2, W, HEAD_DIM), lambda s, c, g: (c, 0, 0, 0)),
                  pl.BlockSpec((1, HEAD_DIM, HEAD_DIM), lambda s, c, g: (c, 0, 0)),
                  pl.BlockSpec((1, HEAD_DIM), lambda s, c, g: (0, 0))],
        out_specs=pl.BlockSpec((1, 1, 1, R, HEAD_DIM), lambda s, c, g: (s, c, g, 0, 0)),
        out_shape=jax.ShapeDtypeStruct((S, 2, G, R, HEAD_DIM), F32),
        compiler_params=_cparams(("parallel", "arbitrary", "arbitrary")), name="kv_compress",
    )(xsub, pe, w1, w2, kg)


def _paged_sub_proj_kernel(pt_ref, cache_hbm, w_ref, o_ref, buf, sem, *, pages_per_step, page, hp):
    lin = pl.program_id(0) * pl.num_programs(1) + pl.program_id(1)
    total = pl.num_programs(0) * pl.num_programs(1)
    slot = lin % 2

    n_lane_pairs = buf.shape[1]

    def page_copies(step, into):
        return [pltpu.make_async_copy(cache_hbm.at[pt_ref[step * pages_per_step + p], pl.ds(0, page), pl.ds(pair * LANES, LANES)],
                                      buf.at[into, pair, pl.ds(p * page, page)], sem.at[into])
                for p in range(pages_per_step) for pair in range(n_lane_pairs)]

    @pl.when(lin == 0)
    def _():
        for cp in page_copies(0, 0):
            cp.start()

    @pl.when(lin + 1 < total)
    def _():
        for cp in page_copies(lin + 1, 1 - slot):
            cp.start()

    for cp in page_copies(lin, slot):
        cp.wait()
    n = pages_per_step * page // CMP_STRIDE
    for pair in range(n_lane_pairs):
        c = pair // (N_KV_HEADS // 2)
        rows = buf.at[slot, pair]
        x2 = jnp.concatenate([rows[pl.ds(s, n, stride=CMP_STRIDE), :] for s in range(CMP_STRIDE)], axis=1)
        o_ref[0, pair] = _mm(x2, w_ref[c], hp=hp and c == 0)


def _paged_sub_proj(page_table, cache, w2x, *, pages_per_step, hp):
    S, n_pages = page_table.shape
    page, width = cache.shape[1:]
    n = pages_per_step * page // CMP_STRIDE
    n_t = n_pages // pages_per_step
    grid_spec = pltpu.PrefetchScalarGridSpec(
        num_scalar_prefetch=1, grid=(S, n_t),
        in_specs=[pl.BlockSpec(memory_space=pl.ANY), pl.BlockSpec(w2x.shape, lambda s, t, pt: (0, 0, 0))],
        out_specs=pl.BlockSpec((1, 4, n, w2x.shape[2]), lambda s, t, pt: (s, 0, t, 0)),
        scratch_shapes=[pltpu.VMEM((2, width // LANES, pages_per_step * page, LANES), F32), pltpu.SemaphoreType.DMA((2,))])
    kern = functools.partial(_paged_sub_proj_kernel, pages_per_step=pages_per_step, page=page, hp=hp)
    return pl.pallas_call(
        kern, grid_spec=grid_spec, out_shape=jax.ShapeDtypeStruct((S, 4, n_t * n, w2x.shape[2]), F32),
        compiler_params=_cparams(("arbitrary", "arbitrary")), name="paged_sub_proj",
    )(page_table.reshape(-1), cache, w2x)


def _compress_tail_kernel(p_ref, pe_ref, w1_ref, w2_ref, kg_ref, o_ref, *, hp):
    R = p_ref.shape[2]
    half_groups = N_KV_HEADS // 2
    for c in range(2):
        bias = (_mm(jnp.broadcast_to(pe_ref[c, 0], (SUBLANES, pe_ref.shape[3])), w1_ref[c, 0], hp=True)
                + _mm(jnp.broadcast_to(pe_ref[c, 1], (SUBLANES, pe_ref.shape[3])), w1_ref[c, 1], hp=True))[0:1]
        outs = []
        for g in range(N_KV_HEADS):
            p = p_ref[0, c * half_groups + g // 2]
            e = g % 2
            p0 = p[:, e * HEAD_DIM:(e + 1) * HEAD_DIM]
            p1 = p[:, (2 + e) * HEAD_DIM:(3 + e) * HEAD_DIM]
            out = _mm(_silu(p0 + pltpu.roll(p1, R - 1, 0) + bias), w2_ref[c], hp=hp)
            outs.append(_rms(out, kg_ref[...]) if c == 0 else out)
        o_ref[0, c] = jnp.concatenate(outs, axis=1)


def _compress_tail(p, pe, w1, w2, kg, *, hp):
    S, _, R, _ = p.shape
    full = lambda a: pl.BlockSpec(a.shape, lambda s: (0,) * a.ndim)
    return pl.pallas_call(
        functools.partial(_compress_tail_kernel, hp=hp), grid=(S,),
        in_specs=[pl.BlockSpec((1,) + p.shape[1:], lambda s: (s, 0, 0, 0)), full(pe), full(w1), full(w2), full(kg)],
        out_specs=pl.BlockSpec((1, 2, R, KV_W), lambda s: (s, 0, 0, 0)),
        out_shape=jax.ShapeDtypeStruct((S, 2, R, KV_W), F32),
        compiler_params=_cparams(("parallel",)), name="kv_compress_tail",
    )(p, pe, w1, w2, kg)


def _stack_heads(q, g, tq):
    return jnp.concatenate([q[:, (HEADS_PER_KV * g + i) * HEAD_DIM:(HEADS_PER_KV * g + i + 1) * HEAD_DIM]
                            for i in range(HEADS_PER_KV)], axis=0)


def _cmp_attn_kernel(q_ref, kc_ref, vc_ref, m_ref, o_o, sel_o, *, n_cmp, n_slc, pos_base, per_tile, hp):
    tq = q_ref.shape[1]
    ncp = kc_ref.shape[1]
    nsp = m_ref.shape[1]
    q0 = pos_base + (pl.program_id(1) * tq if per_tile else 0)
    rows = HEADS_PER_KV * tq
    q_pos = q0 + lax.broadcasted_iota(jnp.int32, (rows, ncp), 0) % tq
    n_id = lax.broadcasted_iota(jnp.int32, (rows, ncp), 1)
    bias = jnp.where((n_id * CMP_STRIDE + CMP_BLOCK - 1 <= q_pos) & (n_id < n_cmp), 0.0, MASKED)
    blk = lax.broadcasted_iota(jnp.int32, (tq, nsp), 1)
    cur = (q0 + lax.broadcasted_iota(jnp.int32, (tq, nsp), 0)) // SLC_BLOCK
    forced = (blk == 0) | (blk == cur) | (blk == cur - 1)
    q = q_ref[0]
    scores = []
    for g in range(N_KV_HEADS):
        qs = _stack_heads(q, g, tq)
        s = _mm(qs, kc_ref[0, :, g * HEAD_DIM:(g + 1) * HEAD_DIM], "nt", hp) + bias
        m = jnp.maximum(jnp.max(s, -1, keepdims=True), M_INIT)
        e = jnp.exp(s - m)
        p = e / jnp.maximum(jnp.sum(e, -1, keepdims=True), 1e-30)
        o = _mm(p, vc_ref[0, :, g * HEAD_DIM:(g + 1) * HEAD_DIM], hp=hp)
        psum = p[0:tq]
        for i in range(HEADS_PER_KV):
            col = (HEADS_PER_KV * g + i) * HEAD_DIM
            o_o[0, :, col:col + HEAD_DIM] = o[i * tq:(i + 1) * tq]
            if i:
                psum = psum + p[i * tq:(i + 1) * tq]
        imp = _dot_hilo(psum, m_ref[...])
        scores.append(jnp.where(blk <= cur, jnp.where(forced, BIG, imp), -BIG))
    score = jnp.concatenate(scores, axis=0)
    blk_r = lax.broadcasted_iota(jnp.int32, score.shape, 1)

    def count_ahead(i, rank):
        s_i = jnp.sum(jnp.where(blk_r == i, score, 0.0), axis=-1, keepdims=True)
        return rank + jnp.where(s_i > score, 1.0, jnp.where((s_i == score) & (i < blk_r), 1.0, 0.0))

    rank = lax.fori_loop(0, n_slc, count_ahead, jnp.zeros(score.shape, F32), unroll=8)
    for g in range(N_KV_HEADS):
        sel_o[0, :, g * nsp:(g + 1) * nsp] = jnp.where(rank[g * tq:(g + 1) * tq] < min(SLC_TOP, n_slc), 1.0, 0.0).astype(sel_o.dtype)


def _cmp_attn(q, kc, vc, m, *, n_cmp, n_slc, pos_base, per_tile, tq, hp):
    B, T, D = q.shape
    ncp, nsp = m.shape
    kv_spec = pl.BlockSpec((1, ncp, KV_W), lambda b, i: (b, 0, 0))
    kern = functools.partial(_cmp_attn_kernel, n_cmp=n_cmp, n_slc=n_slc, pos_base=pos_base, per_tile=per_tile, hp=hp)
    return pl.pallas_call(
        kern, grid=(B, T // tq),
        in_specs=[pl.BlockSpec((1, tq, D), lambda b, i: (b, i, 0)), kv_spec, kv_spec,
                  pl.BlockSpec((ncp, nsp), lambda b, i: (0, 0))],
        out_specs=[pl.BlockSpec((1, tq, D), lambda b, i: (b, i, 0)),
                   pl.BlockSpec((1, tq, N_KV_HEADS * nsp), lambda b, i: (b, i, 0))],
        out_shape=[jax.ShapeDtypeStruct((B, T, D), F32), jax.ShapeDtypeStruct((B, T, N_KV_HEADS * nsp), BF16)],
        compiler_params=_cparams(("parallel", "arbitrary")), name="cmp_attn",
    )(q, kc, vc, m)


def _pair_rows(q, g):
    base = g * HEADS_PER_KV * HEAD_DIM
    return jnp.concatenate([q[:, base:base + LANES], q[:, base + LANES:base + 2 * LANES]], axis=0)


def _finish_pair(acc_even, acc_odd, p, tq):
    a0 = acc_even[p * tq:(p + 1) * tq]
    a1 = acc_odd[p * tq:(p + 1) * tq]
    o0 = a0 / jnp.maximum(a0[:, HEAD_DIM:HEAD_DIM + 1], 1e-30)
    o1 = a1 / jnp.maximum(a1[:, HEAD_DIM:HEAD_DIM + 1], 1e-30)
    lane = lax.broadcasted_iota(jnp.int32, (tq, LANES), 1)
    return jnp.where(lane < HEAD_DIM, o0, pltpu.roll(o1, HEAD_DIM, 1))


def _cmp_sel_kernel(q_ref, kab_ref, v1_ref, m_ref, o_o, sel_o, *, n_cmp, n_slc, bounded):
    tq = q_ref.shape[1]
    ncp = kab_ref.shape[1]
    nsp = LANES // N_KV_HEADS
    q0 = pl.program_id(1) * tq
    q_pos = q0 + lax.broadcasted_iota(jnp.int32, (tq, ncp), 0)
    n_id = lax.broadcasted_iota(jnp.int32, (tq, ncp), 1)
    b = jnp.where((n_id * CMP_STRIDE + CMP_BLOCK - 1 <= q_pos) & (n_id < n_cmp), 0.0, MASKED)
    b2 = jnp.concatenate([b, b], axis=0)
    q = q_ref[0]
    imp = jnp.zeros((tq, LANES), F32)
    for g in range(N_KV_HEADS):
        lhs = _pair_rows(q, g)
        vv = v1_ref[0, :, g * LANES:(g + 1) * LANES]
        accs, psum = [], None
        for half in range(2):
            s = _mm(lhs, kab_ref[0, :, (2 * g + half) * LANES:(2 * g + half + 1) * LANES], "nt") + b2
            e = jnp.exp2(s) if bounded else jnp.exp2(s - jnp.maximum(jnp.max(s, -1, keepdims=True), M_INIT))
            acc = _mm(e, vv)
            accs.append(acc)
            p = e / jnp.maximum(acc[:, HEAD_DIM:HEAD_DIM + 1], 1e-30)
            ph = p[:tq] + p[tq:]
            psum = ph if psum is None else psum + ph
        imp = imp + _dot_hilo(psum, m_ref[g])
        for p_ in range(2):
            col = g * 2 * LANES + p_ * LANES
            o_o[0, :, col:col + LANES] = _finish_pair(accs[0], accs[1], p_, tq)
    lane = lax.broadcasted_iota(jnp.int32, (tq, LANES), 1)
    blk = lane % nsp
    cur = (q0 + lax.broadcasted_iota(jnp.int32, (tq, LANES), 0)) // SLC_BLOCK
    forced = (blk == 0) | (blk == cur) | (blk == cur - 1)
    score = jnp.where(blk <= cur, jnp.where(forced, BIG, imp), -BIG)
    rank = jnp.zeros((tq, LANES), F32)
    for d in range(1, nsp):
        wrapped = blk + d >= nsp
        partner = jnp.where(wrapped, pltpu.roll(score, nsp - d, 1), pltpu.roll(score, LANES - d, 1))
        rank = rank + jnp.where(partner > score, 1.0, jnp.where((partner == score) & wrapped, 1.0, 0.0))
    sel_o[0] = jnp.where(rank < min(SLC_TOP, n_slc), 1.0, 0.0).astype(sel_o.dtype)


def _cmp_sel(q, kab, v1, m, *, n_cmp, n_slc, tq, bounded):
    B, T, D = q.shape
    ncp = kab.shape[1]
    assert m.shape == (N_KV_HEADS, ncp, LANES)
    whole = lambda a: pl.BlockSpec((1,) + a.shape[1:], lambda b, i: (b,) + (0,) * (a.ndim - 1))
    kern = functools.partial(_cmp_sel_kernel, n_cmp=n_cmp, n_slc=n_slc, bounded=bounded)
    return pl.pallas_call(
        kern, grid=(B, T // tq),
        in_specs=[pl.BlockSpec((1, tq, D), lambda b, i: (b, i, 0)), whole(kab), whole(v1),
                  pl.BlockSpec(m.shape, lambda b, i: (0, 0, 0))],
        out_specs=[pl.BlockSpec((1, tq, D), lambda b, i: (b, i, 0)), pl.BlockSpec((1, tq, LANES), lambda b, i: (b, i, 0))],
        out_shape=[jax.ShapeDtypeStruct((B, T, D), F32), jax.ShapeDtypeStruct((B, T, LANES), BF16)],
        compiler_params=_cparams(("parallel", "arbitrary")), name="cmp_sel_attn",
    )(q, kab, v1, m)


def _flash_pairs(lhs, kab_ref, v1_ref, kt_lo, kt_hi, tk, bias_fn, bounded):
    rows = lhs[0].shape[0]

    def body(kt, carry):
        k0 = pl.multiple_of(kt * tk, tk)
        biases = bias_fn(k0)
        out = []
        for g in range(N_KV_HEADS):
            vv = v1_ref[0, pl.ds(k0, tk), g * LANES:(g + 1) * LANES]
            for half in range(2):
                idx = 2 * g + half
                s = _mm(lhs[g], kab_ref[0, pl.ds(k0, tk), idx * LANES:(idx + 1) * LANES], "nt") + biases[g]
                if bounded:
                    out.append(carry[idx] + _mm(jnp.exp2(s), vv))
                else:
                    m, acc = carry[idx]
                    m_new = jnp.maximum(m, jnp.max(s, -1, keepdims=True))
                    out.append((m_new, jnp.exp2(m - m_new) * acc + _mm(jnp.exp2(s - m_new), vv)))
        return tuple(out)

    zero = jnp.zeros((rows, LANES), F32)
    if bounded:
        return lax.fori_loop(kt_lo, kt_hi, body, (zero,) * (2 * N_KV_HEADS))
    start = (jnp.full((rows, 1), M_INIT, F32), zero)
    return tuple(acc for _, acc in lax.fori_loop(kt_lo, kt_hi, body, (start,) * (2 * N_KV_HEADS)))


def _slc_win_attn_kernel(q_ref, sel_ref, skab_ref, sv1_ref, wkab_ref, wv1_ref, os_o, ow_o, *, tk, bounded):
    tq = q_ref.shape[1]
    nsp = sel_ref.shape[2] // N_KV_HEADS
    q0 = pl.program_id(1) * tq
    q_pos = q0 + lax.broadcasted_iota(jnp.int32, (tq, tk), 0)
    k_off = lax.broadcasted_iota(jnp.int32, (tq, tk), 1)
    sel_lane = lax.broadcasted_iota(jnp.int32, (N_KV_HEADS * nsp, tk), 0)
    blk_k = lax.broadcasted_iota(jnp.int32, (N_KV_HEADS * nsp, tk), 1)
    drop = jnp.where(sel_ref[0].astype(F32) > 0.5, 0.0, MASKED).astype(BF16)
    q = q_ref[0]
    kt_hi = (q0 + tq + tk - 1) // tk
    kt_lo_win = jnp.maximum(q0 - WINDOW + 1, 0) // tk

    twice = lambda b: jnp.concatenate([b, b], axis=0)

    def win_bias(k0):
        d = q_pos - (k_off + k0)
        return [twice(jnp.where((d >= 0) & (d < WINDOW), 0.0, MASKED))] * N_KV_HEADS

    def slc_bias(k0):
        causal = jnp.where(k_off + k0 <= q_pos, 0.0, MASKED)
        key_blk = (blk_k + k0) // SLC_BLOCK
        out = []
        for g in range(N_KV_HEADS):
            expand = jnp.where(g * nsp + key_blk == sel_lane, 1.0, 0.0).astype(BF16)
            out.append(twice(jnp.dot(drop, expand, preferred_element_type=F32) + causal))
        return out

    lhs = [_pair_rows(q, g) for g in range(N_KV_HEADS)]
    acc_s = _flash_pairs(lhs, skab_ref, sv1_ref, 0, kt_hi, tk, slc_bias, bounded)
    acc_w = _flash_pairs(lhs, wkab_ref, wv1_ref, kt_lo_win, kt_hi, tk, win_bias, bounded)
    for g in range(N_KV_HEADS):
        for p_ in range(2):
            col = g * 2 * LANES + p_ * LANES
            os_o[0, :, col:col + LANES] = _finish_pair(acc_s[2 * g], acc_s[2 * g + 1], p_, tq)
            ow_o[0, :, col:col + LANES] = _finish_pair(acc_w[2 * g], acc_w[2 * g + 1], p_, tq)


def _slc_win_attn(q, sel, skab, sv1, wkab, wv1, *, tq, tk, bounded):
    B, T, D = q.shape
    whole = lambda a: pl.BlockSpec((1,) + a.shape[1:], lambda b, i: (b, 0, 0))
    row = lambda w: pl.BlockSpec((1, tq, w), lambda b, i: (b, i, 0))
    return pl.pallas_call(
        functools.partial(_slc_win_attn_kernel, tk=tk, bounded=bounded), grid=(B, T // tq),
        in_specs=[row(D), row(sel.shape[2]), whole(skab), whole(sv1), whole(wkab), whole(wv1)],
        out_specs=[row(D), row(D)],
        out_shape=[jax.ShapeDtypeStruct((B, T, D), F32)] * 2,
        compiler_params=_cparams(("parallel", "arbitrary")), name="slc_win_attn",
    )(q, sel, skab, sv1, wkab, wv1)


def _pair_keys(k):
    z = jnp.zeros_like(k)
    return jnp.concatenate([k, z, z, k], axis=-1).reshape(k.shape[0], k.shape[1], -1)


def _ones_values(v):
    return jnp.concatenate([v, jnp.ones_like(v)], axis=-1).reshape(v.shape[0], v.shape[1], -1)


def _sample_attn_kernel(blk_ref, q_ref, kpos_ref, new_ref, win_ref, cache_hbm, os_o, ow_o, kbuf, vbuf, sem, *,
                        n_q, n_sel, pos_base, win_base, n_win, hp):
    b = pl.program_id(0)
    slot = b % 2
    per_seq = n_q * N_KV_HEADS * n_sel
    per_page = cache_hbm.shape[1] // SLC_BLOCK

    def start_copies(step, into):
        def one(i, carry):
            blk = jnp.maximum(blk_ref[step * per_seq + i], 0)
            qg = i // n_sel
            lane0 = pl.multiple_of(((qg % N_KV_HEADS) // 2) * LANES, LANES)
            src_rows = pl.ds(pl.multiple_of((blk % per_page) * SLC_BLOCK, SLC_BLOCK), SLC_BLOCK)
            for c, dst in ((0, kbuf), (1, vbuf)):
                pltpu.make_async_copy(cache_hbm.at[blk // per_page, src_rows, pl.ds(c * KV_W + lane0, LANES)],
                                      dst.at[into, qg, pl.ds((i % n_sel) * SLC_BLOCK, SLC_BLOCK)], sem.at[into]).start()
            return carry
        lax.fori_loop(0, per_seq, one, 0)

    @pl.when(b == 0)
    def _():
        start_copies(0, 0)

    @pl.when(b + 1 < pl.num_programs(0))
    def _():
        start_copies(b + 1, 1 - slot)

    for dst in (kbuf, vbuf):
        pltpu.make_async_copy(dst.at[slot], dst.at[slot], sem.at[slot]).wait()

    def new_rows(i, carry):
        qg = i // n_sel
        win = (qg % N_KV_HEADS) // 2

        @pl.when(blk_ref[b * per_seq + i] < 0)
        def _():
            kbuf[slot, qg, pl.ds((i % n_sel) * SLC_BLOCK, SLC_BLOCK), :] = new_ref[0, win]
            vbuf[slot, qg, pl.ds((i % n_sel) * SLC_BLOCK, SLC_BLOCK), :] = new_ref[0, N_KV_HEADS // 2 + win]
        return carry
    lax.fori_loop(0, per_seq, new_rows, 0)

    tq = q_ref.shape[1]
    rows = HEADS_PER_KV * tq
    nk = n_sel * SLC_BLOCK
    nwp = win_ref.shape[1]
    q = q_ref[0]
    row_q = lax.broadcasted_iota(jnp.int32, (rows, 1), 0) % tq
    w_id = lax.broadcasted_iota(jnp.int32, (rows, nwp), 1)
    d_win = (pos_base + row_q) - (win_base + w_id)
    win_bias = jnp.where((d_win >= 0) & (d_win < WINDOW) & (win_base + w_id >= 0) & (w_id < n_win), 0.0, MASKED)

    def attend(s, v):
        m = jnp.maximum(jnp.max(s, -1, keepdims=True), M_INIT)
        e = jnp.exp(s - m)
        p = e / jnp.maximum(jnp.sum(e, -1, keepdims=True), 1e-30)
        return _mm(p, v, hp=hp)

    for g in range(N_KV_HEADS):
        qs = _stack_heads(q, g, tq)
        zeros = jnp.zeros_like(qs)
        qs_pair = jnp.concatenate([qs, zeros] if g % 2 == 0 else [zeros, qs], axis=1)
        lanes_g = slice((g % 2) * HEAD_DIM, (g % 2 + 1) * HEAD_DIM)
        o_s = jnp.zeros((rows, HEAD_DIM), F32)
        for qi in range(n_q):
            qg = qi * N_KV_HEADS + g
            kpos = kpos_ref[0, qg:qg + 1, :]
            bias = jnp.broadcast_to(jnp.where(kpos <= pos_base + qi, 0.0, MASKED), (rows, nk))
            o_qi = attend(_mm(qs_pair, kbuf[slot, qg], "nt", hp) + bias, vbuf[slot, qg])[:, lanes_g]
            o_s = jnp.where(row_q == qi, o_qi, o_s)
        o_w = attend(_mm(qs, win_ref[0, :, g * HEAD_DIM:(g + 1) * HEAD_DIM], "nt", hp) + win_bias,
                     win_ref[0, :, KV_W + g * HEAD_DIM:KV_W + (g + 1) * HEAD_DIM])
        for i in range(HEADS_PER_KV):
            col = (HEADS_PER_KV * g + i) * HEAD_DIM
            os_o[0, :, col:col + HEAD_DIM] = o_s[i * tq:(i + 1) * tq]
            ow_o[0, :, col:col + HEAD_DIM] = o_w[i * tq:(i + 1) * tq]


def _sample_attn(pool_blk, q, kpos, new_win, win_all, cache_rows, *, n_q, n_sel, pos_base, win_base, n_win, hp):
    B, tq, D = q.shape
    kern = functools.partial(_sample_attn_kernel, n_q=n_q, n_sel=n_sel, pos_base=pos_base, win_base=win_base,
                             n_win=n_win, hp=hp)
    blk = lambda a: pl.BlockSpec((1,) + a.shape[1:], lambda b, pb: (b,) + (0,) * (a.ndim - 1))
    buf = pltpu.VMEM((2, n_q * N_KV_HEADS, n_sel * SLC_BLOCK, LANES), F32)
    grid_spec = pltpu.PrefetchScalarGridSpec(
        num_scalar_prefetch=1, grid=(B,),
        in_specs=[blk(q), blk(kpos), blk(new_win), blk(win_all), pl.BlockSpec(memory_space=pl.ANY)],
        out_specs=[blk(q), blk(q)],
        scratch_shapes=[buf, buf, pltpu.SemaphoreType.DMA((2,))])
    return pl.pallas_call(
        kern, grid_spec=grid_spec, out_shape=[jax.ShapeDtypeStruct((B, tq, D), F32)] * 2,
        compiler_params=_cparams(("arbitrary",)), name="sample_slc_win_attn",
    )(pool_blk.reshape(-1), q, kpos, new_win, win_all, cache_rows)


def _pick_tile(n, cap=512):
    return next(t for t in (512, 256, 128, 64, 32, 16, 8) if t <= cap and n % t == 0)


def _pad_rows(x, t_pad):
    return jnp.pad(x, ((0, 0), (0, t_pad - x.shape[1]), (0, 0)))


def _rwkv_layer(x, shift0, wkv0, p, *, tm, hp, inv_hp):
    B, T, D = x.shape
    t_pad = -(-T // tm) * tm
    xp = _pad_rows(x, t_pad)
    r, lw, k, v, a, g, shift = _rwkv_proj(xp, shift0[:, None, :], p["g"], p["mix"], p["vec"], p["wrkv"], p["dw1"],
                                          p["dw2"], p["aw1"], p["aw2"], p["gw1"], p["gw2"], t_real=T, tm=tm, hp=hp)
    if t_pad != T:
        live = (jnp.arange(t_pad) < T)[None, :, None]
        r, lw, k, v, a = (jnp.where(live, t, 0.0) for t in (r, lw, k, v, a))
    y, s_fin = _wkv(r, lw, k, v, a, p["vec"], wkv0, hp=hp, inv_hp=inv_hp)
    n = B * t_pad
    out = _row_call(functools.partial(_rwkv_out_kernel, hp=hp), [xp.reshape(n, D), y.reshape(n, D), g.reshape(n, D)],
                    [p["wo"]], [D], [F32], _pick_tile(n), "rwkv_out")[0]
    return out.reshape(B, t_pad, D)[:, :T], shift[:, 0], s_fin


def _sub_blocks(rows):
    S, L = rows.shape[:2]
    n_sub = L // CMP_STRIDE
    r = rows[:, :n_sub * CMP_STRIDE].reshape(S, n_sub, CMP_STRIDE, 2, N_KV_HEADS, HEAD_DIM)
    r = jnp.transpose(r, (0, 3, 4, 1, 2, 5)).reshape(S, 2, N_KV_HEADS, n_sub, CMP_STRIDE * HEAD_DIM)
    r_pad = -(-n_sub // SUBLANES) * SUBLANES
    return jnp.pad(r, ((0, 0),) * 3 + ((0, r_pad - n_sub), (0, 0))), n_sub - CMP_BLOCK // CMP_STRIDE + 1


def _compressed_kv(rows, p, *, hp):
    xsub, n_cmp = _sub_blocks(rows)
    out = _compress(xsub, p["cmp_pe"], p["cmp_w1"], p["cmp_w2"], p["k_norm_g"][0:1], hp=hp)
    S, _, G, R, dh = out.shape
    out = jnp.transpose(out, (1, 0, 3, 2, 4)).reshape(2, S, R, G * dh).astype(F32 if hp else BF16)
    return out[0], out[1], n_cmp


def _overlap_matrix(ncp, n_cmp, n_slc, nsp):
    c0 = np.arange(ncp)[:, None] * CMP_STRIDE
    s0 = np.arange(nsp)[None, :] * SLC_BLOCK
    m = (c0 < s0 + SLC_BLOCK) & (c0 + CMP_BLOCK > s0) & (np.arange(ncp)[:, None] < n_cmp) & (np.arange(nsp)[None, :] < n_slc)
    return jnp.asarray(m.astype(np.float32), dtype=BF16)


def _moe(h, p, *, tm, tf, hp):
    n, D = h.shape
    xn, logits = _row_call(_router_kernel, [h], [p["g_moe"], p["w_router"]], [D, LANES], [F32 if hp else BF16, F32],
                           _pick_tile(n), "moe_router")
    top_v, top_e = lax.top_k(logits[:, :N_EXPERTS], TOP_K)
    gates = jax.nn.softmax(top_v, -1)
    n_pairs = n * TOP_K
    flat_e = top_e.reshape(n_pairs)
    order = jnp.argsort(flat_e)
    seen = jnp.cumsum((flat_e[:, None] == jnp.arange(N_EXPERTS)).astype(jnp.int32), axis=0)
    counts = seen[-1]
    padded = (counts + tm - 1) // tm * tm
    pad_end = jnp.cumsum(padded)
    pad_start = pad_end - padded
    start = jnp.cumsum(counts) - counts
    rank = jnp.take_along_axis(seen, flat_e[:, None], axis=1)[:, 0] - 1
    dest = (pad_start[flat_e] + rank).reshape(n, TOP_K)
    n_blocks = -(-n_pairs // tm) + N_EXPERTS
    cap = n_blocks * tm
    block_e = jnp.minimum(jnp.searchsorted(pad_end, jnp.arange(n_blocks) * tm, side="right"), N_EXPERTS - 1).astype(jnp.int32)
    row_e = jnp.repeat(block_e, tm)
    k_in_e = jnp.arange(cap, dtype=jnp.int32) - pad_start[row_e]
    src = order[jnp.clip(start[row_e] + k_in_e, 0, n_pairs - 1)] // TOP_K
    buf_tok = jnp.where((k_in_e >= 0) & (k_in_e < counts[row_e]), src, n).astype(jnp.int32)
    n_used = (pad_end[-1] // tm).astype(jnp.int32).reshape(1)
    xs = xn[jnp.minimum(buf_tok, n - 1)]
    ys = _moe_ffn(block_e, n_used, xs, p["moe_wg"], p["moe_wu"], p["moe_wd"], tm=tm, tf=tf, hp=hp)
    y = ys[dest[:, 0]] * gates[:, 0:1] + ys[dest[:, 1]] * gates[:, 1:2]
    return h + y


def kernel(x_prompt, x_sample, cache_cmp_kv, cache_slc_kv, state_win_kv, state_wkv, state_shift, page_table, norm_g, rw_mix, rw_vec, rw_w_rkv, rw_w_o, rw_decay_w1, rw_decay_w2, rw_iclr_w1, rw_iclr_w2, rw_gate_w1, rw_gate_w2, ffn_w_gate, ffn_w_up, ffn_w_down, moe_router, moe_w_gate, moe_w_up, moe_w_down, kv_norm_g, w_kv, k_norm_g, cmp_pe, cmp_w1, cmp_w2, w_qg, q_norm_g, w_o):
    B, T, D = x_prompt.shape
    SB, S, _ = x_sample.shape
    assert norm_g.shape[0] == 2 and rw_mix.shape[0] == 1 and w_qg.shape[0] == 1, "one RWKV-7 layer + one NSA layer"
    assert D == N_HEADS * HEAD_DIM and T % 512 == 0 and S <= SUBLANES
    page = cache_cmp_kv.shape[1]
    past = page_table.shape[1] * page
    wb = state_win_kv.shape[1]
    assert past % SLC_BLOCK == 0 and page % SLC_BLOCK == 0 and S <= SLC_BLOCK
    bf = lambda w: w.astype(BF16)
    pad8 = lambda m: jnp.pad(m, ((0, SUBLANES - m.shape[0]), (0, 0)))
    dff = ffn_w_gate.shape[2]
    tf = dff // 2 if (dff // 2) % LANES == 0 else dff
    n_p, n_s = B * T, SB * S
    both = lambda w: (bf(w), w)
    pick = lambda d, hp: {k: (v[hp] if isinstance(v, tuple) else v) for k, v in d.items()}

    rw = dict(g=norm_g[0, 0][None], mix=pad8(rw_mix[0]), vec=pad8(rw_vec[0]), wrkv=both(rw_w_rkv[0]), wo=both(rw_w_o[0]),
              dw1=both(rw_decay_w1[0]), dw2=both(rw_decay_w2[0]), aw1=both(rw_iclr_w1[0]), aw2=both(rw_iclr_w2[0]),
              gw1=both(rw_gate_w1[0]), gw2=both(rw_gate_w2[0]))
    H = D // RW_HEAD_DIM
    hp_, shift_p, wkv_p = _rwkv_layer(x_prompt, jnp.zeros((B, D), F32), None,
                                      pick(rw, 0), tm=256, hp=False, inv_hp=False)
    hs, shift_s, wkv_s = _rwkv_layer(x_sample, state_shift[0], state_wkv[0], pick(rw, 1), tm=WKV_CHUNK, hp=True, inv_hp=True)
    ffn_g = norm_g[0, 1][None]
    hp_ = _ffn(hp_.reshape(n_p, D), ffn_g, bf(ffn_w_gate[0]), bf(ffn_w_up[0]), bf(ffn_w_down[0]), tm=512, tf=dff, hp=False)
    hs = _ffn(hs.reshape(n_s, D), ffn_g, ffn_w_gate[0], ffn_w_up[0], ffn_w_down[0], tm=n_s, tf=tf, hp=True)

    kv_consts = lambda hp: [kv_norm_g[None], both(w_kv)[hp], jnp.tile(k_norm_g, (1, N_KV_HEADS)),
                            _block_diag(KV_W, HEAD_DIM, 1.0 / HEAD_DIM)]
    place_k = np.zeros((KV_W, N_KV_HEADS * 4 * HEAD_DIM), np.float32)
    place_v = np.zeros((KV_W, N_KV_HEADS * 2 * HEAD_DIM), np.float32)
    ones_v = np.zeros((1, N_KV_HEADS * 2 * HEAD_DIM), np.float32)
    for g in range(N_KV_HEADS):
        for d in range(HEAD_DIM):
            place_k[g * HEAD_DIM + d, g * 4 * HEAD_DIM + d] = 1.0
            place_k[g * HEAD_DIM + d, g * 4 * HEAD_DIM + 3 * HEAD_DIM + d] = 1.0
            place_v[g * HEAD_DIM + d, g * 2 * HEAD_DIM + d] = 1.0
        ones_v[0, g * 2 * HEAD_DIM + HEAD_DIM:(g + 1) * 2 * HEAD_DIM] = 1.0
    pair_consts = [jnp.asarray(place_k, dtype=BF16), jnp.asarray(place_v, dtype=BF16), jnp.asarray(ones_v)]
    as_rows = lambda t, b: t.reshape(b, -1, 2, N_KV_HEADS, HEAD_DIM)
    kv_p = _row_call(functools.partial(_kv_proj_kernel, hp=False, pair_out=True), [hp_], kv_consts(False) + pair_consts,
                     [2 * KV_W] * 3 + [4 * KV_W, 2 * KV_W] * 2, [F32] * 3 + [BF16] * 4, 512, "kv_proj")
    cmp_p, slc_p, win_p = (as_rows(t, B) for t in kv_p[:3])
    skab, sv1, wkab, wv1 = (t.reshape(B, T, -1) for t in kv_p[3:])
    cmp_s, slc_s, win_s = (as_rows(t, SB) for t in _row_call(
        functools.partial(_kv_proj_kernel, hp=True, pair_out=False), [hs], kv_consts(True), [2 * KV_W] * 3, [F32] * 3,
        n_s, "kv_proj"))
    win_all_s = jnp.concatenate([state_win_kv, win_s], axis=1)

    cp = dict(cmp_pe=cmp_pe.reshape(2, CMP_BLOCK // CMP_STRIDE, 1, CMP_STRIDE * HEAD_DIM),
              cmp_w1=both(cmp_w1.reshape(2, CMP_BLOCK // CMP_STRIDE, CMP_STRIDE * HEAD_DIM, HEAD_DIM)),
              cmp_w2=both(cmp_w2), k_norm_g=k_norm_g)
    kc_p, vc_p, ncmp_p = _compressed_kv(cmp_p, pick(cp, 0), hp=False)
    assert S < CMP_STRIDE and page % CMP_STRIDE == 0
    n_ratio = CMP_BLOCK // CMP_STRIDE
    w1r = cmp_w1.reshape(2, n_ratio, CMP_STRIDE, HEAD_DIM, HEAD_DIM)
    w2x = jnp.einsum("cjsdh,ef->csedjfh", w1r, jnp.eye(2, dtype=F32)).reshape(2, CMP_STRIDE * LANES, n_ratio * LANES)
    n_pages = page_table.shape[1]
    pages_per_step = next(pp for pp in (32, 16, 8, 4, 2, 1) if n_pages % pp == 0)
    part_s = _paged_sub_proj(page_table, cache_cmp_kv.reshape(-1, page, 2 * KV_W), w2x, pages_per_step=pages_per_step, hp=True)
    kcv_s = _compress_tail(part_s, cp["cmp_pe"], cp["cmp_w1"][1], cmp_w2, k_norm_g[0:1], hp=True)
    kc_s, vc_s, ncmp_s = kcv_s[:, 0], kcv_s[:, 1], past // CMP_STRIDE - n_ratio + 1

    ge = np.zeros((LANES, N_BRANCH * D), np.float32)
    for br in range(N_BRANCH):
        for hd in range(N_HEADS):
            ge[br * N_HEADS + hd, br * D + hd * HEAD_DIM: br * D + (hd + 1) * HEAD_DIM] = 1.0
    nsa = dict(g_attn=norm_g[1, 0][None], wq=both(w_qg[0, :, :D]),
               wgate=both(jnp.pad(w_qg[0, :, D:], ((0, 0), (0, LANES - N_BRANCH * N_HEADS)))),
               q_norm_g=jnp.tile(q_norm_g[0], N_HEADS)[None], bd_q=_block_diag(D, HEAD_DIM, 1.0 / HEAD_DIM),
               w_o=both(w_o[0]), gate_expand=jnp.asarray(ge, dtype=BF16))

    def query_side(h, tm, hp, q_scale):
        c = pick(nsa, hp)
        return _row_call(functools.partial(_q_proj_kernel, hp=hp, q_scale=q_scale), [h],
                         [c["g_attn"], c["wq"], c["wgate"], c["q_norm_g"], c["bd_q"]], [D, LANES],
                         [F32 if hp else BF16, F32], tm, "q_proj")

    def merge(h, gates, o_cmp, o_slc, o_win, tm, hp):
        c = pick(nsa, hp)
        return _row_call(functools.partial(_merge_kernel, hp=hp), [h, gates, o_cmp, o_slc, o_win],
                         [c["gate_expand"], c["w_o"]], [D], [F32], tm, "nsa_merge")[0]

    q_p, gates_p = query_side(hp_, 512, False, ATTN_SCALE * LOG2E)
    q_p = q_p.reshape(B, T, D)
    n_slc_p = T // SLC_BLOCK
    nsp_p = LANES // N_KV_HEADS
    assert n_slc_p <= nsp_p, "the packed block-selection layout holds at most 32 selection blocks per kv group"
    m_one = _overlap_matrix(kc_p.shape[1], ncmp_p, n_slc_p, nsp_p)
    m_p = jnp.stack([jnp.pad(m_one, ((0, 0), (g * nsp_p, LANES - (g + 1) * nsp_p))) for g in range(N_KV_HEADS)])
    heads = lambda t: t.reshape(t.shape[0], t.shape[1], N_KV_HEADS, HEAD_DIM)
    kcab, vc1 = _pair_keys(heads(kc_p)), _ones_values(heads(vc_p))

    def prompt_branches(bounded):
        o_cmp, sel = _cmp_sel(q_p, kcab, vc1, m_p, n_cmp=ncmp_p, n_slc=n_slc_p, tq=256, bounded=bounded)
        return (o_cmp,) + tuple(_slc_win_attn(q_p, sel, skab, sv1, wkab, wv1, tq=256, tk=256, bounded=bounded))

    score_bound = HEAD_DIM * ATTN_SCALE * jnp.max(jnp.abs(q_norm_g[0])) * jnp.max(jnp.abs(k_norm_g))
    o_cmp_p, o_slc_p, o_win_p = lax.cond(score_bound <= SCORE_BOUND_MAX, lambda: prompt_branches(True),
                                         lambda: prompt_branches(False))
    hp_ = merge(hp_, gates_p, o_cmp_p.reshape(n_p, D), o_slc_p.reshape(n_p, D), o_win_p.reshape(n_p, D), 512, False)

    kv_lanes = lambda t, dt: t.reshape(t.shape[0], t.shape[1], 2 * KV_W).astype(dt)
    q_s, gates_s = query_side(hs, n_s, True, ATTN_SCALE)
    q_s = _pad_rows(q_s.reshape(SB, S, D), SUBLANES)
    n_slc_s = (past + S - 1) // SLC_BLOCK + 1
    nsp_s = -(-n_slc_s // LANES) * LANES
    m_s = _overlap_matrix(kc_s.shape[1], ncmp_s, n_slc_s, nsp_s)
    o_cmp_s, sel_s = _cmp_attn(q_s, kc_s, vc_s, m_s, n_cmp=ncmp_s, n_slc=n_slc_s, pos_base=past, per_tile=False,
                               tq=SUBLANES, hp=True)
    n_sel = min(SLC_TOP, n_slc_s)
    sel_s = sel_s.reshape(SB, SUBLANES, N_KV_HEADS, nsp_s)[:, :S, :, :n_slc_s] > 0.5
    slot = jnp.cumsum(sel_s, axis=-1) - 1
    idx_s = jnp.sum(jnp.where(sel_s[..., None] & (slot[..., None] == jnp.arange(n_sel)),
                              jnp.arange(n_slc_s)[:, None], 0), axis=-2).astype(jnp.int32)
    in_pool = idx_s < past // SLC_BLOCK
    per_page = page // SLC_BLOCK
    blk_c = jnp.minimum(idx_s, past // SLC_BLOCK - 1)
    pages = jnp.take_along_axis(page_table, (blk_c // per_page).reshape(SB, -1), axis=1).reshape(idx_s.shape)
    pool_blk = jnp.where(in_pool, pages * per_page + blk_c % per_page, -1).astype(jnp.int32)
    new_rows = slc_s[:, jnp.clip(jnp.arange(SLC_BLOCK), 0, S - 1)]
    new_win = jnp.transpose(new_rows.reshape(SB, SLC_BLOCK, 2, N_KV_HEADS // 2, LANES), (0, 2, 3, 1, 4))
    new_win = new_win.reshape(SB, N_KV_HEADS, SLC_BLOCK, LANES)
    kpos = (idx_s[..., None] * SLC_BLOCK + jnp.arange(SLC_BLOCK)).reshape(SB, S * N_KV_HEADS, n_sel * SLC_BLOCK).astype(jnp.int32)
    kpos = jnp.pad(kpos, ((0, 0), (0, -(-S * N_KV_HEADS // SUBLANES) * SUBLANES - S * N_KV_HEADS), (0, 0)))
    cache_rows = cache_slc_kv.reshape(-1, page, 2 * KV_W)
    n_win = wb + S
    nwp = -(-n_win // SUBLANES) * SUBLANES
    win_lanes = _pad_rows(kv_lanes(win_all_s, F32), nwp)
    o_slc_s, o_win_s = _sample_attn(pool_blk, q_s, kpos, new_win, win_lanes, cache_rows, n_q=S, n_sel=n_sel,
                                    pos_base=past, win_base=past - wb, n_win=n_win, hp=True)
    unpad = lambda t: t[:, :S].reshape(n_s, D)
    hs = merge(hs, gates_s, unpad(o_cmp_s), unpad(o_slc_s), unpad(o_win_s), n_s, True)

    moe = dict(g_moe=norm_g[1, 1][None], w_router=jnp.pad(moe_router[0], ((0, 0), (0, LANES - N_EXPERTS))),
               moe_wg=both(moe_w_gate[0]), moe_wu=both(moe_w_up[0]), moe_wd=both(moe_w_down[0]))
    hp_ = _moe(hp_, pick(moe, 0), tm=512, tf=dff, hp=False).reshape(B, T, D)
    hs = _moe(hs, pick(moe, 1), tm=128, tf=tf, hp=True).reshape(SB, S, D)

    keep_p = min(WINDOW, T)
    return (hp_, hs, cmp_p, cmp_s, slc_p, slc_s, win_p[:, T - keep_p:], win_all_s[:, win_all_s.shape[1] - wb:],
            wkv_p[None], wkv_s[None], shift_p[None], shift_s[None])
```

```python
import functools

import numpy as np
import jax
import jax.numpy as jnp
from jax import lax
from jax.experimental import pallas as pl
from jax.experimental.pallas import tpu as pltpu

F32 = jnp.float32
BF16 = jnp.bfloat16

RW_HEAD_DIM = 64
RW_GN_EPS = 64e-5
N_HEADS = 16
HEAD_DIM = 64
N_KV_HEADS = 4
HEADS_PER_KV = N_HEADS // N_KV_HEADS
KV_W = N_KV_HEADS * HEAD_DIM
N_BRANCH = 3
CMP_BLOCK = 32
CMP_STRIDE = 16
SLC_BLOCK = 64
SLC_TOP = 16
WINDOW = 512
ATTN_SCALE = HEAD_DIM ** -0.5
N_EXPERTS = 8
TOP_K = 2
RMS_EPS = 1e-6
MASKED = -1e30
M_INIT = -1e20
BIG = 1e30
LOG2E = 1.4426950408889634
SCORE_BOUND_MAX = 40.0

SUBLANES = 8
LANES = 128
VMEM_LIMIT = 56 * 1024 * 1024
WKV_CHUNK = RW_HEAD_DIM


def _cparams(sem):
    return pltpu.CompilerParams(dimension_semantics=sem, vmem_limit_bytes=VMEM_LIMIT)


_DIMS = {"nn": (((1,), (0,)), ((), ())), "nt": (((1,), (1,)), ((), ())), "tn": (((0,), (0,)), ((), ()))}


def _split2(x):
    hi = x.astype(BF16)
    lo = (x - hi.astype(F32)).astype(BF16)
    return hi, lo


def _mm(a, b, form="nn", hp=False):
    d = lambda s, t: lax.dot_general(s, t, _DIMS[form], preferred_element_type=F32)
    if not hp:
        return d(a.astype(BF16), b.astype(BF16))
    ah, al = _split2(a.astype(F32))
    bh, bl = _split2(b.astype(F32))
    return d(ah, bh) + d(ah, bl) + d(al, bh)


def _dot_hilo(x, m):
    hi, lo = _split2(x)
    return jnp.dot(hi, m, preferred_element_type=F32) + jnp.dot(lo, m, preferred_element_type=F32)


def _dot_exact_rhs(m, x):
    hi = x.astype(BF16)
    r1 = x - hi.astype(F32)
    mid = r1.astype(BF16)
    lo = (r1 - mid.astype(F32)).astype(BF16)
    d = lambda t: jnp.dot(m, t, preferred_element_type=F32)
    return d(hi) + d(mid) + d(lo)


def _rms(x, g):
    return x * lax.rsqrt(jnp.mean(x * x, -1, keepdims=True) + RMS_EPS) * g


def _silu(x):
    return x * jax.nn.sigmoid(x)


def _softplus(z):
    return jnp.maximum(z, 0.0) + jnp.log(1.0 + jnp.exp(-jnp.abs(z)))


def _block_diag(n, seg, value):
    i = np.arange(n)
    return jnp.asarray((i[:, None] // seg == i[None, :] // seg).astype(np.float32) * value, dtype=BF16)


def _rwkv_proj_kernel(x_ref, halo_ref, sh_ref, g_ref, mix_ref, vec_ref, wrkv_ref, dw1_ref, dw2_ref, aw1_ref, aw2_ref,
                      gw1_ref, gw2_ref, r_o, lw_o, k_o, v_o, a_o, g_o, shift_o, *, last_tile, last_row, hp):
    i = pl.program_id(1)
    mm = functools.partial(_mm, hp=hp)
    g = g_ref[...]
    xn = _rms(x_ref[0], g)
    hn = _rms(halo_ref[0, SUBLANES - 1:SUBLANES, :], g)
    prev_last = jnp.where(i == 0, sh_ref[0], hn)
    row = lax.broadcasted_iota(jnp.int32, xn.shape, 0)
    prev = jnp.where(row == 0, prev_last, pltpu.roll(xn, 1, 0))
    xx = prev - xn
    mixed = lambda j: xn + xx * mix_ref[j:j + 1, :]
    r_o[0] = mm(mixed(0), wrkv_ref[0])
    k_o[0] = mm(mixed(2), wrkv_ref[1])
    v_o[0] = mm(mixed(3), wrkv_ref[2])
    w0 = vec_ref[0:1, :]
    a0 = vec_ref[1:2, :]
    log_w = -_softplus(-(w0 + mm(jnp.tanh(mm(mixed(1), dw1_ref[...])), dw2_ref[...]))) - 0.5
    lw_o[0] = -jnp.exp(log_w)
    a_o[0] = jax.nn.sigmoid(a0 + mm(mm(mixed(4), aw1_ref[...]), aw2_ref[...]))
    g_o[0] = mm(jax.nn.sigmoid(mm(mixed(5), gw1_ref[...])), gw2_ref[...])

    @pl.when(i == last_tile)
    def _():
        shift_o[0] = xn[last_row:last_row + 1, :]


def _rwkv_proj(x, shift0, g, mix, vec, wrkv, dw1, dw2, aw1, aw2, gw1, gw2, *, t_real, tm, hp):
    B, T, D = x.shape
    nt = T // tm
    row_spec = pl.BlockSpec((1, tm, D), lambda b, i: (b, i, 0))
    halo_spec = pl.BlockSpec((1, SUBLANES, D), lambda b, i: (b, jnp.maximum(i * (tm // SUBLANES) - 1, 0), 0))
    vec_spec = pl.BlockSpec((1, 1, D), lambda b, i: (b, 0, 0))
    full = lambda a: pl.BlockSpec(a.shape, lambda b, i: (0,) * a.ndim)
    consts = (g, mix, vec, wrkv, dw1, dw2, aw1, aw2, gw1, gw2)
    out_sd = jax.ShapeDtypeStruct((B, T, D), F32)
    kern = functools.partial(_rwkv_proj_kernel, last_tile=(t_real - 1) // tm, last_row=(t_real - 1) % tm, hp=hp)
    return pl.pallas_call(
        kern, grid=(B, nt),
        in_specs=[row_spec, halo_spec, vec_spec] + [full(c) for c in consts],
        out_specs=[row_spec] * 6 + [vec_spec],
        out_shape=[out_sd] * 6 + [jax.ShapeDtypeStruct((B, 1, D), F32)],
        compiler_params=_cparams(("parallel", "arbitrary")), name="rwkv_proj",
    )(x, x, shift0, *consts)


def _wkv_kernel(r_ref, lw_ref, k_ref, v_ref, a_ref, vec_ref, ones_ref, s0_ref, y_o, sT_o, s_sc, *, n_pairs, hp, inv_hp):
    c = pl.program_id(1)
    C = r_ref.shape[1]
    W = 2 * C
    mm = functools.partial(_mm, hp=hp)
    mm_inv = functools.partial(_mm, hp=inv_hp)

    @pl.when(c == 0)
    def _():
        s_sc[...] = s0_ref[0]

    lane = lax.broadcasted_iota(jnp.int32, (C, W), 1)
    row = lax.broadcasted_iota(jnp.int32, (C, W), 0)
    left = lane < C
    col = jnp.where(left, lane, lane - C)
    strict = col < row
    incl = col <= row
    diag_blocks = (lax.broadcasted_iota(jnp.int32, (W, W), 0) < C) == (lax.broadcasted_iota(jnp.int32, (W, W), 1) < C)
    bd = lambda x: jnp.concatenate([jnp.where(left, x, 0.0), jnp.where(left, 0.0, x)], axis=0)
    seg_sum = lambda x: _dot_hilo(x, ones_ref[...])
    rows = lambda x, p: x[p * C:(p + 1) * C]
    tri = jnp.where(lax.broadcasted_iota(jnp.int32, (C, C), 1) <= lax.broadcasted_iota(jnp.int32, (C, C), 0), 1.0, 0.0)
    cum_all = _dot_exact_rhs(tri.astype(BF16), lw_ref[0])
    pairs = range(n_pairs)
    ps = lambda p: slice(p * W, (p + 1) * W)

    r = [r_ref[0, :, ps(p)] for p in pairs]
    v = [v_ref[0, :, ps(p)] for p in pairs]
    kkr = [k_ref[0, :, ps(p)] * vec_ref[2:3, ps(p)] for p in pairs]
    ss = seg_sum(jnp.concatenate([x * x for x in kkr], axis=0))
    k2, at, rt, bt, kt, w_end = [], [], [], [], [], []
    for p in pairs:
        a = a_ref[0, :, ps(p)]
        cum = cum_all[:, ps(p)]
        kk = kkr[p] * lax.rsqrt(jnp.maximum(rows(ss, p), 1e-24))
        k2.append(k_ref[0, :, ps(p)] * (1.0 + (a - 1.0) * vec_ref[3:4, ps(p)]))
        e_pos = jnp.exp(cum)
        e_neg = jnp.exp(-cum)
        rt.append(r[p] * e_pos)
        at.append(-kk * jnp.exp(cum - lw_ref[0, :, ps(p)]))
        bt.append(kk * a * e_neg)
        kt.append(k2[p] * e_neg)
        w_end.append(e_pos[C - 1:C, :])

    P, U, m_rbk, x_r = [], [], [], []
    for p in pairs:
        ar = jnp.concatenate([at[p], rt[p]], axis=0)
        big = mm(ar, jnp.concatenate([bd(bt[p]), bd(kt[p])], axis=0), "nt")
        x = mm(ar, s_sc[p], "nt")
        P.append(jnp.where(strict, big[:C, :W], 0.0))
        l_ak = jnp.where(strict, big[:C, W:], 0.0)
        m_rbk.append(jnp.concatenate([jnp.where(incl, big[C:, :W], 0.0), jnp.where(incl, big[C:, W:], 0.0)], axis=1))
        U.append(x[:C] + mm(l_ak, bd(v[p])))
        x_r.append(x[C:])

    n_it = int(np.log2(C))
    for it in range(n_it):
        for p in pairs:
            if it == n_it - 1:
                U[p] = U[p] + mm_inv(P[p], bd(U[p]))
            else:
                res = mm_inv(P[p], jnp.concatenate([bd(P[p]), bd(U[p])], axis=1))
                U[p] = U[p] + res[:, W:]
                P[p] = res[:, :W]

    y = []
    for p in pairs:
        y.append(x_r[p] + mm(m_rbk[p], jnp.concatenate([bd(U[p]), bd(v[p])], axis=0)))
        upd = mm(jnp.concatenate([U[p], v[p]], axis=0),
                 jnp.concatenate([bt[p] * w_end[p], kt[p] * w_end[p]], axis=0), "tn")
        s_sc[p] = s_sc[p] * w_end[p] + jnp.where(diag_blocks, upd, 0.0)

    inv_n = 1.0 / RW_HEAD_DIM
    y_all = jnp.concatenate(y, axis=0)
    mu = seg_sum(y_all) * inv_n
    var = seg_sum(jnp.square(y_all - mu)) * inv_n
    rk = seg_sum(jnp.concatenate([r[p] * k2[p] * vec_ref[4:5, ps(p)] for p in pairs], axis=0))
    yn = (y_all - mu) * lax.rsqrt(var + RW_GN_EPS)
    for p in pairs:
        y_o[0, :, ps(p)] = rows(yn, p) * vec_ref[5:6, ps(p)] + vec_ref[6:7, ps(p)] + rows(rk, p) * v[p]

    @pl.when(c == pl.num_programs(1) - 1)
    def _():
        sT_o[0] = s_sc[...]


def _wkv(r, lw, k, v, a, vec, s0, *, hp, inv_hp):
    B, T, D = r.shape
    H = D // RW_HEAD_DIM
    C, W, n_pairs = WKV_CHUNK, 2 * RW_HEAD_DIM, H // 2
    if s0 is None:
        s_bd = jnp.zeros((B, n_pairs, W, W), F32)
    else:
        z = jnp.zeros((B, n_pairs, C, C), F32)
        s_bd = jnp.concatenate([jnp.concatenate([s0[:, 0::2], z], axis=-1), jnp.concatenate([z, s0[:, 1::2]], axis=-1)], axis=-2)
    row_spec = pl.BlockSpec((1, C, D), lambda b, c: (b, c, 0))
    st_spec = pl.BlockSpec((1, n_pairs, W, W), lambda b, c: (b, 0, 0, 0))
    ones = _block_diag(W, RW_HEAD_DIM, 1.0)
    kern = functools.partial(_wkv_kernel, n_pairs=n_pairs, hp=hp, inv_hp=inv_hp)
    y, s_out = pl.pallas_call(
        kern, grid=(B, T // C),
        in_specs=[row_spec] * 5 + [pl.BlockSpec(vec.shape, lambda b, c: (0, 0)), pl.BlockSpec((W, W), lambda b, c: (0, 0)),
                                   st_spec],
        out_specs=[row_spec, st_spec],
        out_shape=[jax.ShapeDtypeStruct((B, T, D), F32), jax.ShapeDtypeStruct(s_bd.shape, F32)],
        scratch_shapes=[pltpu.VMEM((n_pairs, W, W), F32)],
        compiler_params=_cparams(("parallel", "arbitrary")), name="wkv_chunk",
    )(r, lw, k, v, a, vec, ones, s_bd)
    s_fin = jnp.stack([s_out[:, :, :C, :C], s_out[:, :, C:, C:]], axis=2).reshape(B, H, RW_HEAD_DIM, RW_HEAD_DIM)
    return y, s_fin


def _row_call(kernel, rows, consts, out_widths, out_dtypes, tm, name):
    n = rows[0].shape[0]
    row_spec = lambda w: pl.BlockSpec((tm, w), lambda i: (i, 0))
    full = lambda a: pl.BlockSpec(a.shape, lambda i: (0,) * a.ndim)
    return pl.pallas_call(
        kernel, grid=(n // tm,),
        in_specs=[row_spec(a.shape[1]) for a in rows] + [full(c) for c in consts],
        out_specs=[row_spec(w) for w in out_widths],
        out_shape=[jax.ShapeDtypeStruct((n, w), dt) for w, dt in zip(out_widths, out_dtypes)],
        compiler_params=_cparams(("parallel",)), name=name,
    )(*rows, *consts)


def _rwkv_out_kernel(x_ref, y_ref, g_ref, wo_ref, o_ref, *, hp):
    o_ref[...] = x_ref[...] + _mm(y_ref[...] * g_ref[...], wo_ref[...], hp=hp)


def _kv_proj_kernel(h_ref, g_ref, wkv_ref, kg_ref, bd_ref, *rest, hp, pair_out):
    if pair_out:
        pk_ref, pv_ref, ones_ref, cmp_o, slc_o, win_o = rest[:6]
        pair_outs = {1: rest[6:8], 2: rest[8:10]}
    else:
        cmp_o, slc_o, win_o = rest
    kv = _mm(_rms(h_ref[...], g_ref[...]), wkv_ref[...], hp=hp)
    cmp_o[...] = kv[:, :2 * KV_W]
    for br, out in ((1, slc_o), (2, win_o)):
        kraw = kv[:, br * 2 * KV_W: br * 2 * KV_W + KV_W]
        ms = _dot_hilo(kraw * kraw, bd_ref[...])
        kn = kraw * lax.rsqrt(ms + RMS_EPS) * kg_ref[br:br + 1, :]
        v = kv[:, br * 2 * KV_W + KV_W: (br + 1) * 2 * KV_W]
        out[:, :KV_W] = kn
        out[:, KV_W:] = v
        if pair_out:
            kab_o, v1_o = pair_outs[br]
            kab_o[...] = jnp.dot(kn.astype(BF16), pk_ref[...], preferred_element_type=F32).astype(BF16)
            v1_o[...] = (jnp.dot(v.astype(BF16), pv_ref[...], preferred_element_type=F32) + ones_ref[...]).astype(BF16)


def _q_proj_kernel(h_ref, g_ref, wq_ref, wg_ref, qg_ref, bd_ref, q_o, gate_o, *, hp, q_scale):
    hn = _rms(h_ref[...], g_ref[...])
    q = _mm(hn, wq_ref[...], hp=hp)
    ms = _dot_hilo(q * q, bd_ref[...])
    q_o[...] = (q * lax.rsqrt(ms + RMS_EPS) * qg_ref[...] * q_scale).astype(q_o.dtype)
    gate_o[...] = jax.nn.sigmoid(_mm(hn, wg_ref[...], hp=hp))


def _merge_kernel(h_ref, gate_ref, oc_ref, os_ref, ow_ref, eg_ref, wo_ref, o_ref, *, hp):
    D = h_ref.shape[1]
    ge = _dot_hilo(gate_ref[...], eg_ref[...])
    o = ge[:, :D] * oc_ref[...] + ge[:, D:2 * D] * os_ref[...] + ge[:, 2 * D:] * ow_ref[...]
    o_ref[...] = h_ref[...] + _mm(o, wo_ref[...], hp=hp)


def _router_kernel(h_ref, g_ref, wr_ref, xn_o, logit_o):
    xn = _rms(h_ref[...], g_ref[...])
    xn_o[...] = xn.astype(xn_o.dtype)
    logit_o[...] = _mm(xn, wr_ref[...], hp=True)


def _ffn_kernel(x_ref, g_ref, wg_ref, wu_ref, wd_ref, o_ref, xn_sc, acc_sc, *, hp):
    f = pl.program_id(1)

    @pl.when(f == 0)
    def _():
        xn_sc[...] = _rms(x_ref[...], g_ref[...]).astype(xn_sc.dtype)
        acc_sc[...] = jnp.zeros_like(acc_sc)

    xn = xn_sc[...]
    hid = _silu(_mm(xn, wg_ref[...], hp=hp)) * _mm(xn, wu_ref[...], hp=hp)
    acc_sc[...] += _mm(hid, wd_ref[...], hp=hp)

    @pl.when(f == pl.num_programs(1) - 1)
    def _():
        o_ref[...] = x_ref[...] + acc_sc[...]


def _ffn(x, g, wg, wu, wd, *, tm, tf, hp):
    n, D = x.shape
    dff = wg.shape[1]
    return pl.pallas_call(
        functools.partial(_ffn_kernel, hp=hp), grid=(n // tm, dff // tf),
        in_specs=[pl.BlockSpec((tm, D), lambda i, f: (i, 0)), pl.BlockSpec((1, D), lambda i, f: (0, 0)),
                  pl.BlockSpec((D, tf), lambda i, f: (0, f)), pl.BlockSpec((D, tf), lambda i, f: (0, f)),
                  pl.BlockSpec((tf, D), lambda i, f: (f, 0))],
        out_specs=pl.BlockSpec((tm, D), lambda i, f: (i, 0)),
        out_shape=jax.ShapeDtypeStruct((n, D), F32),
        scratch_shapes=[pltpu.VMEM((tm, D), F32 if hp else BF16), pltpu.VMEM((tm, D), F32)],
        compiler_params=_cparams(("parallel", "arbitrary")), name="ffn_dense",
    )(x, g, wg, wu, wd)


def _moe_ffn_kernel(be_ref, nb_ref, x_ref, wg_ref, wu_ref, wd_ref, o_ref, acc_sc, *, hp):
    i = pl.program_id(0)
    f = pl.program_id(1)

    @pl.when(i < nb_ref[0])
    def _():
        @pl.when(f == 0)
        def _():
            acc_sc[...] = jnp.zeros_like(acc_sc)

        x = x_ref[...]
        hid = _silu(_mm(x, wg_ref[0], hp=hp)) * _mm(x, wu_ref[0], hp=hp)
        acc_sc[...] += _mm(hid, wd_ref[0], hp=hp)

        @pl.when(f == pl.num_programs(1) - 1)
        def _():
            o_ref[...] = acc_sc[...]

    @pl.when((i >= nb_ref[0]) & (f == pl.num_programs(1) - 1))
    def _():
        o_ref[...] = jnp.zeros_like(o_ref)


def _moe_ffn(block_e, n_used, xs, wg, wu, wd, *, tm, tf, hp):
    cap, D = xs.shape
    dff = wg.shape[2]
    w_idx = lambda i, f, be, nb: jnp.where(i < nb[0], f, dff // tf - 1)
    grid_spec = pltpu.PrefetchScalarGridSpec(
        num_scalar_prefetch=2, grid=(cap // tm, dff // tf),
        in_specs=[pl.BlockSpec((tm, D), lambda i, f, be, nb: (i, 0)),
                  pl.BlockSpec((1, D, tf), lambda i, f, be, nb: (be[i], 0, w_idx(i, f, be, nb))),
                  pl.BlockSpec((1, D, tf), lambda i, f, be, nb: (be[i], 0, w_idx(i, f, be, nb))),
                  pl.BlockSpec((1, tf, D), lambda i, f, be, nb: (be[i], w_idx(i, f, be, nb), 0))],
        out_specs=pl.BlockSpec((tm, D), lambda i, f, be, nb: (i, 0)),
        scratch_shapes=[pltpu.VMEM((tm, D), F32)])
    return pl.pallas_call(
        functools.partial(_moe_ffn_kernel, hp=hp), grid_spec=grid_spec, out_shape=jax.ShapeDtypeStruct((cap, D), F32),
        compiler_params=_cparams(("arbitrary", "arbitrary")), name="moe_ffn",
    )(block_e, n_used, xs, wg, wu, wd)


def _compress_kernel(x_ref, pe_ref, w1_ref, w2_ref, kg_ref, o_ref, *, hp):
    c = pl.program_id(1)
    x = x_ref[0, 0, 0]
    R = x.shape[0]
    p0 = _mm(x + pe_ref[0, 0], w1_ref[0, 0], hp=hp)
    p1 = _mm(x + pe_ref[0, 1], w1_ref[0, 1], hp=hp)
    pre = p0 + pltpu.roll(p1, R - 1, 0)
    out = _mm(_silu(pre), w2_ref[0], hp=hp)
    o_ref[0, 0, 0] = jnp.where(c == 0, _rms(out, kg_ref[...]), out)


def _compress(xsub, pe, w1, w2, kg, *, hp):
    S, _, G, R, W = xsub.shape
    return pl.pallas_call(
        functools.partial(_compress_kernel, hp=hp), grid=(S, 2, G),
        in_specs=[pl.BlockSpec((1, 1, 1, R, W), lambda s, c, g: (s, c, g, 0, 0)),
                  pl.BlockSpec((1, 2, 1, W), lambda s, c, g: (c, 0, 0, 0)),
                  pl.BlockSpec((1, 2, W, HEAD_DIM), lambda s, c, g: (c, 0, 0, 0)),
                  pl.BlockSpec((1, HEAD_DIM, HEAD_DIM), lambda s, c, g: (c, 0, 0)),
                  pl.BlockSpec((1, HEAD_DIM), lambda s, c, g: (0, 0))],
        out_specs=pl.BlockSpec((1, 1, 1, R, HEAD_DIM), lambda s, c, g: (s, c, g, 0, 0)),
        out_shape=jax.ShapeDtypeStruct((S, 2, G, R, HEAD_DIM), F32),
        compiler_params=_cparams(("parallel", "arbitrary", "arbitrary")), name="kv_compress",
    )(xsub, pe, w1, w2, kg)


def _paged_sub_proj_kernel(pt_ref, cache_hbm, w_ref, o_ref, buf, sem, *, pages_per_step, page, hp):
    lin = pl.program_id(0) * pl.num_programs(1) + pl.program_id(1)
    total = pl.num_programs(0) * pl.num_programs(1)
    slot = lin % 2

    n_lane_pairs = buf.shape[1]

    def page_copies(step, into):
        return [pltpu.make_async_copy(cache_hbm.at[pt_ref[step * pages_per_step + p], pl.ds(0, page), pl.ds(pair * LANES, LANES)],
                                      buf.at[into, pair, pl.ds(p * page, page)], sem.at[into])
                for p in range(pages_per_step) for pair in range(n_lane_pairs)]

    @pl.when(lin == 0)
    def _():
        for cp in page_copies(0, 0):
            cp.start()

    @pl.when(lin + 1 < total)
    def _():
        for cp in page_copies(lin + 1, 1 - slot):
            cp.start()

    for cp in page_copies(lin, slot):
        cp.wait()
    n = pages_per_step * page // CMP_STRIDE
    for pair in range(n_lane_pairs):
        c = pair // (N_KV_HEADS // 2)
        rows = buf.at[slot, pair]
        x2 = jnp.concatenate([rows[pl.ds(s, n, stride=CMP_STRIDE), :] for s in range(CMP_STRIDE)], axis=1)
        o_ref[0, pair] = _mm(x2, w_ref[c], hp=hp and c == 0)


def _paged_sub_proj(page_table, cache, w2x, *, pages_per_step, hp):
    S, n_pages = page_table.shape
    page, width = cache.shape[1:]
    n = pages_per_step * page // CMP_STRIDE
    n_t = n_pages // pages_per_step
    grid_spec = pltpu.PrefetchScalarGridSpec(
        num_scalar_prefetch=1, grid=(S, n_t),
        in_specs=[pl.BlockSpec(memory_space=pl.ANY), pl.BlockSpec(w2x.shape, lambda s, t, pt: (0, 0, 0))],
        out_specs=pl.BlockSpec((1, 4, n, w2x.shape[2]), lambda s, t, pt: (s, 0, t, 0)),
        scratch_shapes=[pltpu.VMEM((2, width // LANES, pages_per_step * page, LANES), F32), pltpu.SemaphoreType.DMA((2,))])
    kern = functools.partial(_paged_sub_proj_kernel, pages_per_step=pages_per_step, page=page, hp=hp)
    return pl.pallas_call(
        kern, grid_spec=grid_spec, out_shape=jax.ShapeDtypeStruct((S, 4, n_t * n, w2x.shape[2]), F32),
        compiler_params=_cparams(("arbitrary", "arbitrary")), name="paged_sub_proj",
    )(page_table.reshape(-1), cache, w2x)


def _compress_tail_kernel(p_ref, pe_ref, w1_ref, w2_ref, kg_ref, o_ref, *, hp):
    R = p_ref.shape[2]
    half_groups = N_KV_HEADS // 2
    for c in range(2):
        bias = (_mm(jnp.broadcast_to(pe_ref[c, 0], (SUBLANES, pe_ref.shape[3])), w1_ref[c, 0], hp=True)
                + _mm(jnp.broadcast_to(pe_ref[c, 1], (SUBLANES, pe_ref.shape[3])), w1_ref[c, 1], hp=True))[0:1]
        outs = []
        for g in range(N_KV_HEADS):
            p = p_ref[0, c * half_groups + g // 2]
            e = g % 2
            p0 = p[:, e * HEAD_DIM:(e + 1) * HEAD_DIM]
            p1 = p[:, (2 + e) * HEAD_DIM:(3 + e) * HEAD_DIM]
            out = _mm(_silu(p0 + pltpu.roll(p1, R - 1, 0) + bias), w2_ref[c], hp=hp)
            outs.append(_rms(out, kg_ref[...]) if c == 0 else out)
        o_ref[0, c] = jnp.concatenate(outs, axis=1)


def _compress_tail(p, pe, w1, w2, kg, *, hp):
    S, _, R, _ = p.shape
    full = lambda a: pl.BlockSpec(a.shape, lambda s: (0,) * a.ndim)
    return pl.pallas_call(
        functools.partial(_compress_tail_kernel, hp=hp), grid=(S,),
        in_specs=[pl.BlockSpec((1,) + p.shape[1:], lambda s: (s, 0, 0, 0)), full(pe), full(w1), full(w2), full(kg)],
        out_specs=pl.BlockSpec((1, 2, R, KV_W), lambda s: (s, 0, 0, 0)),
        out_shape=jax.ShapeDtypeStruct((S, 2, R, KV_W), F32),
        compiler_params=_cparams(("parallel",)), name="kv_compress_tail",
    )(p, pe, w1, w2, kg)


def _stack_heads(q, g, tq):
    return jnp.concatenate([q[:, (HEADS_PER_KV * g + i) * HEAD_DIM:(HEADS_PER_KV * g + i + 1) * HEAD_DIM]
                            for i in range(HEADS_PER_KV)], axis=0)


def _cmp_attn_kernel(q_ref, kc_ref, vc_ref, m_ref, o_o, sel_o, *, n_cmp, n_slc, pos_base, per_tile, hp):
    tq = q_ref.shape[1]
    ncp = kc_ref.shape[1]
    nsp = m_ref.shape[1]
    q0 = pos_base + (pl.program_id(1) * tq if per_tile else 0)
    rows = HEADS_PER_KV * tq
    q_pos = q0 + lax.broadcasted_iota(jnp.int32, (rows, ncp), 0) % tq
    n_id = lax.broadcasted_iota(jnp.int32, (rows, ncp), 1)
    bias = jnp.where((n_id * CMP_STRIDE + CMP_BLOCK - 1 <= q_pos) & (n_id < n_cmp), 0.0, MASKED)
    blk = lax.broadcasted_iota(jnp.int32, (tq, nsp), 1)
    cur = (q0 + lax.broadcasted_iota(jnp.int32, (tq, nsp), 0)) // SLC_BLOCK
    forced = (blk == 0) | (blk == cur) | (blk == cur - 1)
    q = q_ref[0]
    scores = []
    for g in range(N_KV_HEADS):
        qs = _stack_heads(q, g, tq)
        s = _mm(qs, kc_ref[0, :, g * HEAD_DIM:(g + 1) * HEAD_DIM], "nt", hp) + bias
        m = jnp.maximum(jnp.max(s, -1, keepdims=True), M_INIT)
        e = jnp.exp(s - m)
        p = e / jnp.maximum(jnp.sum(e, -1, keepdims=True), 1e-30)
        o = _mm(p, vc_ref[0, :, g * HEAD_DIM:(g + 1) * HEAD_DIM], hp=hp)
        psum = p[0:tq]
        for i in range(HEADS_PER_KV):
            col = (HEADS_PER_KV * g + i) * HEAD_DIM
            o_o[0, :, col:col + HEAD_DIM] = o[i * tq:(i + 1) * tq]
            if i:
                psum = psum + p[i * tq:(i + 1) * tq]
        imp = _dot_hilo(psum, m_ref[...])
        scores.append(jnp.where(blk <= cur, jnp.where(forced, BIG, imp), -BIG))
    score = jnp.concatenate(scores, axis=0)
    blk_r = lax.broadcasted_iota(jnp.int32, score.shape, 1)

    def count_ahead(i, rank):
        s_i = jnp.sum(jnp.where(blk_r == i, score, 0.0), axis=-1, keepdims=True)
        return rank + jnp.where(s_i > score, 1.0, jnp.where((s_i == score) & (i < blk_r), 1.0, 0.0))

    rank = lax.fori_loop(0, n_slc, count_ahead, jnp.zeros(score.shape, F32), unroll=8)
    for g in range(N_KV_HEADS):
        sel_o[0, :, g * nsp:(g + 1) * nsp] = jnp.where(rank[g * tq:(g + 1) * tq] < min(SLC_TOP, n_slc), 1.0, 0.0).astype(sel_o.dtype)


def _cmp_attn(q, kc, vc, m, *, n_cmp, n_slc, pos_base, per_tile, tq, hp):
    B, T, D = q.shape
    ncp, nsp = m.shape
    kv_spec = pl.BlockSpec((1, ncp, KV_W), lambda b, i: (b, 0, 0))
    kern = functools.partial(_cmp_attn_kernel, n_cmp=n_cmp, n_slc=n_slc, pos_base=pos_base, per_tile=per_tile, hp=hp)
    return pl.pallas_call(
        kern, grid=(B, T // tq),
        in_specs=[pl.BlockSpec((1, tq, D), lambda b, i: (b, i, 0)), kv_spec, kv_spec,
                  pl.BlockSpec((ncp, nsp), lambda b, i: (0, 0))],
        out_specs=[pl.BlockSpec((1, tq, D), lambda b, i: (b, i, 0)),
                   pl.BlockSpec((1, tq, N_KV_HEADS * nsp), lambda b, i: (b, i, 0))],
        out_shape=[jax.ShapeDtypeStruct((B, T, D), F32), jax.ShapeDtypeStruct((B, T, N_KV_HEADS * nsp), BF16)],
        compiler_params=_cparams(("parallel", "arbitrary")), name="cmp_attn",
    )(q, kc, vc, m)


def _pair_rows(q, g):
    base = g * HEADS_PER_KV * HEAD_DIM
    return jnp.concatenate([q[:, base:base + LANES], q[:, base + LANES:base + 2 * LANES]], axis=0)


def _finish_pair(acc_even, acc_odd, p, tq):
    a0 = acc_even[p * tq:(p + 1) * tq]
    a1 = acc_odd[p * tq:(p + 1) * tq]
    o0 = a0 / jnp.maximum(a0[:, HEAD_DIM:HEAD_DIM + 1], 1e-30)
    o1 = a1 / jnp.maximum(a1[:, HEAD_DIM:HEAD_DIM + 1], 1e-30)
    lane = lax.broadcasted_iota(jnp.int32, (tq, LANES), 1)
    return jnp.where(lane < HEAD_DIM, o0, pltpu.roll(o1, HEAD_DIM, 1))


def _cmp_sel_kernel(q_ref, kab_ref, v1_ref, m_ref, o_o, sel_o, *, n_cmp, n_slc, bounded):
    tq = q_ref.shape[1]
    ncp = kab_ref.shape[1]
    nsp = LANES // N_KV_HEADS
    q0 = pl.program_id(1) * tq
    q_pos = q0 + lax.broadcasted_iota(jnp.int32, (tq, ncp), 0)
    n_id = lax.broadcasted_iota(jnp.int32, (tq, ncp), 1)
    b = jnp.where((n_id * CMP_STRIDE + CMP_BLOCK - 1 <= q_pos) & (n_id < n_cmp), 0.0, MASKED)
    b2 = jnp.concatenate([b, b], axis=0)
    q = q_ref[0]
    imp = jnp.zeros((tq, LANES), F32)
    for g in range(N_KV_HEADS):
        lhs = _pair_rows(q, g)
        vv = v1_ref[0, :, g * LANES:(g + 1) * LANES]
        accs, psum = [], None
        for half in range(2):
            s = _mm(lhs, kab_ref[0, :, (2 * g + half) * LANES:(2 * g + half + 1) * LANES], "nt") + b2
            e = jnp.exp2(s) if bounded else jnp.exp2(s - jnp.maximum(jnp.max(s, -1, keepdims=True), M_INIT))
            acc = _mm(e, vv)
            accs.append(acc)
            p = e / jnp.maximum(acc[:, HEAD_DIM:HEAD_DIM + 1], 1e-30)
            ph = p[:tq] + p[tq:]
            psum = ph if psum is None else psum + ph
        imp = imp + _dot_hilo(psum, m_ref[g])
        for p_ in range(2):
            col = g * 2 * LANES + p_ * LANES
            o_o[0, :, col:col + LANES] = _finish_pair(accs[0], accs[1], p_, tq)
    lane = lax.broadcasted_iota(jnp.int32, (tq, LANES), 1)
    blk = lane % nsp
    cur = (q0 + lax.broadcasted_iota(jnp.int32, (tq, LANES), 0)) // SLC_BLOCK
    forced = (blk == 0) | (blk == cur) | (blk == cur - 1)
    score = jnp.where(blk <= cur, jnp.where(forced, BIG, imp), -BIG)
    rank = jnp.zeros((tq, LANES), F32)
    for d in range(1, nsp):
        wrapped = blk + d >= nsp
        partner = jnp.where(wrapped, pltpu.roll(score, nsp - d, 1), pltpu.roll(score, LANES - d, 1))
        rank = rank + jnp.where(partner > score, 1.0, jnp.where((partner == score) & wrapped, 1.0, 0.0))
    sel_o[0] = jnp.where(rank < min(SLC_TOP, n_slc), 1.0, 0.0).astype(sel_o.dtype)


def _cmp_sel(q, kab, v1, m, *, n_cmp, n_slc, tq, bounded):
    B, T, D = q.shape
    ncp = kab.shape[1]
    assert m.shape == (N_KV_HEADS, ncp, LANES)
    whole = lambda a: pl.BlockSpec((1,) + a.shape[1:], lambda b, i: (b,) + (0,) * (a.ndim - 1))
    kern = functools.partial(_cmp_sel_kernel, n_cmp=n_cmp, n_slc=n_slc, bounded=bounded)
    return pl.pallas_call(
        kern, grid=(B, T // tq),
        in_specs=[pl.BlockSpec((1, tq, D), lambda b, i: (b, i, 0)), whole(kab), whole(v1),
                  pl.BlockSpec(m.shape, lambda b, i: (0, 0, 0))],
        out_specs=[pl.BlockSpec((1, tq, D), lambda b, i: (b, i, 0)), pl.BlockSpec((1, tq, LANES), lambda b, i: (b, i, 0))],
        out_shape=[jax.ShapeDtypeStruct((B, T, D), F32), jax.ShapeDtypeStruct((B, T, LANES), BF16)],
        compiler_params=_cparams(("parallel", "arbitrary")), name="cmp_sel_attn",
    )(q, kab, v1, m)


def _flash_pairs(lhs, kab_ref, v1_ref, kt_lo, kt_hi, tk, bias_fn, bounded):
    rows = lhs[0].shape[0]

    def body(kt, carry):
        k0 = pl.multiple_of(kt * tk, tk)
        biases = bias_fn(k0)
        out = []
        for g in range(N_KV_HEADS):
            vv = v1_ref[0, pl.ds(k0, tk), g * LANES:(g + 1) * LANES]
            for half in range(2):
                idx = 2 * g + half
                s = _mm(lhs[g], kab_ref[0, pl.ds(k0, tk), idx * LANES:(idx + 1) * LANES], "nt") + biases[g]
                if bounded:
                    out.append(carry[idx] + _mm(jnp.exp2(s), vv))
                else:
                    m, acc = carry[idx]
                    m_new = jnp.maximum(m, jnp.max(s, -1, keepdims=True))
                    out.append((m_new, jnp.exp2(m - m_new) * acc + _mm(jnp.exp2(s - m_new), vv)))
        return tuple(out)

    zero = jnp.zeros((rows, LANES), F32)
    if bounded:
        return lax.fori_loop(kt_lo, kt_hi, body, (zero,) * (2 * N_KV_HEADS))
    start = (jnp.full((rows, 1), M_INIT, F32), zero)
    return tuple(acc for _, acc in lax.fori_loop(kt_lo, kt_hi, body, (start,) * (2 * N_KV_HEADS)))


def _slc_win_attn_kernel(q_ref, sel_ref, skab_ref, sv1_ref, wkab_ref, wv1_ref, os_o, ow_o, *, tk, bounded):
    tq = q_ref.shape[1]
    nsp = sel_ref.shape[2] // N_KV_HEADS
    q0 = pl.program_id(1) * tq
    q_pos = q0 + lax.broadcasted_iota(jnp.int32, (tq, tk), 0)
    k_off = lax.broadcasted_iota(jnp.int32, (tq, tk), 1)
    sel_lane = lax.broadcasted_iota(jnp.int32, (N_KV_HEADS * nsp, tk), 0)
    blk_k = lax.broadcasted_iota(jnp.int32, (N_KV_HEADS * nsp, tk), 1)
    drop = jnp.where(sel_ref[0].astype(F32) > 0.5, 0.0, MASKED).astype(BF16)
    q = q_ref[0]
    kt_hi = (q0 + tq + tk - 1) // tk
    kt_lo_win = jnp.maximum(q0 - WINDOW + 1, 0) // tk

    twice = lambda b: jnp.concatenate([b, b], axis=0)

    def win_bias(k0):
        d = q_pos - (k_off + k0)
        return [twice(jnp.where((d >= 0) & (d < WINDOW), 0.0, MASKED))] * N_KV_HEADS

    def slc_bias(k0):
        causal = jnp.where(k_off + k0 <= q_pos, 0.0, MASKED)
        key_blk = (blk_k + k0) // SLC_BLOCK
        out = []
        for g in range(N_KV_HEADS):
            expand = jnp.where(g * nsp + key_blk == sel_lane, 1.0, 0.0).astype(BF16)
            out.append(twice(jnp.dot(drop, expand, preferred_element_type=F32) + causal))
        return out

    lhs = [_pair_rows(q, g) for g in range(N_KV_HEADS)]
    acc_s = _flash_pairs(lhs, skab_ref, sv1_ref, 0, kt_hi, tk, slc_bias, bounded)
    acc_w = _flash_pairs(lhs, wkab_ref, wv1_ref, kt_lo_win, kt_hi, tk, win_bias, bounded)
    for g in range(N_KV_HEADS):
        for p_ in range(2):
            col = g * 2 * LANES + p_ * LANES
            os_o[0, :, col:col + LANES] = _finish_pair(acc_s[2 * g], acc_s[2 * g + 1], p_, tq)
            ow_o[0, :, col:col + LANES] = _finish_pair(acc_w[2 * g], acc_w[2 * g + 1], p_, tq)


def _slc_win_attn(q, sel, skab, sv1, wkab, wv1, *, tq, tk, bounded):
    B, T, D = q.shape
    whole = lambda a: pl.BlockSpec((1,) + a.shape[1:], lambda b, i: (b, 0, 0))
    row = lambda w: pl.BlockSpec((1, tq, w), lambda b, i: (b, i, 0))
    return pl.pallas_call(
        functools.partial(_slc_win_attn_kernel, tk=tk, bounded=bounded), grid=(B, T // tq),
        in_specs=[row(D), row(sel.shape[2]), whole(skab), whole(sv1), whole(wkab), whole(wv1)],
        out_specs=[row(D), row(D)],
        out_shape=[jax.ShapeDtypeStruct((B, T, D), F32)] * 2,
        compiler_params=_cparams(("parallel", "arbitrary")), name="slc_win_attn",
    )(q, sel, skab, sv1, wkab, wv1)


def _pair_keys(k):
    z = jnp.zeros_like(k)
    return jnp.concatenate([k, z, z, k], axis=-1).reshape(k.shape[0], k.shape[1], -1)


def _ones_values(v):
    return jnp.concatenate([v, jnp.ones_like(v)], axis=-1).reshape(v.shape[0], v.shape[1], -1)


def _sample_attn_kernel(blk_ref, newpos_ref, q_ref, kpos_ref, new_ref, win_ref, cache_hbm, os_o, ow_o, kbuf, vbuf, sem, *,
                        n_q, n_sel, pos_base, win_base, n_win, hp):
    b = pl.program_id(0)
    slot = b % 2
    per_seq = n_q * N_KV_HEADS * n_sel
    per_page = cache_hbm.shape[1] // SLC_BLOCK

    def start_copies(step, into):
        def one(i, carry):
            blk = jnp.maximum(blk_ref[step * per_seq + i], 0)
            qg = i // n_sel
            lane0 = pl.multiple_of(((qg % N_KV_HEADS) // 2) * LANES, LANES)
            src_rows = pl.ds(pl.multiple_of((blk % per_page) * SLC_BLOCK, SLC_BLOCK), SLC_BLOCK)
            for c, dst in ((0, kbuf), (1, vbuf)):
                pltpu.make_async_copy(cache_hbm.at[blk // per_page, src_rows, pl.ds(c * KV_W + lane0, LANES)],
                                      dst.at[into, qg, pl.ds((i % n_sel) * SLC_BLOCK, SLC_BLOCK)], sem.at[into]).start()
            return carry
        lax.fori_loop(0, per_seq, one, 0)

    @pl.when(b == 0)
    def _():
        start_copies(0, 0)

    @pl.when(b + 1 < pl.num_programs(0))
    def _():
        start_copies(b + 1, 1 - slot)

    for dst in (kbuf, vbuf):
        pltpu.make_async_copy(dst.at[slot], dst.at[slot], sem.at[slot]).wait()

    for qg in range(n_q * N_KV_HEADS):
        j_new = newpos_ref[b * n_q * N_KV_HEADS + qg]
        win = (qg % N_KV_HEADS) // 2

        @pl.when(j_new >= 0)
        def _(qg=qg, j_new=j_new, win=win):
            at = pl.ds(pl.multiple_of(j_new * SLC_BLOCK, SLC_BLOCK), SLC_BLOCK)
            kbuf[slot, qg, at, :] = new_ref[0, win]
            vbuf[slot, qg, at, :] = new_ref[0, N_KV_HEADS // 2 + win]

    tq = q_ref.shape[1]
    rows = HEADS_PER_KV * tq
    nk = n_sel * SLC_BLOCK
    nwp = win_ref.shape[1]
    q = q_ref[0]
    row_q = lax.broadcasted_iota(jnp.int32, (rows, 1), 0) % tq
    w_id = lax.broadcasted_iota(jnp.int32, (rows, nwp), 1)
    d_win = (pos_base + row_q) - (win_base + w_id)
    win_bias = jnp.where((d_win >= 0) & (d_win < WINDOW) & (win_base + w_id >= 0) & (w_id < n_win), 0.0, MASKED)

    def attend(s, v):
        m = jnp.maximum(jnp.max(s, -1, keepdims=True), M_INIT)
        e = jnp.exp(s - m)
        p = e / jnp.maximum(jnp.sum(e, -1, keepdims=True), 1e-30)
        return _mm(p, v, hp=hp)

    for g in range(N_KV_HEADS):
        qs = _stack_heads(q, g, tq)
        zeros = jnp.zeros_like(qs)
        qs_pair = jnp.concatenate([qs, zeros] if g % 2 == 0 else [zeros, qs], axis=1)
        lanes_g = slice((g % 2) * HEAD_DIM, (g % 2 + 1) * HEAD_DIM)
        o_s = jnp.zeros((rows, HEAD_DIM), F32)
        for qi in range(n_q):
            qg = qi * N_KV_HEADS + g
            kpos = kpos_ref[0, qg:qg + 1, :]
            bias = jnp.broadcast_to(jnp.where(kpos <= pos_base + qi, 0.0, MASKED), (rows, nk))
            o_qi = attend(_mm(qs_pair, kbuf[slot, qg], "nt", hp) + bias, vbuf[slot, qg])[:, lanes_g]
            o_s = jnp.where(row_q == qi, o_qi, o_s)
        o_w = attend(_mm(qs, win_ref[0, :, g * HEAD_DIM:(g + 1) * HEAD_DIM], "nt", hp) + win_bias,
                     win_ref[0, :, KV_W + g * HEAD_DIM:KV_W + (g + 1) * HEAD_DIM])
        for i in range(HEADS_PER_KV):
            col = (HEADS_PER_KV * g + i) * HEAD_DIM
            os_o[0, :, col:col + HEAD_DIM] = o_s[i * tq:(i + 1) * tq]
            ow_o[0, :, col:col + HEAD_DIM] = o_w[i * tq:(i + 1) * tq]


def _sample_attn(pool_blk, q, kpos, new_win, win_all, cache_rows, *, n_q, n_sel, pos_base, win_base, n_win, hp):
    B, tq, D = q.shape
    kern = functools.partial(_sample_attn_kernel, n_q=n_q, n_sel=n_sel, pos_base=pos_base, win_base=win_base,
                             n_win=n_win, hp=hp)
    blk = lambda a: pl.BlockSpec((1,) + a.shape[1:], lambda b, pb, npos: (b,) + (0,) * (a.ndim - 1))
    buf = pltpu.VMEM((2, n_q * N_KV_HEADS, n_sel * SLC_BLOCK, LANES), F32)
    is_new = pool_blk < 0
    new_pos = jnp.where(jnp.any(is_new, axis=-1), jnp.argmax(is_new, axis=-1), -1).astype(jnp.int32)
    grid_spec = pltpu.PrefetchScalarGridSpec(
        num_scalar_prefetch=2, grid=(B,),
        in_specs=[blk(q), blk(kpos), blk(new_win), blk(win_all), pl.BlockSpec(memory_space=pl.ANY)],
        out_specs=[blk(q), blk(q)],
        scratch_shapes=[buf, buf, pltpu.SemaphoreType.DMA((2,))])
    return pl.pallas_call(
        kern, grid_spec=grid_spec, out_shape=[jax.ShapeDtypeStruct((B, tq, D), F32)] * 2,
        compiler_params=_cparams(("arbitrary",)), name="sample_slc_win_attn",
    )(pool_blk.reshape(-1), new_pos.reshape(-1), q, kpos, new_win, win_all, cache_rows)


def _pick_tile(n, cap=512):
    return next(t for t in (512, 256, 128, 64, 32, 16, 8) if t <= cap and n % t == 0)


def _pad_rows(x, t_pad):
    return jnp.pad(x, ((0, 0), (0, t_pad - x.shape[1]), (0, 0)))


def _rwkv_layer(x, shift0, wkv0, p, *, tm, hp, inv_hp):
    B, T, D = x.shape
    t_pad = -(-T // tm) * tm
    xp = _pad_rows(x, t_pad)
    r, lw, k, v, a, g, shift = _rwkv_proj(xp, shift0[:, None, :], p["g"], p["mix"], p["vec"], p["wrkv"], p["dw1"],
                                          p["dw2"], p["aw1"], p["aw2"], p["gw1"], p["gw2"], t_real=T, tm=tm, hp=hp)
    if t_pad != T:
        live = (jnp.arange(t_pad) < T)[None, :, None]
        r, lw, k, v, a = (jnp.where(live, t, 0.0) for t in (r, lw, k, v, a))
    y, s_fin = _wkv(r, lw, k, v, a, p["vec"], wkv0, hp=hp, inv_hp=inv_hp)
    n = B * t_pad
    out = _row_call(functools.partial(_rwkv_out_kernel, hp=hp), [xp.reshape(n, D), y.reshape(n, D), g.reshape(n, D)],
                    [p["wo"]], [D], [F32], _pick_tile(n), "rwkv_out")[0]
    return out.reshape(B, t_pad, D)[:, :T], shift[:, 0], s_fin


def _sub_blocks(rows):
    S, L = rows.shape[:2]
    n_sub = L // CMP_STRIDE
    r = rows[:, :n_sub * CMP_STRIDE].reshape(S, n_sub, CMP_STRIDE, 2, N_KV_HEADS, HEAD_DIM)
    r = jnp.transpose(r, (0, 3, 4, 1, 2, 5)).reshape(S, 2, N_KV_HEADS, n_sub, CMP_STRIDE * HEAD_DIM)
    r_pad = -(-n_sub // SUBLANES) * SUBLANES
    return jnp.pad(r, ((0, 0),) * 3 + ((0, r_pad - n_sub), (0, 0))), n_sub - CMP_BLOCK // CMP_STRIDE + 1


def _compressed_kv(rows, p, *, hp):
    xsub, n_cmp = _sub_blocks(rows)
    out = _compress(xsub, p["cmp_pe"], p["cmp_w1"], p["cmp_w2"], p["k_norm_g"][0:1], hp=hp)
    S, _, G, R, dh = out.shape
    out = jnp.transpose(out, (1, 0, 3, 2, 4)).reshape(2, S, R, G * dh).astype(F32 if hp else BF16)
    return out[0], out[1], n_cmp


def _overlap_matrix(ncp, n_cmp, n_slc, nsp):
    c0 = np.arange(ncp)[:, None] * CMP_STRIDE
    s0 = np.arange(nsp)[None, :] * SLC_BLOCK
    m = (c0 < s0 + SLC_BLOCK) & (c0 + CMP_BLOCK > s0) & (np.arange(ncp)[:, None] < n_cmp) & (np.arange(nsp)[None, :] < n_slc)
    return jnp.asarray(m.astype(np.float32), dtype=BF16)


def _moe(h, p, *, tm, tf, hp):
    n, D = h.shape
    xn, logits = _row_call(_router_kernel, [h], [p["g_moe"], p["w_router"]], [D, LANES], [F32 if hp else BF16, F32],
                           _pick_tile(n), "moe_router")
    top_v, top_e = lax.top_k(logits[:, :N_EXPERTS], TOP_K)
    gates = jax.nn.softmax(top_v, -1)
    n_pairs = n * TOP_K
    flat_e = top_e.reshape(n_pairs)
    order = jnp.argsort(flat_e)
    seen = jnp.cumsum((flat_e[:, None] == jnp.arange(N_EXPERTS)).astype(jnp.int32), axis=0)
    counts = seen[-1]
    padded = (counts + tm - 1) // tm * tm
    pad_end = jnp.cumsum(padded)
    pad_start = pad_end - padded
    start = jnp.cumsum(counts) - counts
    rank = jnp.take_along_axis(seen, flat_e[:, None], axis=1)[:, 0] - 1
    dest = (pad_start[flat_e] + rank).reshape(n, TOP_K)
    n_blocks = -(-n_pairs // tm) + N_EXPERTS
    cap = n_blocks * tm
    block_e = jnp.minimum(jnp.searchsorted(pad_end, jnp.arange(n_blocks) * tm, side="right"), N_EXPERTS - 1).astype(jnp.int32)
    row_e = jnp.repeat(block_e, tm)
    k_in_e = jnp.arange(cap, dtype=jnp.int32) - pad_start[row_e]
    src = order[jnp.clip(start[row_e] + k_in_e, 0, n_pairs - 1)] // TOP_K
    buf_tok = jnp.where((k_in_e >= 0) & (k_in_e < counts[row_e]), src, n).astype(jnp.int32)
    n_used = (pad_end[-1] // tm).astype(jnp.int32).reshape(1)
    xs = xn[jnp.minimum(buf_tok, n - 1)]
    ys = _moe_ffn(block_e, n_used, xs, p["moe_wg"], p["moe_wu"], p["moe_wd"], tm=tm, tf=tf, hp=hp)
    y = ys[dest[:, 0]] * gates[:, 0:1] + ys[dest[:, 1]] * gates[:, 1:2]
    return h + y


def kernel(x_prompt, x_sample, cache_cmp_kv, cache_slc_kv, state_win_kv, state_wkv, state_shift, page_table, norm_g, rw_mix, rw_vec, rw_w_rkv, rw_w_o, rw_decay_w1, rw_decay_w2, rw_iclr_w1, rw_iclr_w2, rw_gate_w1, rw_gate_w2, ffn_w_gate, ffn_w_up, ffn_w_down, moe_router, moe_w_gate, moe_w_up, moe_w_down, kv_norm_g, w_kv, k_norm_g, cmp_pe, cmp_w1, cmp_w2, w_qg, q_norm_g, w_o):
    B, T, D = x_prompt.shape
    SB, S, _ = x_sample.shape
    assert norm_g.shape[0] == 2 and rw_mix.shape[0] == 1 and w_qg.shape[0] == 1, "one RWKV-7 layer + one NSA layer"
    assert D == N_HEADS * HEAD_DIM and T % 512 == 0 and S <= SUBLANES
    page = cache_cmp_kv.shape[1]
    past = page_table.shape[1] * page
    wb = state_win_kv.shape[1]
    assert past % SLC_BLOCK == 0 and page % SLC_BLOCK == 0 and S <= SLC_BLOCK
    bf = lambda w: w.astype(BF16)
    pad8 = lambda m: jnp.pad(m, ((0, SUBLANES - m.shape[0]), (0, 0)))
    dff = ffn_w_gate.shape[2]
    tf = dff // 2 if (dff // 2) % LANES == 0 else dff
    n_p, n_s = B * T, SB * S
    both = lambda w: (bf(w), w)
    pick = lambda d, hp: {k: (v[hp] if isinstance(v, tuple) else v) for k, v in d.items()}

    rw = dict(g=norm_g[0, 0][None], mix=pad8(rw_mix[0]), vec=pad8(rw_vec[0]), wrkv=both(rw_w_rkv[0]), wo=both(rw_w_o[0]),
              dw1=both(rw_decay_w1[0]), dw2=both(rw_decay_w2[0]), aw1=both(rw_iclr_w1[0]), aw2=both(rw_iclr_w2[0]),
              gw1=both(rw_gate_w1[0]), gw2=both(rw_gate_w2[0]))
    H = D // RW_HEAD_DIM
    hp_, shift_p, wkv_p = _rwkv_layer(x_prompt, jnp.zeros((B, D), F32), None,
                                      pick(rw, 0), tm=256, hp=False, inv_hp=False)
    hs, shift_s, wkv_s = _rwkv_layer(x_sample, state_shift[0], state_wkv[0], pick(rw, 1), tm=WKV_CHUNK, hp=True, inv_hp=True)
    ffn_g = norm_g[0, 1][None]
    hp_ = _ffn(hp_.reshape(n_p, D), ffn_g, bf(ffn_w_gate[0]), bf(ffn_w_up[0]), bf(ffn_w_down[0]), tm=512, tf=dff, hp=False)
    hs = _ffn(hs.reshape(n_s, D), ffn_g, ffn_w_gate[0], ffn_w_up[0], ffn_w_down[0], tm=n_s, tf=tf, hp=True)

    kv_consts = lambda hp: [kv_norm_g[None], both(w_kv)[hp], jnp.tile(k_norm_g, (1, N_KV_HEADS)),
                            _block_diag(KV_W, HEAD_DIM, 1.0 / HEAD_DIM)]
    place_k = np.zeros((KV_W, N_KV_HEADS * 4 * HEAD_DIM), np.float32)
    place_v = np.zeros((KV_W, N_KV_HEADS * 2 * HEAD_DIM), np.float32)
    ones_v = np.zeros((1, N_KV_HEADS * 2 * HEAD_DIM), np.float32)
    for g in range(N_KV_HEADS):
        for d in range(HEAD_DIM):
            place_k[g * HEAD_DIM + d, g * 4 * HEAD_DIM + d] = 1.0
            place_k[g * HEAD_DIM + d, g * 4 * HEAD_DIM + 3 * HEAD_DIM + d] = 1.0
            place_v[g * HEAD_DIM + d, g * 2 * HEAD_DIM + d] = 1.0
        ones_v[0, g * 2 * HEAD_DIM + HEAD_DIM:(g + 1) * 2 * HEAD_DIM] = 1.0
    pair_consts = [jnp.asarray(place_k, dtype=BF16), jnp.asarray(place_v, dtype=BF16), jnp.asarray(ones_v)]
    as_rows = lambda t, b: t.reshape(b, -1, 2, N_KV_HEADS, HEAD_DIM)
    kv_p = _row_call(functools.partial(_kv_proj_kernel, hp=False, pair_out=True), [hp_], kv_consts(False) + pair_consts,
                     [2 * KV_W] * 3 + [4 * KV_W, 2 * KV_W] * 2, [F32] * 3 + [BF16] * 4, 512, "kv_proj")
    cmp_p, slc_p, win_p = (as_rows(t, B) for t in kv_p[:3])
    skab, sv1, wkab, wv1 = (t.reshape(B, T, -1) for t in kv_p[3:])
    cmp_s, slc_s, win_s = (as_rows(t, SB) for t in _row_call(
        functools.partial(_kv_proj_kernel, hp=True, pair_out=False), [hs], kv_consts(True), [2 * KV_W] * 3, [F32] * 3,
        n_s, "kv_proj"))
    win_all_s = jnp.concatenate([state_win_kv, win_s], axis=1)

    cp = dict(cmp_pe=cmp_pe.reshape(2, CMP_BLOCK // CMP_STRIDE, 1, CMP_STRIDE * HEAD_DIM),
              cmp_w1=both(cmp_w1.reshape(2, CMP_BLOCK // CMP_STRIDE, CMP_STRIDE * HEAD_DIM, HEAD_DIM)),
              cmp_w2=both(cmp_w2), k_norm_g=k_norm_g)
    kc_p, vc_p, ncmp_p = _compressed_kv(cmp_p, pick(cp, 0), hp=False)
    assert S < CMP_STRIDE and page % CMP_STRIDE == 0
    n_ratio = CMP_BLOCK // CMP_STRIDE
    w1r = cmp_w1.reshape(2, n_ratio, CMP_STRIDE, HEAD_DIM, HEAD_DIM)
    w2x = jnp.einsum("cjsdh,ef->csedjfh", w1r, jnp.eye(2, dtype=F32)).reshape(2, CMP_STRIDE * LANES, n_ratio * LANES)
    n_pages = page_table.shape[1]
    pages_per_step = next(pp for pp in (32, 16, 8, 4, 2, 1) if n_pages % pp == 0)
    part_s = _paged_sub_proj(page_table, cache_cmp_kv.reshape(-1, page, 2 * KV_W), w2x, pages_per_step=pages_per_step, hp=True)
    kcv_s = _compress_tail(part_s, cp["cmp_pe"], cp["cmp_w1"][1], cmp_w2, k_norm_g[0:1], hp=True)
    kc_s, vc_s, ncmp_s = kcv_s[:, 0], kcv_s[:, 1], past // CMP_STRIDE - n_ratio + 1

    ge = np.zeros((LANES, N_BRANCH * D), np.float32)
    for br in range(N_BRANCH):
        for hd in range(N_HEADS):
            ge[br * N_HEADS + hd, br * D + hd * HEAD_DIM: br * D + (hd + 1) * HEAD_DIM] = 1.0
    nsa = dict(g_attn=norm_g[1, 0][None], wq=both(w_qg[0, :, :D]),
               wgate=both(jnp.pad(w_qg[0, :, D:], ((0, 0), (0, LANES - N_BRANCH * N_HEADS)))),
               q_norm_g=jnp.tile(q_norm_g[0], N_HEADS)[None], bd_q=_block_diag(D, HEAD_DIM, 1.0 / HEAD_DIM),
               w_o=both(w_o[0]), gate_expand=jnp.asarray(ge, dtype=BF16))

    def query_side(h, tm, hp, q_scale):
        c = pick(nsa, hp)
        return _row_call(functools.partial(_q_proj_kernel, hp=hp, q_scale=q_scale), [h],
                         [c["g_attn"], c["wq"], c["wgate"], c["q_norm_g"], c["bd_q"]], [D, LANES],
                         [F32 if hp else BF16, F32], tm, "q_proj")

    def merge(h, gates, o_cmp, o_slc, o_win, tm, hp):
        c = pick(nsa, hp)
        return _row_call(functools.partial(_merge_kernel, hp=hp), [h, gates, o_cmp, o_slc, o_win],
                         [c["gate_expand"], c["w_o"]], [D], [F32], tm, "nsa_merge")[0]

    q_p, gates_p = query_side(hp_, 512, False, ATTN_SCALE * LOG2E)
    q_p = q_p.reshape(B, T, D)
    n_slc_p = T // SLC_BLOCK
    nsp_p = LANES // N_KV_HEADS
    assert n_slc_p <= nsp_p, "the packed block-selection layout holds at most 32 selection blocks per kv group"
    m_one = _overlap_matrix(kc_p.shape[1], ncmp_p, n_slc_p, nsp_p)
    m_p = jnp.stack([jnp.pad(m_one, ((0, 0), (g * nsp_p, LANES - (g + 1) * nsp_p))) for g in range(N_KV_HEADS)])
    heads = lambda t: t.reshape(t.shape[0], t.shape[1], N_KV_HEADS, HEAD_DIM)
    kcab, vc1 = _pair_keys(heads(kc_p)), _ones_values(heads(vc_p))

    def prompt_branches(bounded):
        o_cmp, sel = _cmp_sel(q_p, kcab, vc1, m_p, n_cmp=ncmp_p, n_slc=n_slc_p, tq=256, bounded=bounded)
        return (o_cmp,) + tuple(_slc_win_attn(q_p, sel, skab, sv1, wkab, wv1, tq=256, tk=256, bounded=bounded))

    score_bound = HEAD_DIM * ATTN_SCALE * jnp.max(jnp.abs(q_norm_g[0])) * jnp.max(jnp.abs(k_norm_g))
    o_cmp_p, o_slc_p, o_win_p = lax.cond(score_bound <= SCORE_BOUND_MAX, lambda: prompt_branches(True),
                                         lambda: prompt_branches(False))
    hp_ = merge(hp_, gates_p, o_cmp_p.reshape(n_p, D), o_slc_p.reshape(n_p, D), o_win_p.reshape(n_p, D), 512, False)

    kv_lanes = lambda t, dt: t.reshape(t.shape[0], t.shape[1], 2 * KV_W).astype(dt)
    q_s, gates_s = query_side(hs, n_s, True, ATTN_SCALE)
    q_s = _pad_rows(q_s.reshape(SB, S, D), SUBLANES)
    n_slc_s = (past + S - 1) // SLC_BLOCK + 1
    nsp_s = -(-n_slc_s // LANES) * LANES
    m_s = _overlap_matrix(kc_s.shape[1], ncmp_s, n_slc_s, nsp_s)
    o_cmp_s, sel_s = _cmp_attn(q_s, kc_s, vc_s, m_s, n_cmp=ncmp_s, n_slc=n_slc_s, pos_base=past, per_tile=False,
                               tq=SUBLANES, hp=True)
    n_sel = min(SLC_TOP, n_slc_s)
    sel_s = sel_s.reshape(SB, SUBLANES, N_KV_HEADS, nsp_s)[:, :S, :, :n_slc_s] > 0.5
    slot = jnp.cumsum(sel_s, axis=-1) - 1
    idx_s = jnp.sum(jnp.where(sel_s[..., None] & (slot[..., None] == jnp.arange(n_sel)),
                              jnp.arange(n_slc_s)[:, None], 0), axis=-2).astype(jnp.int32)
    in_pool = idx_s < past // SLC_BLOCK
    per_page = page // SLC_BLOCK
    blk_c = jnp.minimum(idx_s, past // SLC_BLOCK - 1)
    pages = jnp.take_along_axis(page_table, (blk_c // per_page).reshape(SB, -1), axis=1).reshape(idx_s.shape)
    pool_blk = jnp.where(in_pool, pages * per_page + blk_c % per_page, -1).astype(jnp.int32)
    new_rows = slc_s[:, jnp.clip(jnp.arange(SLC_BLOCK), 0, S - 1)]
    new_win = jnp.transpose(new_rows.reshape(SB, SLC_BLOCK, 2, N_KV_HEADS // 2, LANES), (0, 2, 3, 1, 4))
    new_win = new_win.reshape(SB, N_KV_HEADS, SLC_BLOCK, LANES)
    kpos = (idx_s[..., None] * SLC_BLOCK + jnp.arange(SLC_BLOCK)).reshape(SB, S * N_KV_HEADS, n_sel * SLC_BLOCK).astype(jnp.int32)
    kpos = jnp.pad(kpos, ((0, 0), (0, -(-S * N_KV_HEADS // SUBLANES) * SUBLANES - S * N_KV_HEADS), (0, 0)))
    cache_rows = cache_slc_kv.reshape(-1, page, 2 * KV_W)
    n_win = wb + S
    nwp = -(-n_win // SUBLANES) * SUBLANES
    win_lanes = _pad_rows(kv_lanes(win_all_s, F32), nwp)
    o_slc_s, o_win_s = _sample_attn(pool_blk, q_s, kpos, new_win, win_lanes, cache_rows, n_q=S, n_sel=n_sel,
                                    pos_base=past, win_base=past - wb, n_win=n_win, hp=True)
    unpad = lambda t: t[:, :S].reshape(n_s, D)
    hs = merge(hs, gates_s, unpad(o_cmp_s), unpad(o_slc_s), unpad(o_win_s), n_s, True)

    moe = dict(g_moe=norm_g[1, 1][None], w_router=jnp.pad(moe_router[0], ((0, 0), (0, LANES - N_EXPERTS))),
               moe_wg=both(moe_w_gate[0]), moe_wu=both(moe_w_up[0]), moe_wd=both(moe_w_down[0]))
    hp_ = _moe(hp_, pick(moe, 0), tm=512, tf=dff, hp=False).reshape(B, T, D)
    hs = _moe(hs, pick(moe, 1), tm=128, tf=tf, hp=True).reshape(SB, S, D)

    keep_p = min(WINDOW, T)
    return (hp_, hs, cmp_p, cmp_s, slc_p, slc_s, win_p[:, T - keep_p:], win_all_s[:, win_all_s.shape[1] - wb:],
            wkv_p[None], wkv_s[None], shift_p[None], shift_s[None])
```
